```python
import math
import jax
import jax.numpy as jnp
from jax import lax
import numpy as np

D_MODEL = 1024
BATCH = 2
SEQ = 8192
DEPTH = 2

GRID_W = 64
CTX_LEN = 256
EPS = 1e-6
CHUNK = 128
Q_BLOCK = 128

MLA_HEADS = 4
MLA_Q_LORA = 256
MLA_KV_LORA = 128
MLA_NOPE = 64
MLA_ROPE = 32
MLA_V = 64
MLA_SCALE = (MLA_NOPE + MLA_ROPE) ** -0.5
ROPE_AXIS_DIM = MLA_ROPE // 2
ROPE_BASE = 10000.0

RET_HEADS = 4
RET_QK = 64
RET_V = 64

SSD_HEADS = 8
SSD_HEADDIM = 64
SSD_GROUPS = 2
SSD_STATE = 64
SSD_CONV = 3
SSD_INNER = SSD_HEADS * SSD_HEADDIM
SSD_CONV_CH = SSD_INNER + 2 * SSD_GROUPS * SSD_STATE

IN_SIZES = (MLA_Q_LORA, MLA_KV_LORA, MLA_ROPE,
            RET_HEADS * RET_QK, RET_HEADS * RET_QK, RET_HEADS * RET_V, RET_HEADS * RET_V,
            SSD_INNER, SSD_CONV_CH, SSD_HEADS, SSD_HEADS)
IN_WIDTH = sum(IN_SIZES)
MIX_WIDTH = MLA_HEADS * MLA_V + RET_HEADS * RET_V + SSD_INNER

D_FF = 2816
N_EXPERTS = 8
TOP_K = 2
D_FF_EXPERT = 1408
N_DENSE = (DEPTH + 1) // 2
N_MOE = DEPTH // 2

kernel_name = "hybrid_mla_retention_ssd_dit_block"


def rmsnorm(x, w):
    xf = x.astype(jnp.float32)
    y = xf * lax.rsqrt(jnp.mean(xf * xf, axis=-1, keepdims=True) + EPS)
    return (y * w.astype(jnp.float32)).astype(x.dtype)


def modulate(h, shift, scale):
    return h * (1.0 + scale) + shift


def axial_rope_tables(rows, dtype):
    f32 = jnp.float32
    row = jnp.repeat(jnp.arange(rows, dtype=f32), GRID_W)
    col = jnp.tile(jnp.arange(GRID_W, dtype=f32), rows)
    inv_freq = ROPE_BASE ** (-jnp.arange(0, ROPE_AXIS_DIM, 2, dtype=f32) / ROPE_AXIS_DIM)
    ang_r = row[:, None] * inv_freq
    ang_c = col[:, None] * inv_freq
    ang = jnp.concatenate([ang_r, ang_r, ang_c, ang_c], axis=-1)
    return jnp.cos(ang).astype(dtype), jnp.sin(ang).astype(dtype)


def apply_axial_rope(x, cos, sin):
    x1, x2, x3, x4 = jnp.split(x, 4, axis=-1)
    rot = jnp.concatenate([-x2, x1, -x4, x3], axis=-1)
    return x * cos + rot * sin


def split_in(proj):
    offs = np.cumsum(IN_SIZES)[:-1].tolist()
    return jnp.split(proj, offs, axis=-1)


def block_attention(q, k, v):
    b, l, h, d = q.shape
    nb = l // Q_BLOCK
    qb = jnp.moveaxis(q.reshape(b, nb, Q_BLOCK, h, d), 1, 0)

    def one(q_blk):
        s = jnp.einsum('bqhd,bkhd->bhqk', q_blk, k).astype(jnp.float32) * MLA_SCALE
        p = jax.nn.softmax(s, axis=-1).astype(v.dtype)
        return jnp.einsum('bhqk,bkhd->bqhd', p, v)

    out = lax.map(one, qb)
    return jnp.moveaxis(out, 0, 1).reshape(b, l, h, v.shape[-1])


def mla_qkv(cq, ckv, kpe, q_norm_w, w_qb, kv_norm_w, w_kb, w_vb, cos, sin):
    b, l, _ = cq.shape
    q = (rmsnorm(cq, q_norm_w) @ w_qb).reshape(b, l, MLA_HEADS, MLA_NOPE + MLA_ROPE)
    q_nope, q_pe = q[..., :MLA_NOPE], q[..., MLA_NOPE:]
    ckv = rmsnorm(ckv, kv_norm_w)
    k_nope = (ckv @ w_kb).reshape(b, l, MLA_HEADS, MLA_NOPE)
    v = (ckv @ w_vb).reshape(b, l, MLA_HEADS, MLA_V)
    kpe = kpe[:, :, None, :]
    if cos is not None:
        q_pe = apply_axial_rope(q_pe, cos[:, None, :], sin[:, None, :])
        kpe = apply_axial_rope(kpe, cos[:, None, :], sin[:, None, :])
    q = jnp.concatenate([q_nope, q_pe], axis=-1)
    k = jnp.concatenate([k_nope, jnp.broadcast_to(kpe, (b, l, MLA_HEADS, MLA_ROPE))], axis=-1)
    return q, k, v


def chunked_scan(q, k, v, log_a, state0, include_diag):
    f32 = jnp.float32
    b, l, h, n = q.shape
    p = v.shape[-1]
    nc = l // CHUNK
    qc = q.astype(f32).reshape(b, nc, CHUNK, h, n)
    kc = k.astype(f32).reshape(b, nc, CHUNK, h, n)
    vc = v.astype(f32).reshape(b, nc, CHUNK, h, p)
    acum = jnp.cumsum(log_a.astype(f32).reshape(b, nc, CHUNK, h), axis=2)
    acum_h = jnp.swapaxes(acum, 2, 3)
    idx = jnp.arange(CHUNK)
    mask = (idx[:, None] >= idx[None, :]) if include_diag else (idx[:, None] > idx[None, :])
    seg = acum_h[..., :, None] - acum_h[..., None, :]
    decay = jnp.exp(jnp.where(mask, seg, -jnp.inf))
    scores = jnp.einsum('bcihn,bcjhn->bchij', qc, kc) * decay
    y = jnp.einsum('bchij,bcjhp->bcihp', scores, vc)
    w_end = jnp.exp(acum[:, :, -1:, :] - acum)
    chunk_states = jnp.einsum('bcjhn,bcjhp->bchnp', kc * w_end[..., None], vc)
    chunk_decay = jnp.exp(acum[:, :, -1, :])

    def step(state, inp):
        s_c, d_c = inp
        return state * d_c[:, :, None, None] + s_c, state

    final, states_in = lax.scan(step, state0,
                                (jnp.moveaxis(chunk_states, 1, 0), jnp.moveaxis(chunk_decay, 1, 0)))
    states_in = jnp.moveaxis(states_in, 0, 1)
    y = y + jnp.einsum('bcihn,bchnp->bcihp', qc * jnp.exp(acum)[..., None], states_in)
    return y.reshape(b, l, h, p), final


def bidir_scan(ctx_in, lat_in, bwd_diag):
    q_c, k_c, v_cf, v_cb, la_cf, la_cb = ctx_in
    q_x, k_x, v_xf, v_xb, la_xf, la_xb = lat_in
    b, _, h, n = q_c.shape
    s0 = jnp.zeros((b, h, n, v_cf.shape[-1]), jnp.float32)
    flip = lambda t: jnp.flip(t, axis=1)
    y_cf, s_cf = chunked_scan(q_c, k_c, v_cf, la_cf, s0, True)
    y_cb, s_cb = chunked_scan(flip(q_c), flip(k_c), flip(v_cb), flip(la_cb), s0, bwd_diag)
    y_xf, _ = chunked_scan(q_x, k_x, v_xf, la_xf, s_cf, True)
    y_xb, _ = chunked_scan(flip(q_x), flip(k_x), flip(v_xb), flip(la_xb), s_cb, bwd_diag)
    return y_cf + flip(y_cb), y_xf + flip(y_xb)


def retention_inputs(rq, rk, rv, log_gamma):
    b, l, _ = rq.shape
    q = rq.reshape(b, l, RET_HEADS, RET_QK) * (RET_QK ** -0.5)
    k = rk.reshape(b, l, RET_HEADS, RET_QK)
    v = rv.reshape(b, l, RET_HEADS, RET_V)
    la_f = jnp.broadcast_to(log_gamma[0], (b, l, RET_HEADS))
    la_b = jnp.broadcast_to(log_gamma[1], (b, l, RET_HEADS))
    return (q, k, v, v, la_f, la_b)


def retention_output(y, g):
    b, l = y.shape[:2]
    mu = jnp.mean(y, axis=-1, keepdims=True)
    var = jnp.mean(jnp.square(y - mu), axis=-1, keepdims=True)
    yn = ((y - mu) * lax.rsqrt(var + EPS)).reshape(b, l, RET_HEADS * RET_V)
    return yn.astype(g.dtype) * jax.nn.silu(g)


def depthwise_conv(x, w, bias):
    y = lax.conv_general_dilated(x, w[:, None, :].astype(x.dtype), window_strides=(1,),
                                 padding=[(SSD_CONV // 2, SSD_CONV // 2)],
                                 dimension_numbers=('NWC', 'WIO', 'NWC'),
                                 feature_group_count=x.shape[-1])
    return y + bias


def ssd_inputs(xbc, dt_f_raw, dt_b_raw, conv_w, conv_b, dt_bias, a_log):
    f32 = jnp.float32
    b, l, _ = xbc.shape
    xbc = jax.nn.silu(depthwise_conv(xbc, conv_w, conv_b))
    xs, bm, cm = jnp.split(xbc, [SSD_INNER, SSD_INNER + SSD_GROUPS * SSD_STATE], axis=-1)
    xs = xs.reshape(b, l, SSD_HEADS, SSD_HEADDIM)
    rep = SSD_HEADS // SSD_GROUPS
    bm = jnp.repeat(bm.reshape(b, l, SSD_GROUPS, SSD_STATE), rep, axis=2)
    cm = jnp.repeat(cm.reshape(b, l, SSD_GROUPS, SSD_STATE), rep, axis=2)
    a = -jnp.exp(a_log.astype(f32))
    dt_bias = dt_bias.astype(f32)
    dt_f = jax.nn.softplus(dt_f_raw.astype(f32) + dt_bias[0])
    dt_b = jax.nn.softplus(dt_b_raw.astype(f32) + dt_bias[1])
    scan_in = (cm, bm, xs * dt_f[..., None], xs * dt_b[..., None], dt_f * a[0], dt_b * a[1])
    return scan_in, xs


def ssd_output(y, xs, z, d_skip, norm_w):
    b, l = y.shape[:2]
    y = y + xs.astype(jnp.float32) * d_skip.astype(jnp.float32)[:, None]
    y = y.reshape(b, l, SSD_INNER) * jax.nn.silu(z.astype(jnp.float32))
    return rmsnorm(y, norm_w).astype(z.dtype)


def hybrid_mixer(hx, hc, cos, sin, w_in, w_out, mla_q_norm_w, mla_w_qb, mla_kv_norm_w, mla_w_kb,
                 mla_w_vb, ret_decay_logit, ssd_conv_w, ssd_conv_b, ssd_dt_bias, ssd_a_log, ssd_d,
                 ssd_norm_w, with_ctx_out):
    px = split_in(hx @ w_in)
    pc = split_in(hc @ w_in)

    q_x, k_x, v_x = mla_qkv(px[0], px[1], px[2], mla_q_norm_w, mla_w_qb, mla_kv_norm_w, mla_w_kb, mla_w_vb, cos, sin)
    q_c, k_c, v_c = mla_qkv(pc[0], pc[1], pc[2], mla_q_norm_w, mla_w_qb, mla_kv_norm_w, mla_w_kb, mla_w_vb, None, None)
    k_all = jnp.concatenate([k_c, k_x], axis=1)
    v_all = jnp.concatenate([v_c, v_x], axis=1)
    b, l = hx.shape[:2]
    att_x = block_attention(q_x, k_all, v_all).reshape(b, l, MLA_HEADS * MLA_V)

    log_gamma = jax.nn.log_sigmoid(ret_decay_logit.astype(jnp.float32))
    ret_yc, ret_yx = bidir_scan(retention_inputs(pc[3], pc[4], pc[5], log_gamma),
                                retention_inputs(px[3], px[4], px[5], log_gamma), False)
    ret_x = retention_output(ret_yx, px[6])

    ssd_in_c, xs_c = ssd_inputs(pc[8], pc[9], pc[10], ssd_conv_w, ssd_conv_b, ssd_dt_bias, ssd_a_log)
    ssd_in_x, xs_x = ssd_inputs(px[8], px[9], px[10], ssd_conv_w, ssd_conv_b, ssd_dt_bias, ssd_a_log)
    ssd_yc, ssd_yx = bidir_scan(ssd_in_c, ssd_in_x, True)
    ssd_x = ssd_output(ssd_yx, xs_x, px[7], ssd_d, ssd_norm_w)

    out_x = jnp.concatenate([att_x, ret_x, ssd_x], axis=-1) @ w_out
    if not with_ctx_out:
        return out_x, None
    lc = hc.shape[1]
    att_c = block_attention(q_c, k_c, v_c).reshape(b, lc, MLA_HEADS * MLA_V)
    ret_c = retention_output(ret_yc, pc[6])
    ssd_c = ssd_output(ssd_yc, xs_c, pc[7], ssd_d, ssd_norm_w)
    out_c = jnp.concatenate([att_c, ret_c, ssd_c], axis=-1) @ w_out
    return out_x, out_c


def swiglu(h, w_gate, w_up, w_down):
    return (jax.nn.silu(h @ w_gate) * (h @ w_up)) @ w_down


def moe_swiglu(h, router, w_gate, w_up, w_down):
    b, l, d = h.shape
    t = h.reshape(b * l, d)
    logits = (t @ router).astype(jnp.float32)
    top_v, top_i = lax.top_k(logits, TOP_K)
    gates = jax.nn.softmax(top_v, axis=-1)
    dense_gate = jnp.einsum('tk,tke->te', gates,
                            jax.nn.one_hot(top_i, N_EXPERTS, dtype=jnp.float32)).astype(t.dtype)
    out = jnp.zeros_like(t)
    for e in range(N_EXPERTS):
        out = out + dense_gate[:, e:e + 1] * swiglu(t, w_gate[e], w_up[e], w_down[e])
    return out.reshape(b, l, d)


def channel_mix(h, layer, ffn_w_gate, ffn_w_up, ffn_w_down, moe_router, moe_w_gate, moe_w_up, moe_w_down):
    j = layer // 2
    if layer % 2 == 0:
        return swiglu(h, ffn_w_gate[j], ffn_w_up[j], ffn_w_down[j])
    return moe_swiglu(h, moe_router[j], moe_w_gate[j], moe_w_up[j], moe_w_down[j])


def setup_inputs(seed: int = 0) -> dict:
    key = jax.random.key(seed)
    ks = iter(jax.random.split(key, 40))
    f32 = jnp.float32
    D = D_MODEL

    def nrm(shape, scale):
        return jax.random.normal(next(ks), shape, f32) * scale

    def gain(shape):
        return 1.0 + 0.05 * jax.random.normal(next(ks), shape, f32)

    gamma0 = 1.0 - 2.0 ** (-5.0 - np.arange(RET_HEADS))
    ret_base = jnp.asarray(np.log(gamma0 / (1.0 - gamma0)).astype(np.float32))
    u = jax.random.uniform(next(ks), (DEPTH, 2, SSD_HEADS), f32)
    dt0 = jnp.exp(u * (math.log(0.1) - math.log(0.001)) + math.log(0.001))

    return {
        "x": nrm((BATCH, SEQ, D), 1.0),
        "c": nrm((BATCH, D), 1.0),
        "ctx": nrm((BATCH, CTX_LEN, D), 1.0),
        "c_ctx": nrm((D,), 1.0),
        "mod_w": nrm((DEPTH, D, 6 * D), 0.5 * D ** -0.5),
        "mod_b": nrm((DEPTH, 6 * D), 0.01),
        "norm1_w": gain((DEPTH, D)),
        "norm2_w": gain((DEPTH, D)),
        "w_in": nrm((DEPTH, D, IN_WIDTH), D ** -0.5),
        "w_out": nrm((DEPTH, MIX_WIDTH, D), MIX_WIDTH ** -0.5),
        "mla_q_norm_w": gain((DEPTH, MLA_Q_LORA)),
        "mla_w_qb": nrm((DEPTH, MLA_Q_LORA, MLA_HEADS * (MLA_NOPE + MLA_ROPE)), MLA_Q_LORA ** -0.5),
        "mla_kv_norm_w": gain((DEPTH, MLA_KV_LORA)),
        "mla_w_kb": nrm((DEPTH, MLA_KV_LORA, MLA_HEADS * MLA_NOPE), MLA_KV_LORA ** -0.5),
        "mla_w_vb": nrm((DEPTH, MLA_KV_LORA, MLA_HEADS * MLA_V), MLA_KV_LORA ** -0.5),
        "ret_decay_logit": ret_base + nrm((DEPTH, 2, RET_HEADS), 0.1),
        "ssd_conv_w": nrm((DEPTH, SSD_CONV, SSD_CONV_CH), SSD_CONV ** -0.5),
        "ssd_conv_b": nrm((DEPTH, SSD_CONV_CH), 0.01),
        "ssd_dt_bias": dt0 + jnp.log(-jnp.expm1(-dt0)),
        "ssd_a_log": jnp.log(jax.random.uniform(next(ks), (DEPTH, 2, SSD_HEADS), f32, 1.0, 16.0)),
        "ssd_d": gain((DEPTH, SSD_HEADS)),
        "ssd_norm_w": gain((DEPTH, SSD_INNER)),
        "ffn_w_gate": nrm((N_DENSE, D, D_FF), D ** -0.5),
        "ffn_w_up": nrm((N_DENSE, D, D_FF), D ** -0.5),
        "ffn_w_down": nrm((N_DENSE, D_FF, D), D_FF ** -0.5),
        "moe_router": nrm((N_MOE, D, N_EXPERTS), D ** -0.5),
        "moe_w_gate": nrm((N_MOE, N_EXPERTS, D, D_FF_EXPERT), D ** -0.5),
        "moe_w_up": nrm((N_MOE, N_EXPERTS, D, D_FF_EXPERT), D ** -0.5),
        "moe_w_down": nrm((N_MOE, N_EXPERTS, D_FF_EXPERT, D), D_FF_EXPERT ** -0.5),
        "final_norm_w": gain((D,)),
    }


def reference(x, c, ctx, c_ctx, mod_w, mod_b, norm1_w, norm2_w, w_in, w_out, mla_q_norm_w, mla_w_qb,
              mla_kv_norm_w, mla_w_kb, mla_w_vb, ret_decay_logit, ssd_conv_w, ssd_conv_b, ssd_dt_bias,
              ssd_a_log, ssd_d, ssd_norm_w, ffn_w_gate, ffn_w_up, ffn_w_down, moe_router, moe_w_gate,
              moe_w_up, moe_w_down, final_norm_w):
    n_lat = x.shape[1]
    rows = n_lat // GRID_W
    cos, sin = axial_rope_tables(rows, x.dtype)
    silu_c = jax.nn.silu(c)
    silu_cc = jax.nn.silu(c_ctx)
    h_ctx = ctx
    for i in range(DEPTH):
        last = i == DEPTH - 1
        mx = jnp.split((silu_c @ mod_w[i] + mod_b[i])[:, None, :], 6, axis=-1)
        mc = jnp.split(silu_cc @ mod_w[i] + mod_b[i], 6, axis=-1)
        hx = modulate(rmsnorm(x, norm1_w[i]), mx[0], mx[1])
        hc = modulate(rmsnorm(h_ctx, norm1_w[i]), mc[0], mc[1])
        out_x, out_c = hybrid_mixer(hx, hc, cos, sin, w_in[i], w_out[i], mla_q_norm_w[i], mla_w_qb[i],
                                    mla_kv_norm_w[i], mla_w_kb[i], mla_w_vb[i], ret_decay_logit[i],
                                    ssd_conv_w[i], ssd_conv_b[i], ssd_dt_bias[i], ssd_a_log[i], ssd_d[i],
                                    ssd_norm_w[i], not last)
        x = x + mx[2] * out_x
        hx = modulate(rmsnorm(x, norm2_w[i]), mx[3], mx[4])
        x = x + mx[5] * channel_mix(hx, i, ffn_w_gate, ffn_w_up, ffn_w_down,
                                    moe_router, moe_w_gate, moe_w_up, moe_w_down)
        if not last:
            h_ctx = h_ctx + mc[2] * out_c
            hc = modulate(rmsnorm(h_ctx, norm2_w[i]), mc[3], mc[4])
            h_ctx = h_ctx + mc[5] * channel_mix(hc, i, ffn_w_gate, ffn_w_up, ffn_w_down,
                                                moe_router, moe_w_gate, moe_w_up, moe_w_down)
    return rmsnorm(x, final_norm_w)
```

```python
import functools
import math

import jax
import jax.numpy as jnp
import numpy as np
from jax import lax
from jax.experimental import pallas as pl
from jax.experimental.pallas import tpu as pltpu

F32 = jnp.float32
BF16 = jnp.bfloat16
HIGHEST = lax.Precision.HIGHEST

EPS = 1e-6
CHUNK = 128
GRID_W = 64
MLA_HEADS = 4
MLA_Q_LORA = 256
MLA_KV_LORA = 128
MLA_NOPE = 64
MLA_ROPE = 32
MLA_V = 64
MLA_SCALE = (MLA_NOPE + MLA_ROPE) ** -0.5
ROPE_AXIS_DIM = MLA_ROPE // 2
ROPE_BASE = 10000.0
RET_HEADS = 4
RET_QK = 64
RET_V = 64
SSD_HEADS = 8
SSD_HEADDIM = 64
SSD_GROUPS = 2
SSD_STATE = 64
SSD_INNER = SSD_HEADS * SSD_HEADDIM
SSD_CONV_CH = SSD_INNER + 2 * SSD_GROUPS * SSD_STATE
N_EXPERTS = 8
IN_SIZES = (MLA_Q_LORA, MLA_KV_LORA, MLA_ROPE,
            RET_HEADS * RET_QK, RET_HEADS * RET_QK, RET_HEADS * RET_V, RET_HEADS * RET_V,
            SSD_INNER, SSD_CONV_CH, SSD_HEADS, SSD_HEADS)

LANES = 128
V7X_VMEM_LIMIT = 56 * 1024 * 1024

MLA_HP = 128
NEG = -1e30


def _cparams(sem):
    return pltpu.CompilerParams(dimension_semantics=sem, vmem_limit_bytes=V7X_VMEM_LIMIT)


def _dot(a, b):
    return jnp.dot(a, b, preferred_element_type=F32)


def _dot_nt(a, b):
    return lax.dot_general(a, b, (((1,), (1,)), ((), ())), preferred_element_type=F32)


def _dot_tn(a, b):
    return lax.dot_general(a, b, (((0,), (0,)), ((), ())), preferred_element_type=F32)


def _split3(x):
    hi = x.astype(BF16)
    r1 = x - hi.astype(F32)
    mid = r1.astype(BF16)
    lo = (r1 - mid.astype(F32)).astype(BF16)
    return hi, mid, lo


def _dot_exact_rhs01(x, e):
    hi, mid, lo = _split3(x)
    return _dot(hi, e) + _dot(mid, e) + _dot(lo, e)


def _dot_exact_lhs01(e, x):
    hi, mid, lo = _split3(x)
    return _dot(e, hi) + _dot(e, mid) + _dot(e, lo)


def _silu(x):
    return x / (1.0 + jnp.exp(-x))


def _rms(x, w):
    return x * lax.rsqrt(jnp.mean(x * x, axis=-1, keepdims=True) + EPS) * w


def _iota(shape, dim):
    return lax.broadcasted_iota(jnp.int32, shape, dim)


def _mod_kernel(a_ref, w_ref, b_ref, o_ref):
    a = _silu(a_ref[...])
    o_ref[0] = jnp.dot(a, w_ref[0], precision=HIGHEST, preferred_element_type=F32) + b_ref[0]


def _modulation(cond, mod_w, mod_b):
    depth, d, n = mod_w.shape
    tn = 1536
    return pl.pallas_call(
        _mod_kernel,
        grid=(depth, n // tn),
        in_specs=[pl.BlockSpec((8, d), lambda i, j: (0, 0)),
                  pl.BlockSpec((1, d, tn), lambda i, j: (i, 0, j)),
                  pl.BlockSpec((1, 1, tn), lambda i, j: (i, 0, j))],
        out_specs=pl.BlockSpec((1, 8, tn), lambda i, j: (i, 0, j)),
        out_shape=jax.ShapeDtypeStruct((depth, 8, n), F32),
        compiler_params=_cparams(("parallel", "parallel")),
        name="modulation",
    )(cond, mod_w, mod_b.reshape(depth, 1, n))


W_MLA, W_RET, W_Z, W_XBC, W_DT = 512, 1024, 512, 768, 128
W_ALL = W_MLA + W_RET + W_Z + W_XBC + W_DT


def _inproj_kernel(x_ref, mod_ref, n1_ref, win_ref, qn_ref, wq_ref, wqr_ref, kvn_ref, wk_ref, wv_ref,
                   cq_ref, sq_ref, ck_ref,
                   q_out, k_out, v_out, ret_out, z_out, xbc_out, dt_out):
    x = x_ref[0]
    h = _rms(x, n1_ref[...]) * (1.0 + mod_ref[0, 1:2, :]) + mod_ref[0, 0:1, :]
    hb = h.astype(BF16)

    o = 0
    pm = _dot(hb, win_ref[:, o:o + W_MLA]); o += W_MLA
    ret_out[0] = _dot(hb, win_ref[:, o:o + W_RET]).astype(BF16); o += W_RET
    z_out[0] = _dot(hb, win_ref[:, o:o + W_Z]).astype(BF16); o += W_Z
    xbc_out[0] = _dot(hb, win_ref[:, o:o + W_XBC]).astype(BF16); o += W_XBC
    dt_out[0] = _dot(hb, win_ref[:, o:o + W_DT])

    cqb = _rms(pm[:, 0:MLA_Q_LORA], qn_ref[...]).astype(BF16)
    qm = _dot(cqb, wq_ref[...])
    qr = _dot(cqb, wqr_ref[...])
    cosq, sinq = cq_ref[...], sq_ref[...]
    for hh in range(MLA_HEADS):
        sl = slice(hh * MLA_HP, (hh + 1) * MLA_HP)
        q_out[0, :, sl] = (qm[:, sl] * cosq + qr[:, sl] * sinq).astype(BF16)

    ckvn = _rms(pm[:, MLA_Q_LORA:MLA_Q_LORA + MLA_KV_LORA], kvn_ref[...])
    u = pm[:, MLA_Q_LORA + MLA_KV_LORA:W_MLA] * ck_ref[...]
    kin = jnp.concatenate([ckvn, u], axis=-1).astype(BF16)
    k_out[0] = _dot(kin, wk_ref[...]).astype(BF16)
    v_out[0] = _dot(ckvn.astype(BF16), wv_ref[...]).astype(BF16)


def _inproj(xs, mods, lw, tabs, tm):
    b, t, d = xs.shape
    const = lambda shape: pl.BlockSpec(shape, lambda i, j: (0,) * len(shape))
    tok = lambda w: pl.BlockSpec((1, tm, w), lambda i, j: (i, j, 0))
    tab = pl.BlockSpec((tm, LANES), lambda i, j: (j, 0))
    widths = (MLA_HEADS * MLA_HP, MLA_HEADS * MLA_HP, MLA_HEADS * MLA_V, W_RET, W_Z, W_XBC, W_DT)
    dtypes = (BF16, BF16, BF16, BF16, BF16, BF16, F32)
    return pl.pallas_call(
        _inproj_kernel,
        grid=(b, t // tm),
        in_specs=[tok(d), pl.BlockSpec((1, 8, d), lambda i, j: (i, 0, 0)), const((1, d)),
                  const((d, W_ALL)), const((1, MLA_Q_LORA)), const((MLA_Q_LORA, MLA_HEADS * MLA_HP)),
                  const((MLA_Q_LORA, MLA_HEADS * MLA_HP)), const((1, MLA_KV_LORA)),
                  const((2 * MLA_KV_LORA, MLA_HEADS * MLA_HP)), const((MLA_KV_LORA, MLA_HEADS * MLA_V)),
                  tab, tab, tab],
        out_specs=[tok(w) for w in widths],
        out_shape=[jax.ShapeDtypeStruct((b, t, w), dt) for w, dt in zip(widths, dtypes)],
        compiler_params=_cparams(("parallel", "parallel")),
        name="inproj",
    )(xs, mods, lw["n1"], lw["w_all"], lw["qn"], lw["wq"], lw["wqr"], lw["kvn"], lw["wk"], lw["wv"], *tabs)


def _attn_kernel(*refs, seg_blocks, tq):
    nseg = len(seg_blocks)
    q_ref = refs[0]
    kv_refs = refs[1:1 + 2 * nseg]
    o_ref = refs[1 + 2 * nseg]
    grp = _iota((tq, MLA_HEADS * MLA_V), 1) // MLA_V
    out = jnp.zeros((tq, MLA_HEADS * MLA_V), F32)
    for hh in range(MLA_HEADS):
        sl = slice(hh * MLA_HP, (hh + 1) * MLA_HP)
        q = q_ref[0, :, sl]
        m = jnp.full((tq, 1), NEG, F32)
        l = jnp.zeros((tq, 1), F32)
        acc = jnp.zeros((tq, MLA_HEADS * MLA_V), F32)
        for si, (tk, nblk) in enumerate(seg_blocks):
            k_ref, v_ref = kv_refs[2 * si], kv_refs[2 * si + 1]

            def body(j, carry, k_ref=k_ref, v_ref=v_ref, tk=tk):
                m, l, acc = carry
                start = 0 if isinstance(j, int) else pl.multiple_of(j * tk, tk)
                s = _dot_nt(q, k_ref[0, pl.ds(start, tk), sl])
                m_new = jnp.maximum(m, jnp.max(s, axis=-1, keepdims=True))
                alpha = jnp.exp2(m - m_new)
                p = jnp.exp2(s - m_new)
                l = alpha * l + jnp.sum(p, axis=-1, keepdims=True)
                acc = alpha * acc + _dot(p.astype(BF16), v_ref[0, pl.ds(start, tk), :])
                return m_new, l, acc

            if nblk == 1:
                m, l, acc = body(0, (m, l, acc))
            else:
                m, l, acc = lax.fori_loop(0, nblk, body, (m, l, acc))
        out = jnp.where(grp == hh, acc / l, out)
    o_ref[0] = out.astype(BF16)


def _attention(q, segs, tq, tk):
    b, l, _ = q.shape
    seg_blocks = []
    in_specs = [pl.BlockSpec((1, tq, MLA_HEADS * MLA_HP), lambda i, j: (i, j, 0))]
    args = [q]
    for k, v in segs:
        s = k.shape[1]
        t = min(tk, s)
        seg_blocks.append((t, s // t))
        in_specs.append(pl.BlockSpec((1, s, MLA_HEADS * MLA_HP), lambda i, j: (i, 0, 0)))
        in_specs.append(pl.BlockSpec((1, s, MLA_HEADS * MLA_V), lambda i, j: (i, 0, 0)))
        args += [k, v]
    return pl.pallas_call(
        functools.partial(_attn_kernel, seg_blocks=tuple(seg_blocks), tq=tq),
        grid=(b, l // tq),
        in_specs=in_specs,
        out_specs=pl.BlockSpec((1, tq, MLA_HEADS * MLA_V), lambda i, j: (i, j, 0)),
        out_shape=jax.ShapeDtypeStruct((b, l, MLA_HEADS * MLA_V), BF16),
        compiler_params=_cparams(("parallel", "arbitrary")),
        name="attention",
    )(*args)


def _bd_mask(hn, hp, n, p):
    return (_iota((hn, hp), 0) // n) == (_iota((hn, hp), 1) // p)


def _compact(s_bd, heads, n):
    out = s_bd[0:n]
    for hh in range(1, heads):
        out = out + s_bd[hh * n:(hh + 1) * n]
    return out


def _expand_bd(s_c, heads, n, p):
    full = jnp.concatenate([s_c] * heads, axis=0)
    return jnp.where(_bd_mask(heads * n, heads * p, n, p), full, 0.0)


def _ret_kernel(x_ref, lgl_ref, lgh_ref, init_ref, y_ref, fin_ref, s_ref, g_ref, gbuf_ref, *, nc):
    C, H, N, P = CHUNK, RET_HEADS, RET_QK, RET_V
    HP = H * P
    ph, c = pl.program_id(1), pl.program_id(2)

    def log_sigmoid(v):
        return jnp.minimum(v, 0.0) - jnp.log(1.0 + jnp.exp(-jnp.abs(v)))

    lgl = log_sigmoid(lgl_ref[...])
    lgf, lgb = lgl[0:1], lgl[1:2]
    rows = _iota((C, HP), 0).astype(F32)
    mask_bd = _bd_mask(H * N, HP, N, P)
    k = x_ref[0, :, HP:2 * HP].astype(F32)
    v = x_ref[0, :, 2 * HP:3 * HP]

    @pl.when((ph == 0) & (c == 0))
    def _():
        g_ref[...] = _expand_bd(init_ref[0, 1], H, N, P)

    @pl.when(ph == 0)
    def _():
        cc = nc - 1 - c
        g = g_ref[...]
        gbuf_ref[cc] = _compact(g, H, N)
        kw = (k * jnp.exp(rows * lgb)).astype(BF16)
        g_new = g * jnp.exp(C * lgb) + jnp.where(mask_bd, _dot_tn(kw, v), 0.0)
        g_ref[...] = g_new

        @pl.when(c == nc - 1)
        def _():
            fin_ref[0, 1] = _compact(g_new, H, N)

    @pl.when((ph == 1) & (c == 0))
    def _():
        s_ref[...] = _expand_bd(init_ref[0, 0], H, N, P)

    @pl.when(ph == 1)
    def _():
        lgh = log_sigmoid(lgh_ref[...])
        q = x_ref[0, :, 0:HP]
        gate = x_ref[0, :, 3 * HP:4 * HP].astype(F32)
        grp = _iota((C, HP), 1) // P
        di = (_iota((C, C), 0) - _iota((C, C), 1)).astype(F32)
        ws, vs = [], []
        for hh in range(H):
            qh = jnp.where(grp == hh, q, jnp.zeros_like(q))
            sc = _dot_nt(qh, x_ref[0, :, HP:2 * HP])
            arg = jnp.where(di >= 0, di * lgh[hh:hh + 1, :], -di * lgh[H + hh:H + hh + 1, :])
            ws.append((sc * jnp.exp(arg)).astype(BF16))
            vs.append(jnp.where(grp == hh, v, jnp.zeros_like(v)))
        y = _dot(jnp.concatenate(ws, axis=1), jnp.concatenate(vs, axis=0))
        qf32 = q.astype(F32)
        qf = (qf32 * jnp.exp((rows + 1.0) * lgf)).astype(BF16)
        qb = (qf32 * jnp.exp((C - rows) * lgb)).astype(BF16)
        s = s_ref[...]
        g_in = _expand_bd(gbuf_ref[c], H, N, P)
        y = y + _dot(jnp.concatenate([qf, qb], axis=1),
                     jnp.concatenate([s, g_in], axis=0).astype(BF16))
        kw = (k * jnp.exp((C - 1.0 - rows) * lgf)).astype(BF16)
        s_new = s * jnp.exp(C * lgf) + jnp.where(mask_bd, _dot_tn(kw, v), 0.0)
        s_ref[...] = s_new

        @pl.when(c == nc - 1)
        def _():
            fin_ref[0, 0] = _compact(s_new, H, N)

        mu = jnp.zeros_like(y)
        for hh in range(H):
            mh = jnp.sum(jnp.where(grp == hh, y, 0.0), axis=-1, keepdims=True) * (1.0 / P)
            mu = jnp.where(grp == hh, mh, mu)
        yc = y - mu
        var = jnp.zeros_like(y)
        for hh in range(H):
            vh = jnp.sum(jnp.where(grp == hh, yc * yc, 0.0), axis=-1, keepdims=True) * (1.0 / P)
            var = jnp.where(grp == hh, vh, var)
        y_ref[0] = (yc * lax.rsqrt(var + EPS) * _silu(gate)).astype(BF16)


def _chunk_index(nc):
    return lambda i, ph, c: (i, (1 - ph) * (nc - 1 - c) + ph * c, 0)


def _ret_scan(ret, lw, init):
    b, t, _ = ret.shape
    nc = t // CHUNK
    hp = RET_HEADS * RET_V
    return pl.pallas_call(
        functools.partial(_ret_kernel, nc=nc),
        grid=(b, 2, nc),
        in_specs=[pl.BlockSpec((1, CHUNK, W_RET), _chunk_index(nc)),
                  pl.BlockSpec((2, hp), lambda i, ph, c: (0, 0)),
                  pl.BlockSpec((2 * RET_HEADS, CHUNK), lambda i, ph, c: (0, 0)),
                  pl.BlockSpec((1, 2, RET_QK, hp), lambda i, ph, c: (i, 0, 0, 0))],
        out_specs=[pl.BlockSpec((1, CHUNK, hp), lambda i, ph, c: (i, ph * c, 0)),
                   pl.BlockSpec((1, 2, RET_QK, hp), lambda i, ph, c: (i, 0, 0, 0))],
        out_shape=[jax.ShapeDtypeStruct((b, t, hp), BF16),
                   jax.ShapeDtypeStruct((b, 2, RET_QK, hp), F32)],
        scratch_shapes=[pltpu.VMEM((RET_HEADS * RET_QK, hp), F32),
                        pltpu.VMEM((RET_HEADS * RET_QK, hp), F32),
                        pltpu.VMEM((nc, RET_QK, hp), F32)],
        compiler_params=_cparams(("parallel", "arbitrary", "arbitrary")),
        name="retention_scan",
    )(ret, lw["ret_lgl"], lw["ret_lgh"], init)


def _ssd_kernel(x_ref, xp_ref, xn_ref, z_ref, dt_ref, cw_ref, cb_ref, dtb_ref, alog_ref, dvec_ref, nw_ref,
                init_ref, y_ref, fin_ref, s_ref, g_ref, gbuf_ref, *, nc):
    C, H, N, P, G = CHUNK, SSD_HEADS, SSD_STATE, SSD_HEADDIM, SSD_GROUPS
    HP = H * P
    HPG = H // G
    HALO = 16
    ph, c = pl.program_id(1), pl.program_id(2)
    cc = jnp.where(ph == 0, nc - 1 - c, c)

    xc = x_ref[0]
    prev = jnp.where(cc > 0, xp_ref[0], jnp.zeros_like(xp_ref[0]))
    nxt = jnp.where(cc < nc - 1, xn_ref[0], jnp.zeros_like(xn_ref[0]))
    stacked = jnp.concatenate([prev, xc, nxt], axis=0)
    ri = _iota((C, C + 2 * HALO), 0)
    ci = _iota((C, C + 2 * HALO), 1)
    sel_prev = (ci == ri + HALO - 1).astype(BF16)
    sel_next = (ci == ri + HALO + 1).astype(BF16)
    cw = cw_ref[...]
    conv = (cw[0:1] * _dot(sel_prev, stacked) + cw[1:2] * xc.astype(F32)
            + cw[2:3] * _dot(sel_next, stacked) + cb_ref[...])
    act = _silu(conv)
    xs = act[:, 0:HP]
    bm = act[:, HP:HP + G * N].astype(BF16)
    cm = act[:, HP + G * N:HP + 2 * G * N].astype(BF16)

    raw = dt_ref[0] + dtb_ref[...]
    dt = jnp.maximum(raw, 0.0) + jnp.log(1.0 + jnp.exp(-jnp.abs(raw)))
    la = dt * (-jnp.exp(alog_ref[...]))

    lane = _iota((LANES, HP), 0)
    head_of = _iota((LANES, HP), 1) // P
    e_f = (lane == head_of).astype(BF16)
    e_b = (lane == head_of + H).astype(BF16)
    grp_row = _iota((LANES, HP), 0) // N
    rep = ((grp_row == head_of // HPG) & (_iota((LANES, HP), 0) % N == _iota((LANES, HP), 1) % P)).astype(BF16)
    r_i = _iota((C, C), 0)
    c_i = _iota((C, C), 1)
    low = (r_i >= c_i).astype(BF16)
    upp = (r_i <= c_i).astype(BF16)
    mask_bd = _bd_mask(H * N, HP, N, P)
    bm_rep = _dot(bm, rep).astype(BF16)

    a_f = _dot_exact_lhs01(low, la)
    a_b = _dot_exact_lhs01(upp, la)

    @pl.when((ph == 0) & (c == 0))
    def _():
        g_ref[...] = _expand_bd(init_ref[0, 1], H, N, P)

    @pl.when(ph == 0)
    def _():
        g = g_ref[...]
        gbuf_ref[cc] = _compact(g, H, N)
        wdt = jnp.exp(a_b[0:1, :] - a_b) * dt
        xw = (xs * _dot_exact_rhs01(wdt, e_b)).astype(BF16)
        tot = _dot_exact_rhs01(jnp.exp(a_b[0:8, :]), e_b)[0:1]
        g_new = g * tot + jnp.where(mask_bd, _dot_tn(bm_rep, xw), 0.0)
        g_ref[...] = g_new

        @pl.when(c == nc - 1)
        def _():
            fin_ref[0, 1] = _compact(g_new, H, N)

    @pl.when((ph == 1) & (c == 0))
    def _():
        s_ref[...] = _expand_bd(init_ref[0, 0], H, N, P)

    @pl.when(ph == 1)
    def _():
        la_t = la.T[0:2 * H]
        dt_t = dt.T[0:2 * H]
        af_row = _dot_exact_rhs01(la_t, upp)
        ab_row = _dot_exact_rhs01(la_t, low)
        colsel = (_iota((LANES, H * C), 0) == _iota((LANES, H * C), 1) // C)
        af_col = _dot_exact_rhs01(a_f, colsel.astype(BF16))
        ab_col = _dot_exact_rhs01(a_b, (_iota((LANES, H * C), 0) == _iota((LANES, H * C), 1) // C + H).astype(BF16))
        grp_l = _iota((C, G * N), 1) // N
        head_l = _iota((C, HP), 1) // P
        tri_f = r_i >= c_i
        tri_b = r_i <= c_i
        sc = [_dot_nt(jnp.where(grp_l == gg, cm, jnp.zeros_like(cm)), bm) for gg in range(G)]
        xsb = xs.astype(BF16)
        ws, vs = [], []
        for hh in range(H):
            seg_f = af_col[:, hh * C:(hh + 1) * C] - af_row[hh:hh + 1, :]
            seg_b = ab_col[:, hh * C:(hh + 1) * C] - ab_row[H + hh:H + hh + 1, :]
            dm = (jnp.exp(jnp.where(tri_f, seg_f, NEG)) * dt_t[hh:hh + 1, :]
                  + jnp.exp(jnp.where(tri_b, seg_b, NEG)) * dt_t[H + hh:H + hh + 1, :])
            ws.append((sc[hh // HPG] * dm).astype(BF16))
            vs.append(jnp.where(head_l == hh, xsb, jnp.zeros_like(xsb)))
        y = _dot(jnp.concatenate(ws, axis=1), jnp.concatenate(vs, axis=0))

        cm_rep = _dot(cm, rep)
        ef = _dot_exact_rhs01(jnp.exp(a_f), e_f)
        eb = _dot_exact_rhs01(jnp.exp(a_b), e_b)
        s = s_ref[...]
        g_in = _expand_bd(gbuf_ref[cc], H, N, P)
        y = y + _dot(jnp.concatenate([(cm_rep * ef).astype(BF16), (cm_rep * eb).astype(BF16)], axis=1),
                     jnp.concatenate([s, g_in], axis=0).astype(BF16))

        wdt = jnp.exp(a_f[C - 1:C, :] - a_f) * dt
        xw = (xs * _dot_exact_rhs01(wdt, e_f)).astype(BF16)
        tot = _dot_exact_rhs01(jnp.exp(a_f[C - 8:C, :]), e_f)[7:8]
        s_new = s * tot + jnp.where(mask_bd, _dot_tn(bm_rep, xw), 0.0)
        s_ref[...] = s_new

        @pl.when(c == nc - 1)
        def _():
            fin_ref[0, 0] = _compact(s_new, H, N)

        y = y + xs * dvec_ref[...]
        y = y * _silu(z_ref[0].astype(F32))
        y_ref[0] = _rms(y, nw_ref[...]).astype(BF16)


def _ssd_scan(xbc, z, dt, lw, init):
    b, t, _ = xbc.shape
    nc = t // CHUNK
    hp = SSD_INNER
    halo = 16
    per = CHUNK // halo
    nblk = t // halo
    cidx = _chunk_index(nc)

    def prev_idx(i, ph, c):
        cc = (1 - ph) * (nc - 1 - c) + ph * c
        return (i, jnp.maximum(cc * per - 1, 0), 0)

    def next_idx(i, ph, c):
        cc = (1 - ph) * (nc - 1 - c) + ph * c
        return (i, jnp.minimum((cc + 1) * per, nblk - 1), 0)

    const = lambda shape: pl.BlockSpec(shape, lambda i, ph, c: (0,) * len(shape))
    return pl.pallas_call(
        functools.partial(_ssd_kernel, nc=nc),
        grid=(b, 2, nc),
        in_specs=[pl.BlockSpec((1, CHUNK, W_XBC), cidx),
                  pl.BlockSpec((1, halo, W_XBC), prev_idx),
                  pl.BlockSpec((1, halo, W_XBC), next_idx),
                  pl.BlockSpec((1, CHUNK, W_Z), cidx),
                  pl.BlockSpec((1, CHUNK, W_DT), cidx),
                  const((8, W_XBC)), const((1, W_XBC)), const((1, LANES)), const((1, LANES)),
                  const((1, hp)), const((1, hp)),
                  pl.BlockSpec((1, 2, SSD_STATE, hp), lambda i, ph, c: (i, 0, 0, 0))],
        out_specs=[pl.BlockSpec((1, CHUNK, hp), lambda i, ph, c: (i, ph * c, 0)),
                   pl.BlockSpec((1, 2, SSD_STATE, hp), lambda i, ph, c: (i, 0, 0, 0))],
        out_shape=[jax.ShapeDtypeStruct((b, t, hp), BF16),
                   jax.ShapeDtypeStruct((b, 2, SSD_STATE, hp), F32)],
        scratch_shapes=[pltpu.VMEM((SSD_HEADS * SSD_STATE, hp), F32),
                        pltpu.VMEM((SSD_HEADS * SSD_STATE, hp), F32),
                        pltpu.VMEM((nc, SSD_STATE, hp), F32)],
        compiler_params=_cparams(("parallel", "arbitrary", "arbitrary")),
        name="ssd_scan",
    )(xbc, xbc, xbc, z, dt, lw["conv_w"], lw["conv_b"], lw["dt_bias"], lw["a_log"], lw["d_vec"], lw["ssd_nw"], init)


def _mix_residual(x_ref, att_ref, ret_ref, ssd_ref, mod_ref, n2_ref, wo_ref):
    mix = jnp.concatenate([att_ref[0], ret_ref[0], ssd_ref[0]], axis=-1)
    x1 = x_ref[0] + mod_ref[0, 2:3, :] * _dot(mix, wo_ref[...])
    h = _rms(x1, n2_ref[...]) * (1.0 + mod_ref[0, 4:5, :]) + mod_ref[0, 3:4, :]
    return x1, h


def _dense_kernel(x_ref, att_ref, ret_ref, ssd_ref, mod_ref, n2_ref, wo_ref, wg_ref, wu_ref, wd_ref, o_ref,
                  *, ff_chunk):
    x1, h = _mix_residual(x_ref, att_ref, ret_ref, ssd_ref, mod_ref, n2_ref, wo_ref)
    hb = h.astype(BF16)
    ff = wg_ref.shape[1]
    f = jnp.zeros_like(x1)
    for s in range(0, ff, ff_chunk):
        g = _dot(hb, wg_ref[:, s:s + ff_chunk])
        u = _dot(hb, wu_ref[:, s:s + ff_chunk])
        f = f + _dot((_silu(g) * u).astype(BF16), wd_ref[s:s + ff_chunk, :])
    o_ref[0] = x1 + mod_ref[0, 5:6, :] * f


def _dense_block(xs, att, ret, ssd, mods, lw, tm):
    b, t, d = xs.shape
    ff = lw["wg"].shape[1]
    tok = lambda w: pl.BlockSpec((1, tm, w), lambda i, j: (i, j, 0))
    const = lambda shape: pl.BlockSpec(shape, lambda i, j: (0,) * len(shape), pipeline_mode=pl.Buffered(1))
    return pl.pallas_call(
        functools.partial(_dense_kernel, ff_chunk=ff // 2),
        grid=(b, t // tm),
        in_specs=[tok(d), tok(att.shape[-1]), tok(ret.shape[-1]), tok(ssd.shape[-1]),
                  pl.BlockSpec((1, 8, d), lambda i, j: (i, 0, 0)), const((1, d)),
                  const((d, d)), const((d, ff)), const((d, ff)), const((ff, d))],
        out_specs=tok(d),
        out_shape=jax.ShapeDtypeStruct((b, t, d), F32),
        compiler_params=_cparams(("parallel", "parallel")),
        name="outproj_ffn",
    )(xs, att, ret, ssd, mods, lw["n2"], lw["wo"], lw["wg"], lw["wu"], lw["wd"])


def _route_kernel(x_ref, att_ref, ret_ref, ssd_ref, mod_ref, n2_ref, wo_ref, wr_ref, x1_ref, h_ref, gate_ref):
    x1, h = _mix_residual(x_ref, att_ref, ret_ref, ssd_ref, mod_ref, n2_ref, wo_ref)
    x1_ref[0] = x1
    h_ref[0] = h.astype(BF16)
    logits = jnp.dot(h, wr_ref[...], precision=HIGHEST, preferred_element_type=F32)
    lane = _iota(logits.shape, 1).astype(F32)
    logits = jnp.where(lane < N_EXPERTS, logits, NEG)
    m1 = jnp.max(logits, axis=-1, keepdims=True)
    i1 = jnp.min(jnp.where(logits == m1, lane, float(LANES)), axis=-1, keepdims=True)
    rest = jnp.where(lane == i1, NEG, logits)
    m2 = jnp.max(rest, axis=-1, keepdims=True)
    i2 = jnp.min(jnp.where(rest == m2, lane, float(LANES)), axis=-1, keepdims=True)
    e2 = jnp.exp(m2 - m1)
    den = 1.0 + e2
    gate_ref[0] = jnp.where(lane == i1, 1.0 / den, 0.0) + jnp.where(lane == i2, e2 / den, 0.0)


def _route_block(xs, att, ret, ssd, mods, lw, tm):
    b, t, d = xs.shape
    tok = lambda w: pl.BlockSpec((1, tm, w), lambda i, j: (i, j, 0))
    const = lambda shape: pl.BlockSpec(shape, lambda i, j: (0,) * len(shape))
    return pl.pallas_call(
        _route_kernel,
        grid=(b, t // tm),
        in_specs=[tok(d), tok(att.shape[-1]), tok(ret.shape[-1]), tok(ssd.shape[-1]),
                  pl.BlockSpec((1, 8, d), lambda i, j: (i, 0, 0)), const((1, d)),
                  const((d, d)), const((d, LANES))],
        out_specs=[tok(d), tok(d), tok(LANES)],
        out_shape=[jax.ShapeDtypeStruct((b, t, d), F32), jax.ShapeDtypeStruct((b, t, d), BF16),
                   jax.ShapeDtypeStruct((b, t, LANES), F32)],
        compiler_params=_cparams(("parallel", "parallel")),
        name="outproj_route",
    )(xs, att, ret, ssd, mods, lw["n2"], lw["wo"], lw["router"])


def _moe_kernel(x1_ref, h_ref, gate_ref, mod_ref, wg_ref, wu_ref, wd_ref, fn_ref, o_ref, acc_ref, *, final_norm):
    e = pl.program_id(2)

    @pl.when(e == 0)
    def _():
        acc_ref[...] = jnp.zeros_like(acc_ref)

    hb = h_ref[0]
    g = _dot(hb, wg_ref[0])
    u = _dot(hb, wu_ref[0])
    f = _dot((_silu(g) * u).astype(BF16), wd_ref[0])
    gates = gate_ref[0]
    ge = jnp.sum(jnp.where(_iota(gates.shape, 1) == e, gates, 0.0), axis=-1, keepdims=True)
    acc_ref[...] += ge * f

    @pl.when(e == N_EXPERTS - 1)
    def _():
        out = x1_ref[0] + mod_ref[0, 5:6, :] * acc_ref[...]
        o_ref[0] = _rms(out, fn_ref[...]) if final_norm else out


def _moe_block(x1, h, gates, mods, lw, fnw, final_norm, tm):
    b, t, d = x1.shape
    ff = lw["moe_wg"].shape[2]
    tok = lambda w: pl.BlockSpec((1, tm, w), lambda i, j, e: (i, j, 0))
    return pl.pallas_call(
        functools.partial(_moe_kernel, final_norm=final_norm),
        grid=(b, t // tm, N_EXPERTS),
        in_specs=[tok(d), tok(d), tok(LANES), pl.BlockSpec((1, 8, d), lambda i, j, e: (i, 0, 0)),
                  pl.BlockSpec((1, d, ff), lambda i, j, e: (e, 0, 0)),
                  pl.BlockSpec((1, d, ff), lambda i, j, e: (e, 0, 0)),
                  pl.BlockSpec((1, ff, d), lambda i, j, e: (e, 0, 0)),
                  pl.BlockSpec((1, d), lambda i, j, e: (0, 0))],
        out_specs=tok(d),
        out_shape=jax.ShapeDtypeStruct((b, t, d), F32),
        scratch_shapes=[pltpu.VMEM((tm, d), F32)],
        compiler_params=_cparams(("parallel", "parallel", "arbitrary")),
        name="moe",
    )(x1, h, gates, mods, lw["moe_wg"], lw["moe_wu"], lw["moe_wd"], fnw)


def _final_norm_kernel(x_ref, w_ref, o_ref):
    o_ref[0] = _rms(x_ref[0], w_ref[...])


def _final_norm(xs, fnw, tm):
    b, t, d = xs.shape
    return pl.pallas_call(
        _final_norm_kernel,
        grid=(b, t // tm),
        in_specs=[pl.BlockSpec((1, tm, d), lambda i, j: (i, j, 0)), pl.BlockSpec((1, d), lambda i, j: (0, 0))],
        out_specs=pl.BlockSpec((1, tm, d), lambda i, j: (i, j, 0)),
        out_shape=jax.ShapeDtypeStruct((b, t, d), F32),
        compiler_params=_cparams(("parallel", "parallel")),
        name="final_norm",
    )(xs, fnw)


def _rot_cols(w):
    a, b2, c, d = jnp.split(w, 4, axis=-1)
    return jnp.concatenate([-b2, a, -d, c], axis=-1)


def _layer_weights(i, norm1_w, norm2_w, w_in, w_out, mla_q_norm_w, mla_w_qb, mla_kv_norm_w, mla_w_kb, mla_w_vb,
                   ret_decay_logit, ssd_conv_w, ssd_conv_b, ssd_dt_bias, ssd_a_log, ssd_d, ssd_norm_w):
    d = w_in.shape[1]
    offs = np.cumsum((0,) + IN_SIZES)
    seg = [w_in[i][:, offs[j]:offs[j + 1]] for j in range(len(IN_SIZES))]
    zeros = lambda n: jnp.zeros((d, n), F32)
    w_all = jnp.concatenate(
        [seg[0], seg[1], seg[2], _rot_cols(seg[2]), zeros(W_MLA - MLA_Q_LORA - MLA_KV_LORA - 2 * MLA_ROPE),
         seg[3] * (RET_QK ** -0.5), seg[4], seg[5], seg[6], seg[7], seg[8], seg[9], seg[10],
         zeros(W_DT - 2 * SSD_HEADS)], axis=1).astype(BF16)

    dh = MLA_NOPE + MLA_ROPE
    wqb = mla_w_qb[i].reshape(MLA_Q_LORA, MLA_HEADS, dh)
    pad = lambda a, lo, hi: jnp.pad(a, ((0, 0), (0, 0), (lo, hi)))
    wq = pad(wqb, 0, MLA_HP - dh).reshape(MLA_Q_LORA, MLA_HEADS * MLA_HP).astype(BF16)
    wqr = pad(_rot_cols(wqb[:, :, MLA_NOPE:]), MLA_NOPE, MLA_HP - dh).reshape(MLA_Q_LORA, MLA_HEADS * MLA_HP).astype(BF16)
    wkb = pad(mla_w_kb[i].reshape(MLA_KV_LORA, MLA_HEADS, MLA_NOPE), 0, MLA_HP - MLA_NOPE)
    place = np.zeros((MLA_ROPE, MLA_HEADS, MLA_HP), np.float32)
    for r in range(MLA_ROPE):
        place[r, :, MLA_NOPE + r] = 1.0
    place = jnp.asarray(place)
    wk = jnp.concatenate([wkb, place, place, jnp.zeros((MLA_KV_LORA - 2 * MLA_ROPE, MLA_HEADS, MLA_HP), F32)],
                         axis=0).reshape(2 * MLA_KV_LORA, MLA_HEADS * MLA_HP).astype(BF16)

    lg = ret_decay_logit[i]
    padl = lambda v: jnp.pad(v.reshape(1, -1), ((0, 0), (0, LANES - 2 * SSD_HEADS)))
    return {
        "n1": norm1_w[i].reshape(1, d), "n2": norm2_w[i].reshape(1, d),
        "w_all": w_all, "qn": mla_q_norm_w[i].reshape(1, -1), "wq": wq, "wqr": wqr,
        "kvn": mla_kv_norm_w[i].reshape(1, -1), "wk": wk, "wv": mla_w_vb[i].astype(BF16),
        "wo": w_out[i].astype(BF16),
        "ret_lgl": jnp.repeat(lg, RET_V, axis=1),
        "ret_lgh": jnp.broadcast_to(lg.reshape(2 * RET_HEADS, 1), (2 * RET_HEADS, CHUNK)),
        "conv_w": jnp.pad(ssd_conv_w[i], ((0, 8 - ssd_conv_w.shape[1]), (0, 0))),
        "conv_b": ssd_conv_b[i].reshape(1, -1),
        "dt_bias": padl(ssd_dt_bias[i]), "a_log": padl(ssd_a_log[i]),
        "d_vec": jnp.repeat(ssd_d[i], SSD_HEADDIM).reshape(1, -1),
        "ssd_nw": ssd_norm_w[i].reshape(1, -1),
    }


def _rope_tables(n_lat, n_ctx):
    rows = n_lat // GRID_W
    row = jnp.repeat(jnp.arange(rows, dtype=F32), GRID_W)
    col = jnp.tile(jnp.arange(GRID_W, dtype=F32), rows)
    inv_freq = ROPE_BASE ** (-jnp.arange(0, ROPE_AXIS_DIM, 2, dtype=F32) / ROPE_AXIS_DIM)
    ang_r = row[:, None] * inv_freq
    ang_c = col[:, None] * inv_freq
    ang = jnp.concatenate([ang_r, ang_r, ang_c, ang_c], axis=-1)
    c2 = MLA_SCALE * math.log2(math.e)

    def build(cos, sin):
        n = cos.shape[0]
        cosq = jnp.concatenate([jnp.full((n, MLA_NOPE), c2, F32), c2 * cos, jnp.zeros((n, MLA_HP - MLA_NOPE - MLA_ROPE), F32)], 1)
        sinq = jnp.concatenate([jnp.zeros((n, MLA_NOPE), F32), c2 * sin, jnp.zeros((n, MLA_HP - MLA_NOPE - MLA_ROPE), F32)], 1)
        ck = jnp.concatenate([cos, sin, jnp.zeros((n, LANES - 2 * MLA_ROPE), F32)], 1)
        return cosq, sinq, ck

    lat = build(jnp.cos(ang), jnp.sin(ang))
    ctx = build(jnp.ones((n_ctx, MLA_ROPE), F32), jnp.zeros((n_ctx, MLA_ROPE), F32))
    return lat, ctx


def _token_tile(t, pref):
    tm = min(pref, t)
    while t % tm:
        tm //= 2
    return tm


def kernel(x, c, ctx, c_ctx, mod_w, mod_b, norm1_w, norm2_w, w_in, w_out, mla_q_norm_w, mla_w_qb, mla_kv_norm_w, mla_w_kb, mla_w_vb, ret_decay_logit, ssd_conv_w, ssd_conv_b, ssd_dt_bias, ssd_a_log, ssd_d, ssd_norm_w, ffn_w_gate, ffn_w_up, ffn_w_down, moe_router, moe_w_gate, moe_w_up, moe_w_down, final_norm_w):
    b, n_lat, d = x.shape
    n_ctx = ctx.shape[1]
    depth = mod_w.shape[0]
    assert b + 1 <= 8 and n_lat % CHUNK == 0 and n_ctx % CHUNK == 0 and n_lat % GRID_W == 0

    cond = jnp.concatenate([c, c_ctx[None, :], jnp.zeros((8 - b - 1, d), F32)], axis=0)
    mod_all = _modulation(cond, mod_w, mod_b)
    tabs_x, tabs_c = _rope_tables(n_lat, n_ctx)
    fnw = final_norm_w.reshape(1, d)

    tm_x = _token_tile(n_lat, 512)
    tm_c = _token_tile(n_ctx, 256)
    zero_ret = jnp.zeros((b, 2, RET_QK, RET_HEADS * RET_V), F32)
    zero_ssd = jnp.zeros((b, 2, SSD_STATE, SSD_INNER), F32)

    h_ctx = ctx
    for i in range(depth):
        last = i == depth - 1
        lw = _layer_weights(i, norm1_w, norm2_w, w_in, w_out, mla_q_norm_w, mla_w_qb, mla_kv_norm_w, mla_w_kb,
                            mla_w_vb, ret_decay_logit, ssd_conv_w, ssd_conv_b, ssd_dt_bias, ssd_a_log, ssd_d,
                            ssd_norm_w)
        m6 = mod_all[i].reshape(8, 6, d)
        pad2 = jnp.zeros((b, 2, d), F32)
        mods_x = jnp.concatenate([m6[:b], pad2], axis=1)
        mods_c = jnp.concatenate([jnp.broadcast_to(m6[b:b + 1], (b, 6, d)), pad2], axis=1)

        qc, kc, vc, retc, zc, xbcc, dtc = _inproj(h_ctx, mods_c, lw, tabs_c, tm_c)
        ret_yc, ret_fin = _ret_scan(retc, lw, zero_ret)
        ssd_yc, ssd_fin = _ssd_scan(xbcc, zc, dtc, lw, zero_ssd)

        qx, kx, vx, retx, zx, xbcx, dtx = _inproj(x, mods_x, lw, tabs_x, tm_x)
        att_x = _attention(qx, [(kc, vc), (kx, vx)], _token_tile(n_lat, 512), 1024)
        ret_yx, _ = _ret_scan(retx, lw, ret_fin)
        ssd_yx, _ = _ssd_scan(xbcx, zx, dtx, lw, ssd_fin)

        if i % 2 == 0:
            j = i // 2
            lw.update(wg=ffn_w_gate[j].astype(BF16), wu=ffn_w_up[j].astype(BF16), wd=ffn_w_down[j].astype(BF16))
            x = _dense_block(x, att_x, ret_yx, ssd_yx, mods_x, lw, tm_x)
            if last:
                x = _final_norm(x, fnw, tm_x)
        else:
            j = i // 2
            lw.update(router=jnp.pad(moe_router[j], ((0, 0), (0, LANES - N_EXPERTS))),
                      moe_wg=moe_w_gate[j].astype(BF16), moe_wu=moe_w_up[j].astype(BF16),
                      moe_wd=moe_w_down[j].astype(BF16))
            x1, hb, gates = _route_block(x, att_x, ret_yx, ssd_yx, mods_x, lw, tm_x)
            x = _moe_block(x1, hb, gates, mods_x, lw, fnw, last, tm_x)

        if not last:
            att_c = _attention(qc, [(kc, vc)], tm_c, 1024)
            if i % 2 == 0:
                h_ctx = _dense_block(h_ctx, att_c, ret_yc, ssd_yc, mods_c, lw, tm_c)
            else:
                c1, chb, cg = _route_block(h_ctx, att_c, ret_yc, ssd_yc, mods_c, lw, tm_c)
                h_ctx = _moe_block(c1, chb, cg, mods_c, lw, fnw, False, tm_c)
    return x
```

```python
import functools
import math

import jax
import jax.numpy as jnp
import numpy as np
from jax import lax
from jax.experimental import pallas as pl
from jax.experimental.pallas import tpu as pltpu

F32 = jnp.float32
BF16 = jnp.bfloat16
HIGHEST = lax.Precision.HIGHEST

EPS = 1e-6
CHUNK = 128
GRID_W = 64
MLA_HEADS = 4
MLA_Q_LORA = 256
MLA_KV_LORA = 128
MLA_NOPE = 64
MLA_ROPE = 32
MLA_V = 64
MLA_SCALE = (MLA_NOPE + MLA_ROPE) ** -0.5
ROPE_AXIS_DIM = MLA_ROPE // 2
ROPE_BASE = 10000.0
RET_HEADS = 4
RET_QK = 64
RET_V = 64
SSD_HEADS = 8
SSD_HEADDIM = 64
SSD_GROUPS = 2
SSD_STATE = 64
SSD_INNER = SSD_HEADS * SSD_HEADDIM
SSD_CONV_CH = SSD_INNER + 2 * SSD_GROUPS * SSD_STATE
N_EXPERTS = 8
IN_SIZES = (MLA_Q_LORA, MLA_KV_LORA, MLA_ROPE,
            RET_HEADS * RET_QK, RET_HEADS * RET_QK, RET_HEADS * RET_V, RET_HEADS * RET_V,
            SSD_INNER, SSD_CONV_CH, SSD_HEADS, SSD_HEADS)

LANES = 128
V7X_VMEM_LIMIT = 56 * 1024 * 1024

MLA_HP = 128
NEG = -1e30


def _cparams(sem):
    return pltpu.CompilerParams(dimension_semantics=sem, vmem_limit_bytes=V7X_VMEM_LIMIT)


def _dot(a, b):
    return jnp.dot(a, b, preferred_element_type=F32)


def _dot_nt(a, b):
    return lax.dot_general(a, b, (((1,), (1,)), ((), ())), preferred_element_type=F32)


def _dot_tn(a, b):
    return lax.dot_general(a, b, (((0,), (0,)), ((), ())), preferred_element_type=F32)


def _split3(x):
    hi = x.astype(BF16)
    r1 = x - hi.astype(F32)
    mid = r1.astype(BF16)
    lo = (r1 - mid.astype(F32)).astype(BF16)
    return hi, mid, lo


def _dot_exact_rhs01(x, e):
    hi, mid, lo = _split3(x)
    return _dot(hi, e) + _dot(mid, e) + _dot(lo, e)


def _dot_exact_lhs01(e, x):
    hi, mid, lo = _split3(x)
    return _dot(e, hi) + _dot(e, mid) + _dot(e, lo)


def _silu(x):
    return x / (1.0 + jnp.exp(-x))


def _rms(x, w):
    return x * lax.rsqrt(jnp.mean(x * x, axis=-1, keepdims=True) + EPS) * w


def _iota(shape, dim):
    return lax.broadcasted_iota(jnp.int32, shape, dim)


def _mod_kernel(a_ref, w_ref, b_ref, o_ref):
    a = _silu(a_ref[...])
    o_ref[0] = jnp.dot(a, w_ref[0], precision=HIGHEST, preferred_element_type=F32) + b_ref[0]


def _modulation(cond, mod_w, mod_b):
    depth, d, n = mod_w.shape
    tn = 1536
    return pl.pallas_call(
        _mod_kernel,
        grid=(depth, n // tn),
        in_specs=[pl.BlockSpec((8, d), lambda i, j: (0, 0)),
                  pl.BlockSpec((1, d, tn), lambda i, j: (i, 0, j)),
                  pl.BlockSpec((1, 1, tn), lambda i, j: (i, 0, j))],
        out_specs=pl.BlockSpec((1, 8, tn), lambda i, j: (i, 0, j)),
        out_shape=jax.ShapeDtypeStruct((depth, 8, n), F32),
        compiler_params=_cparams(("parallel", "parallel")),
        name="modulation",
    )(cond, mod_w, mod_b.reshape(depth, 1, n))


W_MLA, W_RET, W_Z, W_XBC, W_DT = 512, 1024, 512, 768, 128
W_ALL = W_MLA + W_RET + W_Z + W_XBC + W_DT


def _ones_lane_mask(shape):
    lane = _iota(shape, len(shape) - 1) % (2 * MLA_HP)
    return (lane == MLA_HP - 1) | (lane == MLA_HP)


def _inproj_kernel(x_ref, mod_ref, n1_ref, win_ref, qn_ref, wq_ref, wqr_ref, kvn_ref, wk_ref, wv_ref,
                   cq_ref, sq_ref, ck_ref,
                   q_out, k_out, v_out, ret_out, z_out, xbc_out, dt_out):
    x = x_ref[0]
    h = _rms(x, n1_ref[...]) * (1.0 + mod_ref[0, 1:2, :]) + mod_ref[0, 0:1, :]
    hb = h.astype(BF16)

    o = 0
    pm = _dot(hb, win_ref[:, o:o + W_MLA]); o += W_MLA
    ret_out[0] = _dot(hb, win_ref[:, o:o + W_RET]).astype(BF16); o += W_RET
    z_out[0] = _dot(hb, win_ref[:, o:o + W_Z]).astype(BF16); o += W_Z
    xbc_out[0] = _dot(hb, win_ref[:, o:o + W_XBC]).astype(BF16); o += W_XBC
    dt_out[0] = _dot(hb, win_ref[:, o:o + W_DT])

    cqb = _rms(pm[:, 0:MLA_Q_LORA], qn_ref[...]).astype(BF16)
    qm = _dot(cqb, wq_ref[...])
    qr = _dot(cqb, wqr_ref[...])
    cosq, sinq = cq_ref[...], sq_ref[...]
    for hh in range(MLA_HEADS):
        sl = slice(hh * MLA_HP, (hh + 1) * MLA_HP)
        q_out[0, :, sl] = (qm[:, sl] * cosq + qr[:, sl] * sinq).astype(BF16)

    ckvn = _rms(pm[:, MLA_Q_LORA:MLA_Q_LORA + MLA_KV_LORA], kvn_ref[...])
    u = pm[:, MLA_Q_LORA + MLA_KV_LORA:W_MLA] * ck_ref[...]
    kin = jnp.concatenate([ckvn, u], axis=-1).astype(BF16)
    k_out[0] = _dot(kin, wk_ref[...]).astype(BF16)
    v = _dot(ckvn.astype(BF16), wv_ref[...])
    v_out[0] = jnp.where(_ones_lane_mask(v.shape), 1.0, v).astype(BF16)


def _inproj(xs, mods, lw, tabs, tm):
    b, t, d = xs.shape
    const = lambda shape: pl.BlockSpec(shape, lambda i, j: (0,) * len(shape))
    tok = lambda w: pl.BlockSpec((1, tm, w), lambda i, j: (i, j, 0))
    tab = pl.BlockSpec((tm, LANES), lambda i, j: (j, 0))
    widths = (MLA_HEADS * MLA_HP, MLA_HEADS * MLA_HP, MLA_HEADS * MLA_HP, W_RET, W_Z, W_XBC, W_DT)
    dtypes = (BF16, BF16, BF16, BF16, BF16, BF16, F32)
    return pl.pallas_call(
        _inproj_kernel,
        grid=(b, t // tm),
        in_specs=[tok(d), pl.BlockSpec((1, 8, d), lambda i, j: (i, 0, 0)), const((1, d)),
                  const((d, W_ALL)), const((1, MLA_Q_LORA)), const((MLA_Q_LORA, MLA_HEADS * MLA_HP)),
                  const((MLA_Q_LORA, MLA_HEADS * MLA_HP)), const((1, MLA_KV_LORA)),
                  const((2 * MLA_KV_LORA, MLA_HEADS * MLA_HP)), const((MLA_KV_LORA, MLA_HEADS * MLA_HP)),
                  tab, tab, tab],
        out_specs=[tok(w) for w in widths],
        out_shape=[jax.ShapeDtypeStruct((b, t, w), dt) for w, dt in zip(widths, dtypes)],
        compiler_params=_cparams(("parallel", "parallel")),
        name="inproj",
    )(xs, mods, lw["n1"], lw["w_all"], lw["qn"], lw["wq"], lw["wqr"], lw["kvn"], lw["wk"], lw["wv"], *tabs)


def _attn_kernel(*refs, seg_blocks, tq):
    nseg = len(seg_blocks)
    q_ref = refs[0]
    kv_refs = refs[1:1 + 2 * nseg]
    o_ref = refs[1 + 2 * nseg]
    carry = tuple((jnp.full((tq, 1), NEG, F32), jnp.zeros((tq, MLA_HP), F32)) for _ in range(MLA_HEADS))
    for si, (tk, nblk) in enumerate(seg_blocks):
        k_ref, v_ref = kv_refs[2 * si], kv_refs[2 * si + 1]

        def body(j, carry, k_ref=k_ref, v_ref=v_ref, tk=tk):
            start = 0 if isinstance(j, int) else pl.multiple_of(j * tk, tk)
            new = []
            for hh in range(MLA_HEADS):
                sl = slice(hh * MLA_HP, (hh + 1) * MLA_HP)
                m, acc = carry[hh]
                s = _dot_nt(q_ref[0, :, sl], k_ref[0, pl.ds(start, tk), sl])
                m_new = jnp.maximum(m, jnp.max(s, axis=-1, keepdims=True))
                p = jnp.exp2(s - m_new).astype(BF16)
                acc = jnp.exp2(m - m_new) * acc + _dot(p, v_ref[0, pl.ds(start, tk), sl])
                new.append((m_new, acc))
            return tuple(new)

        carry = body(0, carry) if nblk == 1 else lax.fori_loop(0, nblk, body, carry)
    lane = _iota((tq, MLA_HP), 1)
    pairs = []
    for hh in range(0, MLA_HEADS, 2):
        acc_e, acc_o = carry[hh][1], carry[hh + 1][1]
        l_e = jnp.sum(jnp.where(lane == MLA_HP - 1, acc_e, 0.0), axis=-1, keepdims=True)
        l_o = jnp.sum(jnp.where(lane == 0, acc_o, 0.0), axis=-1, keepdims=True)
        pairs.append(jnp.where(lane < MLA_V, acc_e / l_e, acc_o / l_o))
    o_ref[0] = jnp.concatenate(pairs, axis=-1).astype(BF16)


def _attention(q, segs, tq, tk):
    b, l, _ = q.shape
    seg_blocks = []
    in_specs = [pl.BlockSpec((1, tq, MLA_HEADS * MLA_HP), lambda i, j: (i, j, 0))]
    args = [q]
    for k, v in segs:
        s = k.shape[1]
        t = min(tk, s)
        seg_blocks.append((t, s // t))
        for _ in range(2):
            in_specs.append(pl.BlockSpec((1, s, MLA_HEADS * MLA_HP), lambda i, j: (i, 0, 0),
                                         pipeline_mode=pl.Buffered(1)))
        args += [k, v]
    return pl.pallas_call(
        functools.partial(_attn_kernel, seg_blocks=tuple(seg_blocks), tq=tq),
        grid=(b, l // tq),
        in_specs=in_specs,
        out_specs=pl.BlockSpec((1, tq, MLA_HEADS * MLA_V), lambda i, j: (i, j, 0)),
        out_shape=jax.ShapeDtypeStruct((b, l, MLA_HEADS * MLA_V), BF16),
        compiler_params=_cparams(("parallel", "arbitrary")),
        name="attention",
    )(*args)


def _ret_kernel(x_ref, lgl_ref, lgh_ref, init_ref, y_ref, fin_ref, s_ref, g_ref, gbuf_ref, *, nblk, sub):
    C, H, N, P = CHUNK, RET_HEADS, RET_QK, RET_V
    HP = H * P
    ph, c = pl.program_id(1), pl.program_id(2)
    blk = jnp.where(ph == 0, nblk - 1 - c, c)

    def log_sigmoid(v):
        return jnp.minimum(v, 0.0) - jnp.log(1.0 + jnp.exp(-jnp.abs(v)))

    lgl = log_sigmoid(lgl_ref[...])
    lgf, lgb = lgl[0:1], lgl[1:2]
    rows = _iota((C, HP), 0).astype(F32)
    head_n = _iota((N, HP), 1) // P

    def diag_blocks(r):
        out = r[0:N]
        for hh in range(1, H):
            out = jnp.where(head_n == hh, r[hh * N:(hh + 1) * N], out)
        return out

    def as_rows(s_c):
        return jnp.concatenate([jnp.where(head_n == hh, s_c, 0.0) for hh in range(H)], axis=0)

    @pl.when((ph == 0) & (c == 0))
    def _():
        g_ref[...] = init_ref[0, 1]

    @pl.when(ph == 0)
    def _():
        w_b = jnp.exp(rows * lgb)
        tot_b = jnp.exp(C * lgb)
        for sb in reversed(range(sub)):
            r0 = sb * C
            k = x_ref[0, r0:r0 + C, HP:2 * HP].astype(F32)
            v = x_ref[0, r0:r0 + C, 2 * HP:3 * HP]
            g = g_ref[...]
            gbuf_ref[blk * sub + sb] = g
            g_ref[...] = g * tot_b + diag_blocks(_dot_tn((k * w_b).astype(BF16), v))

        @pl.when(c == nblk - 1)
        def _():
            fin_ref[0, 1] = g_ref[...]

    @pl.when((ph == 1) & (c == 0))
    def _():
        s_ref[...] = init_ref[0, 0]

    @pl.when(ph == 1)
    def _():
        lgh = log_sigmoid(lgh_ref[...])
        grp = _iota((C, HP), 1) // P
        di = (_iota((C, C), 0) - _iota((C, C), 1)).astype(F32)
        decay = [jnp.exp(jnp.where(di >= 0, di * lgh[hh:hh + 1, :], -di * lgh[H + hh:H + hh + 1, :]))
                 for hh in range(H)]
        e_f = jnp.exp((rows + 1.0) * lgf)
        e_b = jnp.exp((C - rows) * lgb)
        w_f = jnp.exp((C - 1.0 - rows) * lgf)
        tot_f = jnp.exp(C * lgf)
        for sb in range(sub):
            r0 = sb * C
            q = x_ref[0, r0:r0 + C, 0:HP]
            kb = x_ref[0, r0:r0 + C, HP:2 * HP]
            v = x_ref[0, r0:r0 + C, 2 * HP:3 * HP]
            gate = x_ref[0, r0:r0 + C, 3 * HP:4 * HP].astype(F32)
            ws, vs = [], []
            for hh in range(H):
                sc = _dot_nt(jnp.where(grp == hh, q, jnp.zeros_like(q)), kb)
                ws.append((sc * decay[hh]).astype(BF16))
                vs.append(jnp.where(grp == hh, v, jnp.zeros_like(v)))
            y = _dot(jnp.concatenate(ws, axis=1), jnp.concatenate(vs, axis=0))
            qf32 = q.astype(F32)
            s = s_ref[...]
            g_in = gbuf_ref[blk * sub + sb]
            y = y + _dot(jnp.concatenate([(qf32 * e_f).astype(BF16), (qf32 * e_b).astype(BF16)], axis=1),
                         jnp.concatenate([as_rows(s), as_rows(g_in)], axis=0).astype(BF16))
            s_ref[...] = s * tot_f + diag_blocks(_dot_tn((kb.astype(F32) * w_f).astype(BF16), v))

            mu = jnp.zeros_like(y)
            for hh in range(H):
                mh = jnp.sum(jnp.where(grp == hh, y, 0.0), axis=-1, keepdims=True) * (1.0 / P)
                mu = jnp.where(grp == hh, mh, mu)
            yc = y - mu
            var = jnp.zeros_like(y)
            for hh in range(H):
                vh = jnp.sum(jnp.where(grp == hh, yc * yc, 0.0), axis=-1, keepdims=True) * (1.0 / P)
                var = jnp.where(grp == hh, vh, var)
            y_ref[0, r0:r0 + C, :] = (yc * lax.rsqrt(var + EPS) * _silu(gate)).astype(BF16)

        @pl.when(c == nblk - 1)
        def _():
            fin_ref[0, 0] = s_ref[...]


def _block_index(nblk):
    return lambda i, ph, c: (i, (1 - ph) * (nblk - 1 - c) + ph * c, 0)


def _scan_sub(t):
    nc = t // CHUNK
    return 4 if nc % 4 == 0 else (2 if nc % 2 == 0 else 1)


def _ret_scan(ret, lw, init):
    b, t, _ = ret.shape
    sub = _scan_sub(t)
    rows = sub * CHUNK
    nblk = t // rows
    hp = RET_HEADS * RET_V
    return pl.pallas_call(
        functools.partial(_ret_kernel, nblk=nblk, sub=sub),
        grid=(b, 2, nblk),
        in_specs=[pl.BlockSpec((1, rows, W_RET), _block_index(nblk)),
                  pl.BlockSpec((2, hp), lambda i, ph, c: (0, 0)),
                  pl.BlockSpec((2 * RET_HEADS, CHUNK), lambda i, ph, c: (0, 0)),
                  pl.BlockSpec((1, 2, RET_QK, hp), lambda i, ph, c: (i, 0, 0, 0))],
        out_specs=[pl.BlockSpec((1, rows, hp), lambda i, ph, c: (i, ph * c, 0)),
                   pl.BlockSpec((1, 2, RET_QK, hp), lambda i, ph, c: (i, 0, 0, 0))],
        out_shape=[jax.ShapeDtypeStruct((b, t, hp), BF16),
                   jax.ShapeDtypeStruct((b, 2, RET_QK, hp), F32)],
        scratch_shapes=[pltpu.VMEM((RET_QK, hp), F32),
                        pltpu.VMEM((RET_QK, hp), F32),
                        pltpu.VMEM((t // CHUNK, RET_QK, hp), F32)],
        compiler_params=_cparams(("parallel", "arbitrary", "arbitrary")),
        name="retention_scan",
    )(ret, lw["ret_lgl"], lw["ret_lgh"], init)


def _ssd_kernel(x_ref, xp_ref, xn_ref, z_ref, dt_ref, cw_ref, cb_ref, dtb_ref, alog_ref, dvec_ref, nw_ref,
                init_ref, y_ref, fin_ref, s_ref, g_ref, gbuf_ref, *, nblk, sub):
    C, H, N, P, G = CHUNK, SSD_HEADS, SSD_STATE, SSD_HEADDIM, SSD_GROUPS
    HP = H * P
    HPG = H // G
    HALO = SSD_HALO
    ph, c = pl.program_id(1), pl.program_id(2)
    blk = jnp.where(ph == 0, nblk - 1 - c, c)

    ri = _iota((C, C + 2 * HALO), 0)
    ci = _iota((C, C + 2 * HALO), 1)
    sel_prev = (ci == ri + HALO - 1).astype(BF16)
    sel_next = (ci == ri + HALO + 1).astype(BF16)
    lane = _iota((LANES, HP), 0)
    head_of = _iota((LANES, HP), 1) // P
    e_f = (lane == head_of).astype(BF16)
    e_b = (lane == head_of + H).astype(BF16)
    r_i = _iota((C, C), 0)
    c_i = _iota((C, C), 1)
    low = (r_i >= c_i).astype(BF16)
    upp = (r_i <= c_i).astype(BF16)
    grp_n = _iota((N, HP), 1) // (HPG * P)
    neg_a = -jnp.exp(alog_ref[...])
    cw = cw_ref[...]

    def prep(sb):
        r0 = sb * C
        xc = x_ref[0, r0:r0 + C, :]
        if sb == 0:
            prev = jnp.where(blk > 0, xp_ref[0], jnp.zeros_like(xp_ref[0]))
        else:
            prev = x_ref[0, r0 - HALO:r0, :]
        if sb == sub - 1:
            nxt = jnp.where(blk < nblk - 1, xn_ref[0], jnp.zeros_like(xn_ref[0]))
        else:
            nxt = x_ref[0, r0 + C:r0 + C + HALO, :]
        stacked = jnp.concatenate([prev, xc, nxt], axis=0)
        conv = (cw[0:1] * _dot(sel_prev, stacked) + cw[1:2] * xc.astype(F32)
                + cw[2:3] * _dot(sel_next, stacked) + cb_ref[...])
        act = _silu(conv)
        xs = act[:, 0:HP]
        bm = act[:, HP:HP + G * N].astype(BF16)
        cm = act[:, HP + G * N:HP + 2 * G * N].astype(BF16)
        raw = dt_ref[0, r0:r0 + C, :] + dtb_ref[...]
        dt = jnp.maximum(raw, 0.0) + jnp.log(1.0 + jnp.exp(-jnp.abs(raw)))
        la = dt * neg_a
        a_f = _dot_exact_lhs01(low, la)
        a_b = _dot_exact_lhs01(upp, la)
        return xs, bm, cm, dt, la, a_f, a_b

    def state_update(state, bm, xs, wdt, tot8, e_dir, row):
        xw = (xs * _dot(wdt.astype(BF16), e_dir)).astype(BF16)
        tot = _dot_exact_rhs01(tot8, e_dir)[row:row + 1]
        r = _dot_tn(bm, xw)
        upd = r[0:N]
        for gg in range(1, G):
            upd = jnp.where(grp_n == gg, r[gg * N:(gg + 1) * N], upd)
        return state * tot + upd

    def as_rows(s_c):
        return jnp.concatenate([jnp.where(grp_n == gg, s_c, 0.0) for gg in range(G)], axis=0).astype(BF16)

    @pl.when((ph == 0) & (c == 0))
    def _():
        g_ref[...] = init_ref[0, 1]

    @pl.when(ph == 0)
    def _():
        for sb in reversed(range(sub)):
            xs, bm, _, dt, _, _, a_b = prep(sb)
            g = g_ref[...]
            gbuf_ref[blk * sub + sb] = g
            wdt = jnp.exp(a_b[0:1, :] - a_b) * dt
            g_ref[...] = state_update(g, bm, xs, wdt, jnp.exp(a_b[0:8, :]), e_b, 0)

        @pl.when(c == nblk - 1)
        def _():
            fin_ref[0, 1] = g_ref[...]

    @pl.when((ph == 1) & (c == 0))
    def _():
        s_ref[...] = init_ref[0, 0]

    @pl.when(ph == 1)
    def _():
        grp_l = _iota((C, G * N), 1) // N
        head_l = _iota((C, HP), 1) // P
        tri_f = r_i >= c_i
        tri_b = r_i <= c_i
        for sb in range(sub):
            r0 = sb * C
            xs, bm, cm, dt, la, a_f, a_b = prep(sb)
            la_t = la.T[0:2 * H]
            dt_t = dt.T[0:2 * H]
            af_row = _dot_exact_rhs01(la_t, upp)
            ab_row = _dot_exact_rhs01(la_t, low)
            sc = [_dot_nt(jnp.where(grp_l == gg, cm, jnp.zeros_like(cm)), bm) for gg in range(G)]
            xsb = xs.astype(BF16)
            ws, vs = [], []
            for hh in range(H):
                seg_f = a_f[:, hh:hh + 1] - af_row[hh:hh + 1, :]
                seg_b = a_b[:, H + hh:H + hh + 1] - ab_row[H + hh:H + hh + 1, :]
                dm = (jnp.exp(jnp.where(tri_f, seg_f, NEG)) * dt_t[hh:hh + 1, :]
                      + jnp.exp(jnp.where(tri_b, seg_b, NEG)) * dt_t[H + hh:H + hh + 1, :])
                ws.append((sc[hh // HPG] * dm).astype(BF16))
                vs.append(jnp.where(head_l == hh, xsb, jnp.zeros_like(xsb)))
            y = _dot(jnp.concatenate(ws, axis=1), jnp.concatenate(vs, axis=0))

            s = s_ref[...]
            g_in = gbuf_ref[blk * sub + sb]
            y = y + _dot(jnp.exp(a_f).astype(BF16), e_f) * _dot(cm, as_rows(s))
            y = y + _dot(jnp.exp(a_b).astype(BF16), e_b) * _dot(cm, as_rows(g_in))

            wdt = jnp.exp(a_f[C - 1:C, :] - a_f) * dt
            s_ref[...] = state_update(s, bm, xs, wdt, jnp.exp(a_f[C - 8:C, :]), e_f, 7)

            y = y + xs * dvec_ref[...]
            y = y * _silu(z_ref[0, r0:r0 + C, :].astype(F32))
            y_ref[0, r0:r0 + C, :] = _rms(y, nw_ref[...]).astype(BF16)

        @pl.when(c == nblk - 1)
        def _():
            fin_ref[0, 0] = s_ref[...]


SSD_HALO = 16


def _ssd_scan(xbc, z, dt, lw, init):
    b, t, _ = xbc.shape
    sub = _scan_sub(t)
    rows = sub * CHUNK
    nblk = t // rows
    hp = SSD_INNER
    per = rows // SSD_HALO
    nhalo = t // SSD_HALO
    bidx = _block_index(nblk)

    def prev_idx(i, ph, c):
        blk = (1 - ph) * (nblk - 1 - c) + ph * c
        return (i, jnp.maximum(blk * per - 1, 0), 0)

    def next_idx(i, ph, c):
        blk = (1 - ph) * (nblk - 1 - c) + ph * c
        return (i, jnp.minimum((blk + 1) * per, nhalo - 1), 0)

    const = lambda shape: pl.BlockSpec(shape, lambda i, ph, c: (0,) * len(shape))
    return pl.pallas_call(
        functools.partial(_ssd_kernel, nblk=nblk, sub=sub),
        grid=(b, 2, nblk),
        in_specs=[pl.BlockSpec((1, rows, W_XBC), bidx),
                  pl.BlockSpec((1, SSD_HALO, W_XBC), prev_idx),
                  pl.BlockSpec((1, SSD_HALO, W_XBC), next_idx),
                  pl.BlockSpec((1, rows, W_Z), bidx),
                  pl.BlockSpec((1, rows, W_DT), bidx),
                  const((8, W_XBC)), const((1, W_XBC)), const((1, LANES)), const((1, LANES)),
                  const((1, hp)), const((1, hp)),
                  pl.BlockSpec((1, 2, SSD_STATE, hp), lambda i, ph, c: (i, 0, 0, 0))],
        out_specs=[pl.BlockSpec((1, rows, hp), lambda i, ph, c: (i, ph * c, 0)),
                   pl.BlockSpec((1, 2, SSD_STATE, hp), lambda i, ph, c: (i, 0, 0, 0))],
        out_shape=[jax.ShapeDtypeStruct((b, t, hp), BF16),
                   jax.ShapeDtypeStruct((b, 2, SSD_STATE, hp), F32)],
        scratch_shapes=[pltpu.VMEM((SSD_STATE, hp), F32),
                        pltpu.VMEM((SSD_STATE, hp), F32),
                        pltpu.VMEM((t // CHUNK, SSD_STATE, hp), F32)],
        compiler_params=_cparams(("parallel", "arbitrary", "arbitrary")),
        name="ssd_scan",
    )(xbc, xbc, xbc, z, dt, lw["conv_w"], lw["conv_b"], lw["dt_bias"], lw["a_log"], lw["d_vec"], lw["ssd_nw"], init)


def _mix_residual(x_ref, att_ref, ret_ref, ssd_ref, mod_ref, n2_ref, wo_ref):
    mix = jnp.concatenate([att_ref[0], ret_ref[0], ssd_ref[0]], axis=-1)
    x1 = x_ref[0] + mod_ref[0, 2:3, :] * _dot(mix, wo_ref[...])
    h = _rms(x1, n2_ref[...]) * (1.0 + mod_ref[0, 4:5, :]) + mod_ref[0, 3:4, :]
    return x1, h


def _dense_kernel(x_ref, att_ref, ret_ref, ssd_ref, mod_ref, n2_ref, wo_ref, wg_ref, wu_ref, wd_ref, o_ref,
                  *, ff_chunk):
    x1, h = _mix_residual(x_ref, att_ref, ret_ref, ssd_ref, mod_ref, n2_ref, wo_ref)
    hb = h.astype(BF16)
    ff = wg_ref.shape[1]
    f = jnp.zeros_like(x1)
    for s in range(0, ff, ff_chunk):
        g = _dot(hb, wg_ref[:, s:s + ff_chunk])
        u = _dot(hb, wu_ref[:, s:s + ff_chunk])
        f = f + _dot((_silu(g) * u).astype(BF16), wd_ref[s:s + ff_chunk, :])
    o_ref[0] = x1 + mod_ref[0, 5:6, :] * f


def _dense_block(xs, att, ret, ssd, mods, lw, tm):
    b, t, d = xs.shape
    ff = lw["wg"].shape[1]
    tok = lambda w: pl.BlockSpec((1, tm, w), lambda i, j: (i, j, 0))
    const = lambda shape: pl.BlockSpec(shape, lambda i, j: (0,) * len(shape), pipeline_mode=pl.Buffered(1))
    return pl.pallas_call(
        functools.partial(_dense_kernel, ff_chunk=ff // 2),
        grid=(b, t // tm),
        in_specs=[tok(d), tok(att.shape[-1]), tok(ret.shape[-1]), tok(ssd.shape[-1]),
                  pl.BlockSpec((1, 8, d), lambda i, j: (i, 0, 0)), const((1, d)),
                  const((d, d)), const((d, ff)), const((d, ff)), const((ff, d))],
        out_specs=tok(d),
        out_shape=jax.ShapeDtypeStruct((b, t, d), F32),
        compiler_params=_cparams(("parallel", "parallel")),
        name="outproj_ffn",
    )(xs, att, ret, ssd, mods, lw["n2"], lw["wo"], lw["wg"], lw["wu"], lw["wd"])


def _route_kernel(x_ref, att_ref, ret_ref, ssd_ref, mod_ref, n2_ref, wo_ref, wr_ref, x1_ref, h_ref, gate_ref):
    x1, h = _mix_residual(x_ref, att_ref, ret_ref, ssd_ref, mod_ref, n2_ref, wo_ref)
    x1_ref[0] = x1
    h_hi = h.astype(BF16)
    h_ref[0] = h_hi
    h_lo = (h - h_hi.astype(F32)).astype(BF16)
    logits = _dot(h_hi, wr_ref[0]) + _dot(h_lo, wr_ref[0]) + _dot(h_hi, wr_ref[1])
    lane = _iota(logits.shape, 1).astype(F32)
    logits = jnp.where(lane < N_EXPERTS, logits, NEG)
    m1 = jnp.max(logits, axis=-1, keepdims=True)
    i1 = jnp.min(jnp.where(logits == m1, lane, float(LANES)), axis=-1, keepdims=True)
    rest = jnp.where(lane == i1, NEG, logits)
    m2 = jnp.max(rest, axis=-1, keepdims=True)
    i2 = jnp.min(jnp.where(rest == m2, lane, float(LANES)), axis=-1, keepdims=True)
    e2 = jnp.exp(m2 - m1)
    den = 1.0 + e2
    gate_ref[0] = jnp.where(lane == i1, 1.0 / den, 0.0) + jnp.where(lane == i2, e2 / den, 0.0)


def _route_block(xs, att, ret, ssd, mods, lw, tm):
    b, t, d = xs.shape
    tok = lambda w: pl.BlockSpec((1, tm, w), lambda i, j: (i, j, 0))
    const = lambda shape: pl.BlockSpec(shape, lambda i, j: (0,) * len(shape))
    return pl.pallas_call(
        _route_kernel,
        grid=(b, t // tm),
        in_specs=[tok(d), tok(att.shape[-1]), tok(ret.shape[-1]), tok(ssd.shape[-1]),
                  pl.BlockSpec((1, 8, d), lambda i, j: (i, 0, 0)), const((1, d)),
                  const((d, d)), const((2, d, LANES))],
        out_specs=[tok(d), tok(d), tok(LANES)],
        out_shape=[jax.ShapeDtypeStruct((b, t, d), F32), jax.ShapeDtypeStruct((b, t, d), BF16),
                   jax.ShapeDtypeStruct((b, t, LANES), F32)],
        compiler_params=_cparams(("parallel", "parallel")),
        name="outproj_route",
    )(xs, att, ret, ssd, mods, lw["n2"], lw["wo"], lw["router"])


def _moe_kernel(x1_ref, h_ref, gate_ref, mod_ref, wgu_ref, wd_ref, fn_ref, o_ref, *, final_norm, ff_split):
    e = pl.program_id(2)

    @pl.when(e == 0)
    def _():
        o_ref[...] = jnp.zeros_like(o_ref)

    hb = h_ref[0]
    f = None
    for s0, s1 in ff_split:
        w = s1 - s0
        gu = _dot(hb, wgu_ref[0, :, 2 * s0:2 * s1])
        part = _dot((_silu(gu[:, :w]) * gu[:, w:]).astype(BF16), wd_ref[0, s0:s1, :])
        f = part if f is None else f + part
    gates = gate_ref[0]
    ge = jnp.sum(jnp.where(_iota(gates.shape, 1) == e, gates, 0.0), axis=-1, keepdims=True)
    o_ref[0] += ge * f

    @pl.when(e == N_EXPERTS - 1)
    def _():
        out = x1_ref[0] + mod_ref[0, 5:6, :] * o_ref[0]
        o_ref[0] = _rms(out, fn_ref[...]) if final_norm else out


def _moe_ff_split(ff):
    half = (ff // 2 + 2 * LANES - 1) // (2 * LANES) * (2 * LANES)
    return ((0, half), (half, ff))


def _moe_weights(w_gate, w_up, w_down):
    ff = w_gate.shape[-1]
    wgu = jnp.concatenate([w[..., s0:s1] for s0, s1 in _moe_ff_split(ff) for w in (w_gate, w_up)], axis=-1)
    return wgu.astype(BF16), w_down.astype(BF16)


def _moe_block(x1, h, gates, mods, lw, fnw, final_norm, tm):
    b, t, d = x1.shape
    ff = lw["moe_wd"].shape[1]
    tok = lambda w: pl.BlockSpec((1, tm, w), lambda i, j, e: (i, j, 0))
    return pl.pallas_call(
        functools.partial(_moe_kernel, final_norm=final_norm, ff_split=_moe_ff_split(ff)),
        grid=(b, t // tm, N_EXPERTS),
        in_specs=[tok(d), tok(d), tok(LANES), pl.BlockSpec((1, 8, d), lambda i, j, e: (i, 0, 0)),
                  pl.BlockSpec((1, d, 2 * ff), lambda i, j, e: (e, 0, 0)),
                  pl.BlockSpec((1, ff, d), lambda i, j, e: (e, 0, 0)),
                  pl.BlockSpec((1, d), lambda i, j, e: (0, 0))],
        out_specs=tok(d),
        out_shape=jax.ShapeDtypeStruct((b, t, d), F32),
        compiler_params=_cparams(("parallel", "parallel", "arbitrary")),
        name="moe",
    )(x1, h, gates, mods, lw["moe_wgu"], lw["moe_wd"], fnw)


def _final_norm_kernel(x_ref, w_ref, o_ref):
    o_ref[0] = _rms(x_ref[0], w_ref[...])


def _final_norm(xs, fnw, tm):
    b, t, d = xs.shape
    return pl.pallas_call(
        _final_norm_kernel,
        grid=(b, t // tm),
        in_specs=[pl.BlockSpec((1, tm, d), lambda i, j: (i, j, 0)), pl.BlockSpec((1, d), lambda i, j: (0, 0))],
        out_specs=pl.BlockSpec((1, tm, d), lambda i, j: (i, j, 0)),
        out_shape=jax.ShapeDtypeStruct((b, t, d), F32),
        compiler_params=_cparams(("parallel", "parallel")),
        name="final_norm",
    )(xs, fnw)


def _rot_cols(w):
    a, b2, c, d = jnp.split(w, 4, axis=-1)
    return jnp.concatenate([-b2, a, -d, c], axis=-1)


def _layer_weights(i, norm1_w, norm2_w, w_in, w_out, mla_q_norm_w, mla_w_qb, mla_kv_norm_w, mla_w_kb, mla_w_vb,
                   ret_decay_logit, ssd_conv_w, ssd_conv_b, ssd_dt_bias, ssd_a_log, ssd_d, ssd_norm_w):
    d = w_in.shape[1]
    offs = np.cumsum((0,) + IN_SIZES)
    seg = [w_in[i][:, offs[j]:offs[j + 1]] for j in range(len(IN_SIZES))]
    zeros = lambda n: jnp.zeros((d, n), F32)
    w_all = jnp.concatenate(
        [seg[0], seg[1], seg[2], _rot_cols(seg[2]), zeros(W_MLA - MLA_Q_LORA - MLA_KV_LORA - 2 * MLA_ROPE),
         seg[3] * (RET_QK ** -0.5), seg[4], seg[5], seg[6], seg[7], seg[8], seg[9], seg[10],
         zeros(W_DT - 2 * SSD_HEADS)], axis=1).astype(BF16)

    dh = MLA_NOPE + MLA_ROPE
    wqb = mla_w_qb[i].reshape(MLA_Q_LORA, MLA_HEADS, dh)
    pad = lambda a, lo, hi: jnp.pad(a, ((0, 0), (0, 0), (lo, hi)))
    wq = pad(wqb, 0, MLA_HP - dh).reshape(MLA_Q_LORA, MLA_HEADS * MLA_HP).astype(BF16)
    wqr = pad(_rot_cols(wqb[:, :, MLA_NOPE:]), MLA_NOPE, MLA_HP - dh).reshape(MLA_Q_LORA, MLA_HEADS * MLA_HP).astype(BF16)
    wkb = pad(mla_w_kb[i].reshape(MLA_KV_LORA, MLA_HEADS, MLA_NOPE), 0, MLA_HP - MLA_NOPE)
    place = np.zeros((MLA_ROPE, MLA_HEADS, MLA_HP), np.float32)
    for r in range(MLA_ROPE):
        place[r, :, MLA_NOPE + r] = 1.0
    place = jnp.asarray(place)
    wk = jnp.concatenate([wkb, place, place, jnp.zeros((MLA_KV_LORA - 2 * MLA_ROPE, MLA_HEADS, MLA_HP), F32)],
                         axis=0).reshape(2 * MLA_KV_LORA, MLA_HEADS * MLA_HP).astype(BF16)

    wvb = mla_w_vb[i].reshape(MLA_KV_LORA, MLA_HEADS, MLA_V)
    wv = jnp.concatenate(
        [pad(wvb[:, hh:hh + 1], (hh % 2) * MLA_V, MLA_HP - MLA_V - (hh % 2) * MLA_V) for hh in range(MLA_HEADS)],
        axis=1).reshape(MLA_KV_LORA, MLA_HEADS * MLA_HP).astype(BF16)

    lg = ret_decay_logit[i]
    padl = lambda v: jnp.pad(v.reshape(1, -1), ((0, 0), (0, LANES - 2 * SSD_HEADS)))
    return {
        "n1": norm1_w[i].reshape(1, d), "n2": norm2_w[i].reshape(1, d),
        "w_all": w_all, "qn": mla_q_norm_w[i].reshape(1, -1), "wq": wq, "wqr": wqr,
        "kvn": mla_kv_norm_w[i].reshape(1, -1), "wk": wk, "wv": wv,
        "wo": w_out[i].astype(BF16),
        "ret_lgl": jnp.repeat(lg, RET_V, axis=1),
        "ret_lgh": jnp.broadcast_to(lg.reshape(2 * RET_HEADS, 1), (2 * RET_HEADS, CHUNK)),
        "conv_w": jnp.pad(ssd_conv_w[i], ((0, 8 - ssd_conv_w.shape[1]), (0, 0))),
        "conv_b": ssd_conv_b[i].reshape(1, -1),
        "dt_bias": padl(ssd_dt_bias[i]), "a_log": padl(ssd_a_log[i]),
        "d_vec": jnp.repeat(ssd_d[i], SSD_HEADDIM).reshape(1, -1),
        "ssd_nw": ssd_norm_w[i].reshape(1, -1),
    }


def _rope_tables(n_lat, n_ctx):
    rows = n_lat // GRID_W
    row = jnp.repeat(jnp.arange(rows, dtype=F32), GRID_W)
    col = jnp.tile(jnp.arange(GRID_W, dtype=F32), rows)
    inv_freq = ROPE_BASE ** (-jnp.arange(0, ROPE_AXIS_DIM, 2, dtype=F32) / ROPE_AXIS_DIM)
    ang_r = row[:, None] * inv_freq
    ang_c = col[:, None] * inv_freq
    ang = jnp.concatenate([ang_r, ang_r, ang_c, ang_c], axis=-1)
    c2 = MLA_SCALE * math.log2(math.e)

    def build(cos, sin):
        n = cos.shape[0]
        cosq = jnp.concatenate([jnp.full((n, MLA_NOPE), c2, F32), c2 * cos, jnp.zeros((n, MLA_HP - MLA_NOPE - MLA_ROPE), F32)], 1)
        sinq = jnp.concatenate([jnp.zeros((n, MLA_NOPE), F32), c2 * sin, jnp.zeros((n, MLA_HP - MLA_NOPE - MLA_ROPE), F32)], 1)
        ck = jnp.concatenate([cos, sin, jnp.zeros((n, LANES - 2 * MLA_ROPE), F32)], 1)
        return cosq, sinq, ck

    lat = build(jnp.cos(ang), jnp.sin(ang))
    ctx = build(jnp.ones((n_ctx, MLA_ROPE), F32), jnp.zeros((n_ctx, MLA_ROPE), F32))
    return lat, ctx


def _token_tile(t, pref):
    tm = min(pref, t)
    while t % tm:
        tm //= 2
    return tm


def kernel(x, c, ctx, c_ctx, mod_w, mod_b, norm1_w, norm2_w, w_in, w_out, mla_q_norm_w, mla_w_qb, mla_kv_norm_w, mla_w_kb, mla_w_vb, ret_decay_logit, ssd_conv_w, ssd_conv_b, ssd_dt_bias, ssd_a_log, ssd_d, ssd_norm_w, ffn_w_gate, ffn_w_up, ffn_w_down, moe_router, moe_w_gate, moe_w_up, moe_w_down, final_norm_w):
    b, n_lat, d = x.shape
    n_ctx = ctx.shape[1]
    depth = mod_w.shape[0]
    assert b + 1 <= 8 and n_lat % CHUNK == 0 and n_ctx % CHUNK == 0 and n_lat % GRID_W == 0

    cond = jnp.concatenate([c, c_ctx[None, :], jnp.zeros((8 - b - 1, d), F32)], axis=0)
    mod_all = _modulation(cond, mod_w, mod_b)
    tabs_x, tabs_c = _rope_tables(n_lat, n_ctx)
    fnw = final_norm_w.reshape(1, d)

    tm_x = _token_tile(n_lat, 512)
    tm_c = _token_tile(n_ctx, 256)
    tm_moe = _token_tile(n_lat, 1024)
    tq_x = _token_tile(n_lat, 1024)
    tk_x = 2048
    zero_ret = jnp.zeros((b, 2, RET_QK, RET_HEADS * RET_V), F32)
    zero_ssd = jnp.zeros((b, 2, SSD_STATE, SSD_INNER), F32)

    h_ctx = ctx
    for i in range(depth):
        last = i == depth - 1
        lw = _layer_weights(i, norm1_w, norm2_w, w_in, w_out, mla_q_norm_w, mla_w_qb, mla_kv_norm_w, mla_w_kb,
                            mla_w_vb, ret_decay_logit, ssd_conv_w, ssd_conv_b, ssd_dt_bias, ssd_a_log, ssd_d,
                            ssd_norm_w)
        m6 = mod_all[i].reshape(8, 6, d)
        pad2 = jnp.zeros((b, 2, d), F32)
        mods_x = jnp.concatenate([m6[:b], pad2], axis=1)
        mods_c = jnp.concatenate([jnp.broadcast_to(m6[b:b + 1], (b, 6, d)), pad2], axis=1)

        qc, kc, vc, retc, zc, xbcc, dtc = _inproj(h_ctx, mods_c, lw, tabs_c, tm_c)
        ret_yc, ret_fin = _ret_scan(retc, lw, zero_ret)
        ssd_yc, ssd_fin = _ssd_scan(xbcc, zc, dtc, lw, zero_ssd)

        qx, kx, vx, retx, zx, xbcx, dtx = _inproj(x, mods_x, lw, tabs_x, tm_x)
        att_x = _attention(qx, [(kc, vc), (kx, vx)], tq_x, tk_x)
        ret_yx, _ = _ret_scan(retx, lw, ret_fin)
        ssd_yx, _ = _ssd_scan(xbcx, zx, dtx, lw, ssd_fin)

        if i % 2 == 0:
            j = i // 2
            lw.update(wg=ffn_w_gate[j].astype(BF16), wu=ffn_w_up[j].astype(BF16), wd=ffn_w_down[j].astype(BF16))
            x = _dense_block(x, att_x, ret_yx, ssd_yx, mods_x, lw, tm_x)
            if last:
                x = _final_norm(x, fnw, tm_x)
        else:
            j = i // 2
            wgu, wd = _moe_weights(moe_w_gate[j], moe_w_up[j], moe_w_down[j])
            r_f32 = jnp.pad(moe_router[j], ((0, 0), (0, LANES - N_EXPERTS)))
            r_hi = r_f32.astype(BF16)
            r_lo = (r_f32 - r_hi.astype(F32)).astype(BF16)
            lw.update(router=jnp.stack([r_hi, r_lo]), moe_wgu=wgu, moe_wd=wd)
            x1, hb, gates = _route_block(x, att_x, ret_yx, ssd_yx, mods_x, lw, tm_x)
            x = _moe_block(x1, hb, gates, mods_x, lw, fnw, last, tm_moe)

        if not last:
            att_c = _attention(qc, [(kc, vc)], tm_c, tk_x)
            if i % 2 == 0:
                h_ctx = _dense_block(h_ctx, att_c, ret_yc, ssd_yc, mods_c, lw, tm_c)
            else:
                c1, chb, cg = _route_block(h_ctx, att_c, ret_yc, ssd_yc, mods_c, lw, tm_c)
                h_ctx = _moe_block(c1, chb, cg, mods_c, lw, fnw, False, tm_c)
    return x
```

```python
import functools
import math

import jax
import jax.numpy as jnp
import numpy as np
from jax import lax
from jax.experimental import pallas as pl
from jax.experimental.pallas import tpu as pltpu

F32 = jnp.float32
BF16 = jnp.bfloat16
HIGHEST = lax.Precision.HIGHEST

EPS = 1e-6
CHUNK = 128
GRID_W = 64
MLA_HEADS = 4
MLA_Q_LORA = 256
MLA_KV_LORA = 128
MLA_NOPE = 64
MLA_ROPE = 32
MLA_V = 64
MLA_SCALE = (MLA_NOPE + MLA_ROPE) ** -0.5
ROPE_AXIS_DIM = MLA_ROPE // 2
ROPE_BASE = 10000.0
RET_HEADS = 4
RET_QK = 64
RET_V = 64
SSD_HEADS = 8
SSD_HEADDIM = 64
SSD_GROUPS = 2
SSD_STATE = 64
SSD_INNER = SSD_HEADS * SSD_HEADDIM
SSD_CONV_CH = SSD_INNER + 2 * SSD_GROUPS * SSD_STATE
N_EXPERTS = 8
IN_SIZES = (MLA_Q_LORA, MLA_KV_LORA, MLA_ROPE,
            RET_HEADS * RET_QK, RET_HEADS * RET_QK, RET_HEADS * RET_V, RET_HEADS * RET_V,
            SSD_INNER, SSD_CONV_CH, SSD_HEADS, SSD_HEADS)

LANES = 128
V7X_VMEM_LIMIT = 56 * 1024 * 1024

MLA_HP = 128
NEG = -1e30


def _cparams(sem):
    return pltpu.CompilerParams(dimension_semantics=sem, vmem_limit_bytes=V7X_VMEM_LIMIT)


def _dot(a, b):
    return jnp.dot(a, b, preferred_element_type=F32)


def _dot_nt(a, b):
    return lax.dot_general(a, b, (((1,), (1,)), ((), ())), preferred_element_type=F32)


def _dot_tn(a, b):
    return lax.dot_general(a, b, (((0,), (0,)), ((), ())), preferred_element_type=F32)


def _split3(x):
    hi = x.astype(BF16)
    r1 = x - hi.astype(F32)
    mid = r1.astype(BF16)
    lo = (r1 - mid.astype(F32)).astype(BF16)
    return hi, mid, lo


def _dot_exact_rhs01(x, e):
    hi, mid, lo = _split3(x)
    return _dot(hi, e) + _dot(mid, e) + _dot(lo, e)


def _dot_exact_lhs01(e, x):
    hi, mid, lo = _split3(x)
    return _dot(e, hi) + _dot(e, mid) + _dot(e, lo)


def _silu(x):
    return x / (1.0 + jnp.exp(-x))


def _rms(x, w):
    return x * lax.rsqrt(jnp.mean(x * x, axis=-1, keepdims=True) + EPS) * w


def _iota(shape, dim):
    return lax.broadcasted_iota(jnp.int32, shape, dim)


def _mod_kernel(a_ref, w_ref, b_ref, o_ref):
    a = _silu(a_ref[...])
    o_ref[0] = jnp.dot(a, w_ref[0], precision=HIGHEST, preferred_element_type=F32) + b_ref[0]


def _modulation(cond, mod_w, mod_b):
    depth, d, n = mod_w.shape
    tn = 1536
    return pl.pallas_call(
        _mod_kernel,
        grid=(depth, n // tn),
        in_specs=[pl.BlockSpec((8, d), lambda i, j: (0, 0)),
                  pl.BlockSpec((1, d, tn), lambda i, j: (i, 0, j)),
                  pl.BlockSpec((1, 1, tn), lambda i, j: (i, 0, j))],
        out_specs=pl.BlockSpec((1, 8, tn), lambda i, j: (i, 0, j)),
        out_shape=jax.ShapeDtypeStruct((depth, 8, n), F32),
        compiler_params=_cparams(("parallel", "parallel")),
        name="modulation",
    )(cond, mod_w, mod_b.reshape(depth, 1, n))


W_MLA, W_RET, W_Z, W_XBC, W_DT = 512, 1024, 512, 768, 128
W_ALL = W_MLA + W_RET + W_Z + W_XBC + W_DT


def _ones_lane_mask(shape):
    lane = _iota(shape, len(shape) - 1) % (2 * MLA_HP)
    return (lane == MLA_HP - 1) | (lane == MLA_HP)


def _inproj_kernel(x_ref, mod_ref, n1_ref, win_ref, qn_ref, wq_ref, wqr_ref, kvn_ref, wk_ref, wv_ref,
                   cq_ref, sq_ref, ck_ref,
                   q_out, k_out, v_out, ret_out, z_out, xbc_out, dt_out):
    x = x_ref[0]
    h = _rms(x, n1_ref[...]) * (1.0 + mod_ref[0, 1:2, :]) + mod_ref[0, 0:1, :]
    hb = h.astype(BF16)

    o = 0
    pm = _dot(hb, win_ref[:, o:o + W_MLA]); o += W_MLA
    ret_out[0] = _dot(hb, win_ref[:, o:o + W_RET]).astype(BF16); o += W_RET
    z_out[0] = _dot(hb, win_ref[:, o:o + W_Z]).astype(BF16); o += W_Z
    xbc_out[0] = _dot(hb, win_ref[:, o:o + W_XBC]).astype(BF16); o += W_XBC
    dt_out[0] = _dot(hb, win_ref[:, o:o + W_DT])

    cqb = _rms(pm[:, 0:MLA_Q_LORA], qn_ref[...]).astype(BF16)
    qm = _dot(cqb, wq_ref[...])
    qr = _dot(cqb, wqr_ref[...])
    cosq, sinq = cq_ref[...], sq_ref[...]
    for hh in range(MLA_HEADS):
        sl = slice(hh * MLA_HP, (hh + 1) * MLA_HP)
        q_out[0, :, sl] = (qm[:, sl] * cosq + qr[:, sl] * sinq).astype(BF16)

    ckvn = _rms(pm[:, MLA_Q_LORA:MLA_Q_LORA + MLA_KV_LORA], kvn_ref[...])
    u = pm[:, MLA_Q_LORA + MLA_KV_LORA:W_MLA] * ck_ref[...]
    kin = jnp.concatenate([ckvn, u], axis=-1).astype(BF16)
    k_out[0] = _dot(kin, wk_ref[...]).astype(BF16)
    v = _dot(ckvn.astype(BF16), wv_ref[...])
    v_out[0] = jnp.where(_ones_lane_mask(v.shape), 1.0, v).astype(BF16)


def _inproj(xs, mods, lw, tabs, tm):
    b, t, d = xs.shape
    const = lambda shape: pl.BlockSpec(shape, lambda i, j: (0,) * len(shape))
    tok = lambda w: pl.BlockSpec((1, tm, w), lambda i, j: (i, j, 0))
    tab = pl.BlockSpec((tm, LANES), lambda i, j: (j, 0))
    widths = (MLA_HEADS * MLA_HP, MLA_HEADS * MLA_HP, MLA_HEADS * MLA_HP, W_RET, W_Z, W_XBC, W_DT)
    dtypes = (BF16, BF16, BF16, BF16, BF16, BF16, F32)
    return pl.pallas_call(
        _inproj_kernel,
        grid=(b, t // tm),
        in_specs=[tok(d), pl.BlockSpec((1, 8, d), lambda i, j: (i, 0, 0)), const((1, d)),
                  const((d, W_ALL)), const((1, MLA_Q_LORA)), const((MLA_Q_LORA, MLA_HEADS * MLA_HP)),
                  const((MLA_Q_LORA, MLA_HEADS * MLA_HP)), const((1, MLA_KV_LORA)),
                  const((2 * MLA_KV_LORA, MLA_HEADS * MLA_HP)), const((MLA_KV_LORA, MLA_HEADS * MLA_HP)),
                  tab, tab, tab],
        out_specs=[tok(w) for w in widths],
        out_shape=[jax.ShapeDtypeStruct((b, t, w), dt) for w, dt in zip(widths, dtypes)],
        compiler_params=_cparams(("parallel", "parallel")),
        name="inproj",
    )(xs, mods, lw["n1"], lw["w_all"], lw["qn"], lw["wq"], lw["wqr"], lw["kvn"], lw["wk"], lw["wv"], *tabs)


def _attn_kernel(*refs, seg_blocks, tq):
    nseg = len(seg_blocks)
    q_ref = refs[0]
    kv_refs = refs[1:1 + 2 * nseg]
    o_ref = refs[1 + 2 * nseg]
    carry = tuple((jnp.full((tq, 1), NEG, F32), jnp.zeros((tq, MLA_HP), F32)) for _ in range(MLA_HEADS))
    for si, (tk, nblk) in enumerate(seg_blocks):
        k_ref, v_ref = kv_refs[2 * si], kv_refs[2 * si + 1]

        def body(j, carry, k_ref=k_ref, v_ref=v_ref, tk=tk):
            start = 0 if isinstance(j, int) else pl.multiple_of(j * tk, tk)
            new = []
            for hh in range(MLA_HEADS):
                sl = slice(hh * MLA_HP, (hh + 1) * MLA_HP)
                m, acc = carry[hh]
                s = _dot_nt(q_ref[0, :, sl], k_ref[0, pl.ds(start, tk), sl])
                m_new = jnp.maximum(m, jnp.max(s, axis=-1, keepdims=True))
                p = jnp.exp2(s - m_new).astype(BF16)
                acc = jnp.exp2(m - m_new) * acc + _dot(p, v_ref[0, pl.ds(start, tk), sl])
                new.append((m_new, acc))
            return tuple(new)

        carry = body(0, carry) if nblk == 1 else lax.fori_loop(0, nblk, body, carry)
    lane = _iota((tq, MLA_HP), 1)
    pairs = []
    for hh in range(0, MLA_HEADS, 2):
        acc_e, acc_o = carry[hh][1], carry[hh + 1][1]
        l_e = jnp.sum(jnp.where(lane == MLA_HP - 1, acc_e, 0.0), axis=-1, keepdims=True)
        l_o = jnp.sum(jnp.where(lane == 0, acc_o, 0.0), axis=-1, keepdims=True)
        pairs.append(jnp.where(lane < MLA_V, acc_e / l_e, acc_o / l_o))
    o_ref[0] = jnp.concatenate(pairs, axis=-1).astype(BF16)


def _attention(q, segs, tq, tk):
    b, l, _ = q.shape
    seg_blocks = []
    in_specs = [pl.BlockSpec((1, tq, MLA_HEADS * MLA_HP), lambda i, j: (i, j, 0))]
    args = [q]
    for k, v in segs:
        s = k.shape[1]
        t = min(tk, s)
        seg_blocks.append((t, s // t))
        for _ in range(2):
            in_specs.append(pl.BlockSpec((1, s, MLA_HEADS * MLA_HP), lambda i, j: (i, 0, 0),
                                         pipeline_mode=pl.Buffered(1)))
        args += [k, v]
    return pl.pallas_call(
        functools.partial(_attn_kernel, seg_blocks=tuple(seg_blocks), tq=tq),
        grid=(b, l // tq),
        in_specs=in_specs,
        out_specs=pl.BlockSpec((1, tq, MLA_HEADS * MLA_V), lambda i, j: (i, j, 0)),
        out_shape=jax.ShapeDtypeStruct((b, l, MLA_HEADS * MLA_V), BF16),
        compiler_params=_cparams(("parallel", "arbitrary")),
        name="attention",
    )(*args)


def _ret_kernel(x_ref, lgl_ref, lgh_ref, init_ref, y_ref, fin_ref, s_ref, g_ref, gbuf_ref, *, nblk, sub):
    C, H, N, P = CHUNK, RET_HEADS, RET_QK, RET_V
    HP = H * P
    ph, c = pl.program_id(1), pl.program_id(2)
    blk = jnp.where(ph == 0, nblk - 1 - c, c)

    def log_sigmoid(v):
        return jnp.minimum(v, 0.0) - jnp.log(1.0 + jnp.exp(-jnp.abs(v)))

    lgl = log_sigmoid(lgl_ref[...])
    lgf, lgb = lgl[0:1], lgl[1:2]
    rows = _iota((C, HP), 0).astype(F32)
    head_n = _iota((N, HP), 1) // P

    def diag_blocks(r):
        out = r[0:N]
        for hh in range(1, H):
            out = jnp.where(head_n == hh, r[hh * N:(hh + 1) * N], out)
        return out

    def as_rows(s_c):
        return jnp.concatenate([jnp.where(head_n == hh, s_c, 0.0) for hh in range(H)], axis=0)

    @pl.when((ph == 0) & (c == 0))
    def _():
        g_ref[...] = init_ref[0, 1]

    @pl.when(ph == 0)
    def _():
        w_b = jnp.exp(rows * lgb)
        tot_b = jnp.exp(C * lgb)
        for sb in reversed(range(sub)):
            r0 = sb * C
            k = x_ref[0, r0:r0 + C, HP:2 * HP].astype(F32)
            v = x_ref[0, r0:r0 + C, 2 * HP:3 * HP]
            g = g_ref[...]
            gbuf_ref[blk * sub + sb] = g
            g_ref[...] = g * tot_b + diag_blocks(_dot_tn((k * w_b).astype(BF16), v))

        @pl.when(c == nblk - 1)
        def _():
            fin_ref[0, 1] = g_ref[...]

    @pl.when((ph == 1) & (c == 0))
    def _():
        s_ref[...] = init_ref[0, 0]

    @pl.when(ph == 1)
    def _():
        lgh = log_sigmoid(lgh_ref[...])
        grp = _iota((C, HP), 1) // P
        di = (_iota((C, C), 0) - _iota((C, C), 1)).astype(F32)
        decay = [jnp.exp(jnp.where(di >= 0, di * lgh[hh:hh + 1, :], -di * lgh[H + hh:H + hh + 1, :]))
                 for hh in range(H)]
        e_f = jnp.exp((rows + 1.0) * lgf)
        e_b = jnp.exp((C - rows) * lgb)
        w_f = jnp.exp((C - 1.0 - rows) * lgf)
        tot_f = jnp.exp(C * lgf)
        for sb in range(sub):
            r0 = sb * C
            q = x_ref[0, r0:r0 + C, 0:HP]
            kb = x_ref[0, r0:r0 + C, HP:2 * HP]
            v = x_ref[0, r0:r0 + C, 2 * HP:3 * HP]
            gate = x_ref[0, r0:r0 + C, 3 * HP:4 * HP].astype(F32)
            ws, vs = [], []
            for hh in range(H):
                sc = _dot_nt(jnp.where(grp == hh, q, jnp.zeros_like(q)), kb)
                ws.append((sc * decay[hh]).astype(BF16))
                vs.append(jnp.where(grp == hh, v, jnp.zeros_like(v)))
            y = _dot(jnp.concatenate(ws, axis=1), jnp.concatenate(vs, axis=0))
            qf32 = q.astype(F32)
            s = s_ref[...]
            g_in = gbuf_ref[blk * sub + sb]
            y = y + _dot(jnp.concatenate([(qf32 * e_f).astype(BF16), (qf32 * e_b).astype(BF16)], axis=1),
                         jnp.concatenate([as_rows(s), as_rows(g_in)], axis=0).astype(BF16))
            s_ref[...] = s * tot_f + diag_blocks(_dot_tn((kb.astype(F32) * w_f).astype(BF16), v))

            mu = jnp.zeros_like(y)
            for hh in range(H):
                mh = jnp.sum(jnp.where(grp == hh, y, 0.0), axis=-1, keepdims=True) * (1.0 / P)
                mu = jnp.where(grp == hh, mh, mu)
            yc = y - mu
            var = jnp.zeros_like(y)
            for hh in range(H):
                vh = jnp.sum(jnp.where(grp == hh, yc * yc, 0.0), axis=-1, keepdims=True) * (1.0 / P)
                var = jnp.where(grp == hh, vh, var)
            y_ref[0, r0:r0 + C, :] = (yc * lax.rsqrt(var + EPS) * _silu(gate)).astype(BF16)

        @pl.when(c == nblk - 1)
        def _():
            fin_ref[0, 0] = s_ref[...]


def _block_index(nblk):
    return lambda i, ph, c: (i, (1 - ph) * (nblk - 1 - c) + ph * c, 0)


def _scan_sub(t):
    nc = t // CHUNK
    return 4 if nc % 4 == 0 else (2 if nc % 2 == 0 else 1)


def _ret_scan(ret, lw, init):
    b, t, _ = ret.shape
    sub = _scan_sub(t)
    rows = sub * CHUNK
    nblk = t // rows
    hp = RET_HEADS * RET_V
    return pl.pallas_call(
        functools.partial(_ret_kernel, nblk=nblk, sub=sub),
        grid=(b, 2, nblk),
        in_specs=[pl.BlockSpec((1, rows, W_RET), _block_index(nblk)),
                  pl.BlockSpec((2, hp), lambda i, ph, c: (0, 0)),
                  pl.BlockSpec((2 * RET_HEADS, CHUNK), lambda i, ph, c: (0, 0)),
                  pl.BlockSpec((1, 2, RET_QK, hp), lambda i, ph, c: (i, 0, 0, 0))],
        out_specs=[pl.BlockSpec((1, rows, hp), lambda i, ph, c: (i, ph * c, 0)),
                   pl.BlockSpec((1, 2, RET_QK, hp), lambda i, ph, c: (i, 0, 0, 0))],
        out_shape=[jax.ShapeDtypeStruct((b, t, hp), BF16),
                   jax.ShapeDtypeStruct((b, 2, RET_QK, hp), F32)],
        scratch_shapes=[pltpu.VMEM((RET_QK, hp), F32),
                        pltpu.VMEM((RET_QK, hp), F32),
                        pltpu.VMEM((t // CHUNK, RET_QK, hp), F32)],
        compiler_params=_cparams(("parallel", "arbitrary", "arbitrary")),
        name="retention_scan",
    )(ret, lw["ret_lgl"], lw["ret_lgh"], init)


def _ssd_kernel(x_ref, xp_ref, xn_ref, z_ref, dt_ref, cw_ref, cb_ref, dtb_ref, alog_ref, dvec_ref, nw_ref,
                init_ref, y_ref, fin_ref, s_ref, g_ref, gbuf_ref, *, nblk, sub):
    C, H, N, P, G = CHUNK, SSD_HEADS, SSD_STATE, SSD_HEADDIM, SSD_GROUPS
    HP = H * P
    HPG = H // G
    HALO = SSD_HALO
    ph, c = pl.program_id(1), pl.program_id(2)
    blk = jnp.where(ph == 0, nblk - 1 - c, c)

    ri = _iota((C, C + 2 * HALO), 0)
    ci = _iota((C, C + 2 * HALO), 1)
    sel_prev = (ci == ri + HALO - 1).astype(BF16)
    sel_next = (ci == ri + HALO + 1).astype(BF16)
    lane = _iota((LANES, HP), 0)
    head_of = _iota((LANES, HP), 1) // P
    e_f = (lane == head_of).astype(BF16)
    e_b = (lane == head_of + H).astype(BF16)
    r_i = _iota((C, C), 0)
    c_i = _iota((C, C), 1)
    low = (r_i >= c_i).astype(BF16)
    upp = (r_i <= c_i).astype(BF16)
    grp_n = _iota((N, HP), 1) // (HPG * P)
    neg_a = -jnp.exp(alog_ref[...])
    cw = cw_ref[...]

    def prep(sb):
        r0 = sb * C
        xc = x_ref[0, r0:r0 + C, :]
        if sb == 0:
            prev = jnp.where(blk > 0, xp_ref[0], jnp.zeros_like(xp_ref[0]))
        else:
            prev = x_ref[0, r0 - HALO:r0, :]
        if sb == sub - 1:
            nxt = jnp.where(blk < nblk - 1, xn_ref[0], jnp.zeros_like(xn_ref[0]))
        else:
            nxt = x_ref[0, r0 + C:r0 + C + HALO, :]
        stacked = jnp.concatenate([prev, xc, nxt], axis=0)
        conv = (cw[0:1] * _dot(sel_prev, stacked) + cw[1:2] * xc.astype(F32)
                + cw[2:3] * _dot(sel_next, stacked) + cb_ref[...])
        act = _silu(conv)
        xs = act[:, 0:HP]
        bm = act[:, HP:HP + G * N].astype(BF16)
        cm = act[:, HP + G * N:HP + 2 * G * N].astype(BF16)
        raw = dt_ref[0, r0:r0 + C, :] + dtb_ref[...]
        dt = jnp.maximum(raw, 0.0) + jnp.log(1.0 + jnp.exp(-jnp.abs(raw)))
        la = dt * neg_a
        a_f = _dot_exact_lhs01(low, la)
        a_b = _dot_exact_lhs01(upp, la)
        return xs, bm, cm, dt, la, a_f, a_b

    def state_update(state, bm, xs, wdt, tot8, e_dir, row):
        xw = (xs * _dot(wdt.astype(BF16), e_dir)).astype(BF16)
        tot = _dot_exact_rhs01(tot8, e_dir)[row:row + 1]
        r = _dot_tn(bm, xw)
        upd = r[0:N]
        for gg in range(1, G):
            upd = jnp.where(grp_n == gg, r[gg * N:(gg + 1) * N], upd)
        return state * tot + upd

    def as_rows(s_c):
        return jnp.concatenate([jnp.where(grp_n == gg, s_c, 0.0) for gg in range(G)], axis=0).astype(BF16)

    @pl.when((ph == 0) & (c == 0))
    def _():
        g_ref[...] = init_ref[0, 1]

    @pl.when(ph == 0)
    def _():
        for sb in reversed(range(sub)):
            xs, bm, _, dt, _, _, a_b = prep(sb)
            g = g_ref[...]
            gbuf_ref[blk * sub + sb] = g
            wdt = jnp.exp(a_b[0:1, :] - a_b) * dt
            g_ref[...] = state_update(g, bm, xs, wdt, jnp.exp(a_b[0:8, :]), e_b, 0)

        @pl.when(c == nblk - 1)
        def _():
            fin_ref[0, 1] = g_ref[...]

    @pl.when((ph == 1) & (c == 0))
    def _():
        s_ref[...] = init_ref[0, 0]

    @pl.when(ph == 1)
    def _():
        grp_l = _iota((C, G * N), 1) // N
        head_l = _iota((C, HP), 1) // P
        tri_f = r_i >= c_i
        tri_b = r_i <= c_i
        for sb in range(sub):
            r0 = sb * C
            xs, bm, cm, dt, la, a_f, a_b = prep(sb)
            la_t = la.T[0:2 * H]
            dt_t = dt.T[0:2 * H]
            af_row = _dot_exact_rhs01(la_t, upp)
            ab_row = _dot_exact_rhs01(la_t, low)
            sc = [_dot_nt(jnp.where(grp_l == gg, cm, jnp.zeros_like(cm)), bm) for gg in range(G)]
            xsb = xs.astype(BF16)
            ws, vs = [], []
            for hh in range(H):
                seg_f = a_f[:, hh:hh + 1] - af_row[hh:hh + 1, :]
                seg_b = a_b[:, H + hh:H + hh + 1] - ab_row[H + hh:H + hh + 1, :]
                dm = (jnp.exp(jnp.where(tri_f, seg_f, NEG)) * dt_t[hh:hh + 1, :]
                      + jnp.exp(jnp.where(tri_b, seg_b, NEG)) * dt_t[H + hh:H + hh + 1, :])
                ws.append((sc[hh // HPG] * dm).astype(BF16))
                vs.append(jnp.where(head_l == hh, xsb, jnp.zeros_like(xsb)))
            y = _dot(jnp.concatenate(ws, axis=1), jnp.concatenate(vs, axis=0))

            s = s_ref[...]
            g_in = gbuf_ref[blk * sub + sb]
            y = y + _dot(jnp.exp(a_f).astype(BF16), e_f) * _dot(cm, as_rows(s))
            y = y + _dot(jnp.exp(a_b).astype(BF16), e_b) * _dot(cm, as_rows(g_in))

            wdt = jnp.exp(a_f[C - 1:C, :] - a_f) * dt
            s_ref[...] = state_update(s, bm, xs, wdt, jnp.exp(a_f[C - 8:C, :]), e_f, 7)

            y = y + xs * dvec_ref[...]
            y = y * _silu(z_ref[0, r0:r0 + C, :].astype(F32))
            y_ref[0, r0:r0 + C, :] = _rms(y, nw_ref[...]).astype(BF16)

        @pl.when(c == nblk - 1)
        def _():
            fin_ref[0, 0] = s_ref[...]


SSD_HALO = 16


def _ssd_scan(xbc, z, dt, lw, init):
    b, t, _ = xbc.shape
    sub = _scan_sub(t)
    rows = sub * CHUNK
    nblk = t // rows
    hp = SSD_INNER
    per = rows // SSD_HALO
    nhalo = t // SSD_HALO
    bidx = _block_index(nblk)

    def prev_idx(i, ph, c):
        blk = (1 - ph) * (nblk - 1 - c) + ph * c
        return (i, jnp.maximum(blk * per - 1, 0), 0)

    def next_idx(i, ph, c):
        blk = (1 - ph) * (nblk - 1 - c) + ph * c
        return (i, jnp.minimum((blk + 1) * per, nhalo - 1), 0)

    const = lambda shape: pl.BlockSpec(shape, lambda i, ph, c: (0,) * len(shape))
    return pl.pallas_call(
        functools.partial(_ssd_kernel, nblk=nblk, sub=sub),
        grid=(b, 2, nblk),
        in_specs=[pl.BlockSpec((1, rows, W_XBC), bidx),
                  pl.BlockSpec((1, SSD_HALO, W_XBC), prev_idx),
                  pl.BlockSpec((1, SSD_HALO, W_XBC), next_idx),
                  pl.BlockSpec((1, rows, W_Z), bidx),
                  pl.BlockSpec((1, rows, W_DT), bidx),
                  const((8, W_XBC)), const((1, W_XBC)), const((1, LANES)), const((1, LANES)),
                  const((1, hp)), const((1, hp)),
                  pl.BlockSpec((1, 2, SSD_STATE, hp), lambda i, ph, c: (i, 0, 0, 0))],
        out_specs=[pl.BlockSpec((1, rows, hp), lambda i, ph, c: (i, ph * c, 0)),
                   pl.BlockSpec((1, 2, SSD_STATE, hp), lambda i, ph, c: (i, 0, 0, 0))],
        out_shape=[jax.ShapeDtypeStruct((b, t, hp), BF16),
                   jax.ShapeDtypeStruct((b, 2, SSD_STATE, hp), F32)],
        scratch_shapes=[pltpu.VMEM((SSD_STATE, hp), F32),
                        pltpu.VMEM((SSD_STATE, hp), F32),
                        pltpu.VMEM((t // CHUNK, SSD_STATE, hp), F32)],
        compiler_params=_cparams(("parallel", "arbitrary", "arbitrary")),
        name="ssd_scan",
    )(xbc, xbc, xbc, z, dt, lw["conv_w"], lw["conv_b"], lw["dt_bias"], lw["a_log"], lw["d_vec"], lw["ssd_nw"], init)


def _mix_residual(x_ref, att_ref, ret_ref, ssd_ref, mod_ref, n2_ref, wo_ref):
    mix = jnp.concatenate([att_ref[0], ret_ref[0], ssd_ref[0]], axis=-1)
    x1 = x_ref[0] + mod_ref[0, 2:3, :] * _dot(mix, wo_ref[...])
    h = _rms(x1, n2_ref[...]) * (1.0 + mod_ref[0, 4:5, :]) + mod_ref[0, 3:4, :]
    return x1, h


def _dense_kernel(x_ref, att_ref, ret_ref, ssd_ref, mod_ref, n2_ref, wo_ref, wg_ref, wu_ref, wd_ref, o_ref,
                  *, ff_chunk):
    x1, h = _mix_residual(x_ref, att_ref, ret_ref, ssd_ref, mod_ref, n2_ref, wo_ref)
    hb = h.astype(BF16)
    ff = wg_ref.shape[1]
    f = jnp.zeros_like(x1)
    for s in range(0, ff, ff_chunk):
        g = _dot(hb, wg_ref[:, s:s + ff_chunk])
        u = _dot(hb, wu_ref[:, s:s + ff_chunk])
        f = f + _dot((_silu(g) * u).astype(BF16), wd_ref[s:s + ff_chunk, :])
    o_ref[0] = x1 + mod_ref[0, 5:6, :] * f


def _dense_block(xs, att, ret, ssd, mods, lw, tm):
    b, t, d = xs.shape
    ff = lw["wg"].shape[1]
    tok = lambda w: pl.BlockSpec((1, tm, w), lambda i, j: (i, j, 0))
    const = lambda shape: pl.BlockSpec(shape, lambda i, j: (0,) * len(shape), pipeline_mode=pl.Buffered(1))
    return pl.pallas_call(
        functools.partial(_dense_kernel, ff_chunk=ff // 2),
        grid=(b, t // tm),
        in_specs=[tok(d), tok(att.shape[-1]), tok(ret.shape[-1]), tok(ssd.shape[-1]),
                  pl.BlockSpec((1, 8, d), lambda i, j: (i, 0, 0)), const((1, d)),
                  const((d, d)), const((d, ff)), const((d, ff)), const((ff, d))],
        out_specs=tok(d),
        out_shape=jax.ShapeDtypeStruct((b, t, d), F32),
        compiler_params=_cparams(("parallel", "parallel")),
        name="outproj_ffn",
    )(xs, att, ret, ssd, mods, lw["n2"], lw["wo"], lw["wg"], lw["wu"], lw["wd"])


SUBLANES = 8
TOP_K = 2


def _route_kernel(x_ref, att_ref, ret_ref, ssd_ref, mod_ref, n2_ref, wo_ref, wr_ref,
                  x1_ref, h3_ref, slot_ref, gate_ref, cnt_ref):
    x1, h = _mix_residual(x_ref, att_ref, ret_ref, ssd_ref, mod_ref, n2_ref, wo_ref)
    x1_ref[0] = x1
    tm, d = h.shape
    nsub = d // LANES
    for s in range(nsub):
        h3_ref[0, pl.ds(s, tm, stride=nsub), :] = h[:, s * LANES:(s + 1) * LANES]
    h_hi = h.astype(BF16)
    h_lo = (h - h_hi.astype(F32)).astype(BF16)
    logits = _dot(h_hi, wr_ref[0]) + _dot(h_lo, wr_ref[0]) + _dot(h_hi, wr_ref[1])
    lane = _iota(logits.shape, 1).astype(F32)
    logits = jnp.where(lane < N_EXPERTS, logits, NEG)
    m1 = jnp.max(logits, axis=-1, keepdims=True)
    i1 = jnp.min(jnp.where(logits == m1, lane, float(LANES)), axis=-1, keepdims=True)
    rest = jnp.where(lane == i1, NEG, logits)
    m2 = jnp.max(rest, axis=-1, keepdims=True)
    i2 = jnp.min(jnp.where(rest == m2, lane, float(LANES)), axis=-1, keepdims=True)
    e2 = jnp.exp(m2 - m1)
    den = 1.0 + e2
    sel1, sel2 = lane == i1, lane == i2
    chosen = jnp.where(sel1 | sel2, 1.0, 0.0)
    before = (_iota((tm, tm), 0) > _iota((tm, tm), 1)).astype(BF16)
    pos = _dot(before, chosen.astype(BF16))
    cnt = jnp.sum(chosen, axis=0, keepdims=True)
    nblk = sum(jnp.where(cnt > float(k * MOE_ROWS), 1.0, 0.0) for k in range(pl.cdiv(tm, MOE_ROWS)))
    lower_e = (_iota((LANES, LANES), 0) < _iota((LANES, LANES), 1)).astype(BF16)
    off = _dot_exact_rhs01(jnp.broadcast_to(nblk * float(MOE_ROWS), (SUBLANES, LANES)), lower_e)[0:1]
    slot = off + pos
    s1 = jnp.sum(jnp.where(sel1, slot, 0.0), axis=-1, keepdims=True)
    s2 = jnp.sum(jnp.where(sel2, slot, 0.0), axis=-1, keepdims=True)
    lane_i = _iota(logits.shape, 1)
    slots = jnp.where(lane_i == 0, s1, jnp.where(lane_i == 1, s2, 0.0))
    slot_ref[0, 0] = slots.T[0:TOP_K].astype(jnp.int32)
    gate_ref[0] = jnp.where(lane_i == 0, 1.0 / den, jnp.where(lane_i == 1, e2 / den, 0.0))
    row = _iota((SUBLANES, LANES), 0)
    cnt_ref[0, 0] = jnp.where(row == 0, cnt, jnp.where(row == 1, off, jnp.where(row == 2, nblk, 0.0))
                              ).astype(jnp.int32)


def _route_block(xs, att, ret, ssd, mods, lw, tm):
    b, t, d = xs.shape
    nt = t // tm
    nsub = d // LANES
    tok = lambda w: pl.BlockSpec((1, tm, w), lambda i, j: (i, j, 0))
    const = lambda shape: pl.BlockSpec(shape, lambda i, j: (0,) * len(shape))
    per_tile = lambda r, w: pl.BlockSpec((1, 1, r, w), lambda i, j: (i, j, 0, 0))
    return pl.pallas_call(
        _route_kernel,
        grid=(b, nt),
        in_specs=[tok(d), tok(att.shape[-1]), tok(ret.shape[-1]), tok(ssd.shape[-1]),
                  pl.BlockSpec((1, 8, d), lambda i, j: (i, 0, 0)), const((1, d)),
                  const((d, d)), const((2, d, LANES))],
        out_specs=[tok(d), pl.BlockSpec((1, tm * nsub, LANES), lambda i, j: (i, j, 0)),
                   per_tile(TOP_K, tm), tok(LANES), per_tile(SUBLANES, LANES)],
        out_shape=[jax.ShapeDtypeStruct((b, t, d), F32), jax.ShapeDtypeStruct((b, t * nsub, LANES), F32),
                   jax.ShapeDtypeStruct((b, nt, TOP_K, tm), jnp.int32),
                   jax.ShapeDtypeStruct((b, t, LANES), F32),
                   jax.ShapeDtypeStruct((b, nt, SUBLANES, LANES), jnp.int32)],
        compiler_params=_cparams(("parallel", "parallel")),
        name="outproj_route",
    )(xs, att, ret, ssd, mods, lw["n2"], lw["wo"], lw["router"])


MOE_ROWS = 320


def _moe_kernel(slot_ref, cnt_ref, h3_ref, wgu_ref, wd_ref, o3_ref, code_ref, g_ref, y3_ref, *, tm, ff_split):
    e = pl.program_id(2)
    rb = MOE_ROWS
    nsub = SUBLANES
    spare = TOP_K * tm

    @pl.when(e == 0)
    def _():
        def invert(t, carry):
            for k in range(TOP_K):
                code_ref[slot_ref[0, 0, k, t]] = k * tm + t
            return carry

        lax.fori_loop(0, tm, invert, 0, unroll=8)

    n = cnt_ref[0, 0, 0, e]
    off = cnt_ref[0, 0, 1, e]
    nblocks = cnt_ref[0, 0, 2, e]

    def pad_fill(s, carry):
        code_ref[s] = spare
        return carry

    lax.fori_loop(off + n, off + nblocks * rb, pad_fill, 0)

    def tile_rows(i):
        return pl.ds(pl.multiple_of(i * nsub, nsub), nsub)

    def block(bi, carry):
        base = off + bi * rb

        def gather(i, c):
            t = code_ref[base + i] & (tm - 1)
            g_ref[tile_rows(i), :] = h3_ref[0, tile_rows(t), :]
            return c

        lax.fori_loop(0, rb, gather, 0, unroll=8)
        lhs = jnp.concatenate([g_ref[pl.ds(s, rb, stride=nsub), :] for s in range(nsub)], axis=1).astype(BF16)
        y = None
        for s0, s1 in ff_split:
            w = s1 - s0
            gu = _dot(lhs, wgu_ref[0, :, 2 * s0:2 * s1])
            part = _dot((_silu(gu[:, :w]) * gu[:, w:]).astype(BF16), wd_ref[0, s0:s1, :])
            y = part if y is None else y + part
        for s in range(nsub):
            y3_ref[pl.ds(s, rb, stride=nsub), :] = y[:, s * LANES:(s + 1) * LANES]

        def scatter(i, c):
            o3_ref[0, tile_rows(code_ref[base + i]), :] = y3_ref[tile_rows(i), :]
            return c

        lax.fori_loop(0, rb, scatter, 0, unroll=8)
        return carry

    lax.fori_loop(0, nblocks, block, 0)

    @pl.when(e == N_EXPERTS - 1)
    def _():
        o3_ref[0, spare * nsub:, :] = jnp.zeros((o3_ref.shape[1] - spare * nsub, LANES), F32)


def _moe_out_kernel(x1_ref, a3_ref, gate_ref, mod_ref, fn_ref, o_ref, *, final_norm):
    tm, d = x1_ref.shape[1:]
    nsub = d // LANES
    gates = gate_ref[0]
    acc = None
    for k in range(TOP_K):
        yk = jnp.concatenate([a3_ref[0, pl.ds(k * tm * nsub + s, tm, stride=nsub), :] for s in range(nsub)], axis=1)
        term = gates[:, k:k + 1] * yk
        acc = term if acc is None else acc + term
    out = x1_ref[0] + mod_ref[0, 5:6, :] * acc
    o_ref[0] = _rms(out, fn_ref[...]) if final_norm else out


MOE_FF_SLICE = 512


def _moe_ff_split(ff):
    return tuple((s, min(s + MOE_FF_SLICE, ff)) for s in range(0, ff, MOE_FF_SLICE))


def _moe_weights(w_gate, w_up, w_down):
    ff = w_gate.shape[-1]
    wgu = jnp.concatenate([w[..., s0:s1] for s0, s1 in _moe_ff_split(ff) for w in (w_gate, w_up)], axis=-1)
    return wgu.astype(BF16), w_down.astype(BF16)


def _moe_block(x1, h3, slots, gates, counts, mods, lw, fnw, final_norm, tm):
    b, t, d = x1.shape
    ff = lw["moe_wd"].shape[1]
    nsub = d // LANES
    assert nsub == SUBLANES and MOE_ROWS % (2 * SUBLANES) == 0 and tm & (tm - 1) == 0
    nt = t // tm
    out_rows = (TOP_K * tm + 1) * nsub
    smem = lambda r, w: pl.BlockSpec((1, 1, r, w), lambda i, j, e: (i, j, 0, 0), memory_space=pltpu.SMEM)
    y3 = pl.pallas_call(
        functools.partial(_moe_kernel, tm=tm, ff_split=_moe_ff_split(ff)),
        grid=(b, nt, N_EXPERTS),
        in_specs=[smem(TOP_K, tm), smem(SUBLANES, LANES),
                  pl.BlockSpec((1, tm * nsub, LANES), lambda i, j, e: (i, j, 0)),
                  pl.BlockSpec((1, d, 2 * ff), lambda i, j, e: (e, 0, 0)),
                  pl.BlockSpec((1, ff, d), lambda i, j, e: (e, 0, 0))],
        out_specs=pl.BlockSpec((1, out_rows, LANES), lambda i, j, e: (i, j, 0)),
        out_shape=jax.ShapeDtypeStruct((b, nt * out_rows, LANES), F32),
        scratch_shapes=[pltpu.SMEM((TOP_K * tm + N_EXPERTS * MOE_ROWS,), jnp.int32),
                        pltpu.VMEM((MOE_ROWS * nsub, LANES), F32), pltpu.VMEM((MOE_ROWS * nsub, LANES), F32)],
        compiler_params=_cparams(("parallel", "parallel", "arbitrary")),
        name="moe",
    )(slots, counts, h3, lw["moe_wgu"], lw["moe_wd"])
    tok = lambda w: pl.BlockSpec((1, tm, w), lambda i, j: (i, j, 0))
    return pl.pallas_call(
        functools.partial(_moe_out_kernel, final_norm=final_norm),
        grid=(b, nt),
        in_specs=[tok(d), pl.BlockSpec((1, out_rows, LANES), lambda i, j: (i, j, 0)), tok(LANES),
                  pl.BlockSpec((1, 8, d), lambda i, j: (i, 0, 0)), pl.BlockSpec((1, d), lambda i, j: (0, 0))],
        out_specs=tok(d),
        out_shape=jax.ShapeDtypeStruct((b, t, d), F32),
        compiler_params=_cparams(("parallel", "parallel")),
        name="moe_out",
    )(x1, y3, gates, mods, fnw)


def _final_norm_kernel(x_ref, w_ref, o_ref):
    o_ref[0] = _rms(x_ref[0], w_ref[...])


def _final_norm(xs, fnw, tm):
    b, t, d = xs.shape
    return pl.pallas_call(
        _final_norm_kernel,
        grid=(b, t // tm),
        in_specs=[pl.BlockSpec((1, tm, d), lambda i, j: (i, j, 0)), pl.BlockSpec((1, d), lambda i, j: (0, 0))],
        out_specs=pl.BlockSpec((1, tm, d), lambda i, j: (i, j, 0)),
        out_shape=jax.ShapeDtypeStruct((b, t, d), F32),
        compiler_params=_cparams(("parallel", "parallel")),
        name="final_norm",
    )(xs, fnw)


def _rot_cols(w):
    a, b2, c, d = jnp.split(w, 4, axis=-1)
    return jnp.concatenate([-b2, a, -d, c], axis=-1)


def _layer_weights(i, norm1_w, norm2_w, w_in, w_out, mla_q_norm_w, mla_w_qb, mla_kv_norm_w, mla_w_kb, mla_w_vb,
                   ret_decay_logit, ssd_conv_w, ssd_conv_b, ssd_dt_bias, ssd_a_log, ssd_d, ssd_norm_w):
    d = w_in.shape[1]
    offs = np.cumsum((0,) + IN_SIZES)
    seg = [w_in[i][:, offs[j]:offs[j + 1]] for j in range(len(IN_SIZES))]
    zeros = lambda n: jnp.zeros((d, n), F32)
    w_all = jnp.concatenate(
        [seg[0], seg[1], seg[2], _rot_cols(seg[2]), zeros(W_MLA - MLA_Q_LORA - MLA_KV_LORA - 2 * MLA_ROPE),
         seg[3] * (RET_QK ** -0.5), seg[4], seg[5], seg[6], seg[7], seg[8], seg[9], seg[10],
         zeros(W_DT - 2 * SSD_HEADS)], axis=1).astype(BF16)

    dh = MLA_NOPE + MLA_ROPE
    wqb = mla_w_qb[i].reshape(MLA_Q_LORA, MLA_HEADS, dh)
    pad = lambda a, lo, hi: jnp.pad(a, ((0, 0), (0, 0), (lo, hi)))
    wq = pad(wqb, 0, MLA_HP - dh).reshape(MLA_Q_LORA, MLA_HEADS * MLA_HP).astype(BF16)
    wqr = pad(_rot_cols(wqb[:, :, MLA_NOPE:]), MLA_NOPE, MLA_HP - dh).reshape(MLA_Q_LORA, MLA_HEADS * MLA_HP).astype(BF16)
    wkb = pad(mla_w_kb[i].reshape(MLA_KV_LORA, MLA_HEADS, MLA_NOPE), 0, MLA_HP - MLA_NOPE)
    place = np.zeros((MLA_ROPE, MLA_HEADS, MLA_HP), np.float32)
    for r in range(MLA_ROPE):
        place[r, :, MLA_NOPE + r] = 1.0
    place = jnp.asarray(place)
    wk = jnp.concatenate([wkb, place, place, jnp.zeros((MLA_KV_LORA - 2 * MLA_ROPE, MLA_HEADS, MLA_HP), F32)],
                         axis=0).reshape(2 * MLA_KV_LORA, MLA_HEADS * MLA_HP).astype(BF16)

    wvb = mla_w_vb[i].reshape(MLA_KV_LORA, MLA_HEADS, MLA_V)
    wv = jnp.concatenate(
        [pad(wvb[:, hh:hh + 1], (hh % 2) * MLA_V, MLA_HP - MLA_V - (hh % 2) * MLA_V) for hh in range(MLA_HEADS)],
        axis=1).reshape(MLA_KV_LORA, MLA_HEADS * MLA_HP).astype(BF16)

    lg = ret_decay_logit[i]
    padl = lambda v: jnp.pad(v.reshape(1, -1), ((0, 0), (0, LANES - 2 * SSD_HEADS)))
    return {
        "n1": norm1_w[i].reshape(1, d), "n2": norm2_w[i].reshape(1, d),
        "w_all": w_all, "qn": mla_q_norm_w[i].reshape(1, -1), "wq": wq, "wqr": wqr,
        "kvn": mla_kv_norm_w[i].reshape(1, -1), "wk": wk, "wv": wv,
        "wo": w_out[i].astype(BF16),
        "ret_lgl": jnp.repeat(lg, RET_V, axis=1),
        "ret_lgh": jnp.broadcast_to(lg.reshape(2 * RET_HEADS, 1), (2 * RET_HEADS, CHUNK)),
        "conv_w": jnp.pad(ssd_conv_w[i], ((0, 8 - ssd_conv_w.shape[1]), (0, 0))),
        "conv_b": ssd_conv_b[i].reshape(1, -1),
        "dt_bias": padl(ssd_dt_bias[i]), "a_log": padl(ssd_a_log[i]),
        "d_vec": jnp.repeat(ssd_d[i], SSD_HEADDIM).reshape(1, -1),
        "ssd_nw": ssd_norm_w[i].reshape(1, -1),
    }


def _rope_tables(n_lat, n_ctx):
    rows = n_lat // GRID_W
    row = jnp.repeat(jnp.arange(rows, dtype=F32), GRID_W)
    col = jnp.tile(jnp.arange(GRID_W, dtype=F32), rows)
    inv_freq = ROPE_BASE ** (-jnp.arange(0, ROPE_AXIS_DIM, 2, dtype=F32) / ROPE_AXIS_DIM)
    ang_r = row[:, None] * inv_freq
    ang_c = col[:, None] * inv_freq
    ang = jnp.concatenate([ang_r, ang_r, ang_c, ang_c], axis=-1)
    c2 = MLA_SCALE * math.log2(math.e)

    def build(cos, sin):
        n = cos.shape[0]
        cosq = jnp.concatenate([jnp.full((n, MLA_NOPE), c2, F32), c2 * cos, jnp.zeros((n, MLA_HP - MLA_NOPE - MLA_ROPE), F32)], 1)
        sinq = jnp.concatenate([jnp.zeros((n, MLA_NOPE), F32), c2 * sin, jnp.zeros((n, MLA_HP - MLA_NOPE - MLA_ROPE), F32)], 1)
        ck = jnp.concatenate([cos, sin, jnp.zeros((n, LANES - 2 * MLA_ROPE), F32)], 1)
        return cosq, sinq, ck

    lat = build(jnp.cos(ang), jnp.sin(ang))
    ctx = build(jnp.ones((n_ctx, MLA_ROPE), F32), jnp.zeros((n_ctx, MLA_ROPE), F32))
    return lat, ctx


def _token_tile(t, pref):
    tm = min(pref, t)
    while t % tm:
        tm //= 2
    return tm


def kernel(x, c, ctx, c_ctx, mod_w, mod_b, norm1_w, norm2_w, w_in, w_out, mla_q_norm_w, mla_w_qb, mla_kv_norm_w, mla_w_kb, mla_w_vb, ret_decay_logit, ssd_conv_w, ssd_conv_b, ssd_dt_bias, ssd_a_log, ssd_d, ssd_norm_w, ffn_w_gate, ffn_w_up, ffn_w_down, moe_router, moe_w_gate, moe_w_up, moe_w_down, final_norm_w):
    b, n_lat, d = x.shape
    n_ctx = ctx.shape[1]
    depth = mod_w.shape[0]
    assert b + 1 <= 8 and n_lat % CHUNK == 0 and n_ctx % CHUNK == 0 and n_lat % GRID_W == 0

    cond = jnp.concatenate([c, c_ctx[None, :], jnp.zeros((8 - b - 1, d), F32)], axis=0)
    mod_all = _modulation(cond, mod_w, mod_b)
    tabs_x, tabs_c = _rope_tables(n_lat, n_ctx)
    fnw = final_norm_w.reshape(1, d)

    tm_x = _token_tile(n_lat, 512)
    tm_c = _token_tile(n_ctx, 256)
    tm_moe = _token_tile(n_lat, 1024)
    tq_x = _token_tile(n_lat, 1024)
    tk_x = 2048
    zero_ret = jnp.zeros((b, 2, RET_QK, RET_HEADS * RET_V), F32)
    zero_ssd = jnp.zeros((b, 2, SSD_STATE, SSD_INNER), F32)

    h_ctx = ctx
    for i in range(depth):
        last = i == depth - 1
        lw = _layer_weights(i, norm1_w, norm2_w, w_in, w_out, mla_q_norm_w, mla_w_qb, mla_kv_norm_w, mla_w_kb,
                            mla_w_vb, ret_decay_logit, ssd_conv_w, ssd_conv_b, ssd_dt_bias, ssd_a_log, ssd_d,
                            ssd_norm_w)
        m6 = mod_all[i].reshape(8, 6, d)
        pad2 = jnp.zeros((b, 2, d), F32)
        mods_x = jnp.concatenate([m6[:b], pad2], axis=1)
        mods_c = jnp.concatenate([jnp.broadcast_to(m6[b:b + 1], (b, 6, d)), pad2], axis=1)

        qc, kc, vc, retc, zc, xbcc, dtc = _inproj(h_ctx, mods_c, lw, tabs_c, tm_c)
        ret_yc, ret_fin = _ret_scan(retc, lw, zero_ret)
        ssd_yc, ssd_fin = _ssd_scan(xbcc, zc, dtc, lw, zero_ssd)

        qx, kx, vx, retx, zx, xbcx, dtx = _inproj(x, mods_x, lw, tabs_x, tm_x)
        att_x = _attention(qx, [(kc, vc), (kx, vx)], tq_x, tk_x)
        ret_yx, _ = _ret_scan(retx, lw, ret_fin)
        ssd_yx, _ = _ssd_scan(xbcx, zx, dtx, lw, ssd_fin)

        if i % 2 == 0:
            j = i // 2
            lw.update(wg=ffn_w_gate[j].astype(BF16), wu=ffn_w_up[j].astype(BF16), wd=ffn_w_down[j].astype(BF16))
            x = _dense_block(x, att_x, ret_yx, ssd_yx, mods_x, lw, tm_x)
            if last:
                x = _final_norm(x, fnw, tm_x)
        else:
            j = i // 2
            wgu, wd = _moe_weights(moe_w_gate[j], moe_w_up[j], moe_w_down[j])
            r_f32 = jnp.pad(moe_router[j], ((0, 0), (0, LANES - N_EXPERTS)))
            r_hi = r_f32.astype(BF16)
            r_lo = (r_f32 - r_hi.astype(F32)).astype(BF16)
            lw.update(router=jnp.stack([r_hi, r_lo]), moe_wgu=wgu, moe_wd=wd)
            x1, *routed = _route_block(x, att_x, ret_yx, ssd_yx, mods_x, lw, tm_moe)
            x = _moe_block(x1, *routed, mods_x, lw, fnw, last, tm_moe)

        if not last:
            att_c = _attention(qc, [(kc, vc)], tm_c, tk_x)
            if i % 2 == 0:
                h_ctx = _dense_block(h_ctx, att_c, ret_yc, ssd_yc, mods_c, lw, tm_c)
            else:
                c1, *routed = _route_block(h_ctx, att_c, ret_yc, ssd_yc, mods_c, lw, tm_c)
                h_ctx = _moe_block(c1, *routed, mods_c, lw, fnw, False, tm_c)
    return x
```

```python
import functools
import math

import jax
import jax.numpy as jnp
import numpy as np
from jax import lax
from jax.experimental import pallas as pl
from jax.experimental.pallas import tpu as pltpu

F32 = jnp.float32
BF16 = jnp.bfloat16
HIGHEST = lax.Precision.HIGHEST

EPS = 1e-6
CHUNK = 128
GRID_W = 64
MLA_HEADS = 4
MLA_Q_LORA = 256
MLA_KV_LORA = 128
MLA_NOPE = 64
MLA_ROPE = 32
MLA_V = 64
MLA_SCALE = (MLA_NOPE + MLA_ROPE) ** -0.5
ROPE_AXIS_DIM = MLA_ROPE // 2
ROPE_BASE = 10000.0
RET_HEADS = 4
RET_QK = 64
RET_V = 64
SSD_HEADS = 8
SSD_HEADDIM = 64
SSD_GROUPS = 2
SSD_STATE = 64
SSD_INNER = SSD_HEADS * SSD_HEADDIM
SSD_CONV_CH = SSD_INNER + 2 * SSD_GROUPS * SSD_STATE
N_EXPERTS = 8
IN_SIZES = (MLA_Q_LORA, MLA_KV_LORA, MLA_ROPE,
            RET_HEADS * RET_QK, RET_HEADS * RET_QK, RET_HEADS * RET_V, RET_HEADS * RET_V,
            SSD_INNER, SSD_CONV_CH, SSD_HEADS, SSD_HEADS)

LANES = 128
V7X_VMEM_LIMIT = 56 * 1024 * 1024

MLA_HP = 128
NEG = -1e30


def _cparams(sem):
    return pltpu.CompilerParams(dimension_semantics=sem, vmem_limit_bytes=V7X_VMEM_LIMIT)


def _dot(a, b):
    return jnp.dot(a, b, preferred_element_type=F32)


def _dot_nt(a, b):
    return lax.dot_general(a, b, (((1,), (1,)), ((), ())), preferred_element_type=F32)


def _dot_tn(a, b):
    return lax.dot_general(a, b, (((0,), (0,)), ((), ())), preferred_element_type=F32)


def _split3(x):
    hi = x.astype(BF16)
    r1 = x - hi.astype(F32)
    mid = r1.astype(BF16)
    lo = (r1 - mid.astype(F32)).astype(BF16)
    return hi, mid, lo


def _dot_exact_rhs01(x, e):
    hi, mid, lo = _split3(x)
    return _dot(hi, e) + _dot(mid, e) + _dot(lo, e)


def _dot_exact_lhs01(e, x):
    hi, mid, lo = _split3(x)
    return _dot(e, hi) + _dot(e, mid) + _dot(e, lo)


def _silu(x):
    return x / (1.0 + jnp.exp(-x))


def _rms(x, w):
    return x * lax.rsqrt(jnp.mean(x * x, axis=-1, keepdims=True) + EPS) * w


def _iota(shape, dim):
    return lax.broadcasted_iota(jnp.int32, shape, dim)


def _mod_kernel(a_ref, w_ref, b_ref, o_ref):
    a = _silu(a_ref[...])
    o_ref[0] = jnp.dot(a, w_ref[0], precision=HIGHEST, preferred_element_type=F32) + b_ref[0]


def _modulation(cond, mod_w, mod_b):
    depth, d, n = mod_w.shape
    tn = 1536
    return pl.pallas_call(
        _mod_kernel,
        grid=(depth, n // tn),
        in_specs=[pl.BlockSpec((8, d), lambda i, j: (0, 0)),
                  pl.BlockSpec((1, d, tn), lambda i, j: (i, 0, j)),
                  pl.BlockSpec((1, 1, tn), lambda i, j: (i, 0, j))],
        out_specs=pl.BlockSpec((1, 8, tn), lambda i, j: (i, 0, j)),
        out_shape=jax.ShapeDtypeStruct((depth, 8, n), F32),
        compiler_params=_cparams(("parallel", "parallel")),
        name="modulation",
    )(cond, mod_w, mod_b.reshape(depth, 1, n))


W_MLA, W_RET, W_Z, W_XBC, W_DT = 512, 1024, 512, 768, 128
W_ALL = W_MLA + W_RET + W_Z + W_XBC + W_DT


def _ones_lane_mask(shape):
    lane = _iota(shape, len(shape) - 1) % (2 * MLA_HP)
    return (lane == MLA_HP - 1) | (lane == MLA_HP)


def _inproj_kernel(x_ref, mod_ref, n1_ref, win_ref, qn_ref, wq_ref, wqr_ref, kvn_ref, wk_ref, wv_ref,
                   cq_ref, sq_ref, ck_ref,
                   q_out, k_out, v_out, ret_out, z_out, xbc_out, dt_out):
    x = x_ref[0]
    h = _rms(x, n1_ref[...]) * (1.0 + mod_ref[0, 1:2, :]) + mod_ref[0, 0:1, :]
    hb = h.astype(BF16)

    o = 0
    pm = _dot(hb, win_ref[:, o:o + W_MLA]); o += W_MLA
    ret_out[0] = _dot(hb, win_ref[:, o:o + W_RET]).astype(BF16); o += W_RET
    z_out[0] = _dot(hb, win_ref[:, o:o + W_Z]).astype(BF16); o += W_Z
    xbc_out[0] = _dot(hb, win_ref[:, o:o + W_XBC]).astype(BF16); o += W_XBC
    dt_out[0] = _dot(hb, win_ref[:, o:o + W_DT])

    cqb = _rms(pm[:, 0:MLA_Q_LORA], qn_ref[...]).astype(BF16)
    qm = _dot(cqb, wq_ref[...])
    qr = _dot(cqb, wqr_ref[...])
    cosq, sinq = cq_ref[...], sq_ref[...]
    for hh in range(MLA_HEADS):
        sl = slice(hh * MLA_HP, (hh + 1) * MLA_HP)
        q_out[0, :, sl] = (qm[:, sl] * cosq + qr[:, sl] * sinq).astype(BF16)

    ckvn = _rms(pm[:, MLA_Q_LORA:MLA_Q_LORA + MLA_KV_LORA], kvn_ref[...])
    u = pm[:, MLA_Q_LORA + MLA_KV_LORA:W_MLA] * ck_ref[...]
    kin = jnp.concatenate([ckvn, u], axis=-1).astype(BF16)
    k_out[0] = _dot(kin, wk_ref[...]).astype(BF16)
    v = _dot(ckvn.astype(BF16), wv_ref[...])
    v_out[0] = jnp.where(_ones_lane_mask(v.shape), 1.0, v).astype(BF16)


def _inproj(xs, mods, lw, tabs, tm):
    b, t, d = xs.shape
    const = lambda shape: pl.BlockSpec(shape, lambda i, j: (0,) * len(shape))
    tok = lambda w: pl.BlockSpec((1, tm, w), lambda i, j: (i, j, 0))
    tab = pl.BlockSpec((tm, LANES), lambda i, j: (j, 0))
    widths = (MLA_HEADS * MLA_HP, MLA_HEADS * MLA_HP, MLA_HEADS * MLA_HP, W_RET, W_Z, W_XBC, W_DT)
    dtypes = (BF16, BF16, BF16, BF16, BF16, BF16, F32)
    return pl.pallas_call(
        _inproj_kernel,
        grid=(b, t // tm),
        in_specs=[tok(d), pl.BlockSpec((1, 8, d), lambda i, j: (i, 0, 0)), const((1, d)),
                  const((d, W_ALL)), const((1, MLA_Q_LORA)), const((MLA_Q_LORA, MLA_HEADS * MLA_HP)),
                  const((MLA_Q_LORA, MLA_HEADS * MLA_HP)), const((1, MLA_KV_LORA)),
                  const((2 * MLA_KV_LORA, MLA_HEADS * MLA_HP)), const((MLA_KV_LORA, MLA_HEADS * MLA_HP)),
                  tab, tab, tab],
        out_specs=[tok(w) for w in widths],
        out_shape=[jax.ShapeDtypeStruct((b, t, w), dt) for w, dt in zip(widths, dtypes)],
        compiler_params=_cparams(("parallel", "parallel")),
        name="inproj",
    )(xs, mods, lw["n1"], lw["w_all"], lw["qn"], lw["wq"], lw["wqr"], lw["kvn"], lw["wk"], lw["wv"], *tabs)


def _attn_kernel(*refs, seg_blocks, tq):
    nseg = len(seg_blocks)
    q_ref = refs[0]
    kv_refs = refs[1:1 + 2 * nseg]
    o_ref = refs[1 + 2 * nseg]
    carry = tuple((jnp.full((tq, 1), NEG, F32), jnp.zeros((tq, MLA_HP), F32)) for _ in range(MLA_HEADS))
    for si, (tk, nblk) in enumerate(seg_blocks):
        k_ref, v_ref = kv_refs[2 * si], kv_refs[2 * si + 1]

        def body(j, carry, k_ref=k_ref, v_ref=v_ref, tk=tk):
            start = 0 if isinstance(j, int) else pl.multiple_of(j * tk, tk)
            new = []
            for hh in range(MLA_HEADS):
                sl = slice(hh * MLA_HP, (hh + 1) * MLA_HP)
                m, acc = carry[hh]
                s = _dot_nt(q_ref[0, :, sl], k_ref[0, pl.ds(start, tk), sl])
                m_new = jnp.maximum(m, jnp.max(s, axis=-1, keepdims=True))
                p = jnp.exp2(s - m_new).astype(BF16)
                acc = jnp.exp2(m - m_new) * acc + _dot(p, v_ref[0, pl.ds(start, tk), sl])
                new.append((m_new, acc))
            return tuple(new)

        carry = body(0, carry) if nblk == 1 else lax.fori_loop(0, nblk, body, carry)
    lane = _iota((tq, MLA_HP), 1)
    pairs = []
    for hh in range(0, MLA_HEADS, 2):
        acc_e, acc_o = carry[hh][1], carry[hh + 1][1]
        l_e = jnp.sum(jnp.where(lane == MLA_HP - 1, acc_e, 0.0), axis=-1, keepdims=True)
        l_o = jnp.sum(jnp.where(lane == 0, acc_o, 0.0), axis=-1, keepdims=True)
        pairs.append(jnp.where(lane < MLA_V, acc_e / l_e, acc_o / l_o))
    o_ref[0] = jnp.concatenate(pairs, axis=-1).astype(BF16)


def _attention(q, segs, tq, tk):
    b, l, _ = q.shape
    seg_blocks = []
    in_specs = [pl.BlockSpec((1, tq, MLA_HEADS * MLA_HP), lambda i, j: (i, j, 0))]
    args = [q]
    for k, v in segs:
        s = k.shape[1]
        t = min(tk, s)
        seg_blocks.append((t, s // t))
        for _ in range(2):
            in_specs.append(pl.BlockSpec((1, s, MLA_HEADS * MLA_HP), lambda i, j: (i, 0, 0),
                                         pipeline_mode=pl.Buffered(1)))
        args += [k, v]
    return pl.pallas_call(
        functools.partial(_attn_kernel, seg_blocks=tuple(seg_blocks), tq=tq),
        grid=(b, l // tq),
        in_specs=in_specs,
        out_specs=pl.BlockSpec((1, tq, MLA_HEADS * MLA_V), lambda i, j: (i, j, 0)),
        out_shape=jax.ShapeDtypeStruct((b, l, MLA_HEADS * MLA_V), BF16),
        compiler_params=_cparams(("parallel", "arbitrary")),
        name="attention",
    )(*args)


def _ret_kernel(x_ref, lgl_ref, lgh_ref, init_ref, y_ref, fin_ref, s_ref, g_ref, gbuf_ref, *, nblk, sub):
    C, H, N, P = CHUNK, RET_HEADS, RET_QK, RET_V
    HP = H * P
    ph, c = pl.program_id(1), pl.program_id(2)
    blk = jnp.where(ph == 0, nblk - 1 - c, c)

    def log_sigmoid(v):
        return jnp.minimum(v, 0.0) - jnp.log(1.0 + jnp.exp(-jnp.abs(v)))

    lgl = log_sigmoid(lgl_ref[...])
    lgf, lgb = lgl[0:1], lgl[1:2]
    rows = _iota((C, HP), 0).astype(F32)
    head_n = _iota((N, HP), 1) // P

    def diag_blocks(r):
        out = r[0:N]
        for hh in range(1, H):
            out = jnp.where(head_n == hh, r[hh * N:(hh + 1) * N], out)
        return out

    def as_rows(s_c):
        return jnp.concatenate([jnp.where(head_n == hh, s_c, 0.0) for hh in range(H)], axis=0)

    @pl.when((ph == 0) & (c == 0))
    def _():
        g_ref[...] = init_ref[0, 1]

    @pl.when(ph == 0)
    def _():
        w_b = jnp.exp(rows * lgb)
        tot_b = jnp.exp(C * lgb)
        for sb in reversed(range(sub)):
            r0 = sb * C
            k = x_ref[0, r0:r0 + C, HP:2 * HP].astype(F32)
            v = x_ref[0, r0:r0 + C, 2 * HP:3 * HP]
            g = g_ref[...]
            gbuf_ref[blk * sub + sb] = g
            g_ref[...] = g * tot_b + diag_blocks(_dot_tn((k * w_b).astype(BF16), v))

        @pl.when(c == nblk - 1)
        def _():
            fin_ref[0, 1] = g_ref[...]

    @pl.when((ph == 1) & (c == 0))
    def _():
        s_ref[...] = init_ref[0, 0]

    @pl.when(ph == 1)
    def _():
        lgh = log_sigmoid(lgh_ref[...])
        grp = _iota((C, HP), 1) // P
        di = (_iota((C, C), 0) - _iota((C, C), 1)).astype(F32)
        decay = [jnp.exp(jnp.where(di >= 0, di * lgh[hh:hh + 1, :], -di * lgh[H + hh:H + hh + 1, :]))
                 for hh in range(H)]
        e_f = jnp.exp((rows + 1.0) * lgf)
        e_b = jnp.exp((C - rows) * lgb)
        w_f = jnp.exp((C - 1.0 - rows) * lgf)
        tot_f = jnp.exp(C * lgf)
        for sb in range(sub):
            r0 = sb * C
            q = x_ref[0, r0:r0 + C, 0:HP]
            kb = x_ref[0, r0:r0 + C, HP:2 * HP]
            v = x_ref[0, r0:r0 + C, 2 * HP:3 * HP]
            gate = x_ref[0, r0:r0 + C, 3 * HP:4 * HP].astype(F32)
            ws, vs = [], []
            for hh in range(H):
                sc = _dot_nt(jnp.where(grp == hh, q, jnp.zeros_like(q)), kb)
                ws.append((sc * decay[hh]).astype(BF16))
                vs.append(jnp.where(grp == hh, v, jnp.zeros_like(v)))
            y = _dot(jnp.concatenate(ws, axis=1), jnp.concatenate(vs, axis=0))
            qf32 = q.astype(F32)
            s = s_ref[...]
            g_in = gbuf_ref[blk * sub + sb]
            y = y + _dot(jnp.concatenate([(qf32 * e_f).astype(BF16), (qf32 * e_b).astype(BF16)], axis=1),
                         jnp.concatenate([as_rows(s), as_rows(g_in)], axis=0).astype(BF16))
            s_ref[...] = s * tot_f + diag_blocks(_dot_tn((kb.astype(F32) * w_f).astype(BF16), v))

            mu = jnp.zeros_like(y)
            for hh in range(H):
                mh = jnp.sum(jnp.where(grp == hh, y, 0.0), axis=-1, keepdims=True) * (1.0 / P)
                mu = jnp.where(grp == hh, mh, mu)
            yc = y - mu
            var = jnp.zeros_like(y)
            for hh in range(H):
                vh = jnp.sum(jnp.where(grp == hh, yc * yc, 0.0), axis=-1, keepdims=True) * (1.0 / P)
                var = jnp.where(grp == hh, vh, var)
            y_ref[0, r0:r0 + C, :] = (yc * lax.rsqrt(var + EPS) * _silu(gate)).astype(BF16)

        @pl.when(c == nblk - 1)
        def _():
            fin_ref[0, 0] = s_ref[...]


def _block_index(nblk):
    return lambda i, ph, c: (i, (1 - ph) * (nblk - 1 - c) + ph * c, 0)


def _scan_sub(t):
    nc = t // CHUNK
    return 4 if nc % 4 == 0 else (2 if nc % 2 == 0 else 1)


def _ret_scan(ret, lw, init):
    b, t, _ = ret.shape
    sub = _scan_sub(t)
    rows = sub * CHUNK
    nblk = t // rows
    hp = RET_HEADS * RET_V
    return pl.pallas_call(
        functools.partial(_ret_kernel, nblk=nblk, sub=sub),
        grid=(b, 2, nblk),
        in_specs=[pl.BlockSpec((1, rows, W_RET), _block_index(nblk)),
                  pl.BlockSpec((2, hp), lambda i, ph, c: (0, 0)),
                  pl.BlockSpec((2 * RET_HEADS, CHUNK), lambda i, ph, c: (0, 0)),
                  pl.BlockSpec((1, 2, RET_QK, hp), lambda i, ph, c: (i, 0, 0, 0))],
        out_specs=[pl.BlockSpec((1, rows, hp), lambda i, ph, c: (i, ph * c, 0)),
                   pl.BlockSpec((1, 2, RET_QK, hp), lambda i, ph, c: (i, 0, 0, 0))],
        out_shape=[jax.ShapeDtypeStruct((b, t, hp), BF16),
                   jax.ShapeDtypeStruct((b, 2, RET_QK, hp), F32)],
        scratch_shapes=[pltpu.VMEM((RET_QK, hp), F32),
                        pltpu.VMEM((RET_QK, hp), F32),
                        pltpu.VMEM((t // CHUNK, RET_QK, hp), F32)],
        compiler_params=_cparams(("parallel", "arbitrary", "arbitrary")),
        name="retention_scan",
    )(ret, lw["ret_lgl"], lw["ret_lgh"], init)


def _ssd_kernel(x_ref, xp_ref, xn_ref, z_ref, dt_ref, cw_ref, cb_ref, dtb_ref, alog_ref, dvec_ref, nw_ref,
                init_ref, y_ref, fin_ref, s_ref, g_ref, gbuf_ref, *, nblk, sub):
    C, H, N, P, G = CHUNK, SSD_HEADS, SSD_STATE, SSD_HEADDIM, SSD_GROUPS
    HP = H * P
    HPG = H // G
    HALO = SSD_HALO
    ph, c = pl.program_id(1), pl.program_id(2)
    blk = jnp.where(ph == 0, nblk - 1 - c, c)

    ri = _iota((C, C + 2 * HALO), 0)
    ci = _iota((C, C + 2 * HALO), 1)
    sel_prev = (ci == ri + HALO - 1).astype(BF16)
    sel_next = (ci == ri + HALO + 1).astype(BF16)
    lane = _iota((LANES, HP), 0)
    head_of = _iota((LANES, HP), 1) // P
    e_f = (lane == head_of).astype(BF16)
    e_b = (lane == head_of + H).astype(BF16)
    r_i = _iota((C, C), 0)
    c_i = _iota((C, C), 1)
    low = (r_i >= c_i).astype(BF16)
    upp = (r_i <= c_i).astype(BF16)
    grp_n = _iota((N, HP), 1) // (HPG * P)
    neg_a = -jnp.exp(alog_ref[...])
    cw = cw_ref[...]

    def prep(sb):
        r0 = sb * C
        xc = x_ref[0, r0:r0 + C, :]
        if sb == 0:
            prev = jnp.where(blk > 0, xp_ref[0], jnp.zeros_like(xp_ref[0]))
        else:
            prev = x_ref[0, r0 - HALO:r0, :]
        if sb == sub - 1:
            nxt = jnp.where(blk < nblk - 1, xn_ref[0], jnp.zeros_like(xn_ref[0]))
        else:
            nxt = x_ref[0, r0 + C:r0 + C + HALO, :]
        stacked = jnp.concatenate([prev, xc, nxt], axis=0)
        conv = (cw[0:1] * _dot(sel_prev, stacked) + cw[1:2] * xc.astype(F32)
                + cw[2:3] * _dot(sel_next, stacked) + cb_ref[...])
        act = _silu(conv)
        xs = act[:, 0:HP]
        bm = act[:, HP:HP + G * N].astype(BF16)
        cm = act[:, HP + G * N:HP + 2 * G * N].astype(BF16)
        raw = dt_ref[0, r0:r0 + C, :] + dtb_ref[...]
        dt = jnp.maximum(raw, 0.0) + jnp.log(1.0 + jnp.exp(-jnp.abs(raw)))
        la = dt * neg_a
        a_f = _dot_exact_lhs01(low, la)
        a_b = _dot_exact_lhs01(upp, la)
        return xs, bm, cm, dt, la, a_f, a_b

    def state_update(state, bm, xs, wdt, tot8, e_dir, row):
        xw = (xs * _dot(wdt.astype(BF16), e_dir)).astype(BF16)
        tot = _dot_exact_rhs01(tot8, e_dir)[row:row + 1]
        r = _dot_tn(bm, xw)
        upd = r[0:N]
        for gg in range(1, G):
            upd = jnp.where(grp_n == gg, r[gg * N:(gg + 1) * N], upd)
        return state * tot + upd

    def as_rows(s_c):
        return jnp.concatenate([jnp.where(grp_n == gg, s_c, 0.0) for gg in range(G)], axis=0).astype(BF16)

    @pl.when((ph == 0) & (c == 0))
    def _():
        g_ref[...] = init_ref[0, 1]

    @pl.when(ph == 0)
    def _():
        for sb in reversed(range(sub)):
            xs, bm, _, dt, _, _, a_b = prep(sb)
            g = g_ref[...]
            gbuf_ref[blk * sub + sb] = g
            wdt = jnp.exp(a_b[0:1, :] - a_b) * dt
            g_ref[...] = state_update(g, bm, xs, wdt, jnp.exp(a_b[0:8, :]), e_b, 0)

        @pl.when(c == nblk - 1)
        def _():
            fin_ref[0, 1] = g_ref[...]

    @pl.when((ph == 1) & (c == 0))
    def _():
        s_ref[...] = init_ref[0, 0]

    @pl.when(ph == 1)
    def _():
        grp_l = _iota((C, G * N), 1) // N
        head_l = _iota((C, HP), 1) // P
        tri_f = r_i >= c_i
        diag = r_i == c_i
        for sb in range(sub):
            r0 = sb * C
            xs, bm, cm, dt, la, a_f, a_b = prep(sb)
            la_t = la.T[0:2 * H]
            dt_t = dt.T[0:2 * H]
            af_row = _dot_exact_rhs01(la_t, upp)
            ab_row = _dot_exact_rhs01(la_t, low)
            sc = [_dot_nt(jnp.where(grp_l == gg, cm, jnp.zeros_like(cm)), bm) for gg in range(G)]
            xsb = xs.astype(BF16)
            ws, vs = [], []
            for hh in range(H):
                seg_f = a_f[:, hh:hh + 1] - af_row[hh:hh + 1, :]
                seg_b = a_b[:, H + hh:H + hh + 1] - ab_row[H + hh:H + hh + 1, :]
                dt_f, dt_b = dt_t[hh:hh + 1, :], dt_t[H + hh:H + hh + 1, :]
                dm = (jnp.exp(jnp.where(tri_f, seg_f, seg_b)) * jnp.where(tri_f, dt_f, dt_b)
                      + jnp.where(diag, dt_b, 0.0))
                ws.append((sc[hh // HPG] * dm).astype(BF16))
                vs.append(jnp.where(head_l == hh, xsb, jnp.zeros_like(xsb)))
            y = _dot(jnp.concatenate(ws, axis=1), jnp.concatenate(vs, axis=0))

            s = s_ref[...]
            g_in = gbuf_ref[blk * sub + sb]
            y = y + _dot(jnp.exp(a_f).astype(BF16), e_f) * _dot(cm, as_rows(s))
            y = y + _dot(jnp.exp(a_b).astype(BF16), e_b) * _dot(cm, as_rows(g_in))

            wdt = jnp.exp(a_f[C - 1:C, :] - a_f) * dt
            s_ref[...] = state_update(s, bm, xs, wdt, jnp.exp(a_f[C - 8:C, :]), e_f, 7)

            y = y + xs * dvec_ref[...]
            y = y * _silu(z_ref[0, r0:r0 + C, :].astype(F32))
            y_ref[0, r0:r0 + C, :] = _rms(y, nw_ref[...]).astype(BF16)

        @pl.when(c == nblk - 1)
        def _():
            fin_ref[0, 0] = s_ref[...]


SSD_HALO = 16


def _ssd_scan(xbc, z, dt, lw, init):
    b, t, _ = xbc.shape
    sub = _scan_sub(t)
    rows = sub * CHUNK
    nblk = t // rows
    hp = SSD_INNER
    per = rows // SSD_HALO
    nhalo = t // SSD_HALO
    bidx = _block_index(nblk)

    def prev_idx(i, ph, c):
        blk = (1 - ph) * (nblk - 1 - c) + ph * c
        return (i, jnp.maximum(blk * per - 1, 0), 0)

    def next_idx(i, ph, c):
        blk = (1 - ph) * (nblk - 1 - c) + ph * c
        return (i, jnp.minimum((blk + 1) * per, nhalo - 1), 0)

    const = lambda shape: pl.BlockSpec(shape, lambda i, ph, c: (0,) * len(shape))
    return pl.pallas_call(
        functools.partial(_ssd_kernel, nblk=nblk, sub=sub),
        grid=(b, 2, nblk),
        in_specs=[pl.BlockSpec((1, rows, W_XBC), bidx),
                  pl.BlockSpec((1, SSD_HALO, W_XBC), prev_idx),
                  pl.BlockSpec((1, SSD_HALO, W_XBC), next_idx),
                  pl.BlockSpec((1, rows, W_Z), bidx),
                  pl.BlockSpec((1, rows, W_DT), bidx),
                  const((8, W_XBC)), const((1, W_XBC)), const((1, LANES)), const((1, LANES)),
                  const((1, hp)), const((1, hp)),
                  pl.BlockSpec((1, 2, SSD_STATE, hp), lambda i, ph, c: (i, 0, 0, 0))],
        out_specs=[pl.BlockSpec((1, rows, hp), lambda i, ph, c: (i, ph * c, 0)),
                   pl.BlockSpec((1, 2, SSD_STATE, hp), lambda i, ph, c: (i, 0, 0, 0))],
        out_shape=[jax.ShapeDtypeStruct((b, t, hp), BF16),
                   jax.ShapeDtypeStruct((b, 2, SSD_STATE, hp), F32)],
        scratch_shapes=[pltpu.VMEM((SSD_STATE, hp), F32),
                        pltpu.VMEM((SSD_STATE, hp), F32),
                        pltpu.VMEM((t // CHUNK, SSD_STATE, hp), F32)],
        compiler_params=_cparams(("parallel", "arbitrary", "arbitrary")),
        name="ssd_scan",
    )(xbc, xbc, xbc, z, dt, lw["conv_w"], lw["conv_b"], lw["dt_bias"], lw["a_log"], lw["d_vec"], lw["ssd_nw"], init)


def _mix_residual(x_ref, att_ref, ret_ref, ssd_ref, mod_ref, n2_ref, wo_ref):
    mix = jnp.concatenate([att_ref[0], ret_ref[0], ssd_ref[0]], axis=-1)
    x1 = x_ref[0] + mod_ref[0, 2:3, :] * _dot(mix, wo_ref[...])
    h = _rms(x1, n2_ref[...]) * (1.0 + mod_ref[0, 4:5, :]) + mod_ref[0, 3:4, :]
    return x1, h


def _dense_kernel(x_ref, att_ref, ret_ref, ssd_ref, mod_ref, n2_ref, wo_ref, wg_ref, wu_ref, wd_ref, o_ref,
                  *, ff_chunk):
    x1, h = _mix_residual(x_ref, att_ref, ret_ref, ssd_ref, mod_ref, n2_ref, wo_ref)
    hb = h.astype(BF16)
    ff = wg_ref.shape[1]
    f = jnp.zeros_like(x1)
    for s in range(0, ff, ff_chunk):
        g = _dot(hb, wg_ref[:, s:s + ff_chunk])
        u = _dot(hb, wu_ref[:, s:s + ff_chunk])
        f = f + _dot((_silu(g) * u).astype(BF16), wd_ref[s:s + ff_chunk, :])
    o_ref[0] = x1 + mod_ref[0, 5:6, :] * f


def _dense_block(xs, att, ret, ssd, mods, lw, tm):
    b, t, d = xs.shape
    ff = lw["wg"].shape[1]
    tok = lambda w: pl.BlockSpec((1, tm, w), lambda i, j: (i, j, 0))
    const = lambda shape: pl.BlockSpec(shape, lambda i, j: (0,) * len(shape), pipeline_mode=pl.Buffered(1))
    return pl.pallas_call(
        functools.partial(_dense_kernel, ff_chunk=ff // 2),
        grid=(b, t // tm),
        in_specs=[tok(d), tok(att.shape[-1]), tok(ret.shape[-1]), tok(ssd.shape[-1]),
                  pl.BlockSpec((1, 8, d), lambda i, j: (i, 0, 0)), const((1, d)),
                  const((d, d)), const((d, ff)), const((d, ff)), const((ff, d))],
        out_specs=tok(d),
        out_shape=jax.ShapeDtypeStruct((b, t, d), F32),
        compiler_params=_cparams(("parallel", "parallel")),
        name="outproj_ffn",
    )(xs, att, ret, ssd, mods, lw["n2"], lw["wo"], lw["wg"], lw["wu"], lw["wd"])


SUBLANES = 8
TOP_K = 2


def _route_kernel(x_ref, att_ref, ret_ref, ssd_ref, mod_ref, n2_ref, wo_ref, wr_ref,
                  x1_ref, h3_ref, slot_ref, gate_ref, cnt_ref):
    x1, h = _mix_residual(x_ref, att_ref, ret_ref, ssd_ref, mod_ref, n2_ref, wo_ref)
    x1_ref[0] = x1
    tm, d = h.shape
    nsub = d // LANES
    for s in range(nsub):
        h3_ref[0, pl.ds(s, tm, stride=nsub), :] = h[:, s * LANES:(s + 1) * LANES]
    h_hi = h.astype(BF16)
    h_lo = (h - h_hi.astype(F32)).astype(BF16)
    logits = _dot(h_hi, wr_ref[0]) + _dot(h_lo, wr_ref[0]) + _dot(h_hi, wr_ref[1])
    lane = _iota(logits.shape, 1).astype(F32)
    logits = jnp.where(lane < N_EXPERTS, logits, NEG)
    m1 = jnp.max(logits, axis=-1, keepdims=True)
    i1 = jnp.min(jnp.where(logits == m1, lane, float(LANES)), axis=-1, keepdims=True)
    rest = jnp.where(lane == i1, NEG, logits)
    m2 = jnp.max(rest, axis=-1, keepdims=True)
    i2 = jnp.min(jnp.where(rest == m2, lane, float(LANES)), axis=-1, keepdims=True)
    e2 = jnp.exp(m2 - m1)
    den = 1.0 + e2
    sel1, sel2 = lane == i1, lane == i2
    chosen = jnp.where(sel1 | sel2, 1.0, 0.0)
    before = (_iota((tm, tm), 0) > _iota((tm, tm), 1)).astype(BF16)
    pos = _dot(before, chosen.astype(BF16))
    cnt = jnp.sum(chosen, axis=0, keepdims=True)
    nblk = sum(jnp.where(cnt > float(k * MOE_ROWS), 1.0, 0.0) for k in range(pl.cdiv(tm, MOE_ROWS)))
    lower_e = (_iota((LANES, LANES), 0) < _iota((LANES, LANES), 1)).astype(BF16)
    off = _dot_exact_rhs01(jnp.broadcast_to(nblk * float(MOE_ROWS), (SUBLANES, LANES)), lower_e)[0:1]
    slot = off + pos
    s1 = jnp.sum(jnp.where(sel1, slot, 0.0), axis=-1, keepdims=True)
    s2 = jnp.sum(jnp.where(sel2, slot, 0.0), axis=-1, keepdims=True)
    lane_i = _iota(logits.shape, 1)
    slots = jnp.where(lane_i == 0, s1, jnp.where(lane_i == 1, s2, 0.0))
    slot_ref[0, 0] = slots.T[0:TOP_K].astype(jnp.int32)
    gate_ref[0] = jnp.where(lane_i == 0, 1.0 / den, jnp.where(lane_i == 1, e2 / den, 0.0))
    row = _iota((SUBLANES, LANES), 0)
    cnt_ref[0, 0] = jnp.where(row == 0, cnt, jnp.where(row == 1, off, jnp.where(row == 2, nblk, 0.0))
                              ).astype(jnp.int32)


def _route_block(xs, att, ret, ssd, mods, lw, tm):
    b, t, d = xs.shape
    nt = t // tm
    nsub = d // LANES
    tok = lambda w: pl.BlockSpec((1, tm, w), lambda i, j: (i, j, 0))
    const = lambda shape: pl.BlockSpec(shape, lambda i, j: (0,) * len(shape))
    per_tile = lambda r, w: pl.BlockSpec((1, 1, r, w), lambda i, j: (i, j, 0, 0))
    return pl.pallas_call(
        _route_kernel,
        grid=(b, nt),
        in_specs=[tok(d), tok(att.shape[-1]), tok(ret.shape[-1]), tok(ssd.shape[-1]),
                  pl.BlockSpec((1, 8, d), lambda i, j: (i, 0, 0)), const((1, d)),
                  const((d, d)), const((2, d, LANES))],
        out_specs=[tok(d), pl.BlockSpec((1, tm * nsub, LANES), lambda i, j: (i, j, 0)),
                   per_tile(TOP_K, tm), tok(LANES), per_tile(SUBLANES, LANES)],
        out_shape=[jax.ShapeDtypeStruct((b, t, d), F32), jax.ShapeDtypeStruct((b, t * nsub, LANES), F32),
                   jax.ShapeDtypeStruct((b, nt, TOP_K, tm), jnp.int32),
                   jax.ShapeDtypeStruct((b, t, LANES), F32),
                   jax.ShapeDtypeStruct((b, nt, SUBLANES, LANES), jnp.int32)],
        compiler_params=_cparams(("parallel", "parallel")),
        name="outproj_route",
    )(xs, att, ret, ssd, mods, lw["n2"], lw["wo"], lw["router"])


MOE_UNROLL_LOG2 = 5
MOE_UNROLL = 1 << MOE_UNROLL_LOG2
MOE_ROWS = 320


def _moe_kernel(slot_ref, cnt_ref, h3_ref, wgu_ref, wd_ref, o3_ref, code_ref, g_ref, y3_ref, *, tm, ff_split):
    e = pl.program_id(2)
    rb = MOE_ROWS
    nsub = SUBLANES
    spare = TOP_K * tm

    @pl.when(e == 0)
    def _():
        def invert(t, carry):
            for k in range(TOP_K):
                code_ref[slot_ref[0, 0, k, t]] = k * tm + t
            return carry

        lax.fori_loop(0, tm, invert, 0, unroll=MOE_UNROLL)

    n = cnt_ref[0, 0, 0, e]
    off = cnt_ref[0, 0, 1, e]
    nblocks = cnt_ref[0, 0, 2, e]
    end = off + nblocks * rb

    def pad_fill(c, carry):
        for j in range(MOE_UNROLL):
            code_ref[jnp.minimum(off + n + c * MOE_UNROLL + j, end - 1)] = spare
        return carry

    lax.fori_loop(0, lax.shift_right_logical(end - off - n + (MOE_UNROLL - 1), jnp.int32(MOE_UNROLL_LOG2)),
                  pad_fill, 0)

    def tile_rows(i):
        return pl.ds(pl.multiple_of(i * nsub, nsub), nsub)

    def block(bi, carry):
        base = off + bi * rb

        def gather(i, c):
            t = code_ref[base + i] & (tm - 1)
            g_ref[tile_rows(i), :] = h3_ref[0, tile_rows(t), :]
            return c

        lax.fori_loop(0, rb, gather, 0, unroll=MOE_UNROLL)
        lhs =jnp.concatenate([g_ref[pl.ds(s, rb, stride=nsub), :] for s in range(nsub)], axis=1).astype(BF16)
        y = None
        for s0, s1 in ff_split:
            w = s1 - s0
            gu = _dot(lhs, wgu_ref[0, :, 2 * s0:2 * s1])
            part = _dot((_silu(gu[:, :w]) * gu[:, w:]).astype(BF16), wd_ref[0, s0:s1, :])
            y = part if y is None else y + part
        for s in range(nsub):
            y3_ref[pl.ds(s, rb, stride=nsub), :] = y[:, s * LANES:(s + 1) * LANES]

        def scatter(i, c):
            o3_ref[0, tile_rows(code_ref[base + i]), :] = y3_ref[tile_rows(i), :]
            return c

        lax.fori_loop(0, rb, scatter, 0, unroll=MOE_UNROLL)
        return carry

    lax.fori_loop(0, nblocks, block, 0)

    @pl.when(e == N_EXPERTS - 1)
    def _():
        o3_ref[0, spare * nsub:, :] = jnp.zeros((o3_ref.shape[1] - spare * nsub, LANES), F32)


def _moe_out_kernel(x1_ref, a3_ref, gate_ref, mod_ref, fn_ref, o_ref, *, final_norm):
    tm, d = x1_ref.shape[1:]
    nsub = d // LANES
    gates = gate_ref[0]
    acc = None
    for k in range(TOP_K):
        yk = jnp.concatenate([a3_ref[0, pl.ds(k * tm * nsub + s, tm, stride=nsub), :] for s in range(nsub)], axis=1)
        term = gates[:, k:k + 1] * yk
        acc = term if acc is None else acc + term
    out = x1_ref[0] + mod_ref[0, 5:6, :] * acc
    o_ref[0] = _rms(out, fn_ref[...]) if final_norm else out


MOE_FF_SLICE = 512


def _moe_ff_split(ff):
    return tuple((s, min(s + MOE_FF_SLICE, ff)) for s in range(0, ff, MOE_FF_SLICE))


def _moe_weights(w_gate, w_up, w_down):
    ff = w_gate.shape[-1]
    wgu = jnp.concatenate([w[..., s0:s1] for s0, s1 in _moe_ff_split(ff) for w in (w_gate, w_up)], axis=-1)
    return wgu.astype(BF16), w_down.astype(BF16)


def _moe_block(x1, h3, slots, gates, counts, mods, lw, fnw, final_norm, tm):
    b, t, d = x1.shape
    ff = lw["moe_wd"].shape[1]
    nsub = d // LANES
    assert nsub == SUBLANES and MOE_ROWS % (2 * SUBLANES) == 0 and tm & (tm - 1) == 0
    assert MOE_ROWS % MOE_UNROLL == 0 and tm % MOE_UNROLL == 0
    nt = t // tm
    out_rows = (TOP_K * tm + 1) * nsub
    smem = lambda r, w: pl.BlockSpec((1, 1, r, w), lambda i, j, e: (i, j, 0, 0), memory_space=pltpu.SMEM)
    y3 = pl.pallas_call(
        functools.partial(_moe_kernel, tm=tm, ff_split=_moe_ff_split(ff)),
        grid=(b, nt, N_EXPERTS),
        in_specs=[smem(TOP_K, tm), smem(SUBLANES, LANES),
                  pl.BlockSpec((1, tm * nsub, LANES), lambda i, j, e: (i, j, 0)),
                  pl.BlockSpec((1, d, 2 * ff), lambda i, j, e: (e, 0, 0)),
                  pl.BlockSpec((1, ff, d), lambda i, j, e: (e, 0, 0))],
        out_specs=pl.BlockSpec((1, out_rows, LANES), lambda i, j, e: (i, j, 0)),
        out_shape=jax.ShapeDtypeStruct((b, nt * out_rows, LANES), F32),
        scratch_shapes=[pltpu.SMEM((TOP_K * tm + N_EXPERTS * MOE_ROWS,), jnp.int32),
                        pltpu.VMEM((MOE_ROWS * nsub, LANES), F32), pltpu.VMEM((MOE_ROWS * nsub, LANES), F32)],
        compiler_params=_cparams(("parallel", "parallel", "arbitrary")),
        name="moe",
    )(slots, counts, h3, lw["moe_wgu"], lw["moe_wd"])
    tok = lambda w: pl.BlockSpec((1, tm, w), lambda i, j: (i, j, 0))
    return pl.pallas_call(
        functools.partial(_moe_out_kernel, final_norm=final_norm),
        grid=(b, nt),
        in_specs=[tok(d), pl.BlockSpec((1, out_rows, LANES), lambda i, j: (i, j, 0)), tok(LANES),
                  pl.BlockSpec((1, 8, d), lambda i, j: (i, 0, 0)), pl.BlockSpec((1, d), lambda i, j: (0, 0))],
        out_specs=tok(d),
        out_shape=jax.ShapeDtypeStruct((b, t, d), F32),
        compiler_params=_cparams(("parallel", "parallel")),
        name="moe_out",
    )(x1, y3, gates, mods, fnw)


def _final_norm_kernel(x_ref, w_ref, o_ref):
    o_ref[0] = _rms(x_ref[0], w_ref[...])


def _final_norm(xs, fnw, tm):
    b, t, d = xs.shape
    return pl.pallas_call(
        _final_norm_kernel,
        grid=(b, t // tm),
        in_specs=[pl.BlockSpec((1, tm, d), lambda i, j: (i, j, 0)), pl.BlockSpec((1, d), lambda i, j: (0, 0))],
        out_specs=pl.BlockSpec((1, tm, d), lambda i, j: (i, j, 0)),
        out_shape=jax.ShapeDtypeStruct((b, t, d), F32),
        compiler_params=_cparams(("parallel", "parallel")),
        name="final_norm",
    )(xs, fnw)


def _rot_cols(w):
    a, b2, c, d = jnp.split(w, 4, axis=-1)
    return jnp.concatenate([-b2, a, -d, c], axis=-1)


def _layer_weights(i, norm1_w, norm2_w, w_in, w_out, mla_q_norm_w, mla_w_qb, mla_kv_norm_w, mla_w_kb, mla_w_vb,
                   ret_decay_logit, ssd_conv_w, ssd_conv_b, ssd_dt_bias, ssd_a_log, ssd_d, ssd_norm_w):
    d = w_in.shape[1]
    offs = np.cumsum((0,) + IN_SIZES)
    seg = [w_in[i][:, offs[j]:offs[j + 1]] for j in range(len(IN_SIZES))]
    zeros = lambda n: jnp.zeros((d, n), F32)
    w_all = jnp.concatenate(
        [seg[0], seg[1], seg[2], _rot_cols(seg[2]), zeros(W_MLA - MLA_Q_LORA - MLA_KV_LORA - 2 * MLA_ROPE),
         seg[3] * (RET_QK ** -0.5), seg[4], seg[5], seg[6], seg[7], seg[8], seg[9], seg[10],
         zeros(W_DT - 2 * SSD_HEADS)], axis=1).astype(BF16)

    dh = MLA_NOPE + MLA_ROPE
    wqb = mla_w_qb[i].reshape(MLA_Q_LORA, MLA_HEADS, dh)
    pad = lambda a, lo, hi: jnp.pad(a, ((0, 0), (0, 0), (lo, hi)))
    wq = pad(wqb, 0, MLA_HP - dh).reshape(MLA_Q_LORA, MLA_HEADS * MLA_HP).astype(BF16)
    wqr = pad(_rot_cols(wqb[:, :, MLA_NOPE:]), MLA_NOPE, MLA_HP - dh).reshape(MLA_Q_LORA, MLA_HEADS * MLA_HP).astype(BF16)
    wkb = pad(mla_w_kb[i].reshape(MLA_KV_LORA, MLA_HEADS, MLA_NOPE), 0, MLA_HP - MLA_NOPE)
    place = np.zeros((MLA_ROPE, MLA_HEADS, MLA_HP), np.float32)
    for r in range(MLA_ROPE):
        place[r, :, MLA_NOPE + r] = 1.0
    place = jnp.asarray(place)
    wk = jnp.concatenate([wkb, place, place, jnp.zeros((MLA_KV_LORA - 2 * MLA_ROPE, MLA_HEADS, MLA_HP), F32)],
                         axis=0).reshape(2 * MLA_KV_LORA, MLA_HEADS * MLA_HP).astype(BF16)

    wvb = mla_w_vb[i].reshape(MLA_KV_LORA, MLA_HEADS, MLA_V)
    wv = jnp.concatenate(
        [pad(wvb[:, hh:hh + 1], (hh % 2) * MLA_V, MLA_HP - MLA_V - (hh % 2) * MLA_V) for hh in range(MLA_HEADS)],
        axis=1).reshape(MLA_KV_LORA, MLA_HEADS * MLA_HP).astype(BF16)

    lg = ret_decay_logit[i]
    padl = lambda v: jnp.pad(v.reshape(1, -1), ((0, 0), (0, LANES - 2 * SSD_HEADS)))
    return {
        "n1": norm1_w[i].reshape(1, d), "n2": norm2_w[i].reshape(1, d),
        "w_all": w_all, "qn": mla_q_norm_w[i].reshape(1, -1), "wq": wq, "wqr": wqr,
        "kvn": mla_kv_norm_w[i].reshape(1, -1), "wk": wk, "wv": wv,
        "wo": w_out[i].astype(BF16),
        "ret_lgl": jnp.repeat(lg, RET_V, axis=1),
        "ret_lgh": jnp.broadcast_to(lg.reshape(2 * RET_HEADS, 1), (2 * RET_HEADS, CHUNK)),
        "conv_w": jnp.pad(ssd_conv_w[i], ((0, 8 - ssd_conv_w.shape[1]), (0, 0))),
        "conv_b": ssd_conv_b[i].reshape(1, -1),
        "dt_bias": padl(ssd_dt_bias[i]), "a_log": padl(ssd_a_log[i]),
        "d_vec": jnp.repeat(ssd_d[i], SSD_HEADDIM).reshape(1, -1),
        "ssd_nw": ssd_norm_w[i].reshape(1, -1),
    }


def _rope_tables(n_lat, n_ctx):
    rows = n_lat // GRID_W
    row = jnp.repeat(jnp.arange(rows, dtype=F32), GRID_W)
    col = jnp.tile(jnp.arange(GRID_W, dtype=F32), rows)
    inv_freq = ROPE_BASE ** (-jnp.arange(0, ROPE_AXIS_DIM, 2, dtype=F32) / ROPE_AXIS_DIM)
    ang_r = row[:, None] * inv_freq
    ang_c = col[:, None] * inv_freq
    ang = jnp.concatenate([ang_r, ang_r, ang_c, ang_c], axis=-1)
    c2 = MLA_SCALE * math.log2(math.e)

    def build(cos, sin):
        n = cos.shape[0]
        cosq = jnp.concatenate([jnp.full((n, MLA_NOPE), c2, F32), c2 * cos, jnp.zeros((n, MLA_HP - MLA_NOPE - MLA_ROPE), F32)], 1)
        sinq = jnp.concatenate([jnp.zeros((n, MLA_NOPE), F32), c2 * sin, jnp.zeros((n, MLA_HP - MLA_NOPE - MLA_ROPE), F32)], 1)
        ck = jnp.concatenate([cos, sin, jnp.zeros((n, LANES - 2 * MLA_ROPE), F32)], 1)
        return cosq, sinq, ck

    lat = build(jnp.cos(ang), jnp.sin(ang))
    ctx = build(jnp.ones((n_ctx, MLA_ROPE), F32), jnp.zeros((n_ctx, MLA_ROPE), F32))
    return lat, ctx


def _token_tile(t, pref):
    tm = min(pref, t)
    while t % tm:
        tm //= 2
    return tm


def kernel(x, c, ctx, c_ctx, mod_w, mod_b, norm1_w, norm2_w, w_in, w_out, mla_q_norm_w, mla_w_qb, mla_kv_norm_w, mla_w_kb, mla_w_vb, ret_decay_logit, ssd_conv_w, ssd_conv_b, ssd_dt_bias, ssd_a_log, ssd_d, ssd_norm_w, ffn_w_gate, ffn_w_up, ffn_w_down, moe_router, moe_w_gate, moe_w_up, moe_w_down, final_norm_w):
    b, n_lat, d = x.shape
    n_ctx = ctx.shape[1]
    depth = mod_w.shape[0]
    assert b + 1 <= 8 and n_lat % CHUNK == 0 and n_ctx % CHUNK == 0 and n_lat % GRID_W == 0

    cond = jnp.concatenate([c, c_ctx[None, :], jnp.zeros((8 - b - 1, d), F32)], axis=0)
    mod_all = _modulation(cond, mod_w, mod_b)
    tabs_x, tabs_c = _rope_tables(n_lat, n_ctx)
    fnw = final_norm_w.reshape(1, d)

    tm_x = _token_tile(n_lat, 512)
    tm_c = _token_tile(n_ctx, 256)
    tm_moe = _token_tile(n_lat, 1024)
    tq_x = _token_tile(n_lat, 1024)
    tk_x = 2048
    zero_ret = jnp.zeros((b, 2, RET_QK, RET_HEADS * RET_V), F32)
    zero_ssd = jnp.zeros((b, 2, SSD_STATE, SSD_INNER), F32)

    h_ctx = ctx
    for i in range(depth):
        last = i == depth - 1
        lw = _layer_weights(i, norm1_w, norm2_w, w_in, w_out, mla_q_norm_w, mla_w_qb, mla_kv_norm_w, mla_w_kb,
                            mla_w_vb, ret_decay_logit, ssd_conv_w, ssd_conv_b, ssd_dt_bias, ssd_a_log, ssd_d,
                            ssd_norm_w)
        m6 = mod_all[i].reshape(8, 6, d)
        pad2 = jnp.zeros((b, 2, d), F32)
        mods_x = jnp.concatenate([m6[:b], pad2], axis=1)
        mods_c = jnp.concatenate([jnp.broadcast_to(m6[b:b + 1], (b, 6, d)), pad2], axis=1)

        qc, kc, vc, retc, zc, xbcc, dtc = _inproj(h_ctx, mods_c, lw, tabs_c, tm_c)
        ret_yc, ret_fin = _ret_scan(retc, lw, zero_ret)
        ssd_yc, ssd_fin = _ssd_scan(xbcc, zc, dtc, lw, zero_ssd)

        qx, kx, vx, retx, zx, xbcx, dtx = _inproj(x, mods_x, lw, tabs_x, tm_x)
        att_x = _attention(qx, [(kc, vc), (kx, vx)], tq_x, tk_x)
        ret_yx, _ = _ret_scan(retx, lw, ret_fin)
        ssd_yx, _ = _ssd_scan(xbcx, zx, dtx, lw, ssd_fin)

        if i % 2 == 0:
            j = i // 2
            lw.update(wg=ffn_w_gate[j].astype(BF16), wu=ffn_w_up[j].astype(BF16), wd=ffn_w_down[j].astype(BF16))
            x = _dense_block(x, att_x, ret_yx, ssd_yx, mods_x, lw, tm_x)
            if last:
                x = _final_norm(x, fnw, tm_x)
        else:
            j = i // 2
            wgu, wd = _moe_weights(moe_w_gate[j], moe_w_up[j], moe_w_down[j])
            r_f32 = jnp.pad(moe_router[j], ((0, 0), (0, LANES - N_EXPERTS)))
            r_hi = r_f32.astype(BF16)
            r_lo = (r_f32 - r_hi.astype(F32)).astype(BF16)
            lw.update(router=jnp.stack([r_hi, r_lo]), moe_wgu=wgu, moe_wd=wd)
            x1, *routed = _route_block(x, att_x, ret_yx, ssd_yx, mods_x, lw, tm_moe)
            x = _moe_block(x1, *routed, mods_x, lw, fnw, last, tm_moe)

        if not last:
            att_c = _attention(qc, [(kc, vc)], tm_c, tk_x)
            if i % 2 == 0:
                h_ctx = _dense_block(h_ctx, att_c, ret_yc, ssd_yc, mods_c, lw, tm_c)
            else:
                c1, *routed = _route_block(h_ctx, att_c, ret_yc, ssd_yc, mods_c, lw, tm_c)
                h_ctx = _moe_block(c1, *routed, mods_c, lw, fnw, False, tm_c)
    return x
```

```python
import functools
import math

import jax
import jax.numpy as jnp
import numpy as np
from jax import lax
from jax.experimental import pallas as pl
from jax.experimental.pallas import tpu as pltpu

F32 = jnp.float32
BF16 = jnp.bfloat16
HIGHEST = lax.Precision.HIGHEST

EPS = 1e-6
CHUNK = 128
GRID_W = 64
MLA_HEADS = 4
MLA_Q_LORA = 256
MLA_KV_LORA = 128
MLA_NOPE = 64
MLA_ROPE = 32
MLA_V = 64
MLA_SCALE = (MLA_NOPE + MLA_ROPE) ** -0.5
ROPE_AXIS_DIM = MLA_ROPE // 2
ROPE_BASE = 10000.0
RET_HEADS = 4
RET_QK = 64
RET_V = 64
SSD_HEADS = 8
SSD_HEADDIM = 64
SSD_GROUPS = 2
SSD_STATE = 64
SSD_INNER = SSD_HEADS * SSD_HEADDIM
SSD_CONV_CH = SSD_INNER + 2 * SSD_GROUPS * SSD_STATE
N_EXPERTS = 8
IN_SIZES = (MLA_Q_LORA, MLA_KV_LORA, MLA_ROPE,
            RET_HEADS * RET_QK, RET_HEADS * RET_QK, RET_HEADS * RET_V, RET_HEADS * RET_V,
            SSD_INNER, SSD_CONV_CH, SSD_HEADS, SSD_HEADS)

LANES = 128
V7X_VMEM_LIMIT = 56 * 1024 * 1024

MLA_HP = 128
NEG = -1e30


def _cparams(sem):
    return pltpu.CompilerParams(dimension_semantics=sem, vmem_limit_bytes=V7X_VMEM_LIMIT)


def _dot(a, b):
    return jnp.dot(a, b, preferred_element_type=F32)


def _dot_nt(a, b):
    return lax.dot_general(a, b, (((1,), (1,)), ((), ())), preferred_element_type=F32)


def _dot_tn(a, b):
    return lax.dot_general(a, b, (((0,), (0,)), ((), ())), preferred_element_type=F32)


def _split3(x):
    hi = x.astype(BF16)
    r1 = x - hi.astype(F32)
    mid = r1.astype(BF16)
    lo = (r1 - mid.astype(F32)).astype(BF16)
    return hi, mid, lo


def _dot_exact_rhs01(x, e):
    hi, mid, lo = _split3(x)
    return _dot(hi, e) + _dot(mid, e) + _dot(lo, e)


def _dot_exact_lhs01(e, x):
    hi, mid, lo = _split3(x)
    return _dot(e, hi) + _dot(e, mid) + _dot(e, lo)


def _silu(x):
    return x / (1.0 + jnp.exp(-x))


def _rms(x, w):
    return x * lax.rsqrt(jnp.mean(x * x, axis=-1, keepdims=True) + EPS) * w


def _iota(shape, dim):
    return lax.broadcasted_iota(jnp.int32, shape, dim)


def _mod_kernel(a_ref, w_ref, b_ref, o_ref):
    a = _silu(a_ref[...])
    o_ref[0] = jnp.dot(a, w_ref[0], precision=HIGHEST, preferred_element_type=F32) + b_ref[0]


def _modulation(cond, mod_w, mod_b):
    depth, d, n = mod_w.shape
    tn = 1536
    return pl.pallas_call(
        _mod_kernel,
        grid=(depth, n // tn),
        in_specs=[pl.BlockSpec((8, d), lambda i, j: (0, 0)),
                  pl.BlockSpec((1, d, tn), lambda i, j: (i, 0, j)),
                  pl.BlockSpec((1, 1, tn), lambda i, j: (i, 0, j))],
        out_specs=pl.BlockSpec((1, 8, tn), lambda i, j: (i, 0, j)),
        out_shape=jax.ShapeDtypeStruct((depth, 8, n), F32),
        compiler_params=_cparams(("parallel", "parallel")),
        name="modulation",
    )(cond, mod_w, mod_b.reshape(depth, 1, n))


W_MLA, W_RET, W_Z, W_XBC, W_DT = 512, 1024, 512, 768, 128
W_ALL = W_MLA + W_RET + W_Z + W_XBC + W_DT


VT_ROWS = 80
ATT_TK = 1024


def _inproj_kernel(x_ref, mod_ref, n1_ref, win_ref, qn_ref, wq_ref, wqr_ref, kvn_ref, wk_ref, wv_ref,
                   cq_ref, sq_ref, ck_ref,
                   q_out, k_out, v_out, ret_out, z_out, xbc_out, dt_out):
    x = x_ref[0]
    h = _rms(x, n1_ref[...]) * (1.0 + mod_ref[0, 1:2, :]) + mod_ref[0, 0:1, :]
    hb = h.astype(BF16)

    o = 0
    pm = _dot(hb, win_ref[:, o:o + W_MLA]); o += W_MLA
    ret_out[0] = _dot(hb, win_ref[:, o:o + W_RET]).astype(BF16); o += W_RET
    z_out[0] = _dot(hb, win_ref[:, o:o + W_Z]).astype(BF16); o += W_Z
    xbc_out[0] = _dot(hb, win_ref[:, o:o + W_XBC]).astype(BF16); o += W_XBC
    dt_out[0] = _dot(hb, win_ref[:, o:o + W_DT])

    cqb = _rms(pm[:, 0:MLA_Q_LORA], qn_ref[...]).astype(BF16)
    qm = _dot(cqb, wq_ref[...])
    qr = _dot(cqb, wqr_ref[...])
    cosq, sinq = cq_ref[...], sq_ref[...]
    for hh in range(MLA_HEADS):
        sl = slice(hh * MLA_HP, (hh + 1) * MLA_HP)
        q_out[0, :, sl] = (qm[:, sl] * cosq + qr[:, sl] * sinq).astype(BF16)

    ckvn = _rms(pm[:, MLA_Q_LORA:MLA_Q_LORA + MLA_KV_LORA], kvn_ref[...])
    u = pm[:, MLA_Q_LORA + MLA_KV_LORA:W_MLA] * ck_ref[...]
    kin = jnp.concatenate([ckvn, u], axis=-1).astype(BF16)
    k_out[0] = _dot(kin, wk_ref[...]).astype(BF16)
    vt = _dot(ckvn.astype(BF16), wv_ref[...]).T
    extra = jnp.where(_iota((VT_ROWS - MLA_V, vt.shape[1]), 0) == 0, 1.0, 0.0)
    rows = []
    for hh in range(MLA_HEADS):
        rows += [vt[hh * MLA_V:(hh + 1) * MLA_V], extra]
    v_out[0, 0] = jnp.concatenate(rows, axis=0).astype(BF16)


def _inproj(xs, mods, lw, tabs, tm):
    b, t, d = xs.shape
    const = lambda shape: pl.BlockSpec(shape, lambda i, j: (0,) * len(shape))
    tok = lambda w: pl.BlockSpec((1, tm, w), lambda i, j: (i, j, 0))
    tab = pl.BlockSpec((tm, LANES), lambda i, j: (j, 0))
    widths = (MLA_HEADS * MLA_HP, MLA_HEADS * MLA_HP, W_RET, W_Z, W_XBC, W_DT)
    dtypes = (BF16, BF16, BF16, BF16, BF16, F32)
    kb = min(ATT_TK, t)
    per = kb // tm
    vt_rows = MLA_HEADS * VT_ROWS
    out_specs = [tok(w) for w in widths]
    out_shape = [jax.ShapeDtypeStruct((b, t, w), dt) for w, dt in zip(widths, dtypes)]
    out_specs.insert(2, pl.BlockSpec((1, 1, vt_rows, tm), lambda i, j: (i, j // per, 0, j % per)))
    out_shape.insert(2, jax.ShapeDtypeStruct((b, t // kb, vt_rows, kb), BF16))
    return pl.pallas_call(
        _inproj_kernel,
        grid=(b, t // tm),
        in_specs=[tok(d), pl.BlockSpec((1, 8, d), lambda i, j: (i, 0, 0)), const((1, d)),
                  const((d, W_ALL)), const((1, MLA_Q_LORA)), const((MLA_Q_LORA, MLA_HEADS * MLA_HP)),
                  const((MLA_Q_LORA, MLA_HEADS * MLA_HP)), const((1, MLA_KV_LORA)),
                  const((2 * MLA_KV_LORA, MLA_HEADS * MLA_HP)), const((MLA_KV_LORA, MLA_HEADS * MLA_V)),
                  tab, tab, tab],
        out_specs=out_specs,
        out_shape=out_shape,
        compiler_params=_cparams(("parallel", "parallel")),
        name="inproj",
    )(xs, mods, lw["n1"], lw["w_all"], lw["qn"], lw["wq"], lw["wqr"], lw["kvn"], lw["wk"], lw["wv"], *tabs)


def _attn_kernel(*refs, seg_blocks, tq):
    nseg = len(seg_blocks)
    q_ref = refs[0]
    kv_refs = refs[1:1 + 2 * nseg]
    o_ref = refs[1 + 2 * nseg]
    carry = tuple((jnp.full((1, tq), NEG, F32), jnp.zeros((VT_ROWS, tq), F32)) for _ in range(MLA_HEADS))
    for si, (tk, nblk) in enumerate(seg_blocks):
        k_ref, v_ref = kv_refs[2 * si], kv_refs[2 * si + 1]

        def body(j, carry, k_ref=k_ref, v_ref=v_ref, tk=tk):
            start = 0 if isinstance(j, int) else pl.multiple_of(j * tk, tk)
            new = []
            for hh in range(MLA_HEADS):
                sl = slice(hh * MLA_HP, (hh + 1) * MLA_HP)
                m, acc = carry[hh]
                s = _dot_nt(k_ref[0, pl.ds(start, tk), sl], q_ref[0, :, sl])
                m_new = jnp.maximum(m, jnp.max(s, axis=0, keepdims=True))
                p = jnp.exp2(s - m_new).astype(BF16)
                vt = v_ref[0, j, hh * VT_ROWS:(hh + 1) * VT_ROWS, :]
                acc = jnp.exp2(m - m_new) * acc + _dot(vt, p)
                new.append((m_new, acc))
            return tuple(new)

        carry = body(0, carry) if nblk == 1 else lax.fori_loop(0, nblk, body, carry)
    out_t = jnp.concatenate([acc[0:MLA_V] / acc[MLA_V:MLA_V + 1] for _, acc in carry], axis=0)
    o_ref[0] = out_t.T.astype(BF16)


def _attention(q, segs, tq):
    b, l, _ = q.shape
    seg_blocks = []
    in_specs = [pl.BlockSpec((1, tq, MLA_HEADS * MLA_HP), lambda i, j: (i, j, 0))]
    args = [q]
    for k, v in segs:
        s = k.shape[1]
        nkb, vt_rows, kb = v.shape[1:]
        assert nkb * kb == s
        seg_blocks.append((kb, nkb))
        in_specs.append(pl.BlockSpec((1, s, MLA_HEADS * MLA_HP), lambda i, j: (i, 0, 0),
                                     pipeline_mode=pl.Buffered(1)))
        in_specs.append(pl.BlockSpec((1, nkb, vt_rows, kb), lambda i, j: (i, 0, 0, 0),
                                     pipeline_mode=pl.Buffered(1)))
        args += [k, v]
    return pl.pallas_call(
        functools.partial(_attn_kernel, seg_blocks=tuple(seg_blocks), tq=tq),
        grid=(b, l // tq),
        in_specs=in_specs,
        out_specs=pl.BlockSpec((1, tq, MLA_HEADS * MLA_V), lambda i, j: (i, j, 0)),
        out_shape=jax.ShapeDtypeStruct((b, l, MLA_HEADS * MLA_V), BF16),
        compiler_params=_cparams(("parallel", "arbitrary")),
        name="attention",
    )(*args)


def _ret_kernel(x_ref, lgl_ref, lgh_ref, init_ref, y_ref, fin_ref, s_ref, g_ref, gbuf_ref, *, nblk, sub):
    C, H, N, P = CHUNK, RET_HEADS, RET_QK, RET_V
    HP = H * P
    ph, c = pl.program_id(1), pl.program_id(2)
    blk = jnp.where(ph == 0, nblk - 1 - c, c)

    def log_sigmoid(v):
        return jnp.minimum(v, 0.0) - jnp.log(1.0 + jnp.exp(-jnp.abs(v)))

    lgl = log_sigmoid(lgl_ref[...])
    lgf, lgb = lgl[0:1], lgl[1:2]
    rows = _iota((C, HP), 0).astype(F32)
    head_n = _iota((N, HP), 1) // P

    def diag_blocks(r):
        out = r[0:N]
        for hh in range(1, H):
            out = jnp.where(head_n == hh, r[hh * N:(hh + 1) * N], out)
        return out

    def as_rows(s_c):
        return jnp.concatenate([jnp.where(head_n == hh, s_c, 0.0) for hh in range(H)], axis=0)

    @pl.when((ph == 0) & (c == 0))
    def _():
        g_ref[...] = init_ref[0, 1]

    @pl.when(ph == 0)
    def _():
        w_b = jnp.exp(rows * lgb)
        tot_b = jnp.exp(C * lgb)
        for sb in reversed(range(sub)):
            r0 = sb * C
            k = x_ref[0, r0:r0 + C, HP:2 * HP].astype(F32)
            v = x_ref[0, r0:r0 + C, 2 * HP:3 * HP]
            g = g_ref[...]
            gbuf_ref[blk * sub + sb] = g
            g_ref[...] = g * tot_b + diag_blocks(_dot_tn((k * w_b).astype(BF16), v))

        @pl.when(c == nblk - 1)
        def _():
            fin_ref[0, 1] = g_ref[...]

    @pl.when((ph == 1) & (c == 0))
    def _():
        s_ref[...] = init_ref[0, 0]

    @pl.when(ph == 1)
    def _():
        lgh = log_sigmoid(lgh_ref[...])
        grp = _iota((C, HP), 1) // P
        di = (_iota((C, C), 0) - _iota((C, C), 1)).astype(F32)
        decay = [jnp.exp(jnp.where(di >= 0, di * lgh[hh:hh + 1, :], -di * lgh[H + hh:H + hh + 1, :]))
                 for hh in range(H)]
        e_f = jnp.exp((rows + 1.0) * lgf)
        e_b = jnp.exp((C - rows) * lgb)
        w_f = jnp.exp((C - 1.0 - rows) * lgf)
        tot_f = jnp.exp(C * lgf)
        for sb in range(sub):
            r0 = sb * C
            q = x_ref[0, r0:r0 + C, 0:HP]
            kb = x_ref[0, r0:r0 + C, HP:2 * HP]
            v = x_ref[0, r0:r0 + C, 2 * HP:3 * HP]
            gate = x_ref[0, r0:r0 + C, 3 * HP:4 * HP].astype(F32)
            ws, vs = [], []
            for hh in range(H):
                sc = _dot_nt(jnp.where(grp == hh, q, jnp.zeros_like(q)), kb)
                ws.append((sc * decay[hh]).astype(BF16))
                vs.append(jnp.where(grp == hh, v, jnp.zeros_like(v)))
            y = _dot(jnp.concatenate(ws, axis=1), jnp.concatenate(vs, axis=0))
            qf32 = q.astype(F32)
            s = s_ref[...]
            g_in = gbuf_ref[blk * sub + sb]
            y = y + _dot(jnp.concatenate([(qf32 * e_f).astype(BF16), (qf32 * e_b).astype(BF16)], axis=1),
                         jnp.concatenate([as_rows(s), as_rows(g_in)], axis=0).astype(BF16))
            s_ref[...] = s * tot_f + diag_blocks(_dot_tn((kb.astype(F32) * w_f).astype(BF16), v))

            mu = jnp.zeros_like(y)
            for hh in range(H):
                mh = jnp.sum(jnp.where(grp == hh, y, 0.0), axis=-1, keepdims=True) * (1.0 / P)
                mu = jnp.where(grp == hh, mh, mu)
            yc = y - mu
            var = jnp.zeros_like(y)
            for hh in range(H):
                vh = jnp.sum(jnp.where(grp == hh, yc * yc, 0.0), axis=-1, keepdims=True) * (1.0 / P)
                var = jnp.where(grp == hh, vh, var)
            y_ref[0, r0:r0 + C, :] = (yc * lax.rsqrt(var + EPS) * _silu(gate)).astype(BF16)

        @pl.when(c == nblk - 1)
        def _():
            fin_ref[0, 0] = s_ref[...]


def _block_index(nblk):
    return lambda i, ph, c: (i, (1 - ph) * (nblk - 1 - c) + ph * c, 0)


def _scan_sub(t):
    nc = t // CHUNK
    return 4 if nc % 4 == 0 else (2 if nc % 2 == 0 else 1)


def _ret_scan(ret, lw, init):
    b, t, _ = ret.shape
    sub = _scan_sub(t)
    rows = sub * CHUNK
    nblk = t // rows
    hp = RET_HEADS * RET_V
    return pl.pallas_call(
        functools.partial(_ret_kernel, nblk=nblk, sub=sub),
        grid=(b, 2, nblk),
        in_specs=[pl.BlockSpec((1, rows, W_RET), _block_index(nblk)),
                  pl.BlockSpec((2, hp), lambda i, ph, c: (0, 0)),
                  pl.BlockSpec((2 * RET_HEADS, CHUNK), lambda i, ph, c: (0, 0)),
                  pl.BlockSpec((1, 2, RET_QK, hp), lambda i, ph, c: (i, 0, 0, 0))],
        out_specs=[pl.BlockSpec((1, rows, hp), lambda i, ph, c: (i, ph * c, 0)),
                   pl.BlockSpec((1, 2, RET_QK, hp), lambda i, ph, c: (i, 0, 0, 0))],
        out_shape=[jax.ShapeDtypeStruct((b, t, hp), BF16),
                   jax.ShapeDtypeStruct((b, 2, RET_QK, hp), F32)],
        scratch_shapes=[pltpu.VMEM((RET_QK, hp), F32),
                        pltpu.VMEM((RET_QK, hp), F32),
                        pltpu.VMEM((t // CHUNK, RET_QK, hp), F32)],
        compiler_params=_cparams(("parallel", "arbitrary", "arbitrary")),
        name="retention_scan",
    )(ret, lw["ret_lgl"], lw["ret_lgh"], init)


def _ssd_kernel(x_ref, xp_ref, xn_ref, z_ref, dt_ref, cw_ref, cb_ref, dtb_ref, alog_ref, dvec_ref, nw_ref,
                init_ref, y_ref, fin_ref, s_ref, g_ref, gbuf_ref, *, nblk, sub):
    C, H, N, P, G = CHUNK, SSD_HEADS, SSD_STATE, SSD_HEADDIM, SSD_GROUPS
    HP = H * P
    HPG = H // G
    HALO = SSD_HALO
    ph, c = pl.program_id(1), pl.program_id(2)
    blk = jnp.where(ph == 0, nblk - 1 - c, c)

    ri = _iota((C, C + 2 * HALO), 0)
    ci = _iota((C, C + 2 * HALO), 1)
    sel_prev = (ci == ri + HALO - 1).astype(BF16)
    sel_next = (ci == ri + HALO + 1).astype(BF16)
    lane = _iota((LANES, HP), 0)
    head_of = _iota((LANES, HP), 1) // P
    e_f = (lane == head_of).astype(BF16)
    e_b = (lane == head_of + H).astype(BF16)
    r_i = _iota((C, C), 0)
    c_i = _iota((C, C), 1)
    low = (r_i >= c_i).astype(BF16)
    upp = (r_i <= c_i).astype(BF16)
    grp_n = _iota((N, HP), 1) // (HPG * P)
    neg_a = -jnp.exp(alog_ref[...])
    cw = cw_ref[...]

    def prep(sb):
        r0 = sb * C
        xc = x_ref[0, r0:r0 + C, :]
        if sb == 0:
            prev = jnp.where(blk > 0, xp_ref[0], jnp.zeros_like(xp_ref[0]))
        else:
            prev = x_ref[0, r0 - HALO:r0, :]
        if sb == sub - 1:
            nxt = jnp.where(blk < nblk - 1, xn_ref[0], jnp.zeros_like(xn_ref[0]))
        else:
            nxt = x_ref[0, r0 + C:r0 + C + HALO, :]
        stacked = jnp.concatenate([prev, xc, nxt], axis=0)
        conv = (cw[0:1] * _dot(sel_prev, stacked) + cw[1:2] * xc.astype(F32)
                + cw[2:3] * _dot(sel_next, stacked) + cb_ref[...])
        act = _silu(conv)
        xs = act[:, 0:HP]
        bm = act[:, HP:HP + G * N].astype(BF16)
        cm = act[:, HP + G * N:HP + 2 * G * N].astype(BF16)
        raw = dt_ref[0, r0:r0 + C, :] + dtb_ref[...]
        dt = jnp.maximum(raw, 0.0) + jnp.log(1.0 + jnp.exp(-jnp.abs(raw)))
        la = dt * neg_a
        a_f = _dot_exact_lhs01(low, la)
        a_b = _dot_exact_lhs01(upp, la)
        return xs, bm, cm, dt, la, a_f, a_b

    def state_update(state, bm, xs, wdt, tot8, e_dir, row):
        xw = (xs * _dot(wdt.astype(BF16), e_dir)).astype(BF16)
        tot = _dot_exact_rhs01(tot8, e_dir)[row:row + 1]
        r = _dot_tn(bm, xw)
        upd = r[0:N]
        for gg in range(1, G):
            upd = jnp.where(grp_n == gg, r[gg * N:(gg + 1) * N], upd)
        return state * tot + upd

    def as_rows(s_c):
        return jnp.concatenate([jnp.where(grp_n == gg, s_c, 0.0) for gg in range(G)], axis=0).astype(BF16)

    @pl.when((ph == 0) & (c == 0))
    def _():
        g_ref[...] = init_ref[0, 1]

    @pl.when(ph == 0)
    def _():
        for sb in reversed(range(sub)):
            xs, bm, _, dt, _, _, a_b = prep(sb)
            g = g_ref[...]
            gbuf_ref[blk * sub + sb] = g
            wdt = jnp.exp(a_b[0:1, :] - a_b) * dt
            g_ref[...] = state_update(g, bm, xs, wdt, jnp.exp(a_b[0:8, :]), e_b, 0)

        @pl.when(c == nblk - 1)
        def _():
            fin_ref[0, 1] = g_ref[...]

    @pl.when((ph == 1) & (c == 0))
    def _():
        s_ref[...] = init_ref[0, 0]

    @pl.when(ph == 1)
    def _():
        grp_l = _iota((C, G * N), 1) // N
        head_l = _iota((C, HP), 1) // P
        tri_f = r_i >= c_i
        diag = r_i == c_i
        for sb in range(sub):
            r0 = sb * C
            xs, bm, cm, dt, la, a_f, a_b = prep(sb)
            la_t = la.T[0:2 * H]
            dt_t = dt.T[0:2 * H]
            af_row = _dot_exact_rhs01(la_t, upp)
            ab_row = _dot_exact_rhs01(la_t, low)
            sc = [_dot_nt(jnp.where(grp_l == gg, cm, jnp.zeros_like(cm)), bm) for gg in range(G)]
            xsb = xs.astype(BF16)
            ws, vs = [], []
            for hh in range(H):
                seg_f = a_f[:, hh:hh + 1] - af_row[hh:hh + 1, :]
                seg_b = a_b[:, H + hh:H + hh + 1] - ab_row[H + hh:H + hh + 1, :]
                dt_f, dt_b = dt_t[hh:hh + 1, :], dt_t[H + hh:H + hh + 1, :]
                dm = (jnp.exp(jnp.where(tri_f, seg_f, seg_b)) * jnp.where(tri_f, dt_f, dt_b)
                      + jnp.where(diag, dt_b, 0.0))
                ws.append((sc[hh // HPG] * dm).astype(BF16))
                vs.append(jnp.where(head_l == hh, xsb, jnp.zeros_like(xsb)))
            y = _dot(jnp.concatenate(ws, axis=1), jnp.concatenate(vs, axis=0))

            s = s_ref[...]
            g_in = gbuf_ref[blk * sub + sb]
            y = y + _dot(jnp.exp(a_f).astype(BF16), e_f) * _dot(cm, as_rows(s))
            y = y + _dot(jnp.exp(a_b).astype(BF16), e_b) * _dot(cm, as_rows(g_in))

            wdt = jnp.exp(a_f[C - 1:C, :] - a_f) * dt
            s_ref[...] = state_update(s, bm, xs, wdt, jnp.exp(a_f[C - 8:C, :]), e_f, 7)

            y = y + xs * dvec_ref[...]
            y = y * _silu(z_ref[0, r0:r0 + C, :].astype(F32))
            y_ref[0, r0:r0 + C, :] = _rms(y, nw_ref[...]).astype(BF16)

        @pl.when(c == nblk - 1)
        def _():
            fin_ref[0, 0] = s_ref[...]


SSD_HALO = 16


def _ssd_scan(xbc, z, dt, lw, init):
    b, t, _ = xbc.shape
    sub = _scan_sub(t)
    rows = sub * CHUNK
    nblk = t // rows
    hp = SSD_INNER
    per = rows // SSD_HALO
    nhalo = t // SSD_HALO
    bidx = _block_index(nblk)

    def prev_idx(i, ph, c):
        blk = (1 - ph) * (nblk - 1 - c) + ph * c
        return (i, jnp.maximum(blk * per - 1, 0), 0)

    def next_idx(i, ph, c):
        blk = (1 - ph) * (nblk - 1 - c) + ph * c
        return (i, jnp.minimum((blk + 1) * per, nhalo - 1), 0)

    const = lambda shape: pl.BlockSpec(shape, lambda i, ph, c: (0,) * len(shape))
    return pl.pallas_call(
        functools.partial(_ssd_kernel, nblk=nblk, sub=sub),
        grid=(b, 2, nblk),
        in_specs=[pl.BlockSpec((1, rows, W_XBC), bidx),
                  pl.BlockSpec((1, SSD_HALO, W_XBC), prev_idx),
                  pl.BlockSpec((1, SSD_HALO, W_XBC), next_idx),
                  pl.BlockSpec((1, rows, W_Z), bidx),
                  pl.BlockSpec((1, rows, W_DT), bidx),
                  const((8, W_XBC)), const((1, W_XBC)), const((1, LANES)), const((1, LANES)),
                  const((1, hp)), const((1, hp)),
                  pl.BlockSpec((1, 2, SSD_STATE, hp), lambda i, ph, c: (i, 0, 0, 0))],
        out_specs=[pl.BlockSpec((1, rows, hp), lambda i, ph, c: (i, ph * c, 0)),
                   pl.BlockSpec((1, 2, SSD_STATE, hp), lambda i, ph, c: (i, 0, 0, 0))],
        out_shape=[jax.ShapeDtypeStruct((b, t, hp), BF16),
                   jax.ShapeDtypeStruct((b, 2, SSD_STATE, hp), F32)],
        scratch_shapes=[pltpu.VMEM((SSD_STATE, hp), F32),
                        pltpu.VMEM((SSD_STATE, hp), F32),
                        pltpu.VMEM((t // CHUNK, SSD_STATE, hp), F32)],
        compiler_params=_cparams(("parallel", "arbitrary", "arbitrary")),
        name="ssd_scan",
    )(xbc, xbc, xbc, z, dt, lw["conv_w"], lw["conv_b"], lw["dt_bias"], lw["a_log"], lw["d_vec"], lw["ssd_nw"], init)


def _mix_residual(x_ref, att_ref, ret_ref, ssd_ref, mod_ref, n2_ref, wo_ref):
    mix = jnp.concatenate([att_ref[0], ret_ref[0], ssd_ref[0]], axis=-1)
    x1 = x_ref[0] + mod_ref[0, 2:3, :] * _dot(mix, wo_ref[...])
    h = _rms(x1, n2_ref[...]) * (1.0 + mod_ref[0, 4:5, :]) + mod_ref[0, 3:4, :]
    return x1, h


def _dense_kernel(x_ref, att_ref, ret_ref, ssd_ref, mod_ref, n2_ref, wo_ref, wg_ref, wu_ref, wd_ref, o_ref,
                  *, ff_chunk):
    x1, h = _mix_residual(x_ref, att_ref, ret_ref, ssd_ref, mod_ref, n2_ref, wo_ref)
    hb = h.astype(BF16)
    ff = wg_ref.shape[1]
    f = jnp.zeros_like(x1)
    for s in range(0, ff, ff_chunk):
        g = _dot(hb, wg_ref[:, s:s + ff_chunk])
        u = _dot(hb, wu_ref[:, s:s + ff_chunk])
        f = f + _dot((_silu(g) * u).astype(BF16), wd_ref[s:s + ff_chunk, :])
    o_ref[0] = x1 + mod_ref[0, 5:6, :] * f


def _dense_block(xs, att, ret, ssd, mods, lw, tm):
    b, t, d = xs.shape
    ff = lw["wg"].shape[1]
    tok = lambda w: pl.BlockSpec((1, tm, w), lambda i, j: (i, j, 0))
    const = lambda shape: pl.BlockSpec(shape, lambda i, j: (0,) * len(shape), pipeline_mode=pl.Buffered(1))
    return pl.pallas_call(
        functools.partial(_dense_kernel, ff_chunk=ff // 2),
        grid=(b, t // tm),
        in_specs=[tok(d), tok(att.shape[-1]), tok(ret.shape[-1]), tok(ssd.shape[-1]),
                  pl.BlockSpec((1, 8, d), lambda i, j: (i, 0, 0)), const((1, d)),
                  const((d, d)), const((d, ff)), const((d, ff)), const((ff, d))],
        out_specs=tok(d),
        out_shape=jax.ShapeDtypeStruct((b, t, d), F32),
        compiler_params=_cparams(("parallel", "parallel")),
        name="outproj_ffn",
    )(xs, att, ret, ssd, mods, lw["n2"], lw["wo"], lw["wg"], lw["wu"], lw["wd"])


SUBLANES = 8
TOP_K = 2


def _route_kernel(x_ref, att_ref, ret_ref, ssd_ref, mod_ref, n2_ref, wo_ref, wr_ref,
                  x1_ref, h3_ref, slot_ref, gate_ref, cnt_ref):
    x1, h = _mix_residual(x_ref, att_ref, ret_ref, ssd_ref, mod_ref, n2_ref, wo_ref)
    x1_ref[0] = x1
    tm, d = h.shape
    nsub = d // LANES
    for s in range(nsub):
        h3_ref[0, pl.ds(s, tm, stride=nsub), :] = h[:, s * LANES:(s + 1) * LANES]
    h_hi = h.astype(BF16)
    h_lo = (h - h_hi.astype(F32)).astype(BF16)
    logits = _dot(h_hi, wr_ref[0]) + _dot(h_lo, wr_ref[0]) + _dot(h_hi, wr_ref[1])
    lane = _iota(logits.shape, 1).astype(F32)
    logits = jnp.where(lane < N_EXPERTS, logits, NEG)
    m1 = jnp.max(logits, axis=-1, keepdims=True)
    i1 = jnp.min(jnp.where(logits == m1, lane, float(LANES)), axis=-1, keepdims=True)
    rest = jnp.where(lane == i1, NEG, logits)
    m2 = jnp.max(rest, axis=-1, keepdims=True)
    i2 = jnp.min(jnp.where(rest == m2, lane, float(LANES)), axis=-1, keepdims=True)
    e2 = jnp.exp(m2 - m1)
    den = 1.0 + e2
    sel1, sel2 = lane == i1, lane == i2
    chosen = jnp.where(sel1 | sel2, 1.0, 0.0)
    before = (_iota((tm, tm), 0) > _iota((tm, tm), 1)).astype(BF16)
    pos = _dot(before, chosen.astype(BF16))
    cnt = jnp.sum(chosen, axis=0, keepdims=True)
    nblk = sum(jnp.where(cnt > float(k * MOE_ROWS), 1.0, 0.0) for k in range(pl.cdiv(tm, MOE_ROWS)))
    lower_e = (_iota((LANES, LANES), 0) < _iota((LANES, LANES), 1)).astype(BF16)
    off = _dot_exact_rhs01(jnp.broadcast_to(nblk * float(MOE_ROWS), (SUBLANES, LANES)), lower_e)[0:1]
    slot = off + pos
    s1 = jnp.sum(jnp.where(sel1, slot, 0.0), axis=-1, keepdims=True)
    s2 = jnp.sum(jnp.where(sel2, slot, 0.0), axis=-1, keepdims=True)
    lane_i = _iota(logits.shape, 1)
    slots = jnp.where(lane_i == 0, s1, jnp.where(lane_i == 1, s2, 0.0))
    slot_ref[0, 0] = slots.T[0:TOP_K].astype(jnp.int32)
    gate_ref[0] = jnp.where(lane_i == 0, 1.0 / den, jnp.where(lane_i == 1, e2 / den, 0.0))
    row = _iota((SUBLANES, LANES), 0)
    cnt_ref[0, 0] = jnp.where(row == 0, cnt, jnp.where(row == 1, off, jnp.where(row == 2, nblk, 0.0))
                              ).astype(jnp.int32)


def _route_block(xs, att, ret, ssd, mods, lw, tm):
    b, t, d = xs.shape
    nt = t // tm
    nsub = d // LANES
    tok = lambda w: pl.BlockSpec((1, tm, w), lambda i, j: (i, j, 0))
    const = lambda shape: pl.BlockSpec(shape, lambda i, j: (0,) * len(shape))
    per_tile = lambda r, w: pl.BlockSpec((1, 1, r, w), lambda i, j: (i, j, 0, 0))
    return pl.pallas_call(
        _route_kernel,
        grid=(b, nt),
        in_specs=[tok(d), tok(att.shape[-1]), tok(ret.shape[-1]), tok(ssd.shape[-1]),
                  pl.BlockSpec((1, 8, d), lambda i, j: (i, 0, 0)), const((1, d)),
                  const((d, d)), const((2, d, LANES))],
        out_specs=[tok(d), pl.BlockSpec((1, tm * nsub, LANES), lambda i, j: (i, j, 0)),
                   per_tile(TOP_K, tm), tok(LANES), per_tile(SUBLANES, LANES)],
        out_shape=[jax.ShapeDtypeStruct((b, t, d), F32), jax.ShapeDtypeStruct((b, t * nsub, LANES), F32),
                   jax.ShapeDtypeStruct((b, nt, TOP_K, tm), jnp.int32),
                   jax.ShapeDtypeStruct((b, t, LANES), F32),
                   jax.ShapeDtypeStruct((b, nt, SUBLANES, LANES), jnp.int32)],
        compiler_params=_cparams(("parallel", "parallel")),
        name="outproj_route",
    )(xs, att, ret, ssd, mods, lw["n2"], lw["wo"], lw["router"])


MOE_UNROLL_LOG2 = 5
MOE_UNROLL = 1 << MOE_UNROLL_LOG2
MOE_ROWS = 320


def _moe_kernel(slot_ref, cnt_ref, h3_ref, wgu_ref, wd_ref, o3_ref, code_ref, g_ref, y3_ref, *, tm, ff_split):
    e = pl.program_id(2)
    rb = MOE_ROWS
    nsub = SUBLANES
    spare = TOP_K * tm

    @pl.when(e == 0)
    def _():
        def invert(t, carry):
            for k in range(TOP_K):
                code_ref[slot_ref[0, 0, k, t]] = k * tm + t
            return carry

        lax.fori_loop(0, tm, invert, 0, unroll=MOE_UNROLL)

    n = cnt_ref[0, 0, 0, e]
    off = cnt_ref[0, 0, 1, e]
    nblocks = cnt_ref[0, 0, 2, e]
    end = off + nblocks * rb

    def pad_fill(c, carry):
        for j in range(MOE_UNROLL):
            code_ref[jnp.minimum(off + n + c * MOE_UNROLL + j, end - 1)] = spare
        return carry

    lax.fori_loop(0, lax.shift_right_logical(end - off - n + (MOE_UNROLL - 1), jnp.int32(MOE_UNROLL_LOG2)),
                  pad_fill, 0)

    def tile_rows(i):
        return pl.ds(pl.multiple_of(i * nsub, nsub), nsub)

    def block(bi, carry):
        base = off + bi * rb

        def gather(i, c):
            t = code_ref[base + i] & (tm - 1)
            g_ref[tile_rows(i), :] = h3_ref[0, tile_rows(t), :]
            return c

        lax.fori_loop(0, rb, gather, 0, unroll=MOE_UNROLL)
        lhs =jnp.concatenate([g_ref[pl.ds(s, rb, stride=nsub), :] for s in range(nsub)], axis=1).astype(BF16)
        y = None
        for s0, s1 in ff_split:
            w = s1 - s0
            gu = _dot(lhs, wgu_ref[0, :, 2 * s0:2 * s1])
            part = _dot((_silu(gu[:, :w]) * gu[:, w:]).astype(BF16), wd_ref[0, s0:s1, :])
            y = part if y is None else y + part
        for s in range(nsub):
            y3_ref[pl.ds(s, rb, stride=nsub), :] = y[:, s * LANES:(s + 1) * LANES]

        def scatter(i, c):
            o3_ref[0, tile_rows(code_ref[base + i]), :] = y3_ref[tile_rows(i), :]
            return c

        lax.fori_loop(0, rb, scatter, 0, unroll=MOE_UNROLL)
        return carry

    lax.fori_loop(0, nblocks, block, 0)

    @pl.when(e == N_EXPERTS - 1)
    def _():
        o3_ref[0, spare * nsub:, :] = jnp.zeros((o3_ref.shape[1] - spare * nsub, LANES), F32)


def _moe_out_kernel(x1_ref, a3_ref, gate_ref, mod_ref, fn_ref, o_ref, *, final_norm):
    tm, d = x1_ref.shape[1:]
    nsub = d // LANES
    gates = gate_ref[0]
    acc = None
    for k in range(TOP_K):
        yk = jnp.concatenate([a3_ref[0, pl.ds(k * tm * nsub + s, tm, stride=nsub), :] for s in range(nsub)], axis=1)
        term = gates[:, k:k + 1] * yk
        acc = term if acc is None else acc + term
    out = x1_ref[0] + mod_ref[0, 5:6, :] * acc
    o_ref[0] = _rms(out, fn_ref[...]) if final_norm else out


MOE_FF_SLICE = 512


def _moe_ff_split(ff):
    return tuple((s, min(s + MOE_FF_SLICE, ff)) for s in range(0, ff, MOE_FF_SLICE))


def _moe_weights(w_gate, w_up, w_down):
    ff = w_gate.shape[-1]
    wgu = jnp.concatenate([w[..., s0:s1] for s0, s1 in _moe_ff_split(ff) for w in (w_gate, w_up)], axis=-1)
    return wgu.astype(BF16), w_down.astype(BF16)


def _moe_block(x1, h3, slots, gates, counts, mods, lw, fnw, final_norm, tm):
    b, t, d = x1.shape
    ff = lw["moe_wd"].shape[1]
    nsub = d // LANES
    assert nsub == SUBLANES and MOE_ROWS % (2 * SUBLANES) == 0 and tm & (tm - 1) == 0
    assert MOE_ROWS % MOE_UNROLL == 0 and tm % MOE_UNROLL == 0
    nt = t // tm
    out_rows = (TOP_K * tm + 1) * nsub
    smem = lambda r, w: pl.BlockSpec((1, 1, r, w), lambda i, j, e: (i, j, 0, 0), memory_space=pltpu.SMEM)
    y3 = pl.pallas_call(
        functools.partial(_moe_kernel, tm=tm, ff_split=_moe_ff_split(ff)),
        grid=(b, nt, N_EXPERTS),
        in_specs=[smem(TOP_K, tm), smem(SUBLANES, LANES),
                  pl.BlockSpec((1, tm * nsub, LANES), lambda i, j, e: (i, j, 0)),
                  pl.BlockSpec((1, d, 2 * ff), lambda i, j, e: (e, 0, 0)),
                  pl.BlockSpec((1, ff, d), lambda i, j, e: (e, 0, 0))],
        out_specs=pl.BlockSpec((1, out_rows, LANES), lambda i, j, e: (i, j, 0)),
        out_shape=jax.ShapeDtypeStruct((b, nt * out_rows, LANES), F32),
        scratch_shapes=[pltpu.SMEM((TOP_K * tm + N_EXPERTS * MOE_ROWS,), jnp.int32),
                        pltpu.VMEM((MOE_ROWS * nsub, LANES), F32), pltpu.VMEM((MOE_ROWS * nsub, LANES), F32)],
        compiler_params=_cparams(("parallel", "parallel", "arbitrary")),
        name="moe",
    )(slots, counts, h3, lw["moe_wgu"], lw["moe_wd"])
    tok = lambda w: pl.BlockSpec((1, tm, w), lambda i, j: (i, j, 0))
    return pl.pallas_call(
        functools.partial(_moe_out_kernel, final_norm=final_norm),
        grid=(b, nt),
        in_specs=[tok(d), pl.BlockSpec((1, out_rows, LANES), lambda i, j: (i, j, 0)), tok(LANES),
                  pl.BlockSpec((1, 8, d), lambda i, j: (i, 0, 0)), pl.BlockSpec((1, d), lambda i, j: (0, 0))],
        out_specs=tok(d),
        out_shape=jax.ShapeDtypeStruct((b, t, d), F32),
        compiler_params=_cparams(("parallel", "parallel")),
        name="moe_out",
    )(x1, y3, gates, mods, fnw)


def _final_norm_kernel(x_ref, w_ref, o_ref):
    o_ref[0] = _rms(x_ref[0], w_ref[...])


def _final_norm(xs, fnw, tm):
    b, t, d = xs.shape
    return pl.pallas_call(
        _final_norm_kernel,
        grid=(b, t // tm),
        in_specs=[pl.BlockSpec((1, tm, d), lambda i, j: (i, j, 0)), pl.BlockSpec((1, d), lambda i, j: (0, 0))],
        out_specs=pl.BlockSpec((1, tm, d), lambda i, j: (i, j, 0)),
        out_shape=jax.ShapeDtypeStruct((b, t, d), F32),
        compiler_params=_cparams(("parallel", "parallel")),
        name="final_norm",
    )(xs, fnw)


def _rot_cols(w):
    a, b2, c, d = jnp.split(w, 4, axis=-1)
    return jnp.concatenate([-b2, a, -d, c], axis=-1)


def _layer_weights(i, norm1_w, norm2_w, w_in, w_out, mla_q_norm_w, mla_w_qb, mla_kv_norm_w, mla_w_kb, mla_w_vb,
                   ret_decay_logit, ssd_conv_w, ssd_conv_b, ssd_dt_bias, ssd_a_log, ssd_d, ssd_norm_w):
    d = w_in.shape[1]
    offs = np.cumsum((0,) + IN_SIZES)
    seg = [w_in[i][:, offs[j]:offs[j + 1]] for j in range(len(IN_SIZES))]
    zeros = lambda n: jnp.zeros((d, n), F32)
    w_all = jnp.concatenate(
        [seg[0], seg[1], seg[2], _rot_cols(seg[2]), zeros(W_MLA - MLA_Q_LORA - MLA_KV_LORA - 2 * MLA_ROPE),
         seg[3] * (RET_QK ** -0.5), seg[4], seg[5], seg[6], seg[7], seg[8], seg[9], seg[10],
         zeros(W_DT - 2 * SSD_HEADS)], axis=1).astype(BF16)

    dh = MLA_NOPE + MLA_ROPE
    wqb = mla_w_qb[i].reshape(MLA_Q_LORA, MLA_HEADS, dh)
    pad = lambda a, lo, hi: jnp.pad(a, ((0, 0), (0, 0), (lo, hi)))
    wq = pad(wqb, 0, MLA_HP - dh).reshape(MLA_Q_LORA, MLA_HEADS * MLA_HP).astype(BF16)
    wqr = pad(_rot_cols(wqb[:, :, MLA_NOPE:]), MLA_NOPE, MLA_HP - dh).reshape(MLA_Q_LORA, MLA_HEADS * MLA_HP).astype(BF16)
    wkb = pad(mla_w_kb[i].reshape(MLA_KV_LORA, MLA_HEADS, MLA_NOPE), 0, MLA_HP - MLA_NOPE)
    place = np.zeros((MLA_ROPE, MLA_HEADS, MLA_HP), np.float32)
    for r in range(MLA_ROPE):
        place[r, :, MLA_NOPE + r] = 1.0
    place = jnp.asarray(place)
    wk = jnp.concatenate([wkb, place, place, jnp.zeros((MLA_KV_LORA - 2 * MLA_ROPE, MLA_HEADS, MLA_HP), F32)],
                         axis=0).reshape(2 * MLA_KV_LORA, MLA_HEADS * MLA_HP).astype(BF16)

    wv = mla_w_vb[i].astype(BF16)

    lg = ret_decay_logit[i]
    padl = lambda v: jnp.pad(v.reshape(1, -1), ((0, 0), (0, LANES - 2 * SSD_HEADS)))
    return {
        "n1": norm1_w[i].reshape(1, d), "n2": norm2_w[i].reshape(1, d),
        "w_all": w_all, "qn": mla_q_norm_w[i].reshape(1, -1), "wq": wq, "wqr": wqr,
        "kvn": mla_kv_norm_w[i].reshape(1, -1), "wk": wk, "wv": wv,
        "wo": w_out[i].astype(BF16),
        "ret_lgl": jnp.repeat(lg, RET_V, axis=1),
        "ret_lgh": jnp.broadcast_to(lg.reshape(2 * RET_HEADS, 1), (2 * RET_HEADS, CHUNK)),
        "conv_w": jnp.pad(ssd_conv_w[i], ((0, 8 - ssd_conv_w.shape[1]), (0, 0))),
        "conv_b": ssd_conv_b[i].reshape(1, -1),
        "dt_bias": padl(ssd_dt_bias[i]), "a_log": padl(ssd_a_log[i]),
        "d_vec": jnp.repeat(ssd_d[i], SSD_HEADDIM).reshape(1, -1),
        "ssd_nw": ssd_norm_w[i].reshape(1, -1),
    }


def _rope_tables(n_lat, n_ctx):
    rows = n_lat // GRID_W
    row = jnp.repeat(jnp.arange(rows, dtype=F32), GRID_W)
    col = jnp.tile(jnp.arange(GRID_W, dtype=F32), rows)
    inv_freq = ROPE_BASE ** (-jnp.arange(0, ROPE_AXIS_DIM, 2, dtype=F32) / ROPE_AXIS_DIM)
    ang_r = row[:, None] * inv_freq
    ang_c = col[:, None] * inv_freq
    ang = jnp.concatenate([ang_r, ang_r, ang_c, ang_c], axis=-1)
    c2 = MLA_SCALE * math.log2(math.e)

    def build(cos, sin):
        n = cos.shape[0]
        cosq = jnp.concatenate([jnp.full((n, MLA_NOPE), c2, F32), c2 * cos, jnp.zeros((n, MLA_HP - MLA_NOPE - MLA_ROPE), F32)], 1)
        sinq = jnp.concatenate([jnp.zeros((n, MLA_NOPE), F32), c2 * sin, jnp.zeros((n, MLA_HP - MLA_NOPE - MLA_ROPE), F32)], 1)
        ck = jnp.concatenate([cos, sin, jnp.zeros((n, LANES - 2 * MLA_ROPE), F32)], 1)
        return cosq, sinq, ck

    lat = build(jnp.cos(ang), jnp.sin(ang))
    ctx = build(jnp.ones((n_ctx, MLA_ROPE), F32), jnp.zeros((n_ctx, MLA_ROPE), F32))
    return lat, ctx


def _token_tile(t, pref):
    tm = min(pref, t)
    while t % tm:
        tm //= 2
    return tm


def kernel(x, c, ctx, c_ctx, mod_w, mod_b, norm1_w, norm2_w, w_in, w_out, mla_q_norm_w, mla_w_qb, mla_kv_norm_w, mla_w_kb, mla_w_vb, ret_decay_logit, ssd_conv_w, ssd_conv_b, ssd_dt_bias, ssd_a_log, ssd_d, ssd_norm_w, ffn_w_gate, ffn_w_up, ffn_w_down, moe_router, moe_w_gate, moe_w_up, moe_w_down, final_norm_w):
    b, n_lat, d = x.shape
    n_ctx = ctx.shape[1]
    depth = mod_w.shape[0]
    assert b + 1 <= 8 and n_lat % CHUNK == 0 and n_ctx % CHUNK == 0 and n_lat % GRID_W == 0

    cond = jnp.concatenate([c, c_ctx[None, :], jnp.zeros((8 - b - 1, d), F32)], axis=0)
    mod_all = _modulation(cond, mod_w, mod_b)
    tabs_x, tabs_c = _rope_tables(n_lat, n_ctx)
    fnw = final_norm_w.reshape(1, d)

    tm_x = _token_tile(n_lat, 512)
    tm_c = _token_tile(n_ctx, 256)
    tm_moe = _token_tile(n_lat, 1024)
    tq_x = _token_tile(n_lat, 2048)
    zero_ret = jnp.zeros((b, 2, RET_QK, RET_HEADS * RET_V), F32)
    zero_ssd = jnp.zeros((b, 2, SSD_STATE, SSD_INNER), F32)

    h_ctx = ctx
    for i in range(depth):
        last = i == depth - 1
        lw = _layer_weights(i, norm1_w, norm2_w, w_in, w_out, mla_q_norm_w, mla_w_qb, mla_kv_norm_w, mla_w_kb,
                            mla_w_vb, ret_decay_logit, ssd_conv_w, ssd_conv_b, ssd_dt_bias, ssd_a_log, ssd_d,
                            ssd_norm_w)
        m6 = mod_all[i].reshape(8, 6, d)
        pad2 = jnp.zeros((b, 2, d), F32)
        mods_x = jnp.concatenate([m6[:b], pad2], axis=1)
        mods_c = jnp.concatenate([jnp.broadcast_to(m6[b:b + 1], (b, 6, d)), pad2], axis=1)

        qc, kc, vc, retc, zc, xbcc, dtc = _inproj(h_ctx, mods_c, lw, tabs_c, tm_c)
        ret_yc, ret_fin = _ret_scan(retc, lw, zero_ret)
        ssd_yc, ssd_fin = _ssd_scan(xbcc, zc, dtc, lw, zero_ssd)

        qx, kx, vx, retx, zx, xbcx, dtx = _inproj(x, mods_x, lw, tabs_x, tm_x)
        att_x = _attention(qx, [(kc, vc), (kx, vx)], tq_x)
        ret_yx, _ = _ret_scan(retx, lw, ret_fin)
        ssd_yx, _ = _ssd_scan(xbcx, zx, dtx, lw, ssd_fin)

        if i % 2 == 0:
            j = i // 2
            lw.update(wg=ffn_w_gate[j].astype(BF16), wu=ffn_w_up[j].astype(BF16), wd=ffn_w_down[j].astype(BF16))
            x = _dense_block(x, att_x, ret_yx, ssd_yx, mods_x, lw, tm_x)
            if last:
                x = _final_norm(x, fnw, tm_x)
        else:
            j = i // 2
            wgu, wd = _moe_weights(moe_w_gate[j], moe_w_up[j], moe_w_down[j])
            r_f32 = jnp.pad(moe_router[j], ((0, 0), (0, LANES - N_EXPERTS)))
            r_hi = r_f32.astype(BF16)
            r_lo = (r_f32 - r_hi.astype(F32)).astype(BF16)
            lw.update(router=jnp.stack([r_hi, r_lo]), moe_wgu=wgu, moe_wd=wd)
            x1, *routed = _route_block(x, att_x, ret_yx, ssd_yx, mods_x, lw, tm_moe)
            x = _moe_block(x1, *routed, mods_x, lw, fnw, last, tm_moe)

        if not last:
            att_c = _attention(qc, [(kc, vc)], tm_c)
            if i % 2 == 0:
                h_ctx = _dense_block(h_ctx, att_c, ret_yc, ssd_yc, mods_c, lw, tm_c)
            else:
                c1, *routed = _route_block(h_ctx, att_c, ret_yc, ssd_yc, mods_c, lw, tm_c)
                h_ctx = _moe_block(c1, *routed, mods_c, lw, fnw, False, tm_c)
    return x
```

```python
import functools
import math

import jax
import jax.numpy as jnp
import numpy as np
from jax import lax
from jax.experimental import pallas as pl
from jax.experimental.pallas import tpu as pltpu

F32 = jnp.float32
BF16 = jnp.bfloat16
HIGHEST = lax.Precision.HIGHEST

EPS = 1e-6
CHUNK = 128
GRID_W = 64
MLA_HEADS = 4
MLA_Q_LORA = 256
MLA_KV_LORA = 128
MLA_NOPE = 64
MLA_ROPE = 32
MLA_V = 64
MLA_SCALE = (MLA_NOPE + MLA_ROPE) ** -0.5
ROPE_AXIS_DIM = MLA_ROPE // 2
ROPE_BASE = 10000.0
RET_HEADS = 4
RET_QK = 64
RET_V = 64
SSD_HEADS = 8
SSD_HEADDIM = 64
SSD_GROUPS = 2
SSD_STATE = 64
SSD_INNER = SSD_HEADS * SSD_HEADDIM
SSD_CONV_CH = SSD_INNER + 2 * SSD_GROUPS * SSD_STATE
N_EXPERTS = 8
IN_SIZES = (MLA_Q_LORA, MLA_KV_LORA, MLA_ROPE,
            RET_HEADS * RET_QK, RET_HEADS * RET_QK, RET_HEADS * RET_V, RET_HEADS * RET_V,
            SSD_INNER, SSD_CONV_CH, SSD_HEADS, SSD_HEADS)

LANES = 128
V7X_VMEM_LIMIT = 56 * 1024 * 1024

MLA_HP = 128
NEG = -1e30


def _cparams(sem):
    return pltpu.CompilerParams(dimension_semantics=sem, vmem_limit_bytes=V7X_VMEM_LIMIT)


def _dot(a, b):
    return jnp.dot(a, b, preferred_element_type=F32)


def _dot_nt(a, b):
    return lax.dot_general(a, b, (((1,), (1,)), ((), ())), preferred_element_type=F32)


def _dot_tn(a, b):
    return lax.dot_general(a, b, (((0,), (0,)), ((), ())), preferred_element_type=F32)


def _split3(x):
    hi = x.astype(BF16)
    r1 = x - hi.astype(F32)
    mid = r1.astype(BF16)
    lo = (r1 - mid.astype(F32)).astype(BF16)
    return hi, mid, lo


def _dot_exact_rhs01(x, e):
    hi, mid, lo = _split3(x)
    return _dot(hi, e) + _dot(mid, e) + _dot(lo, e)


def _dot_exact_lhs01(e, x):
    hi, mid, lo = _split3(x)
    return _dot(e, hi) + _dot(e, mid) + _dot(e, lo)


def _silu(x):
    return x / (1.0 + jnp.exp(-x))


def _rms(x, w):
    return x * lax.rsqrt(jnp.mean(x * x, axis=-1, keepdims=True) + EPS) * w


def _iota(shape, dim):
    return lax.broadcasted_iota(jnp.int32, shape, dim)


def _mod_kernel(a_ref, w_ref, b_ref, o_ref):
    a = _silu(a_ref[...])
    o_ref[0] = jnp.dot(a, w_ref[0], precision=HIGHEST, preferred_element_type=F32) + b_ref[0]


def _modulation(cond, mod_w, mod_b):
    depth, d, n = mod_w.shape
    tn = 1536
    return pl.pallas_call(
        _mod_kernel,
        grid=(depth, n // tn),
        in_specs=[pl.BlockSpec((8, d), lambda i, j: (0, 0)),
                  pl.BlockSpec((1, d, tn), lambda i, j: (i, 0, j)),
                  pl.BlockSpec((1, 1, tn), lambda i, j: (i, 0, j))],
        out_specs=pl.BlockSpec((1, 8, tn), lambda i, j: (i, 0, j)),
        out_shape=jax.ShapeDtypeStruct((depth, 8, n), F32),
        compiler_params=_cparams(("parallel", "parallel")),
        name="modulation",
    )(cond, mod_w, mod_b.reshape(depth, 1, n))


W_MLA, W_RET, W_Z, W_XBC, W_DT = 512, 1024, 512, 768, 128
W_ALL = W_MLA + W_RET + W_Z + W_XBC + W_DT


def _ones_lane_mask(shape):
    lane = _iota(shape, len(shape) - 1) % (2 * MLA_HP)
    return (lane == MLA_HP - 1) | (lane == MLA_HP)


def _inproj_kernel(x_ref, mod_ref, n1_ref, win_ref, qn_ref, wq_ref, wqr_ref, kvn_ref, wk_ref, wv_ref,
                   cq_ref, sq_ref, ck_ref,
                   q_out, k_out, v_out, ret_out, z_out, xbc_out, dt_out):
    x = x_ref[0]
    h = _rms(x, n1_ref[...]) * (1.0 + mod_ref[0, 1:2, :]) + mod_ref[0, 0:1, :]
    hb = h.astype(BF16)

    o = 0
    pm = _dot(hb, win_ref[:, o:o + W_MLA]); o += W_MLA
    ret_out[0] = _dot(hb, win_ref[:, o:o + W_RET]).astype(BF16); o += W_RET
    z_out[0] = _dot(hb, win_ref[:, o:o + W_Z]).astype(BF16); o += W_Z
    xbc_out[0] = _dot(hb, win_ref[:, o:o + W_XBC]).astype(BF16); o += W_XBC
    dt_out[0] = _dot(hb, win_ref[:, o:o + W_DT])

    cqb = _rms(pm[:, 0:MLA_Q_LORA], qn_ref[...]).astype(BF16)
    qm = _dot(cqb, wq_ref[...])
    qr = _dot(cqb, wqr_ref[...])
    cosq, sinq = cq_ref[...], sq_ref[...]
    for hh in range(MLA_HEADS):
        sl = slice(hh * MLA_HP, (hh + 1) * MLA_HP)
        q_out[0, :, sl] = (qm[:, sl] * cosq + qr[:, sl] * sinq).astype(BF16)

    ckvn = _rms(pm[:, MLA_Q_LORA:MLA_Q_LORA + MLA_KV_LORA], kvn_ref[...])
    u = pm[:, MLA_Q_LORA + MLA_KV_LORA:W_MLA] * ck_ref[...]
    kin = jnp.concatenate([ckvn, u], axis=-1).astype(BF16)
    k_out[0] = _dot(kin, wk_ref[...]).astype(BF16)
    v = _dot(ckvn.astype(BF16), wv_ref[...])
    v_out[0] = jnp.where(_ones_lane_mask(v.shape), 1.0, v).astype(BF16)


def _inproj(xs, mods, lw, tabs, tm):
    b, t, d = xs.shape
    const = lambda shape: pl.BlockSpec(shape, lambda i, j: (0,) * len(shape))
    tok = lambda w: pl.BlockSpec((1, tm, w), lambda i, j: (i, j, 0))
    tab = pl.BlockSpec((tm, LANES), lambda i, j: (j, 0))
    widths = (MLA_HEADS * MLA_HP, MLA_HEADS * MLA_HP, MLA_HEADS * MLA_HP, W_RET, W_Z, W_XBC, W_DT)
    dtypes = (BF16, BF16, BF16, BF16, BF16, BF16, F32)
    return pl.pallas_call(
        _inproj_kernel,
        grid=(b, t // tm),
        in_specs=[tok(d), pl.BlockSpec((1, 8, d), lambda i, j: (i, 0, 0)), const((1, d)),
                  const((d, W_ALL)), const((1, MLA_Q_LORA)), const((MLA_Q_LORA, MLA_HEADS * MLA_HP)),
                  const((MLA_Q_LORA, MLA_HEADS * MLA_HP)), const((1, MLA_KV_LORA)),
                  const((2 * MLA_KV_LORA, MLA_HEADS * MLA_HP)), const((MLA_KV_LORA, MLA_HEADS * MLA_HP)),
                  tab, tab, tab],
        out_specs=[tok(w) for w in widths],
        out_shape=[jax.ShapeDtypeStruct((b, t, w), dt) for w, dt in zip(widths, dtypes)],
        compiler_params=_cparams(("parallel", "parallel")),
        name="inproj",
    )(xs, mods, lw["n1"], lw["w_all"], lw["qn"], lw["wq"], lw["wqr"], lw["kvn"], lw["wk"], lw["wv"], *tabs)


def _attn_kernel(*refs, seg_blocks, tq):
    nseg = len(seg_blocks)
    q_ref = refs[0]
    kv_refs = refs[1:1 + 2 * nseg]
    o_ref = refs[1 + 2 * nseg]
    carry = tuple((jnp.full((tq, 1), NEG, F32), jnp.zeros((tq, MLA_HP), F32)) for _ in range(MLA_HEADS))
    for si, (tk, nblk) in enumerate(seg_blocks):
        k_ref, v_ref = kv_refs[2 * si], kv_refs[2 * si + 1]

        def body(j, carry, k_ref=k_ref, v_ref=v_ref, tk=tk):
            start = 0 if isinstance(j, int) else pl.multiple_of(j * tk, tk)
            new = []
            for hh in range(MLA_HEADS):
                sl = slice(hh * MLA_HP, (hh + 1) * MLA_HP)
                m, acc = carry[hh]
                s = _dot_nt(q_ref[0, :, sl], k_ref[0, pl.ds(start, tk), sl])
                m_new = jnp.maximum(m, jnp.max(s, axis=-1, keepdims=True))
                p = jnp.exp2(s - m_new).astype(BF16)
                acc = jnp.exp2(m - m_new) * acc + _dot(p, v_ref[0, pl.ds(start, tk), sl])
                new.append((m_new, acc))
            return tuple(new)

        carry = body(0, carry) if nblk == 1 else lax.fori_loop(0, nblk, body, carry)
    lane = _iota((tq, MLA_HP), 1)
    pairs = []
    for hh in range(0, MLA_HEADS, 2):
        acc_e, acc_o = carry[hh][1], carry[hh + 1][1]
        l_e = jnp.sum(jnp.where(lane == MLA_HP - 1, acc_e, 0.0), axis=-1, keepdims=True)
        l_o = jnp.sum(jnp.where(lane == 0, acc_o, 0.0), axis=-1, keepdims=True)
        pairs.append(jnp.where(lane < MLA_V, acc_e / l_e, acc_o / l_o))
    o_ref[0] = jnp.concatenate(pairs, axis=-1).astype(BF16)


def _attention(q, segs, tq, tk):
    b, l, _ = q.shape
    seg_blocks = []
    in_specs = [pl.BlockSpec((1, tq, MLA_HEADS * MLA_HP), lambda i, j: (i, j, 0))]
    args = [q]
    for k, v in segs:
        s = k.shape[1]
        t = min(tk, s)
        seg_blocks.append((t, s // t))
        for _ in range(2):
            in_specs.append(pl.BlockSpec((1, s, MLA_HEADS * MLA_HP), lambda i, j: (i, 0, 0),
                                         pipeline_mode=pl.Buffered(1)))
        args += [k, v]
    return pl.pallas_call(
        functools.partial(_attn_kernel, seg_blocks=tuple(seg_blocks), tq=tq),
        grid=(b, l // tq),
        in_specs=in_specs,
        out_specs=pl.BlockSpec((1, tq, MLA_HEADS * MLA_V), lambda i, j: (i, j, 0)),
        out_shape=jax.ShapeDtypeStruct((b, l, MLA_HEADS * MLA_V), BF16),
        compiler_params=_cparams(("parallel", "arbitrary")),
        name="attention",
    )(*args)


def _ret_kernel(x_ref, lgl_ref, lgh_ref, init_ref, y_ref, fin_ref, s_ref, g_ref, gbuf_ref, *, nblk, sub):
    C, H, N, P = CHUNK, RET_HEADS, RET_QK, RET_V
    HP = H * P
    ph, c = pl.program_id(1), pl.program_id(2)
    blk = jnp.where(ph == 0, nblk - 1 - c, c)

    def log_sigmoid(v):
        return jnp.minimum(v, 0.0) - jnp.log(1.0 + jnp.exp(-jnp.abs(v)))

    lgl = log_sigmoid(lgl_ref[...])
    lgf, lgb = lgl[0:1], lgl[1:2]
    rows = _iota((C, HP), 0).astype(F32)
    head_n = _iota((N, HP), 1) // P

    def diag_blocks(r):
        out = r[0:N]
        for hh in range(1, H):
            out = jnp.where(head_n == hh, r[hh * N:(hh + 1) * N], out)
        return out

    def as_rows(s_c):
        return jnp.concatenate([jnp.where(head_n == hh, s_c, 0.0) for hh in range(H)], axis=0)

    @pl.when((ph == 0) & (c == 0))
    def _():
        g_ref[...] = init_ref[0, 1]

    @pl.when(ph == 0)
    def _():
        w_b = jnp.exp(rows * lgb)
        tot_b = jnp.exp(C * lgb)
        for sb in reversed(range(sub)):
            r0 = sb * C
            k = x_ref[0, r0:r0 + C, HP:2 * HP].astype(F32)
            v = x_ref[0, r0:r0 + C, 2 * HP:3 * HP]
            g = g_ref[...]
            gbuf_ref[blk * sub + sb] = g
            g_ref[...] = g * tot_b + diag_blocks(_dot_tn((k * w_b).astype(BF16), v))

        @pl.when(c == nblk - 1)
        def _():
            fin_ref[0, 1] = g_ref[...]

    @pl.when((ph == 1) & (c == 0))
    def _():
        s_ref[...] = init_ref[0, 0]

    @pl.when(ph == 1)
    def _():
        lgh = log_sigmoid(lgh_ref[...])
        grp = _iota((C, HP), 1) // P
        di = (_iota((C, C), 0) - _iota((C, C), 1)).astype(F32)
        decay = [jnp.exp(jnp.where(di >= 0, di * lgh[hh:hh + 1, :], -di * lgh[H + hh:H + hh + 1, :]))
                 for hh in range(H)]
        e_f = jnp.exp((rows + 1.0) * lgf)
        e_b = jnp.exp((C - rows) * lgb)
        w_f = jnp.exp((C - 1.0 - rows) * lgf)
        tot_f = jnp.exp(C * lgf)
        for sb in range(sub):
            r0 = sb * C
            q = x_ref[0, r0:r0 + C, 0:HP]
            kb = x_ref[0, r0:r0 + C, HP:2 * HP]
            v = x_ref[0, r0:r0 + C, 2 * HP:3 * HP]
            gate = x_ref[0, r0:r0 + C, 3 * HP:4 * HP].astype(F32)
            ws, vs = [], []
            for hh in range(H):
                sc = _dot_nt(jnp.where(grp == hh, q, jnp.zeros_like(q)), kb)
                ws.append((sc * decay[hh]).astype(BF16))
                vs.append(jnp.where(grp == hh, v, jnp.zeros_like(v)))
            y = _dot(jnp.concatenate(ws, axis=1), jnp.concatenate(vs, axis=0))
            qf32 = q.astype(F32)
            s = s_ref[...]
            g_in = gbuf_ref[blk * sub + sb]
            y = y + _dot(jnp.concatenate([(qf32 * e_f).astype(BF16), (qf32 * e_b).astype(BF16)], axis=1),
                         jnp.concatenate([as_rows(s), as_rows(g_in)], axis=0).astype(BF16))
            s_ref[...] = s * tot_f + diag_blocks(_dot_tn((kb.astype(F32) * w_f).astype(BF16), v))

            mu = jnp.zeros_like(y)
            for hh in range(H):
                mh = jnp.sum(jnp.where(grp == hh, y, 0.0), axis=-1, keepdims=True) * (1.0 / P)
                mu = jnp.where(grp == hh, mh, mu)
            yc = y - mu
            var = jnp.zeros_like(y)
            for hh in range(H):
                vh = jnp.sum(jnp.where(grp == hh, yc * yc, 0.0), axis=-1, keepdims=True) * (1.0 / P)
                var = jnp.where(grp == hh, vh, var)
            y_ref[0, r0:r0 + C, :] = (yc * lax.rsqrt(var + EPS) * _silu(gate)).astype(BF16)

        @pl.when(c == nblk - 1)
        def _():
            fin_ref[0, 0] = s_ref[...]


def _block_index(nblk):
    return lambda i, ph, c: (i, (1 - ph) * (nblk - 1 - c) + ph * c, 0)


def _scan_sub(t):
    nc = t // CHUNK
    return 4 if nc % 4 == 0 else (2 if nc % 2 == 0 else 1)


def _ret_scan(ret, lw, init):
    b, t, _ = ret.shape
    sub = _scan_sub(t)
    rows = sub * CHUNK
    nblk = t // rows
    hp = RET_HEADS * RET_V
    return pl.pallas_call(
        functools.partial(_ret_kernel, nblk=nblk, sub=sub),
        grid=(b, 2, nblk),
        in_specs=[pl.BlockSpec((1, rows, W_RET), _block_index(nblk)),
                  pl.BlockSpec((2, hp), lambda i, ph, c: (0, 0)),
                  pl.BlockSpec((2 * RET_HEADS, CHUNK), lambda i, ph, c: (0, 0)),
                  pl.BlockSpec((1, 2, RET_QK, hp), lambda i, ph, c: (i, 0, 0, 0))],
        out_specs=[pl.BlockSpec((1, rows, hp), lambda i, ph, c: (i, ph * c, 0)),
                   pl.BlockSpec((1, 2, RET_QK, hp), lambda i, ph, c: (i, 0, 0, 0))],
        out_shape=[jax.ShapeDtypeStruct((b, t, hp), BF16),
                   jax.ShapeDtypeStruct((b, 2, RET_QK, hp), F32)],
        scratch_shapes=[pltpu.VMEM((RET_QK, hp), F32),
                        pltpu.VMEM((RET_QK, hp), F32),
                        pltpu.VMEM((t // CHUNK, RET_QK, hp), F32)],
        compiler_params=_cparams(("parallel", "arbitrary", "arbitrary")),
        name="retention_scan",
    )(ret, lw["ret_lgl"], lw["ret_lgh"], init)


def _ssd_kernel(x_ref, xp_ref, xn_ref, z_ref, dt_ref, cw_ref, cb_ref, dtb_ref, alog_ref, dvec_ref, nw_ref,
                init_ref, y_ref, fin_ref, s_ref, g_ref, gbuf_ref, *, nblk, sub):
    C, H, N, P, G = CHUNK, SSD_HEADS, SSD_STATE, SSD_HEADDIM, SSD_GROUPS
    HP = H * P
    HPG = H // G
    HALO = SSD_HALO
    ph, c = pl.program_id(1), pl.program_id(2)
    blk = jnp.where(ph == 0, nblk - 1 - c, c)

    ri = _iota((C, C + 2 * HALO), 0)
    ci = _iota((C, C + 2 * HALO), 1)
    sel_prev = (ci == ri + HALO - 1).astype(BF16)
    sel_next = (ci == ri + HALO + 1).astype(BF16)
    lane = _iota((LANES, HP), 0)
    head_of = _iota((LANES, HP), 1) // P
    e_f = (lane == head_of).astype(BF16)
    e_b = (lane == head_of + H).astype(BF16)
    r_i = _iota((C, C), 0)
    c_i = _iota((C, C), 1)
    low = (r_i >= c_i).astype(BF16)
    upp = (r_i <= c_i).astype(BF16)
    grp_n = _iota((N, HP), 1) // (HPG * P)
    neg_a = -jnp.exp(alog_ref[...])
    cw = cw_ref[...]

    def prep(sb):
        r0 = sb * C
        xc = x_ref[0, r0:r0 + C, :]
        if sb == 0:
            prev = jnp.where(blk > 0, xp_ref[0], jnp.zeros_like(xp_ref[0]))
        else:
            prev = x_ref[0, r0 - HALO:r0, :]
        if sb == sub - 1:
            nxt = jnp.where(blk < nblk - 1, xn_ref[0], jnp.zeros_like(xn_ref[0]))
        else:
            nxt = x_ref[0, r0 + C:r0 + C + HALO, :]
        stacked = jnp.concatenate([prev, xc, nxt], axis=0)
        conv = (cw[0:1] * _dot(sel_prev, stacked) + cw[1:2] * xc.astype(F32)
                + cw[2:3] * _dot(sel_next, stacked) + cb_ref[...])
        act = _silu(conv)
        xs = act[:, 0:HP]
        bm = act[:, HP:HP + G * N].astype(BF16)
        cm = act[:, HP + G * N:HP + 2 * G * N].astype(BF16)
        raw = dt_ref[0, r0:r0 + C, :] + dtb_ref[...]
        dt = jnp.maximum(raw, 0.0) + jnp.log(1.0 + jnp.exp(-jnp.abs(raw)))
        la = dt * neg_a
        a_f = _dot_exact_lhs01(low, la)
        a_b = _dot_exact_lhs01(upp, la)
        return xs, bm, cm, dt, la, a_f, a_b

    def state_update(state, bm, xs, wdt, tot8, e_dir, row):
        xw = (xs * _dot(wdt.astype(BF16), e_dir)).astype(BF16)
        tot = _dot_exact_rhs01(tot8, e_dir)[row:row + 1]
        r = _dot_tn(bm, xw)
        upd = r[0:N]
        for gg in range(1, G):
            upd = jnp.where(grp_n == gg, r[gg * N:(gg + 1) * N], upd)
        return state * tot + upd

    def as_rows(s_c):
        return jnp.concatenate([jnp.where(grp_n == gg, s_c, 0.0) for gg in range(G)], axis=0).astype(BF16)

    @pl.when((ph == 0) & (c == 0))
    def _():
        g_ref[...] = init_ref[0, 1]

    @pl.when(ph == 0)
    def _():
        for sb in reversed(range(sub)):
            xs, bm, _, dt, _, _, a_b = prep(sb)
            g = g_ref[...]
            gbuf_ref[blk * sub + sb] = g
            wdt = jnp.exp(a_b[0:1, :] - a_b) * dt
            g_ref[...] = state_update(g, bm, xs, wdt, jnp.exp(a_b[0:8, :]), e_b, 0)

        @pl.when(c == nblk - 1)
        def _():
            fin_ref[0, 1] = g_ref[...]

    @pl.when((ph == 1) & (c == 0))
    def _():
        s_ref[...] = init_ref[0, 0]

    @pl.when(ph == 1)
    def _():
        grp_l = _iota((C, G * N), 1) // N
        head_l = _iota((C, HP), 1) // P
        tri_f = r_i >= c_i
        diag = r_i == c_i
        for sb in range(sub):
            r0 = sb * C
            xs, bm, cm, dt, la, a_f, a_b = prep(sb)
            la_t = la.T[0:2 * H]
            dt_t = dt.T[0:2 * H]
            af_row = _dot_exact_rhs01(la_t, upp)
            ab_row = _dot_exact_rhs01(la_t, low)
            sc = [_dot_nt(jnp.where(grp_l == gg, cm, jnp.zeros_like(cm)), bm) for gg in range(G)]
            xsb = xs.astype(BF16)
            ws, vs = [], []
            for hh in range(H):
                seg_f = a_f[:, hh:hh + 1] - af_row[hh:hh + 1, :]
                seg_b = a_b[:, H + hh:H + hh + 1] - ab_row[H + hh:H + hh + 1, :]
                dt_f, dt_b = dt_t[hh:hh + 1, :], dt_t[H + hh:H + hh + 1, :]
                dm = (jnp.exp(jnp.where(tri_f, seg_f, seg_b)) * jnp.where(tri_f, dt_f, dt_b)
                      + jnp.where(diag, dt_b, 0.0))
                ws.append((sc[hh // HPG] * dm).astype(BF16))
                vs.append(jnp.where(head_l == hh, xsb, jnp.zeros_like(xsb)))
            y = _dot(jnp.concatenate(ws, axis=1), jnp.concatenate(vs, axis=0))

            s = s_ref[...]
            g_in = gbuf_ref[blk * sub + sb]
            y = y + _dot(jnp.exp(a_f).astype(BF16), e_f) * _dot(cm, as_rows(s))
            y = y + _dot(jnp.exp(a_b).astype(BF16), e_b) * _dot(cm, as_rows(g_in))

            wdt = jnp.exp(a_f[C - 1:C, :] - a_f) * dt
            s_ref[...] = state_update(s, bm, xs, wdt, jnp.exp(a_f[C - 8:C, :]), e_f, 7)

            y = y + xs * dvec_ref[...]
            y = y * _silu(z_ref[0, r0:r0 + C, :].astype(F32))
            y_ref[0, r0:r0 + C, :] = _rms(y, nw_ref[...]).astype(BF16)

        @pl.when(c == nblk - 1)
        def _():
            fin_ref[0, 0] = s_ref[...]


SSD_HALO = 16


def _ssd_scan(xbc, z, dt, lw, init):
    b, t, _ = xbc.shape
    sub = _scan_sub(t)
    rows = sub * CHUNK
    nblk = t // rows
    hp = SSD_INNER
    per = rows // SSD_HALO
    nhalo = t // SSD_HALO
    bidx = _block_index(nblk)

    def prev_idx(i, ph, c):
        blk = (1 - ph) * (nblk - 1 - c) + ph * c
        return (i, jnp.maximum(blk * per - 1, 0), 0)

    def next_idx(i, ph, c):
        blk = (1 - ph) * (nblk - 1 - c) + ph * c
        return (i, jnp.minimum((blk + 1) * per, nhalo - 1), 0)

    const = lambda shape: pl.BlockSpec(shape, lambda i, ph, c: (0,) * len(shape))
    return pl.pallas_call(
        functools.partial(_ssd_kernel, nblk=nblk, sub=sub),
        grid=(b, 2, nblk),
        in_specs=[pl.BlockSpec((1, rows, W_XBC), bidx),
                  pl.BlockSpec((1, SSD_HALO, W_XBC), prev_idx),
                  pl.BlockSpec((1, SSD_HALO, W_XBC), next_idx),
                  pl.BlockSpec((1, rows, W_Z), bidx),
                  pl.BlockSpec((1, rows, W_DT), bidx),
                  const((8, W_XBC)), const((1, W_XBC)), const((1, LANES)), const((1, LANES)),
                  const((1, hp)), const((1, hp)),
                  pl.BlockSpec((1, 2, SSD_STATE, hp), lambda i, ph, c: (i, 0, 0, 0))],
        out_specs=[pl.BlockSpec((1, rows, hp), lambda i, ph, c: (i, ph * c, 0)),
                   pl.BlockSpec((1, 2, SSD_STATE, hp), lambda i, ph, c: (i, 0, 0, 0))],
        out_shape=[jax.ShapeDtypeStruct((b, t, hp), BF16),
                   jax.ShapeDtypeStruct((b, 2, SSD_STATE, hp), F32)],
        scratch_shapes=[pltpu.VMEM((SSD_STATE, hp), F32),
                        pltpu.VMEM((SSD_STATE, hp), F32),
                        pltpu.VMEM((t // CHUNK, SSD_STATE, hp), F32)],
        compiler_params=_cparams(("parallel", "arbitrary", "arbitrary")),
        name="ssd_scan",
    )(xbc, xbc, xbc, z, dt, lw["conv_w"], lw["conv_b"], lw["dt_bias"], lw["a_log"], lw["d_vec"], lw["ssd_nw"], init)


def _mix_residual(x_ref, att_ref, ret_ref, ssd_ref, mod_ref, n2_ref, wo_ref):
    mix = jnp.concatenate([att_ref[0], ret_ref[0], ssd_ref[0]], axis=-1)
    x1 = x_ref[0] + mod_ref[0, 2:3, :] * _dot(mix, wo_ref[...])
    h = _rms(x1, n2_ref[...]) * (1.0 + mod_ref[0, 4:5, :]) + mod_ref[0, 3:4, :]
    return x1, h


def _dense_kernel(x_ref, att_ref, ret_ref, ssd_ref, mod_ref, n2_ref, wo_ref, wg_ref, wu_ref, wd_ref, o_ref,
                  *, ff_chunk):
    x1, h = _mix_residual(x_ref, att_ref, ret_ref, ssd_ref, mod_ref, n2_ref, wo_ref)
    hb = h.astype(BF16)
    ff = wg_ref.shape[1]
    f = jnp.zeros_like(x1)
    for s in range(0, ff, ff_chunk):
        g = _dot(hb, wg_ref[:, s:s + ff_chunk])
        u = _dot(hb, wu_ref[:, s:s + ff_chunk])
        f = f + _dot((_silu(g) * u).astype(BF16), wd_ref[s:s + ff_chunk, :])
    o_ref[0] = x1 + mod_ref[0, 5:6, :] * f


def _dense_block(xs, att, ret, ssd, mods, lw, tm):
    b, t, d = xs.shape
    ff = lw["wg"].shape[1]
    tok = lambda w: pl.BlockSpec((1, tm, w), lambda i, j: (i, j, 0))
    const = lambda shape: pl.BlockSpec(shape, lambda i, j: (0,) * len(shape), pipeline_mode=pl.Buffered(1))
    return pl.pallas_call(
        functools.partial(_dense_kernel, ff_chunk=ff // 2),
        grid=(b, t // tm),
        in_specs=[tok(d), tok(att.shape[-1]), tok(ret.shape[-1]), tok(ssd.shape[-1]),
                  pl.BlockSpec((1, 8, d), lambda i, j: (i, 0, 0)), const((1, d)),
                  const((d, d)), const((d, ff)), const((d, ff)), const((ff, d))],
        out_specs=tok(d),
        out_shape=jax.ShapeDtypeStruct((b, t, d), F32),
        compiler_params=_cparams(("parallel", "parallel")),
        name="outproj_ffn",
    )(xs, att, ret, ssd, mods, lw["n2"], lw["wo"], lw["wg"], lw["wu"], lw["wd"])


SUBLANES = 8
TOP_K = 2


def _route_kernel(x_ref, att_ref, ret_ref, ssd_ref, mod_ref, n2_ref, wo_ref, wr_ref,
                  x1_ref, h3_ref, slot_ref, gate_ref, cnt_ref):
    x1, h = _mix_residual(x_ref, att_ref, ret_ref, ssd_ref, mod_ref, n2_ref, wo_ref)
    x1_ref[0] = x1
    tm, d = h.shape
    nsub = d // LANES
    for s in range(nsub):
        h3_ref[0, pl.ds(s, tm, stride=nsub), :] = h[:, s * LANES:(s + 1) * LANES]
    h_hi = h.astype(BF16)
    h_lo = (h - h_hi.astype(F32)).astype(BF16)
    logits = _dot(h_hi, wr_ref[0]) + _dot(h_lo, wr_ref[0]) + _dot(h_hi, wr_ref[1])
    lane = _iota(logits.shape, 1).astype(F32)
    logits = jnp.where(lane < N_EXPERTS, logits, NEG)
    m1 = jnp.max(logits, axis=-1, keepdims=True)
    i1 = jnp.min(jnp.where(logits == m1, lane, float(LANES)), axis=-1, keepdims=True)
    rest = jnp.where(lane == i1, NEG, logits)
    m2 = jnp.max(rest, axis=-1, keepdims=True)
    i2 = jnp.min(jnp.where(rest == m2, lane, float(LANES)), axis=-1, keepdims=True)
    e2 = jnp.exp(m2 - m1)
    den = 1.0 + e2
    sel1, sel2 = lane == i1, lane == i2
    chosen = jnp.where(sel1 | sel2, 1.0, 0.0)
    before = (_iota((tm, tm), 0) > _iota((tm, tm), 1)).astype(BF16)
    pos = _dot(before, chosen.astype(BF16))
    cnt = jnp.sum(chosen, axis=0, keepdims=True)
    nblk = sum(jnp.where(cnt > float(k * MOE_ROWS), 1.0, 0.0) for k in range(pl.cdiv(tm, MOE_ROWS)))
    lower_e = (_iota((LANES, LANES), 0) < _iota((LANES, LANES), 1)).astype(BF16)
    off = _dot_exact_rhs01(jnp.broadcast_to(nblk * float(MOE_ROWS), (SUBLANES, LANES)), lower_e)[0:1]
    slot = off + pos
    s1 = jnp.sum(jnp.where(sel1, slot, 0.0), axis=-1, keepdims=True)
    s2 = jnp.sum(jnp.where(sel2, slot, 0.0), axis=-1, keepdims=True)
    lane_i = _iota(logits.shape, 1)
    slots = jnp.where(lane_i == 0, s1, jnp.where(lane_i == 1, s2, 0.0))
    slot_ref[0, 0] = slots.T[0:TOP_K].astype(jnp.int32)
    gate_ref[0] = jnp.where(lane_i == 0, 1.0 / den, jnp.where(lane_i == 1, e2 / den, 0.0))
    row = _iota((SUBLANES, LANES), 0)
    cnt_ref[0, 0] = jnp.where(row == 0, cnt, jnp.where(row == 1, off, jnp.where(row == 2, nblk, 0.0))
                              ).astype(jnp.int32)


def _route_block(xs, att, ret, ssd, mods, lw, tm):
    b, t, d = xs.shape
    nt = t // tm
    nsub = d // LANES
    tok = lambda w: pl.BlockSpec((1, tm, w), lambda i, j: (i, j, 0))
    const = lambda shape: pl.BlockSpec(shape, lambda i, j: (0,) * len(shape))
    per_tile = lambda r, w: pl.BlockSpec((1, 1, r, w), lambda i, j: (i, j, 0, 0))
    return pl.pallas_call(
        _route_kernel,
        grid=(b, nt),
        in_specs=[tok(d), tok(att.shape[-1]), tok(ret.shape[-1]), tok(ssd.shape[-1]),
                  pl.BlockSpec((1, 8, d), lambda i, j: (i, 0, 0)), const((1, d)),
                  const((d, d)), const((2, d, LANES))],
        out_specs=[tok(d), pl.BlockSpec((1, tm * nsub, LANES), lambda i, j: (i, j, 0)),
                   per_tile(TOP_K, tm), tok(LANES), per_tile(SUBLANES, LANES)],
        out_shape=[jax.ShapeDtypeStruct((b, t, d), F32), jax.ShapeDtypeStruct((b, t * nsub, LANES), F32),
                   jax.ShapeDtypeStruct((b, nt, TOP_K, tm), jnp.int32),
                   jax.ShapeDtypeStruct((b, t, LANES), F32),
                   jax.ShapeDtypeStruct((b, nt, SUBLANES, LANES), jnp.int32)],
        compiler_params=_cparams(("parallel", "parallel")),
        name="outproj_route",
    )(xs, att, ret, ssd, mods, lw["n2"], lw["wo"], lw["router"])


MOE_UNROLL_LOG2 = 5
MOE_UNROLL = 1 << MOE_UNROLL_LOG2
MOE_GROUP = 2
MOE_ROWS = 320


def _moe_kernel(slot_ref, cnt_ref, h3_ref, wgu_ref, wd_ref, o3_ref, code_ref, g_ref, y3_ref,
                *, tm, ff_split, group):
    e = pl.program_id(2)
    rb = MOE_ROWS
    nsub = SUBLANES
    spare = TOP_K * tm
    nslot = code_ref.shape[0] // group
    out_tiles = spare + 1

    def tile_rows(i):
        return pl.ds(pl.multiple_of(i * nsub, nsub), nsub)

    for p in range(group):
        cbase = p * nslot
        hbase = p * tm
        obase = p * out_tiles

        @pl.when(e == 0)
        def _():
            def invert(t, carry):
                for k in range(TOP_K):
                    code_ref[cbase + slot_ref[0, p, k, t]] = k * tm + t
                return carry

            lax.fori_loop(0, tm, invert, 0, unroll=MOE_UNROLL)

        n = cnt_ref[0, p, 0, e]
        off = cbase + cnt_ref[0, p, 1, e]
        nblocks = cnt_ref[0, p, 2, e]
        end = off + nblocks * rb

        def pad_fill(c, carry):
            for j in range(MOE_UNROLL):
                code_ref[jnp.minimum(off + n + c * MOE_UNROLL + j, end - 1)] = spare
            return carry

        lax.fori_loop(0, lax.shift_right_logical(end - off - n + (MOE_UNROLL - 1), jnp.int32(MOE_UNROLL_LOG2)),
                      pad_fill, 0)

        def block(bi, carry):
            base = off + bi * rb

            def gather(i, c):
                t = hbase + (code_ref[base + i] & (tm - 1))
                g_ref[tile_rows(i), :] = h3_ref[0, tile_rows(t), :]
                return c

            lax.fori_loop(0, rb, gather, 0, unroll=MOE_UNROLL)
            lhs = jnp.concatenate([g_ref[pl.ds(s, rb, stride=nsub), :] for s in range(nsub)], axis=1).astype(BF16)
            y = None
            for s0, s1 in ff_split:
                w = s1 - s0
                gu = _dot(lhs, wgu_ref[0, :, 2 * s0:2 * s1])
                part = _dot((_silu(gu[:, :w]) * gu[:, w:]).astype(BF16), wd_ref[0, s0:s1, :])
                y = part if y is None else y + part
            for s in range(nsub):
                y3_ref[pl.ds(s, rb, stride=nsub), :] = y[:, s * LANES:(s + 1) * LANES]

            def scatter(i, c):
                o3_ref[0, tile_rows(obase + code_ref[base + i]), :] = y3_ref[tile_rows(i), :]
                return c

            lax.fori_loop(0, rb, scatter, 0, unroll=MOE_UNROLL)
            return carry

        lax.fori_loop(0, nblocks, block, 0)

        @pl.when(e == N_EXPERTS - 1)
        def _():
            o3_ref[0, (obase + spare) * nsub:(obase + out_tiles) * nsub, :] = jnp.zeros((nsub, LANES), F32)


def _moe_out_kernel(x1_ref, a3_ref, gate_ref, mod_ref, fn_ref, o_ref, *, final_norm):
    tm, d = x1_ref.shape[1:]
    nsub = d // LANES
    gates = gate_ref[0]
    acc = None
    for k in range(TOP_K):
        yk = jnp.concatenate([a3_ref[0, pl.ds(k * tm * nsub + s, tm, stride=nsub), :] for s in range(nsub)], axis=1)
        term = gates[:, k:k + 1] * yk
        acc = term if acc is None else acc + term
    out = x1_ref[0] + mod_ref[0, 5:6, :] * acc
    o_ref[0] = _rms(out, fn_ref[...]) if final_norm else out


MOE_FF_SLICE = 512


def _moe_ff_split(ff):
    return tuple((s, min(s + MOE_FF_SLICE, ff)) for s in range(0, ff, MOE_FF_SLICE))


def _moe_weights(w_gate, w_up, w_down):
    ff = w_gate.shape[-1]
    wgu = jnp.concatenate([w[..., s0:s1] for s0, s1 in _moe_ff_split(ff) for w in (w_gate, w_up)], axis=-1)
    return wgu.astype(BF16), w_down.astype(BF16)


def _moe_block(x1, h3, slots, gates, counts, mods, lw, fnw, final_norm, tm):
    b, t, d = x1.shape
    ff = lw["moe_wd"].shape[1]
    nsub = d // LANES
    assert nsub == SUBLANES and MOE_ROWS % (2 * SUBLANES) == 0 and tm & (tm - 1) == 0
    assert MOE_ROWS % MOE_UNROLL == 0 and tm % MOE_UNROLL == 0
    nt = t // tm
    group = MOE_GROUP if nt % MOE_GROUP == 0 else 1
    out_rows = (TOP_K * tm + 1) * nsub
    smem = lambda r, w: pl.BlockSpec((1, group, r, w), lambda i, j, e: (i, j, 0, 0), memory_space=pltpu.SMEM)
    once = pl.Buffered(1)
    y3 = pl.pallas_call(
        functools.partial(_moe_kernel, tm=tm, ff_split=_moe_ff_split(ff), group=group),
        grid=(b, nt // group, N_EXPERTS),
        in_specs=[smem(TOP_K, tm), smem(SUBLANES, LANES),
                  pl.BlockSpec((1, group * tm * nsub, LANES), lambda i, j, e: (i, j, 0), pipeline_mode=once),
                  pl.BlockSpec((1, d, 2 * ff), lambda i, j, e: (e, 0, 0)),
                  pl.BlockSpec((1, ff, d), lambda i, j, e: (e, 0, 0))],
        out_specs=pl.BlockSpec((1, group * out_rows, LANES), lambda i, j, e: (i, j, 0), pipeline_mode=once),
        out_shape=jax.ShapeDtypeStruct((b, nt * out_rows, LANES), F32),
        scratch_shapes=[pltpu.SMEM((group * (TOP_K * tm + N_EXPERTS * MOE_ROWS),), jnp.int32),
                        pltpu.VMEM((MOE_ROWS * nsub, LANES), F32), pltpu.VMEM((MOE_ROWS * nsub, LANES), F32)],
        compiler_params=_cparams(("parallel", "parallel", "arbitrary")),
        name="moe",
    )(slots, counts, h3, lw["moe_wgu"], lw["moe_wd"])
    tok = lambda w: pl.BlockSpec((1, tm, w), lambda i, j: (i, j, 0))
    return pl.pallas_call(
        functools.partial(_moe_out_kernel, final_norm=final_norm),
        grid=(b, nt),
        in_specs=[tok(d), pl.BlockSpec((1, out_rows, LANES), lambda i, j: (i, j, 0)), tok(LANES),
                  pl.BlockSpec((1, 8, d), lambda i, j: (i, 0, 0)), pl.BlockSpec((1, d), lambda i, j: (0, 0))],
        out_specs=tok(d),
        out_shape=jax.ShapeDtypeStruct((b, t, d), F32),
        compiler_params=_cparams(("parallel", "parallel")),
        name="moe_out",
    )(x1, y3, gates, mods, fnw)


def _final_norm_kernel(x_ref, w_ref, o_ref):
    o_ref[0] = _rms(x_ref[0], w_ref[...])


def _final_norm(xs, fnw, tm):
    b, t, d = xs.shape
    return pl.pallas_call(
        _final_norm_kernel,
        grid=(b, t // tm),
        in_specs=[pl.BlockSpec((1, tm, d), lambda i, j: (i, j, 0)), pl.BlockSpec((1, d), lambda i, j: (0, 0))],
        out_specs=pl.BlockSpec((1, tm, d), lambda i, j: (i, j, 0)),
        out_shape=jax.ShapeDtypeStruct((b, t, d), F32),
        compiler_params=_cparams(("parallel", "parallel")),
        name="final_norm",
    )(xs, fnw)


def _rot_cols(w):
    a, b2, c, d = jnp.split(w, 4, axis=-1)
    return jnp.concatenate([-b2, a, -d, c], axis=-1)


def _layer_weights(i, norm1_w, norm2_w, w_in, w_out, mla_q_norm_w, mla_w_qb, mla_kv_norm_w, mla_w_kb, mla_w_vb,
                   ret_decay_logit, ssd_conv_w, ssd_conv_b, ssd_dt_bias, ssd_a_log, ssd_d, ssd_norm_w):
    d = w_in.shape[1]
    offs = np.cumsum((0,) + IN_SIZES)
    seg = [w_in[i][:, offs[j]:offs[j + 1]] for j in range(len(IN_SIZES))]
    zeros = lambda n: jnp.zeros((d, n), F32)
    w_all = jnp.concatenate(
        [seg[0], seg[1], seg[2], _rot_cols(seg[2]), zeros(W_MLA - MLA_Q_LORA - MLA_KV_LORA - 2 * MLA_ROPE),
         seg[3] * (RET_QK ** -0.5), seg[4], seg[5], seg[6], seg[7], seg[8], seg[9], seg[10],
         zeros(W_DT - 2 * SSD_HEADS)], axis=1).astype(BF16)

    dh = MLA_NOPE + MLA_ROPE
    wqb = mla_w_qb[i].reshape(MLA_Q_LORA, MLA_HEADS, dh)
    pad = lambda a, lo, hi: jnp.pad(a, ((0, 0), (0, 0), (lo, hi)))
    wq = pad(wqb, 0, MLA_HP - dh).reshape(MLA_Q_LORA, MLA_HEADS * MLA_HP).astype(BF16)
    wqr = pad(_rot_cols(wqb[:, :, MLA_NOPE:]), MLA_NOPE, MLA_HP - dh).reshape(MLA_Q_LORA, MLA_HEADS * MLA_HP).astype(BF16)
    wkb = pad(mla_w_kb[i].reshape(MLA_KV_LORA, MLA_HEADS, MLA_NOPE), 0, MLA_HP - MLA_NOPE)
    place = np.zeros((MLA_ROPE, MLA_HEADS, MLA_HP), np.float32)
    for r in range(MLA_ROPE):
        place[r, :, MLA_NOPE + r] = 1.0
    place = jnp.asarray(place)
    wk = jnp.concatenate([wkb, place, place, jnp.zeros((MLA_KV_LORA - 2 * MLA_ROPE, MLA_HEADS, MLA_HP), F32)],
                         axis=0).reshape(2 * MLA_KV_LORA, MLA_HEADS * MLA_HP).astype(BF16)

    wvb = mla_w_vb[i].reshape(MLA_KV_LORA, MLA_HEADS, MLA_V)
    wv = jnp.concatenate(
        [pad(wvb[:, hh:hh + 1], (hh % 2) * MLA_V, MLA_HP - MLA_V - (hh % 2) * MLA_V) for hh in range(MLA_HEADS)],
        axis=1).reshape(MLA_KV_LORA, MLA_HEADS * MLA_HP).astype(BF16)

    lg = ret_decay_logit[i]
    padl = lambda v: jnp.pad(v.reshape(1, -1), ((0, 0), (0, LANES - 2 * SSD_HEADS)))
    return {
        "n1": norm1_w[i].reshape(1, d), "n2": norm2_w[i].reshape(1, d),
        "w_all": w_all, "qn": mla_q_norm_w[i].reshape(1, -1), "wq": wq, "wqr": wqr,
        "kvn": mla_kv_norm_w[i].reshape(1, -1), "wk": wk, "wv": wv,
        "wo": w_out[i].astype(BF16),
        "ret_lgl": jnp.repeat(lg, RET_V, axis=1),
        "ret_lgh": jnp.broadcast_to(lg.reshape(2 * RET_HEADS, 1), (2 * RET_HEADS, CHUNK)),
        "conv_w": jnp.pad(ssd_conv_w[i], ((0, 8 - ssd_conv_w.shape[1]), (0, 0))),
        "conv_b": ssd_conv_b[i].reshape(1, -1),
        "dt_bias": padl(ssd_dt_bias[i]), "a_log": padl(ssd_a_log[i]),
        "d_vec": jnp.repeat(ssd_d[i], SSD_HEADDIM).reshape(1, -1),
        "ssd_nw": ssd_norm_w[i].reshape(1, -1),
    }


def _rope_tables(n_lat, n_ctx):
    rows = n_lat // GRID_W
    row = jnp.repeat(jnp.arange(rows, dtype=F32), GRID_W)
    col = jnp.tile(jnp.arange(GRID_W, dtype=F32), rows)
    inv_freq = ROPE_BASE ** (-jnp.arange(0, ROPE_AXIS_DIM, 2, dtype=F32) / ROPE_AXIS_DIM)
    ang_r = row[:, None] * inv_freq
    ang_c = col[:, None] * inv_freq
    ang = jnp.concatenate([ang_r, ang_r, ang_c, ang_c], axis=-1)
    c2 = MLA_SCALE * math.log2(math.e)

    def build(cos, sin):
        n = cos.shape[0]
        cosq = jnp.concatenate([jnp.full((n, MLA_NOPE), c2, F32), c2 * cos, jnp.zeros((n, MLA_HP - MLA_NOPE - MLA_ROPE), F32)], 1)
        sinq = jnp.concatenate([jnp.zeros((n, MLA_NOPE), F32), c2 * sin, jnp.zeros((n, MLA_HP - MLA_NOPE - MLA_ROPE), F32)], 1)
        ck = jnp.concatenate([cos, sin, jnp.zeros((n, LANES - 2 * MLA_ROPE), F32)], 1)
        return cosq, sinq, ck

    lat = build(jnp.cos(ang), jnp.sin(ang))
    ctx = build(jnp.ones((n_ctx, MLA_ROPE), F32), jnp.zeros((n_ctx, MLA_ROPE), F32))
    return lat, ctx


def _token_tile(t, pref):
    tm = min(pref, t)
    while t % tm:
        tm //= 2
    return tm


def kernel(x, c, ctx, c_ctx, mod_w, mod_b, norm1_w, norm2_w, w_in, w_out, mla_q_norm_w, mla_w_qb, mla_kv_norm_w, mla_w_kb, mla_w_vb, ret_decay_logit, ssd_conv_w, ssd_conv_b, ssd_dt_bias, ssd_a_log, ssd_d, ssd_norm_w, ffn_w_gate, ffn_w_up, ffn_w_down, moe_router, moe_w_gate, moe_w_up, moe_w_down, final_norm_w):
    b, n_lat, d = x.shape
    n_ctx = ctx.shape[1]
    depth = mod_w.shape[0]
    assert b + 1 <= 8 and n_lat % CHUNK == 0 and n_ctx % CHUNK == 0 and n_lat % GRID_W == 0

    cond = jnp.concatenate([c, c_ctx[None, :], jnp.zeros((8 - b - 1, d), F32)], axis=0)
    mod_all = _modulation(cond, mod_w, mod_b)
    tabs_x, tabs_c = _rope_tables(n_lat, n_ctx)
    fnw = final_norm_w.reshape(1, d)

    tm_x = _token_tile(n_lat, 512)
    tm_c = _token_tile(n_ctx, 256)
    tm_moe = _token_tile(n_lat, 1024)
    tq_x = _token_tile(n_lat, 1024)
    tk_x = 2048
    zero_ret = jnp.zeros((b, 2, RET_QK, RET_HEADS * RET_V), F32)
    zero_ssd = jnp.zeros((b, 2, SSD_STATE, SSD_INNER), F32)

    h_ctx = ctx
    for i in range(depth):
        last = i == depth - 1
        lw = _layer_weights(i, norm1_w, norm2_w, w_in, w_out, mla_q_norm_w, mla_w_qb, mla_kv_norm_w, mla_w_kb,
                            mla_w_vb, ret_decay_logit, ssd_conv_w, ssd_conv_b, ssd_dt_bias, ssd_a_log, ssd_d,
                            ssd_norm_w)
        m6 = mod_all[i].reshape(8, 6, d)
        pad2 = jnp.zeros((b, 2, d), F32)
        mods_x = jnp.concatenate([m6[:b], pad2], axis=1)
        mods_c = jnp.concatenate([jnp.broadcast_to(m6[b:b + 1], (b, 6, d)), pad2], axis=1)

        qc, kc, vc, retc, zc, xbcc, dtc = _inproj(h_ctx, mods_c, lw, tabs_c, tm_c)
        ret_yc, ret_fin = _ret_scan(retc, lw, zero_ret)
        ssd_yc, ssd_fin = _ssd_scan(xbcc, zc, dtc, lw, zero_ssd)

        qx, kx, vx, retx, zx, xbcx, dtx = _inproj(x, mods_x, lw, tabs_x, tm_x)
        att_x = _attention(qx, [(kc, vc), (kx, vx)], tq_x, tk_x)
        ret_yx, _ = _ret_scan(retx, lw, ret_fin)
        ssd_yx, _ = _ssd_scan(xbcx, zx, dtx, lw, ssd_fin)

        if i % 2 == 0:
            j = i // 2
            lw.update(wg=ffn_w_gate[j].astype(BF16), wu=ffn_w_up[j].astype(BF16), wd=ffn_w_down[j].astype(BF16))
            x = _dense_block(x, att_x, ret_yx, ssd_yx, mods_x, lw, tm_x)
            if last:
                x = _final_norm(x, fnw, tm_x)
        else:
            j = i // 2
            wgu, wd = _moe_weights(moe_w_gate[j], moe_w_up[j], moe_w_down[j])
            r_f32 = jnp.pad(moe_router[j], ((0, 0), (0, LANES - N_EXPERTS)))
            r_hi = r_f32.astype(BF16)
            r_lo = (r_f32 - r_hi.astype(F32)).astype(BF16)
            lw.update(router=jnp.stack([r_hi, r_lo]), moe_wgu=wgu, moe_wd=wd)
            x1, *routed = _route_block(x, att_x, ret_yx, ssd_yx, mods_x, lw, tm_moe)
            x = _moe_block(x1, *routed, mods_x, lw, fnw, last, tm_moe)

        if not last:
            att_c = _attention(qc, [(kc, vc)], tm_c, tk_x)
            if i % 2 == 0:
                h_ctx = _dense_block(h_ctx, att_c, ret_yc, ssd_yc, mods_c, lw, tm_c)
            else:
                c1, *routed = _route_block(h_ctx, att_c, ret_yc, ssd_yc, mods_c, lw, tm_c)
                h_ctx = _moe_block(c1, *routed, mods_c, lw, fnw, False, tm_c)
    return x
```

```python
import functools
import math

import jax
import jax.numpy as jnp
import numpy as np
from jax import lax
from jax.experimental import pallas as pl
from jax.experimental.pallas import tpu as pltpu

F32 = jnp.float32
BF16 = jnp.bfloat16
HIGHEST = lax.Precision.HIGHEST

EPS = 1e-6
CHUNK = 128
GRID_W = 64
MLA_HEADS = 4
MLA_Q_LORA = 256
MLA_KV_LORA = 128
MLA_NOPE = 64
MLA_ROPE = 32
MLA_V = 64
MLA_SCALE = (MLA_NOPE + MLA_ROPE) ** -0.5
ROPE_AXIS_DIM = MLA_ROPE // 2
ROPE_BASE = 10000.0
RET_HEADS = 4
RET_QK = 64
RET_V = 64
SSD_HEADS = 8
SSD_HEADDIM = 64
SSD_GROUPS = 2
SSD_STATE = 64
SSD_INNER = SSD_HEADS * SSD_HEADDIM
SSD_CONV_CH = SSD_INNER + 2 * SSD_GROUPS * SSD_STATE
N_EXPERTS = 8
IN_SIZES = (MLA_Q_LORA, MLA_KV_LORA, MLA_ROPE,
            RET_HEADS * RET_QK, RET_HEADS * RET_QK, RET_HEADS * RET_V, RET_HEADS * RET_V,
            SSD_INNER, SSD_CONV_CH, SSD_HEADS, SSD_HEADS)

LANES = 128
V7X_VMEM_LIMIT = 56 * 1024 * 1024

MLA_HP = 128
NEG = -1e30


def _cparams(sem):
    return pltpu.CompilerParams(dimension_semantics=sem, vmem_limit_bytes=V7X_VMEM_LIMIT)


def _dot(a, b):
    return jnp.dot(a, b, preferred_element_type=F32)


def _dot_nt(a, b):
    return lax.dot_general(a, b, (((1,), (1,)), ((), ())), preferred_element_type=F32)


def _dot_tn(a, b):
    return lax.dot_general(a, b, (((0,), (0,)), ((), ())), preferred_element_type=F32)


def _split3(x):
    hi = x.astype(BF16)
    r1 = x - hi.astype(F32)
    mid = r1.astype(BF16)
    lo = (r1 - mid.astype(F32)).astype(BF16)
    return hi, mid, lo


def _dot_exact_rhs01(x, e):
    hi, mid, lo = _split3(x)
    return _dot(hi, e) + _dot(mid, e) + _dot(lo, e)


def _dot_exact_lhs01(e, x):
    hi, mid, lo = _split3(x)
    return _dot(e, hi) + _dot(e, mid) + _dot(e, lo)


def _silu(x):
    return x / (1.0 + jnp.exp(-x))


def _rms(x, w):
    return x * lax.rsqrt(jnp.mean(x * x, axis=-1, keepdims=True) + EPS) * w


def _iota(shape, dim):
    return lax.broadcasted_iota(jnp.int32, shape, dim)


def _mod_kernel(a_ref, w_ref, b_ref, o_ref):
    a = _silu(a_ref[...])
    o_ref[0] = jnp.dot(a, w_ref[0], precision=HIGHEST, preferred_element_type=F32) + b_ref[0]


def _modulation(cond, mod_w, mod_b):
    depth, d, n = mod_w.shape
    tn = 1536
    return pl.pallas_call(
        _mod_kernel,
        grid=(depth, n // tn),
        in_specs=[pl.BlockSpec((8, d), lambda i, j: (0, 0)),
                  pl.BlockSpec((1, d, tn), lambda i, j: (i, 0, j)),
                  pl.BlockSpec((1, 1, tn), lambda i, j: (i, 0, j))],
        out_specs=pl.BlockSpec((1, 8, tn), lambda i, j: (i, 0, j)),
        out_shape=jax.ShapeDtypeStruct((depth, 8, n), F32),
        compiler_params=_cparams(("parallel", "parallel")),
        name="modulation",
    )(cond, mod_w, mod_b.reshape(depth, 1, n))


W_MLA, W_RET, W_Z, W_XBC, W_DT = 512, 1024, 512, 768, 128
W_ALL = W_MLA + W_RET + W_Z + W_XBC + W_DT


def _ones_lane_mask(shape):
    lane = _iota(shape, len(shape) - 1) % (2 * MLA_HP)
    return (lane == MLA_HP - 1) | (lane == MLA_HP)


def _inproj_kernel(x_ref, mod_ref, n1_ref, win_ref, qn_ref, wq_ref, wqr_ref, kvn_ref, wk_ref, wv_ref,
                   cq_ref, sq_ref, ck_ref,
                   q_out, k_out, v_out, ret_out, z_out, xbc_out, dt_out):
    x = x_ref[0]
    h = _rms(x, n1_ref[...]) * (1.0 + mod_ref[0, 1:2, :]) + mod_ref[0, 0:1, :]
    hb = h.astype(BF16)

    o = 0
    pm = _dot(hb, win_ref[:, o:o + W_MLA]); o += W_MLA
    ret_out[0] = _dot(hb, win_ref[:, o:o + W_RET]).astype(BF16); o += W_RET
    z_out[0] = _dot(hb, win_ref[:, o:o + W_Z]).astype(BF16); o += W_Z
    xbc_out[0] = _dot(hb, win_ref[:, o:o + W_XBC]).astype(BF16); o += W_XBC
    dt_out[0] = _dot(hb, win_ref[:, o:o + W_DT])

    cqb = _rms(pm[:, 0:MLA_Q_LORA], qn_ref[...]).astype(BF16)
    qm = _dot(cqb, wq_ref[...])
    qr = _dot(cqb, wqr_ref[...])
    cosq, sinq = cq_ref[...], sq_ref[...]
    for hh in range(MLA_HEADS):
        sl = slice(hh * MLA_HP, (hh + 1) * MLA_HP)
        q_out[0, :, sl] = (qm[:, sl] * cosq + qr[:, sl] * sinq).astype(BF16)

    ckvn = _rms(pm[:, MLA_Q_LORA:MLA_Q_LORA + MLA_KV_LORA], kvn_ref[...])
    u = pm[:, MLA_Q_LORA + MLA_KV_LORA:W_MLA] * ck_ref[...]
    kin = jnp.concatenate([ckvn, u], axis=-1).astype(BF16)
    k_out[0] = _dot(kin, wk_ref[...]).astype(BF16)
    v = _dot(ckvn.astype(BF16), wv_ref[...])
    v_out[0] = jnp.where(_ones_lane_mask(v.shape), 1.0, v).astype(BF16)


def _inproj(xs, mods, lw, tabs, tm):
    b, t, d = xs.shape
    const = lambda shape: pl.BlockSpec(shape, lambda i, j: (0,) * len(shape))
    tok = lambda w: pl.BlockSpec((1, tm, w), lambda i, j: (i, j, 0))
    tab = pl.BlockSpec((tm, LANES), lambda i, j: (j, 0))
    widths = (MLA_HEADS * MLA_HP, MLA_HEADS * MLA_HP, MLA_HEADS * MLA_HP, W_RET, W_Z, W_XBC, W_DT)
    dtypes = (BF16, BF16, BF16, BF16, BF16, BF16, F32)
    return pl.pallas_call(
        _inproj_kernel,
        grid=(b, t // tm),
        in_specs=[tok(d), pl.BlockSpec((1, 8, d), lambda i, j: (i, 0, 0)), const((1, d)),
                  const((d, W_ALL)), const((1, MLA_Q_LORA)), const((MLA_Q_LORA, MLA_HEADS * MLA_HP)),
                  const((MLA_Q_LORA, MLA_HEADS * MLA_HP)), const((1, MLA_KV_LORA)),
                  const((2 * MLA_KV_LORA, MLA_HEADS * MLA_HP)), const((MLA_KV_LORA, MLA_HEADS * MLA_HP)),
                  tab, tab, tab],
        out_specs=[tok(w) for w in widths],
        out_shape=[jax.ShapeDtypeStruct((b, t, w), dt) for w, dt in zip(widths, dtypes)],
        compiler_params=_cparams(("parallel", "parallel")),
        name="inproj",
    )(xs, mods, lw["n1"], lw["w_all"], lw["qn"], lw["wq"], lw["wqr"], lw["kvn"], lw["wk"], lw["wv"], *tabs)


def _attn_kernel(*refs, seg_blocks, tq):
    nseg = len(seg_blocks)
    q_ref = refs[0]
    kv_refs = refs[1:1 + 2 * nseg]
    o_ref = refs[1 + 2 * nseg]
    carry = tuple((jnp.full((tq, 1), NEG, F32), jnp.zeros((tq, MLA_HP), F32)) for _ in range(MLA_HEADS))
    for si, (tk, nblk) in enumerate(seg_blocks):
        k_ref, v_ref = kv_refs[2 * si], kv_refs[2 * si + 1]

        def body(j, carry, k_ref=k_ref, v_ref=v_ref, tk=tk):
            start = 0 if isinstance(j, int) else pl.multiple_of(j * tk, tk)
            new = []
            for hh in range(MLA_HEADS):
                sl = slice(hh * MLA_HP, (hh + 1) * MLA_HP)
                m, acc = carry[hh]
                s = _dot_nt(q_ref[0, :, sl], k_ref[0, pl.ds(start, tk), sl])
                m_new = jnp.maximum(m, jnp.max(s, axis=-1, keepdims=True))
                p = jnp.exp2(s - m_new).astype(BF16)
                acc = jnp.exp2(m - m_new) * acc + _dot(p, v_ref[0, pl.ds(start, tk), sl])
                new.append((m_new, acc))
            return tuple(new)

        carry = body(0, carry) if nblk == 1 else lax.fori_loop(0, nblk, body, carry)
    lane = _iota((tq, MLA_HP), 1)
    pairs = []
    for hh in range(0, MLA_HEADS, 2):
        acc_e, acc_o = carry[hh][1], carry[hh + 1][1]
        l_e = jnp.sum(jnp.where(lane == MLA_HP - 1, acc_e, 0.0), axis=-1, keepdims=True)
        l_o = jnp.sum(jnp.where(lane == 0, acc_o, 0.0), axis=-1, keepdims=True)
        pairs.append(jnp.where(lane < MLA_V, acc_e / l_e, acc_o / l_o))
    o_ref[0] = jnp.concatenate(pairs, axis=-1).astype(BF16)


def _attention(q, segs, tq, tk):
    b, l, _ = q.shape
    seg_blocks = []
    in_specs = [pl.BlockSpec((1, tq, MLA_HEADS * MLA_HP), lambda i, j: (i, j, 0))]
    args = [q]
    for k, v in segs:
        s = k.shape[1]
        t = min(tk, s)
        seg_blocks.append((t, s // t))
        for _ in range(2):
            in_specs.append(pl.BlockSpec((1, s, MLA_HEADS * MLA_HP), lambda i, j: (i, 0, 0),
                                         pipeline_mode=pl.Buffered(1)))
        args += [k, v]
    return pl.pallas_call(
        functools.partial(_attn_kernel, seg_blocks=tuple(seg_blocks), tq=tq),
        grid=(b, l // tq),
        in_specs=in_specs,
        out_specs=pl.BlockSpec((1, tq, MLA_HEADS * MLA_V), lambda i, j: (i, j, 0)),
        out_shape=jax.ShapeDtypeStruct((b, l, MLA_HEADS * MLA_V), BF16),
        compiler_params=_cparams(("parallel", "arbitrary")),
        name="attention",
    )(*args)


def _ret_kernel(x_ref, lgl_ref, lgh_ref, init_ref, y_ref, fin_ref, s_ref, g_ref, gbuf_ref, *, nblk, sub):
    C, H, N, P = CHUNK, RET_HEADS, RET_QK, RET_V
    HP = H * P
    ph, c = pl.program_id(1), pl.program_id(2)
    blk = jnp.where(ph == 0, nblk - 1 - c, c)

    def log_sigmoid(v):
        return jnp.minimum(v, 0.0) - jnp.log(1.0 + jnp.exp(-jnp.abs(v)))

    lgl = log_sigmoid(lgl_ref[...])
    lgf, lgb = lgl[0:1], lgl[1:2]
    rows = _iota((C, HP), 0).astype(F32)
    head_n = _iota((N, HP), 1) // P

    def diag_blocks(r):
        out = r[0:N]
        for hh in range(1, H):
            out = jnp.where(head_n == hh, r[hh * N:(hh + 1) * N], out)
        return out

    def as_rows(s_c):
        return jnp.concatenate([jnp.where(head_n == hh, s_c, 0.0) for hh in range(H)], axis=0)

    @pl.when((ph == 0) & (c == 0))
    def _():
        g_ref[...] = init_ref[0, 1]

    @pl.when(ph == 0)
    def _():
        w_b = jnp.exp(rows * lgb)
        tot_b = jnp.exp(C * lgb)
        for sb in reversed(range(sub)):
            r0 = sb * C
            k = x_ref[0, r0:r0 + C, HP:2 * HP].astype(F32)
            v = x_ref[0, r0:r0 + C, 2 * HP:3 * HP]
            g = g_ref[...]
            gbuf_ref[blk * sub + sb] = g
            g_ref[...] = g * tot_b + diag_blocks(_dot_tn((k * w_b).astype(BF16), v))

        @pl.when(c == nblk - 1)
        def _():
            fin_ref[0, 1] = g_ref[...]

    @pl.when((ph == 1) & (c == 0))
    def _():
        s_ref[...] = init_ref[0, 0]

    @pl.when(ph == 1)
    def _():
        lgh = log_sigmoid(lgh_ref[...])
        grp = _iota((C, HP), 1) // P
        di = (_iota((C, C), 0) - _iota((C, C), 1)).astype(F32)
        decay = [jnp.exp(jnp.where(di >= 0, di * lgh[hh:hh + 1, :], -di * lgh[H + hh:H + hh + 1, :]))
                 for hh in range(H)]
        e_f = jnp.exp((rows + 1.0) * lgf)
        e_b = jnp.exp((C - rows) * lgb)
        w_f = jnp.exp((C - 1.0 - rows) * lgf)
        tot_f = jnp.exp(C * lgf)
        for sb in range(sub):
            r0 = sb * C
            q = x_ref[0, r0:r0 + C, 0:HP]
            kb = x_ref[0, r0:r0 + C, HP:2 * HP]
            v = x_ref[0, r0:r0 + C, 2 * HP:3 * HP]
            gate = x_ref[0, r0:r0 + C, 3 * HP:4 * HP].astype(F32)
            ws, vs = [], []
            for hh in range(H):
                sc = _dot_nt(jnp.where(grp == hh, q, jnp.zeros_like(q)), kb)
                ws.append((sc * decay[hh]).astype(BF16))
                vs.append(jnp.where(grp == hh, v, jnp.zeros_like(v)))
            y = _dot(jnp.concatenate(ws, axis=1), jnp.concatenate(vs, axis=0))
            qf32 = q.astype(F32)
            s = s_ref[...]
            g_in = gbuf_ref[blk * sub + sb]
            y = y + _dot(jnp.concatenate([(qf32 * e_f).astype(BF16), (qf32 * e_b).astype(BF16)], axis=1),
                         jnp.concatenate([as_rows(s), as_rows(g_in)], axis=0).astype(BF16))
            s_ref[...] = s * tot_f + diag_blocks(_dot_tn((kb.astype(F32) * w_f).astype(BF16), v))

            mu = jnp.zeros_like(y)
            for hh in range(H):
                mh = jnp.sum(jnp.where(grp == hh, y, 0.0), axis=-1, keepdims=True) * (1.0 / P)
                mu = jnp.where(grp == hh, mh, mu)
            yc = y - mu
            var = jnp.zeros_like(y)
            for hh in range(H):
                vh = jnp.sum(jnp.where(grp == hh, yc * yc, 0.0), axis=-1, keepdims=True) * (1.0 / P)
                var = jnp.where(grp == hh, vh, var)
            y_ref[0, r0:r0 + C, :] = (yc * lax.rsqrt(var + EPS) * _silu(gate)).astype(BF16)

        @pl.when(c == nblk - 1)
        def _():
            fin_ref[0, 0] = s_ref[...]


def _block_index(nblk):
    return lambda i, ph, c: (i, (1 - ph) * (nblk - 1 - c) + ph * c, 0)


def _scan_sub(t):
    nc = t // CHUNK
    return 4 if nc % 4 == 0 else (2 if nc % 2 == 0 else 1)


def _ret_scan(ret, lw, init):
    b, t, _ = ret.shape
    sub = _scan_sub(t)
    rows = sub * CHUNK
    nblk = t // rows
    hp = RET_HEADS * RET_V
    return pl.pallas_call(
        functools.partial(_ret_kernel, nblk=nblk, sub=sub),
        grid=(b, 2, nblk),
        in_specs=[pl.BlockSpec((1, rows, W_RET), _block_index(nblk)),
                  pl.BlockSpec((2, hp), lambda i, ph, c: (0, 0)),
                  pl.BlockSpec((2 * RET_HEADS, CHUNK), lambda i, ph, c: (0, 0)),
                  pl.BlockSpec((1, 2, RET_QK, hp), lambda i, ph, c: (i, 0, 0, 0))],
        out_specs=[pl.BlockSpec((1, rows, hp), lambda i, ph, c: (i, ph * c, 0)),
                   pl.BlockSpec((1, 2, RET_QK, hp), lambda i, ph, c: (i, 0, 0, 0))],
        out_shape=[jax.ShapeDtypeStruct((b, t, hp), BF16),
                   jax.ShapeDtypeStruct((b, 2, RET_QK, hp), F32)],
        scratch_shapes=[pltpu.VMEM((RET_QK, hp), F32),
                        pltpu.VMEM((RET_QK, hp), F32),
                        pltpu.VMEM((t // CHUNK, RET_QK, hp), F32)],
        compiler_params=_cparams(("parallel", "arbitrary", "arbitrary")),
        name="retention_scan",
    )(ret, lw["ret_lgl"], lw["ret_lgh"], init)


def _ssd_kernel(x_ref, xp_ref, xn_ref, z_ref, dt_ref, cw_ref, cb_ref, dtb_ref, alog_ref, dvec_ref, nw_ref,
                init_ref, y_ref, fin_ref, s_ref, g_ref, gbuf_ref, *, nblk, sub):
    C, H, N, P, G = CHUNK, SSD_HEADS, SSD_STATE, SSD_HEADDIM, SSD_GROUPS
    HP = H * P
    HPG = H // G
    HALO = SSD_HALO
    ph, c = pl.program_id(1), pl.program_id(2)
    blk = jnp.where(ph == 0, nblk - 1 - c, c)

    ri = _iota((C, C + 2 * HALO), 0)
    ci = _iota((C, C + 2 * HALO), 1)
    sel_prev = (ci == ri + HALO - 1).astype(BF16)
    sel_next = (ci == ri + HALO + 1).astype(BF16)
    lane = _iota((LANES, HP), 0)
    head_of = _iota((LANES, HP), 1) // P
    e_f = (lane == head_of).astype(BF16)
    e_b = (lane == head_of + H).astype(BF16)
    r_i = _iota((C, C), 0)
    c_i = _iota((C, C), 1)
    low = (r_i >= c_i).astype(BF16)
    upp = (r_i <= c_i).astype(BF16)
    grp_n = _iota((N, HP), 1) // (HPG * P)
    neg_a = -jnp.exp(alog_ref[...])
    cw = cw_ref[...]

    def prep(sb):
        r0 = sb * C
        xc = x_ref[0, r0:r0 + C, :]
        if sb == 0:
            prev = jnp.where(blk > 0, xp_ref[0], jnp.zeros_like(xp_ref[0]))
        else:
            prev = x_ref[0, r0 - HALO:r0, :]
        if sb == sub - 1:
            nxt = jnp.where(blk < nblk - 1, xn_ref[0], jnp.zeros_like(xn_ref[0]))
        else:
            nxt = x_ref[0, r0 + C:r0 + C + HALO, :]
        stacked = jnp.concatenate([prev, xc, nxt], axis=0)
        conv = (cw[0:1] * _dot(sel_prev, stacked) + cw[1:2] * xc.astype(F32)
                + cw[2:3] * _dot(sel_next, stacked) + cb_ref[...])
        act = _silu(conv)
        xs = act[:, 0:HP]
        bm = act[:, HP:HP + G * N].astype(BF16)
        cm = act[:, HP + G * N:HP + 2 * G * N].astype(BF16)
        raw = dt_ref[0, r0:r0 + C, :] + dtb_ref[...]
        dt = jnp.maximum(raw, 0.0) + jnp.log(1.0 + jnp.exp(-jnp.abs(raw)))
        la = dt * neg_a
        a_f = _dot_exact_lhs01(low, la)
        a_b = _dot_exact_lhs01(upp, la)
        return xs, bm, cm, dt, la, a_f, a_b

    def state_update(state, bm, xs, wdt, tot8, e_dir, row):
        xw = (xs * _dot(wdt.astype(BF16), e_dir)).astype(BF16)
        tot = _dot_exact_rhs01(tot8, e_dir)[row:row + 1]
        r = _dot_tn(bm, xw)
        upd = r[0:N]
        for gg in range(1, G):
            upd = jnp.where(grp_n == gg, r[gg * N:(gg + 1) * N], upd)
        return state * tot + upd

    def as_rows(s_c):
        return jnp.concatenate([jnp.where(grp_n == gg, s_c, 0.0) for gg in range(G)], axis=0).astype(BF16)

    @pl.when((ph == 0) & (c == 0))
    def _():
        g_ref[...] = init_ref[0, 1]

    @pl.when(ph == 0)
    def _():
        for sb in reversed(range(sub)):
            xs, bm, _, dt, _, _, a_b = prep(sb)
            g = g_ref[...]
            gbuf_ref[blk * sub + sb] = g
            wdt = jnp.exp(a_b[0:1, :] - a_b) * dt
            g_ref[...] = state_update(g, bm, xs, wdt, jnp.exp(a_b[0:8, :]), e_b, 0)

        @pl.when(c == nblk - 1)
        def _():
            fin_ref[0, 1] = g_ref[...]

    @pl.when((ph == 1) & (c == 0))
    def _():
        s_ref[...] = init_ref[0, 0]

    @pl.when(ph == 1)
    def _():
        grp_l = _iota((C, G * N), 1) // N
        head_l = _iota((C, HP), 1) // P
        tri_f = r_i >= c_i
        diag = r_i == c_i
        for sb in range(sub):
            r0 = sb * C
            xs, bm, cm, dt, la, a_f, a_b = prep(sb)
            la_t = la.T[0:2 * H]
            dt_t = dt.T[0:2 * H]
            af_row = _dot_exact_rhs01(la_t, upp)
            ab_row = _dot_exact_rhs01(la_t, low)
            sc = [_dot_nt(jnp.where(grp_l == gg, cm, jnp.zeros_like(cm)), bm) for gg in range(G)]
            xsb = xs.astype(BF16)
            ws, vs = [], []
            for hh in range(H):
                seg_f = a_f[:, hh:hh + 1] - af_row[hh:hh + 1, :]
                seg_b = a_b[:, H + hh:H + hh + 1] - ab_row[H + hh:H + hh + 1, :]
                dt_f, dt_b = dt_t[hh:hh + 1, :], dt_t[H + hh:H + hh + 1, :]
                dm = (jnp.exp(jnp.where(tri_f, seg_f, seg_b)) * jnp.where(tri_f, dt_f, dt_b)
                      + jnp.where(diag, dt_b, 0.0))
                ws.append((sc[hh // HPG] * dm).astype(BF16))
                vs.append(jnp.where(head_l == hh, xsb, jnp.zeros_like(xsb)))
            y = _dot(jnp.concatenate(ws, axis=1), jnp.concatenate(vs, axis=0))

            s = s_ref[...]
            g_in = gbuf_ref[blk * sub + sb]
            y = y + _dot(jnp.exp(a_f).astype(BF16), e_f) * _dot(cm, as_rows(s))
            y = y + _dot(jnp.exp(a_b).astype(BF16), e_b) * _dot(cm, as_rows(g_in))

            wdt = jnp.exp(a_f[C - 1:C, :] - a_f) * dt
            s_ref[...] = state_update(s, bm, xs, wdt, jnp.exp(a_f[C - 8:C, :]), e_f, 7)

            y = y + xs * dvec_ref[...]
            y = y * _silu(z_ref[0, r0:r0 + C, :].astype(F32))
            y_ref[0, r0:r0 + C, :] = _rms(y, nw_ref[...]).astype(BF16)

        @pl.when(c == nblk - 1)
        def _():
            fin_ref[0, 0] = s_ref[...]


SSD_HALO = 16


def _ssd_scan(xbc, z, dt, lw, init):
    b, t, _ = xbc.shape
    sub = _scan_sub(t)
    rows = sub * CHUNK
    nblk = t // rows
    hp = SSD_INNER
    per = rows // SSD_HALO
    nhalo = t // SSD_HALO
    bidx = _block_index(nblk)

    def prev_idx(i, ph, c):
        blk = (1 - ph) * (nblk - 1 - c) + ph * c
        return (i, jnp.maximum(blk * per - 1, 0), 0)

    def next_idx(i, ph, c):
        blk = (1 - ph) * (nblk - 1 - c) + ph * c
        return (i, jnp.minimum((blk + 1) * per, nhalo - 1), 0)

    const = lambda shape: pl.BlockSpec(shape, lambda i, ph, c: (0,) * len(shape))
    return pl.pallas_call(
        functools.partial(_ssd_kernel, nblk=nblk, sub=sub),
        grid=(b, 2, nblk),
        in_specs=[pl.BlockSpec((1, rows, W_XBC), bidx),
                  pl.BlockSpec((1, SSD_HALO, W_XBC), prev_idx),
                  pl.BlockSpec((1, SSD_HALO, W_XBC), next_idx),
                  pl.BlockSpec((1, rows, W_Z), bidx),
                  pl.BlockSpec((1, rows, W_DT), bidx),
                  const((8, W_XBC)), const((1, W_XBC)), const((1, LANES)), const((1, LANES)),
                  const((1, hp)), const((1, hp)),
                  pl.BlockSpec((1, 2, SSD_STATE, hp), lambda i, ph, c: (i, 0, 0, 0))],
        out_specs=[pl.BlockSpec((1, rows, hp), lambda i, ph, c: (i, ph * c, 0)),
                   pl.BlockSpec((1, 2, SSD_STATE, hp), lambda i, ph, c: (i, 0, 0, 0))],
        out_shape=[jax.ShapeDtypeStruct((b, t, hp), BF16),
                   jax.ShapeDtypeStruct((b, 2, SSD_STATE, hp), F32)],
        scratch_shapes=[pltpu.VMEM((SSD_STATE, hp), F32),
                        pltpu.VMEM((SSD_STATE, hp), F32),
                        pltpu.VMEM((t // CHUNK, SSD_STATE, hp), F32)],
        compiler_params=_cparams(("parallel", "arbitrary", "arbitrary")),
        name="ssd_scan",
    )(xbc, xbc, xbc, z, dt, lw["conv_w"], lw["conv_b"], lw["dt_bias"], lw["a_log"], lw["d_vec"], lw["ssd_nw"], init)


def _mix_residual(x_ref, att_ref, ret_ref, ssd_ref, mod_ref, n2_ref, wo_ref):
    mix = jnp.concatenate([att_ref[0], ret_ref[0], ssd_ref[0]], axis=-1)
    x1 = x_ref[0] + mod_ref[0, 2:3, :] * _dot(mix, wo_ref[...])
    h = _rms(x1, n2_ref[...]) * (1.0 + mod_ref[0, 4:5, :]) + mod_ref[0, 3:4, :]
    return x1, h


def _dense_kernel(x_ref, att_ref, ret_ref, ssd_ref, mod_ref, n2_ref, wo_ref, wg_ref, wu_ref, wd_ref, o_ref,
                  *, ff_chunk):
    x1, h = _mix_residual(x_ref, att_ref, ret_ref, ssd_ref, mod_ref, n2_ref, wo_ref)
    hb = h.astype(BF16)
    ff = wg_ref.shape[1]
    f = jnp.zeros_like(x1)
    for s in range(0, ff, ff_chunk):
        g = _dot(hb, wg_ref[:, s:s + ff_chunk])
        u = _dot(hb, wu_ref[:, s:s + ff_chunk])
        f = f + _dot((_silu(g) * u).astype(BF16), wd_ref[s:s + ff_chunk, :])
    o_ref[0] = x1 + mod_ref[0, 5:6, :] * f


def _dense_block(xs, att, ret, ssd, mods, lw, tm):
    b, t, d = xs.shape
    ff = lw["wg"].shape[1]
    tok = lambda w: pl.BlockSpec((1, tm, w), lambda i, j: (i, j, 0))
    const = lambda shape: pl.BlockSpec(shape, lambda i, j: (0,) * len(shape), pipeline_mode=pl.Buffered(1))
    return pl.pallas_call(
        functools.partial(_dense_kernel, ff_chunk=ff // 2),
        grid=(b, t // tm),
        in_specs=[tok(d), tok(att.shape[-1]), tok(ret.shape[-1]), tok(ssd.shape[-1]),
                  pl.BlockSpec((1, 8, d), lambda i, j: (i, 0, 0)), const((1, d)),
                  const((d, d)), const((d, ff)), const((d, ff)), const((ff, d))],
        out_specs=tok(d),
        out_shape=jax.ShapeDtypeStruct((b, t, d), F32),
        compiler_params=_cparams(("parallel", "parallel")),
        name="outproj_ffn",
    )(xs, att, ret, ssd, mods, lw["n2"], lw["wo"], lw["wg"], lw["wu"], lw["wd"])


SUBLANES = 8


def _route_kernel(x_ref, att_ref, ret_ref, ssd_ref, mod_ref, n2_ref, wo_ref, wr_ref,
                  x1_ref, h_ref, slot_ref, gate_ref, cnt_ref):
    x1, h = _mix_residual(x_ref, att_ref, ret_ref, ssd_ref, mod_ref, n2_ref, wo_ref)
    x1_ref[0] = x1
    tm = h.shape[0]
    h_hi = h.astype(BF16)
    h_ref[0] = h_hi
    h_lo = (h - h_hi.astype(F32)).astype(BF16)
    logits = _dot(h_hi, wr_ref[0]) + _dot(h_lo, wr_ref[0]) + _dot(h_hi, wr_ref[1])
    lane = _iota(logits.shape, 1).astype(F32)
    logits = jnp.where(lane < N_EXPERTS, logits, NEG)
    m1 = jnp.max(logits, axis=-1, keepdims=True)
    i1 = jnp.min(jnp.where(logits == m1, lane, float(LANES)), axis=-1, keepdims=True)
    rest = jnp.where(lane == i1, NEG, logits)
    m2 = jnp.max(rest, axis=-1, keepdims=True)
    i2 = jnp.min(jnp.where(rest == m2, lane, float(LANES)), axis=-1, keepdims=True)
    e2 = jnp.exp(m2 - m1)
    den = 1.0 + e2
    sel1, sel2 = lane == i1, lane == i2
    chosen = jnp.where(sel1 | sel2, 1.0, 0.0)
    before = (_iota((tm, tm), 0) > _iota((tm, tm), 1)).astype(BF16)
    pos = _dot(before, chosen.astype(BF16))
    cnt = jnp.sum(chosen, axis=0, keepdims=True)
    nblk = sum(jnp.where(cnt > float(k * MOE_ROWS), 1.0, 0.0) for k in range(pl.cdiv(tm, MOE_ROWS)))
    lower_e = (_iota((LANES, LANES), 0) < _iota((LANES, LANES), 1)).astype(BF16)
    off = _dot_exact_rhs01(jnp.broadcast_to(nblk * float(MOE_ROWS), (SUBLANES, LANES)), lower_e)[0:1]
    slot = off + pos
    s1 = jnp.sum(jnp.where(sel1, slot, 0.0), axis=-1, keepdims=True)
    s2 = jnp.sum(jnp.where(sel2, slot, 0.0), axis=-1, keepdims=True)
    lane_i = _iota(logits.shape, 1)
    slots = jnp.where(lane_i == 0, s1, jnp.where(lane_i == 1, s2, -1.0))
    slot_ref[0, 0] = slots.T[0:SUBLANES].astype(jnp.int32)
    gate_ref[0] = jnp.where(sel1, 1.0 / den, 0.0) + jnp.where(sel2, e2 / den, 0.0)
    row = _iota((SUBLANES, LANES), 0)
    cnt_ref[0, 0] = jnp.where(row == 0, cnt, jnp.where(row == 1, off, jnp.where(row == 2, nblk, 0.0))
                              ).astype(jnp.int32)


def _route_block(xs, att, ret, ssd, mods, lw, tm):
    b, t, d = xs.shape
    nt = t // tm
    tok = lambda w: pl.BlockSpec((1, tm, w), lambda i, j: (i, j, 0))
    const = lambda shape: pl.BlockSpec(shape, lambda i, j: (0,) * len(shape))
    per_tile = lambda r, w: pl.BlockSpec((1, 1, r, w), lambda i, j: (i, j, 0, 0))
    return pl.pallas_call(
        _route_kernel,
        grid=(b, nt),
        in_specs=[tok(d), tok(att.shape[-1]), tok(ret.shape[-1]), tok(ssd.shape[-1]),
                  pl.BlockSpec((1, 8, d), lambda i, j: (i, 0, 0)), const((1, d)),
                  const((d, d)), const((2, d, LANES))],
        out_specs=[tok(d), tok(d), per_tile(SUBLANES, tm), tok(LANES), per_tile(SUBLANES, LANES)],
        out_shape=[jax.ShapeDtypeStruct((b, t, d), F32), jax.ShapeDtypeStruct((b, t, d), BF16),
                   jax.ShapeDtypeStruct((b, nt, SUBLANES, tm), jnp.int32),
                   jax.ShapeDtypeStruct((b, t, LANES), F32),
                   jax.ShapeDtypeStruct((b, nt, SUBLANES, LANES), jnp.int32)],
        compiler_params=_cparams(("parallel", "parallel")),
        name="outproj_route",
    )(xs, att, ret, ssd, mods, lw["n2"], lw["wo"], lw["router"])


MOE_ROWS = 320


def _moe_kernel(cnt_ref, slot_ref, h_ref, gate_ref, x1_ref, mod_ref, wgu_ref, wd_ref, fn_ref, o_ref,
                *, final_norm, ff_split):
    e = pl.program_id(2)
    rb = MOE_ROWS
    tm = h_ref.shape[1]

    @pl.when(e == 0)
    def _():
        o_ref[...] = jnp.zeros_like(o_ref)

    off = cnt_ref[0, 0, 1, e]
    nblocks = cnt_ref[0, 0, 2, e]
    slot1 = slot_ref[0, 0, 0:1, :]
    slot2 = slot_ref[0, 0, 1:2, :]
    gates = gate_ref[0]
    ge = jnp.sum(jnp.where(_iota(gates.shape, 1) == e, gates, 0.0), axis=-1, keepdims=True)

    def block(bi, carry):
        rows = off + bi * rb + _iota((rb, tm), 0)
        sel = jnp.where((rows == slot1) | (rows == slot2), 1.0, 0.0).astype(BF16)
        lhs = _dot(sel, h_ref[0]).astype(BF16)
        y = None
        for s0, s1 in ff_split:
            w = s1 - s0
            gu = _dot(lhs, wgu_ref[0, :, 2 * s0:2 * s1])
            part = _dot((_silu(gu[:, :w]) * gu[:, w:]).astype(BF16), wd_ref[0, s0:s1, :])
            y = part if y is None else y + part
        o_ref[0] += ge * _dot_tn(sel, y.astype(BF16))
        return carry

    lax.fori_loop(0, nblocks, block, 0)

    @pl.when(e == N_EXPERTS - 1)
    def _():
        out = x1_ref[0] + mod_ref[0, 5:6, :] * o_ref[0]
        o_ref[0] = _rms(out, fn_ref[...]) if final_norm else out


MOE_FF_SLICE = 512


def _moe_ff_split(ff):
    return tuple((s, min(s + MOE_FF_SLICE, ff)) for s in range(0, ff, MOE_FF_SLICE))


def _moe_weights(w_gate, w_up, w_down):
    ff = w_gate.shape[-1]
    wgu = jnp.concatenate([w[..., s0:s1] for s0, s1 in _moe_ff_split(ff) for w in (w_gate, w_up)], axis=-1)
    return wgu.astype(BF16), w_down.astype(BF16)


def _moe_block(x1, h, slots, gates, counts, mods, lw, fnw, final_norm, tm):
    b, t, d = x1.shape
    ff = lw["moe_wd"].shape[1]
    assert MOE_ROWS % (2 * SUBLANES) == 0
    tok = lambda w: pl.BlockSpec((1, tm, w), lambda i, j, e: (i, j, 0))
    per_tile = lambda w, **kw: pl.BlockSpec((1, 1, SUBLANES, w), lambda i, j, e: (i, j, 0, 0), **kw)
    return pl.pallas_call(
        functools.partial(_moe_kernel, final_norm=final_norm, ff_split=_moe_ff_split(ff)),
        grid=(b, t // tm, N_EXPERTS),
        in_specs=[per_tile(LANES, memory_space=pltpu.SMEM), per_tile(tm), tok(d), tok(LANES), tok(d),
                  pl.BlockSpec((1, 8, d), lambda i, j, e: (i, 0, 0)),
                  pl.BlockSpec((1, d, 2 * ff), lambda i, j, e: (e, 0, 0)),
                  pl.BlockSpec((1, ff, d), lambda i, j, e: (e, 0, 0)),
                  pl.BlockSpec((1, d), lambda i, j, e: (0, 0))],
        out_specs=tok(d),
        out_shape=jax.ShapeDtypeStruct((b, t, d), F32),
        compiler_params=_cparams(("parallel", "parallel", "arbitrary")),
        name="moe",
    )(counts, slots, h, gates, x1, mods, lw["moe_wgu"], lw["moe_wd"], fnw)


def _final_norm_kernel(x_ref, w_ref, o_ref):
    o_ref[0] = _rms(x_ref[0], w_ref[...])


def _final_norm(xs, fnw, tm):
    b, t, d = xs.shape
    return pl.pallas_call(
        _final_norm_kernel,
        grid=(b, t // tm),
        in_specs=[pl.BlockSpec((1, tm, d), lambda i, j: (i, j, 0)), pl.BlockSpec((1, d), lambda i, j: (0, 0))],
        out_specs=pl.BlockSpec((1, tm, d), lambda i, j: (i, j, 0)),
        out_shape=jax.ShapeDtypeStruct((b, t, d), F32),
        compiler_params=_cparams(("parallel", "parallel")),
        name="final_norm",
    )(xs, fnw)


def _rot_cols(w):
    a, b2, c, d = jnp.split(w, 4, axis=-1)
    return jnp.concatenate([-b2, a, -d, c], axis=-1)


def _layer_weights(i, norm1_w, norm2_w, w_in, w_out, mla_q_norm_w, mla_w_qb, mla_kv_norm_w, mla_w_kb, mla_w_vb,
                   ret_decay_logit, ssd_conv_w, ssd_conv_b, ssd_dt_bias, ssd_a_log, ssd_d, ssd_norm_w):
    d = w_in.shape[1]
    offs = np.cumsum((0,) + IN_SIZES)
    seg = [w_in[i][:, offs[j]:offs[j + 1]] for j in range(len(IN_SIZES))]
    zeros = lambda n: jnp.zeros((d, n), F32)
    w_all = jnp.concatenate(
        [seg[0], seg[1], seg[2], _rot_cols(seg[2]), zeros(W_MLA - MLA_Q_LORA - MLA_KV_LORA - 2 * MLA_ROPE),
         seg[3] * (RET_QK ** -0.5), seg[4], seg[5], seg[6], seg[7], seg[8], seg[9], seg[10],
         zeros(W_DT - 2 * SSD_HEADS)], axis=1).astype(BF16)

    dh = MLA_NOPE + MLA_ROPE
    wqb = mla_w_qb[i].reshape(MLA_Q_LORA, MLA_HEADS, dh)
    pad = lambda a, lo, hi: jnp.pad(a, ((0, 0), (0, 0), (lo, hi)))
    wq = pad(wqb, 0, MLA_HP - dh).reshape(MLA_Q_LORA, MLA_HEADS * MLA_HP).astype(BF16)
    wqr = pad(_rot_cols(wqb[:, :, MLA_NOPE:]), MLA_NOPE, MLA_HP - dh).reshape(MLA_Q_LORA, MLA_HEADS * MLA_HP).astype(BF16)
    wkb = pad(mla_w_kb[i].reshape(MLA_KV_LORA, MLA_HEADS, MLA_NOPE), 0, MLA_HP - MLA_NOPE)
    place = np.zeros((MLA_ROPE, MLA_HEADS, MLA_HP), np.float32)
    for r in range(MLA_ROPE):
        place[r, :, MLA_NOPE + r] = 1.0
    place = jnp.asarray(place)
    wk = jnp.concatenate([wkb, place, place, jnp.zeros((MLA_KV_LORA - 2 * MLA_ROPE, MLA_HEADS, MLA_HP), F32)],
                         axis=0).reshape(2 * MLA_KV_LORA, MLA_HEADS * MLA_HP).astype(BF16)

    wvb = mla_w_vb[i].reshape(MLA_KV_LORA, MLA_HEADS, MLA_V)
    wv = jnp.concatenate(
        [pad(wvb[:, hh:hh + 1], (hh % 2) * MLA_V, MLA_HP - MLA_V - (hh % 2) * MLA_V) for hh in range(MLA_HEADS)],
        axis=1).reshape(MLA_KV_LORA, MLA_HEADS * MLA_HP).astype(BF16)

    lg = ret_decay_logit[i]
    padl = lambda v: jnp.pad(v.reshape(1, -1), ((0, 0), (0, LANES - 2 * SSD_HEADS)))
    return {
        "n1": norm1_w[i].reshape(1, d), "n2": norm2_w[i].reshape(1, d),
        "w_all": w_all, "qn": mla_q_norm_w[i].reshape(1, -1), "wq": wq, "wqr": wqr,
        "kvn": mla_kv_norm_w[i].reshape(1, -1), "wk": wk, "wv": wv,
        "wo": w_out[i].astype(BF16),
        "ret_lgl": jnp.repeat(lg, RET_V, axis=1),
        "ret_lgh": jnp.broadcast_to(lg.reshape(2 * RET_HEADS, 1), (2 * RET_HEADS, CHUNK)),
        "conv_w": jnp.pad(ssd_conv_w[i], ((0, 8 - ssd_conv_w.shape[1]), (0, 0))),
        "conv_b": ssd_conv_b[i].reshape(1, -1),
        "dt_bias": padl(ssd_dt_bias[i]), "a_log": padl(ssd_a_log[i]),
        "d_vec": jnp.repeat(ssd_d[i], SSD_HEADDIM).reshape(1, -1),
        "ssd_nw": ssd_norm_w[i].reshape(1, -1),
    }


def _rope_tables(n_lat, n_ctx):
    rows = n_lat // GRID_W
    row = jnp.repeat(jnp.arange(rows, dtype=F32), GRID_W)
    col = jnp.tile(jnp.arange(GRID_W, dtype=F32), rows)
    inv_freq = ROPE_BASE ** (-jnp.arange(0, ROPE_AXIS_DIM, 2, dtype=F32) / ROPE_AXIS_DIM)
    ang_r = row[:, None] * inv_freq
    ang_c = col[:, None] * inv_freq
    ang = jnp.concatenate([ang_r, ang_r, ang_c, ang_c], axis=-1)
    c2 = MLA_SCALE * math.log2(math.e)

    def build(cos, sin):
        n = cos.shape[0]
        cosq = jnp.concatenate([jnp.full((n, MLA_NOPE), c2, F32), c2 * cos, jnp.zeros((n, MLA_HP - MLA_NOPE - MLA_ROPE), F32)], 1)
        sinq = jnp.concatenate([jnp.zeros((n, MLA_NOPE), F32), c2 * sin, jnp.zeros((n, MLA_HP - MLA_NOPE - MLA_ROPE), F32)], 1)
        ck = jnp.concatenate([cos, sin, jnp.zeros((n, LANES - 2 * MLA_ROPE), F32)], 1)
        return cosq, sinq, ck

    lat = build(jnp.cos(ang), jnp.sin(ang))
    ctx = build(jnp.ones((n_ctx, MLA_ROPE), F32), jnp.zeros((n_ctx, MLA_ROPE), F32))
    return lat, ctx


def _token_tile(t, pref):
    tm = min(pref, t)
    while t % tm:
        tm //= 2
    return tm


def kernel(x, c, ctx, c_ctx, mod_w, mod_b, norm1_w, norm2_w, w_in, w_out, mla_q_norm_w, mla_w_qb, mla_kv_norm_w, mla_w_kb, mla_w_vb, ret_decay_logit, ssd_conv_w, ssd_conv_b, ssd_dt_bias, ssd_a_log, ssd_d, ssd_norm_w, ffn_w_gate, ffn_w_up, ffn_w_down, moe_router, moe_w_gate, moe_w_up, moe_w_down, final_norm_w):
    b, n_lat, d = x.shape
    n_ctx = ctx.shape[1]
    depth = mod_w.shape[0]
    assert b + 1 <= 8 and n_lat % CHUNK == 0 and n_ctx % CHUNK == 0 and n_lat % GRID_W == 0

    cond = jnp.concatenate([c, c_ctx[None, :], jnp.zeros((8 - b - 1, d), F32)], axis=0)
    mod_all = _modulation(cond, mod_w, mod_b)
    tabs_x, tabs_c = _rope_tables(n_lat, n_ctx)
    fnw = final_norm_w.reshape(1, d)

    tm_x = _token_tile(n_lat, 512)
    tm_c = _token_tile(n_ctx, 256)
    tm_moe = _token_tile(n_lat, 1024)
    tq_x = _token_tile(n_lat, 1024)
    tk_x = 2048
    zero_ret = jnp.zeros((b, 2, RET_QK, RET_HEADS * RET_V), F32)
    zero_ssd = jnp.zeros((b, 2, SSD_STATE, SSD_INNER), F32)

    h_ctx = ctx
    for i in range(depth):
        last = i == depth - 1
        lw = _layer_weights(i, norm1_w, norm2_w, w_in, w_out, mla_q_norm_w, mla_w_qb, mla_kv_norm_w, mla_w_kb,
                            mla_w_vb, ret_decay_logit, ssd_conv_w, ssd_conv_b, ssd_dt_bias, ssd_a_log, ssd_d,
                            ssd_norm_w)
        m6 = mod_all[i].reshape(8, 6, d)
        pad2 = jnp.zeros((b, 2, d), F32)
        mods_x = jnp.concatenate([m6[:b], pad2], axis=1)
        mods_c = jnp.concatenate([jnp.broadcast_to(m6[b:b + 1], (b, 6, d)), pad2], axis=1)

        qc, kc, vc, retc, zc, xbcc, dtc = _inproj(h_ctx, mods_c, lw, tabs_c, tm_c)
        ret_yc, ret_fin = _ret_scan(retc, lw, zero_ret)
        ssd_yc, ssd_fin = _ssd_scan(xbcc, zc, dtc, lw, zero_ssd)

        qx, kx, vx, retx, zx, xbcx, dtx = _inproj(x, mods_x, lw, tabs_x, tm_x)
        att_x = _attention(qx, [(kc, vc), (kx, vx)], tq_x, tk_x)
        ret_yx, _ = _ret_scan(retx, lw, ret_fin)
        ssd_yx, _ = _ssd_scan(xbcx, zx, dtx, lw, ssd_fin)

        if i % 2 == 0:
            j = i // 2
            lw.update(wg=ffn_w_gate[j].astype(BF16), wu=ffn_w_up[j].astype(BF16), wd=ffn_w_down[j].astype(BF16))
            x = _dense_block(x, att_x, ret_yx, ssd_yx, mods_x, lw, tm_x)
            if last:
                x = _final_norm(x, fnw, tm_x)
        else:
            j = i // 2
            wgu, wd = _moe_weights(moe_w_gate[j], moe_w_up[j], moe_w_down[j])
            r_f32 = jnp.pad(moe_router[j], ((0, 0), (0, LANES - N_EXPERTS)))
            r_hi = r_f32.astype(BF16)
            r_lo = (r_f32 - r_hi.astype(F32)).astype(BF16)
            lw.update(router=jnp.stack([r_hi, r_lo]), moe_wgu=wgu, moe_wd=wd)
            x1, *routed = _route_block(x, att_x, ret_yx, ssd_yx, mods_x, lw, tm_moe)
            x = _moe_block(x1, *routed, mods_x, lw, fnw, last, tm_moe)

        if not last:
            att_c = _attention(qc, [(kc, vc)], tm_c, tk_x)
            if i % 2 == 0:
                h_ctx = _dense_block(h_ctx, att_c, ret_yc, ssd_yc, mods_c, lw, tm_c)
            else:
                c1, *routed = _route_block(h_ctx, att_c, ret_yc, ssd_yc, mods_c, lw, tm_c)
                h_ctx = _moe_block(c1, *routed, mods_c, lw, fnw, False, tm_c)
    return x
```

```python
import functools
import math

import jax
import jax.numpy as jnp
import numpy as np
from jax import lax
from jax.experimental import pallas as pl
from jax.experimental.pallas import tpu as pltpu

F32 = jnp.float32
BF16 = jnp.bfloat16
HIGHEST = lax.Precision.HIGHEST

EPS = 1e-6
CHUNK = 128
GRID_W = 64
MLA_HEADS = 4
MLA_Q_LORA = 256
MLA_KV_LORA = 128
MLA_NOPE = 64
MLA_ROPE = 32
MLA_V = 64
MLA_SCALE = (MLA_NOPE + MLA_ROPE) ** -0.5
ROPE_AXIS_DIM = MLA_ROPE // 2
ROPE_BASE = 10000.0
RET_HEADS = 4
RET_QK = 64
RET_V = 64
SSD_HEADS = 8
SSD_HEADDIM = 64
SSD_GROUPS = 2
SSD_STATE = 64
SSD_INNER = SSD_HEADS * SSD_HEADDIM
SSD_CONV_CH = SSD_INNER + 2 * SSD_GROUPS * SSD_STATE
N_EXPERTS = 8
IN_SIZES = (MLA_Q_LORA, MLA_KV_LORA, MLA_ROPE,
            RET_HEADS * RET_QK, RET_HEADS * RET_QK, RET_HEADS * RET_V, RET_HEADS * RET_V,
            SSD_INNER, SSD_CONV_CH, SSD_HEADS, SSD_HEADS)

LANES = 128
V7X_VMEM_LIMIT = 56 * 1024 * 1024

MLA_HP = 128
NEG = -1e30


def _cparams(sem):
    return pltpu.CompilerParams(dimension_semantics=sem, vmem_limit_bytes=V7X_VMEM_LIMIT)


def _dot(a, b):
    return jnp.dot(a, b, preferred_element_type=F32)


def _dot_nt(a, b):
    return lax.dot_general(a, b, (((1,), (1,)), ((), ())), preferred_element_type=F32)


def _dot_tn(a, b):
    return lax.dot_general(a, b, (((0,), (0,)), ((), ())), preferred_element_type=F32)


def _split3(x):
    hi = x.astype(BF16)
    r1 = x - hi.astype(F32)
    mid = r1.astype(BF16)
    lo = (r1 - mid.astype(F32)).astype(BF16)
    return hi, mid, lo


def _dot_exact_rhs01(x, e):
    hi, mid, lo = _split3(x)
    return _dot(hi, e) + _dot(mid, e) + _dot(lo, e)


def _dot_exact_lhs01(e, x):
    hi, mid, lo = _split3(x)
    return _dot(e, hi) + _dot(e, mid) + _dot(e, lo)


def _silu(x):
    return x / (1.0 + jnp.exp(-x))


def _rms(x, w):
    return x * lax.rsqrt(jnp.mean(x * x, axis=-1, keepdims=True) + EPS) * w


def _iota(shape, dim):
    return lax.broadcasted_iota(jnp.int32, shape, dim)


def _mod_kernel(a_ref, w_ref, b_ref, o_ref):
    a = _silu(a_ref[...])
    o_ref[0] = jnp.dot(a, w_ref[0], precision=HIGHEST, preferred_element_type=F32) + b_ref[0]


def _modulation(cond, mod_w, mod_b):
    depth, d, n = mod_w.shape
    tn = 1536
    return pl.pallas_call(
        _mod_kernel,
        grid=(depth, n // tn),
        in_specs=[pl.BlockSpec((8, d), lambda i, j: (0, 0)),
                  pl.BlockSpec((1, d, tn), lambda i, j: (i, 0, j)),
                  pl.BlockSpec((1, 1, tn), lambda i, j: (i, 0, j))],
        out_specs=pl.BlockSpec((1, 8, tn), lambda i, j: (i, 0, j)),
        out_shape=jax.ShapeDtypeStruct((depth, 8, n), F32),
        compiler_params=_cparams(("parallel", "parallel")),
        name="modulation",
    )(cond, mod_w, mod_b.reshape(depth, 1, n))


W_MLA, W_RET, W_Z, W_XBC, W_DT = 512, 1024, 512, 768, 128
W_ALL = W_MLA + W_RET + W_Z + W_XBC + W_DT


def _ones_lane_mask(shape):
    lane = _iota(shape, len(shape) - 1) % (2 * MLA_HP)
    return (lane == MLA_HP - 1) | (lane == MLA_HP)


def _inproj_kernel(x_ref, mod_ref, n1_ref, win_ref, qn_ref, wq_ref, wqr_ref, kvn_ref, wk_ref, wv_ref,
                   cq_ref, sq_ref, ck_ref,
                   q_out, k_out, v_out, ret_out, z_out, xbc_out, dt_out):
    x = x_ref[0]
    h = _rms(x, n1_ref[...]) * (1.0 + mod_ref[0, 1:2, :]) + mod_ref[0, 0:1, :]
    hb = h.astype(BF16)

    o = 0
    pm = _dot(hb, win_ref[:, o:o + W_MLA]); o += W_MLA
    ret_out[0] = _dot(hb, win_ref[:, o:o + W_RET]).astype(BF16); o += W_RET
    z_out[0] = _dot(hb, win_ref[:, o:o + W_Z]).astype(BF16); o += W_Z
    xbc_out[0] = _dot(hb, win_ref[:, o:o + W_XBC]).astype(BF16); o += W_XBC
    dt_out[0] = _dot(hb, win_ref[:, o:o + W_DT])

    cqb = _rms(pm[:, 0:MLA_Q_LORA], qn_ref[...]).astype(BF16)
    qm = _dot(cqb, wq_ref[...])
    qr = _dot(cqb, wqr_ref[...])
    cosq, sinq = cq_ref[...], sq_ref[...]
    for hh in range(MLA_HEADS):
        sl = slice(hh * MLA_HP, (hh + 1) * MLA_HP)
        q_out[0, :, sl] = (qm[:, sl] * cosq + qr[:, sl] * sinq).astype(BF16)

    ckvn = _rms(pm[:, MLA_Q_LORA:MLA_Q_LORA + MLA_KV_LORA], kvn_ref[...])
    u = pm[:, MLA_Q_LORA + MLA_KV_LORA:W_MLA] * ck_ref[...]
    kin = jnp.concatenate([ckvn, u], axis=-1).astype(BF16)
    k_out[0] = _dot(kin, wk_ref[...]).astype(BF16)
    v = _dot(ckvn.astype(BF16), wv_ref[...])
    v_out[0] = jnp.where(_ones_lane_mask(v.shape), 1.0, v).astype(BF16)


def _inproj(xs, mods, lw, tabs, tm):
    b, t, d = xs.shape
    const = lambda shape: pl.BlockSpec(shape, lambda i, j: (0,) * len(shape))
    tok = lambda w: pl.BlockSpec((1, tm, w), lambda i, j: (i, j, 0))
    tab = pl.BlockSpec((tm, LANES), lambda i, j: (j, 0))
    widths = (MLA_HEADS * MLA_HP, MLA_HEADS * MLA_HP, MLA_HEADS * MLA_HP, W_RET, W_Z, W_XBC, W_DT)
    dtypes = (BF16, BF16, BF16, BF16, BF16, BF16, F32)
    return pl.pallas_call(
        _inproj_kernel,
        grid=(b, t // tm),
        in_specs=[tok(d), pl.BlockSpec((1, 8, d), lambda i, j: (i, 0, 0)), const((1, d)),
                  const((d, W_ALL)), const((1, MLA_Q_LORA)), const((MLA_Q_LORA, MLA_HEADS * MLA_HP)),
                  const((MLA_Q_LORA, MLA_HEADS * MLA_HP)), const((1, MLA_KV_LORA)),
                  const((2 * MLA_KV_LORA, MLA_HEADS * MLA_HP)), const((MLA_KV_LORA, MLA_HEADS * MLA_HP)),
                  tab, tab, tab],
        out_specs=[tok(w) for w in widths],
        out_shape=[jax.ShapeDtypeStruct((b, t, w), dt) for w, dt in zip(widths, dtypes)],
        compiler_params=_cparams(("parallel", "parallel")),
        name="inproj",
    )(xs, mods, lw["n1"], lw["w_all"], lw["qn"], lw["wq"], lw["wqr"], lw["kvn"], lw["wk"], lw["wv"], *tabs)


def _attn_kernel(*refs, seg_blocks, tq):
    nseg = len(seg_blocks)
    q_ref = refs[0]
    kv_refs = refs[1:1 + 2 * nseg]
    o_ref = refs[1 + 2 * nseg]
    carry = tuple((jnp.full((tq, 1), NEG, F32), jnp.zeros((tq, MLA_HP), F32)) for _ in range(MLA_HEADS))
    for si, (tk, nblk) in enumerate(seg_blocks):
        k_ref, v_ref = kv_refs[2 * si], kv_refs[2 * si + 1]

        def body(j, carry, k_ref=k_ref, v_ref=v_ref, tk=tk):
            start = 0 if isinstance(j, int) else pl.multiple_of(j * tk, tk)
            new = []
            for hh in range(MLA_HEADS):
                sl = slice(hh * MLA_HP, (hh + 1) * MLA_HP)
                m, acc = carry[hh]
                s = _dot_nt(q_ref[0, :, sl], k_ref[0, pl.ds(start, tk), sl])
                m_new = jnp.maximum(m, jnp.max(s, axis=-1, keepdims=True))
                p = jnp.exp2(s - m_new).astype(BF16)
                acc = jnp.exp2(m - m_new) * acc + _dot(p, v_ref[0, pl.ds(start, tk), sl])
                new.append((m_new, acc))
            return tuple(new)

        carry = body(0, carry) if nblk == 1 else lax.fori_loop(0, nblk, body, carry)
    lane = _iota((tq, MLA_HP), 1)
    pairs = []
    for hh in range(0, MLA_HEADS, 2):
        acc_e, acc_o = carry[hh][1], carry[hh + 1][1]
        l_e = jnp.sum(jnp.where(lane == MLA_HP - 1, acc_e, 0.0), axis=-1, keepdims=True)
        l_o = jnp.sum(jnp.where(lane == 0, acc_o, 0.0), axis=-1, keepdims=True)
        pairs.append(jnp.where(lane < MLA_V, acc_e / l_e, acc_o / l_o))
    o_ref[0] = jnp.concatenate(pairs, axis=-1).astype(BF16)


def _attention(q, segs, tq, tk):
    b, l, _ = q.shape
    seg_blocks = []
    in_specs = [pl.BlockSpec((1, tq, MLA_HEADS * MLA_HP), lambda i, j: (i, j, 0))]
    args = [q]
    for k, v in segs:
        s = k.shape[1]
        t = min(tk, s)
        seg_blocks.append((t, s // t))
        for _ in range(2):
            in_specs.append(pl.BlockSpec((1, s, MLA_HEADS * MLA_HP), lambda i, j: (i, 0, 0),
                                         pipeline_mode=pl.Buffered(1)))
        args += [k, v]
    return pl.pallas_call(
        functools.partial(_attn_kernel, seg_blocks=tuple(seg_blocks), tq=tq),
        grid=(b, l // tq),
        in_specs=in_specs,
        out_specs=pl.BlockSpec((1, tq, MLA_HEADS * MLA_V), lambda i, j: (i, j, 0)),
        out_shape=jax.ShapeDtypeStruct((b, l, MLA_HEADS * MLA_V), BF16),
        compiler_params=_cparams(("parallel", "arbitrary")),
        name="attention",
    )(*args)


def _ret_kernel(x_ref, lgl_ref, lgh_ref, init_ref, y_ref, fin_ref, s_ref, g_ref, gbuf_ref, *, nblk, sub):
    C, H, N, P = CHUNK, RET_HEADS, RET_QK, RET_V
    HP = H * P
    ph, c = pl.program_id(1), pl.program_id(2)
    blk = jnp.where(ph == 0, nblk - 1 - c, c)

    def log_sigmoid(v):
        return jnp.minimum(v, 0.0) - jnp.log(1.0 + jnp.exp(-jnp.abs(v)))

    lgl = log_sigmoid(lgl_ref[...])
    lgf, lgb = lgl[0:1], lgl[1:2]
    rows = _iota((C, HP), 0).astype(F32)
    head_n = _iota((N, HP), 1) // P

    def diag_blocks(r):
        out = r[0:N]
        for hh in range(1, H):
            out = jnp.where(head_n == hh, r[hh * N:(hh + 1) * N], out)
        return out

    def as_rows(s_c):
        return jnp.concatenate([jnp.where(head_n == hh, s_c, 0.0) for hh in range(H)], axis=0)

    @pl.when((ph == 0) & (c == 0))
    def _():
        g_ref[...] = init_ref[0, 1]

    @pl.when(ph == 0)
    def _():
        w_b = jnp.exp(rows * lgb)
        tot_b = jnp.exp(C * lgb)
        for sb in reversed(range(sub)):
            r0 = sb * C
            k = x_ref[0, r0:r0 + C, HP:2 * HP].astype(F32)
            v = x_ref[0, r0:r0 + C, 2 * HP:3 * HP]
            g = g_ref[...]
            gbuf_ref[blk * sub + sb] = g
            g_ref[...] = g * tot_b + diag_blocks(_dot_tn((k * w_b).astype(BF16), v))

        @pl.when(c == nblk - 1)
        def _():
            fin_ref[0, 1] = g_ref[...]

    @pl.when((ph == 1) & (c == 0))
    def _():
        s_ref[...] = init_ref[0, 0]

    @pl.when(ph == 1)
    def _():
        lgh = log_sigmoid(lgh_ref[...])
        grp = _iota((C, HP), 1) // P
        di = (_iota((C, C), 0) - _iota((C, C), 1)).astype(F32)
        decay = [jnp.exp(jnp.where(di >= 0, di * lgh[hh:hh + 1, :], -di * lgh[H + hh:H + hh + 1, :]))
                 for hh in range(H)]
        e_f = jnp.exp((rows + 1.0) * lgf)
        e_b = jnp.exp((C - rows) * lgb)
        w_f = jnp.exp((C - 1.0 - rows) * lgf)
        tot_f = jnp.exp(C * lgf)
        for sb in range(sub):
            r0 = sb * C
            q = x_ref[0, r0:r0 + C, 0:HP]
            kb = x_ref[0, r0:r0 + C, HP:2 * HP]
            v = x_ref[0, r0:r0 + C, 2 * HP:3 * HP]
            gate = x_ref[0, r0:r0 + C, 3 * HP:4 * HP].astype(F32)
            ws, vs = [], []
            for hh in range(H):
                sc = _dot_nt(jnp.where(grp == hh, q, jnp.zeros_like(q)), kb)
                ws.append((sc * decay[hh]).astype(BF16))
                vs.append(jnp.where(grp == hh, v, jnp.zeros_like(v)))
            y = _dot(jnp.concatenate(ws, axis=1), jnp.concatenate(vs, axis=0))
            qf32 = q.astype(F32)
            s = s_ref[...]
            g_in = gbuf_ref[blk * sub + sb]
            y = y + _dot(jnp.concatenate([(qf32 * e_f).astype(BF16), (qf32 * e_b).astype(BF16)], axis=1),
                         jnp.concatenate([as_rows(s), as_rows(g_in)], axis=0).astype(BF16))
            s_ref[...] = s * tot_f + diag_blocks(_dot_tn((kb.astype(F32) * w_f).astype(BF16), v))

            mu = jnp.zeros_like(y)
            for hh in range(H):
                mh = jnp.sum(jnp.where(grp == hh, y, 0.0), axis=-1, keepdims=True) * (1.0 / P)
                mu = jnp.where(grp == hh, mh, mu)
            yc = y - mu
            var = jnp.zeros_like(y)
            for hh in range(H):
                vh = jnp.sum(jnp.where(grp == hh, yc * yc, 0.0), axis=-1, keepdims=True) * (1.0 / P)
                var = jnp.where(grp == hh, vh, var)
            y_ref[0, r0:r0 + C, :] = (yc * lax.rsqrt(var + EPS) * _silu(gate)).astype(BF16)

        @pl.when(c == nblk - 1)
        def _():
            fin_ref[0, 0] = s_ref[...]


def _block_index(nblk):
    return lambda i, ph, c: (i, (1 - ph) * (nblk - 1 - c) + ph * c, 0)


def _scan_sub(t):
    nc = t // CHUNK
    return 4 if nc % 4 == 0 else (2 if nc % 2 == 0 else 1)


def _ret_scan(ret, lw, init):
    b, t, _ = ret.shape
    sub = _scan_sub(t)
    rows = sub * CHUNK
    nblk = t // rows
    hp = RET_HEADS * RET_V
    return pl.pallas_call(
        functools.partial(_ret_kernel, nblk=nblk, sub=sub),
        grid=(b, 2, nblk),
        in_specs=[pl.BlockSpec((1, rows, W_RET), _block_index(nblk)),
                  pl.BlockSpec((2, hp), lambda i, ph, c: (0, 0)),
                  pl.BlockSpec((2 * RET_HEADS, CHUNK), lambda i, ph, c: (0, 0)),
                  pl.BlockSpec((1, 2, RET_QK, hp), lambda i, ph, c: (i, 0, 0, 0))],
        out_specs=[pl.BlockSpec((1, rows, hp), lambda i, ph, c: (i, ph * c, 0)),
                   pl.BlockSpec((1, 2, RET_QK, hp), lambda i, ph, c: (i, 0, 0, 0))],
        out_shape=[jax.ShapeDtypeStruct((b, t, hp), BF16),
                   jax.ShapeDtypeStruct((b, 2, RET_QK, hp), F32)],
        scratch_shapes=[pltpu.VMEM((RET_QK, hp), F32),
                        pltpu.VMEM((RET_QK, hp), F32),
                        pltpu.VMEM((t // CHUNK, RET_QK, hp), F32)],
        compiler_params=_cparams(("parallel", "arbitrary", "arbitrary")),
        name="retention_scan",
    )(ret, lw["ret_lgl"], lw["ret_lgh"], init)


def _ssd_kernel(x_ref, xp_ref, xn_ref, z_ref, dt_ref, cw_ref, cb_ref, dtb_ref, alog_ref, dvec_ref, nw_ref,
                init_ref, y_ref, fin_ref, s_ref, g_ref, gbuf_ref, *, nblk, sub):
    C, H, N, P, G = CHUNK, SSD_HEADS, SSD_STATE, SSD_HEADDIM, SSD_GROUPS
    HP = H * P
    HPG = H // G
    HALO = SSD_HALO
    ph, c = pl.program_id(1), pl.program_id(2)
    blk = jnp.where(ph == 0, nblk - 1 - c, c)

    ri = _iota((C, C + 2 * HALO), 0)
    ci = _iota((C, C + 2 * HALO), 1)
    sel_prev = (ci == ri + HALO - 1).astype(BF16)
    sel_next = (ci == ri + HALO + 1).astype(BF16)
    lane = _iota((LANES, HP), 0)
    head_of = _iota((LANES, HP), 1) // P
    e_f = (lane == head_of).astype(BF16)
    e_b = (lane == head_of + H).astype(BF16)
    r_i = _iota((C, C), 0)
    c_i = _iota((C, C), 1)
    low = (r_i >= c_i).astype(BF16)
    upp = (r_i <= c_i).astype(BF16)
    grp_n = _iota((N, HP), 1) // (HPG * P)
    neg_a = -jnp.exp(alog_ref[...])
    cw = cw_ref[...]

    def prep(sb):
        r0 = sb * C
        xc = x_ref[0, r0:r0 + C, :]
        if sb == 0:
            prev = jnp.where(blk > 0, xp_ref[0], jnp.zeros_like(xp_ref[0]))
        else:
            prev = x_ref[0, r0 - HALO:r0, :]
        if sb == sub - 1:
            nxt = jnp.where(blk < nblk - 1, xn_ref[0], jnp.zeros_like(xn_ref[0]))
        else:
            nxt = x_ref[0, r0 + C:r0 + C + HALO, :]
        stacked = jnp.concatenate([prev, xc, nxt], axis=0)
        conv = (cw[0:1] * _dot(sel_prev, stacked) + cw[1:2] * xc.astype(F32)
                + cw[2:3] * _dot(sel_next, stacked) + cb_ref[...])
        act = _silu(conv)
        xs = act[:, 0:HP]
        bm = act[:, HP:HP + G * N].astype(BF16)
        cm = act[:, HP + G * N:HP + 2 * G * N].astype(BF16)
        raw = dt_ref[0, r0:r0 + C, :] + dtb_ref[...]
        dt = jnp.maximum(raw, 0.0) + jnp.log(1.0 + jnp.exp(-jnp.abs(raw)))
        la = dt * neg_a
        a_f = _dot_exact_lhs01(low, la)
        a_b = _dot_exact_lhs01(upp, la)
        return xs, bm, cm, dt, la, a_f, a_b

    def state_update(state, bm, xs, wdt, tot8, e_dir, row):
        xw = (xs * _dot(wdt.astype(BF16), e_dir)).astype(BF16)
        tot = _dot_exact_rhs01(tot8, e_dir)[row:row + 1]
        r = _dot_tn(bm, xw)
        upd = r[0:N]
        for gg in range(1, G):
            upd = jnp.where(grp_n == gg, r[gg * N:(gg + 1) * N], upd)
        return state * tot + upd

    def as_rows(s_c):
        return jnp.concatenate([jnp.where(grp_n == gg, s_c, 0.0) for gg in range(G)], axis=0).astype(BF16)

    @pl.when((ph == 0) & (c == 0))
    def _():
        g_ref[...] = init_ref[0, 1]

    @pl.when(ph == 0)
    def _():
        for sb in reversed(range(sub)):
            xs, bm, _, dt, _, _, a_b = prep(sb)
            g = g_ref[...]
            gbuf_ref[blk * sub + sb] = g
            wdt = jnp.exp(a_b[0:1, :] - a_b) * dt
            g_ref[...] = state_update(g, bm, xs, wdt, jnp.exp(a_b[0:8, :]), e_b, 0)

        @pl.when(c == nblk - 1)
        def _():
            fin_ref[0, 1] = g_ref[...]

    @pl.when((ph == 1) & (c == 0))
    def _():
        s_ref[...] = init_ref[0, 0]

    @pl.when(ph == 1)
    def _():
        grp_l = _iota((C, G * N), 1) // N
        head_l = _iota((C, HP), 1) // P
        tri_f = r_i >= c_i
        diag = r_i == c_i
        for sb in range(sub):
            r0 = sb * C
            xs, bm, cm, dt, la, a_f, a_b = prep(sb)
            la_t = la.T[0:2 * H]
            dt_t = dt.T[0:2 * H]
            af_row = _dot_exact_rhs01(la_t, upp)
            ab_row = _dot_exact_rhs01(la_t, low)
            sc = [_dot_nt(jnp.where(grp_l == gg, cm, jnp.zeros_like(cm)), bm) for gg in range(G)]
            xsb = xs.astype(BF16)
            ws, vs = [], []
            for hh in range(H):
                seg_f = a_f[:, hh:hh + 1] - af_row[hh:hh + 1, :]
                seg_b = a_b[:, H + hh:H + hh + 1] - ab_row[H + hh:H + hh + 1, :]
                dt_f, dt_b = dt_t[hh:hh + 1, :], dt_t[H + hh:H + hh + 1, :]
                dm = (jnp.exp(jnp.where(tri_f, seg_f, seg_b)) * jnp.where(tri_f, dt_f, dt_b)
                      + jnp.where(diag, dt_b, 0.0))
                ws.append((sc[hh // HPG] * dm).astype(BF16))
                vs.append(jnp.where(head_l == hh, xsb, jnp.zeros_like(xsb)))
            y = _dot(jnp.concatenate(ws, axis=1), jnp.concatenate(vs, axis=0))

            s = s_ref[...]
            g_in = gbuf_ref[blk * sub + sb]
            y = y + _dot(jnp.exp(a_f).astype(BF16), e_f) * _dot(cm, as_rows(s))
            y = y + _dot(jnp.exp(a_b).astype(BF16), e_b) * _dot(cm, as_rows(g_in))

            wdt = jnp.exp(a_f[C - 1:C, :] - a_f) * dt
            s_ref[...] = state_update(s, bm, xs, wdt, jnp.exp(a_f[C - 8:C, :]), e_f, 7)

            y = y + xs * dvec_ref[...]
            y = y * _silu(z_ref[0, r0:r0 + C, :].astype(F32))
            y_ref[0, r0:r0 + C, :] = _rms(y, nw_ref[...]).astype(BF16)

        @pl.when(c == nblk - 1)
        def _():
            fin_ref[0, 0] = s_ref[...]


SSD_HALO = 16


def _ssd_scan(xbc, z, dt, lw, init):
    b, t, _ = xbc.shape
    sub = _scan_sub(t)
    rows = sub * CHUNK
    nblk = t // rows
    hp = SSD_INNER
    per = rows // SSD_HALO
    nhalo = t // SSD_HALO
    bidx = _block_index(nblk)

    def prev_idx(i, ph, c):
        blk = (1 - ph) * (nblk - 1 - c) + ph * c
        return (i, jnp.maximum(blk * per - 1, 0), 0)

    def next_idx(i, ph, c):
        blk = (1 - ph) * (nblk - 1 - c) + ph * c
        return (i, jnp.minimum((blk + 1) * per, nhalo - 1), 0)

    const = lambda shape: pl.BlockSpec(shape, lambda i, ph, c: (0,) * len(shape))
    return pl.pallas_call(
        functools.partial(_ssd_kernel, nblk=nblk, sub=sub),
        grid=(b, 2, nblk),
        in_specs=[pl.BlockSpec((1, rows, W_XBC), bidx),
                  pl.BlockSpec((1, SSD_HALO, W_XBC), prev_idx),
                  pl.BlockSpec((1, SSD_HALO, W_XBC), next_idx),
                  pl.BlockSpec((1, rows, W_Z), bidx),
                  pl.BlockSpec((1, rows, W_DT), bidx),
                  const((8, W_XBC)), const((1, W_XBC)), const((1, LANES)), const((1, LANES)),
                  const((1, hp)), const((1, hp)),
                  pl.BlockSpec((1, 2, SSD_STATE, hp), lambda i, ph, c: (i, 0, 0, 0))],
        out_specs=[pl.BlockSpec((1, rows, hp), lambda i, ph, c: (i, ph * c, 0)),
                   pl.BlockSpec((1, 2, SSD_STATE, hp), lambda i, ph, c: (i, 0, 0, 0))],
        out_shape=[jax.ShapeDtypeStruct((b, t, hp), BF16),
                   jax.ShapeDtypeStruct((b, 2, SSD_STATE, hp), F32)],
        scratch_shapes=[pltpu.VMEM((SSD_STATE, hp), F32),
                        pltpu.VMEM((SSD_STATE, hp), F32),
                        pltpu.VMEM((t // CHUNK, SSD_STATE, hp), F32)],
        compiler_params=_cparams(("parallel", "arbitrary", "arbitrary")),
        name="ssd_scan",
    )(xbc, xbc, xbc, z, dt, lw["conv_w"], lw["conv_b"], lw["dt_bias"], lw["a_log"], lw["d_vec"], lw["ssd_nw"], init)


def _mix_residual(x_ref, att_ref, ret_ref, ssd_ref, mod_ref, n2_ref, wo_ref):
    mix = jnp.concatenate([att_ref[0], ret_ref[0], ssd_ref[0]], axis=-1)
    x1 = x_ref[0] + mod_ref[0, 2:3, :] * _dot(mix, wo_ref[...])
    h = _rms(x1, n2_ref[...]) * (1.0 + mod_ref[0, 4:5, :]) + mod_ref[0, 3:4, :]
    return x1, h


def _dense_kernel(x_ref, att_ref, ret_ref, ssd_ref, mod_ref, n2_ref, wo_ref, wg_ref, wu_ref, wd_ref, o_ref,
                  *, ff_chunk):
    x1, h = _mix_residual(x_ref, att_ref, ret_ref, ssd_ref, mod_ref, n2_ref, wo_ref)
    hb = h.astype(BF16)
    ff = wg_ref.shape[1]
    f = jnp.zeros_like(x1)
    for s in range(0, ff, ff_chunk):
        g = _dot(hb, wg_ref[:, s:s + ff_chunk])
        u = _dot(hb, wu_ref[:, s:s + ff_chunk])
        f = f + _dot((_silu(g) * u).astype(BF16), wd_ref[s:s + ff_chunk, :])
    o_ref[0] = x1 + mod_ref[0, 5:6, :] * f


def _dense_block(xs, att, ret, ssd, mods, lw, tm):
    b, t, d = xs.shape
    ff = lw["wg"].shape[1]
    tok = lambda w: pl.BlockSpec((1, tm, w), lambda i, j: (i, j, 0))
    const = lambda shape: pl.BlockSpec(shape, lambda i, j: (0,) * len(shape), pipeline_mode=pl.Buffered(1))
    return pl.pallas_call(
        functools.partial(_dense_kernel, ff_chunk=ff // 2),
        grid=(b, t // tm),
        in_specs=[tok(d), tok(att.shape[-1]), tok(ret.shape[-1]), tok(ssd.shape[-1]),
                  pl.BlockSpec((1, 8, d), lambda i, j: (i, 0, 0)), const((1, d)),
                  const((d, d)), const((d, ff)), const((d, ff)), const((ff, d))],
        out_specs=tok(d),
        out_shape=jax.ShapeDtypeStruct((b, t, d), F32),
        compiler_params=_cparams(("parallel", "parallel")),
        name="outproj_ffn",
    )(xs, att, ret, ssd, mods, lw["n2"], lw["wo"], lw["wg"], lw["wu"], lw["wd"])


SUBLANES = 8


def _route_kernel(x_ref, att_ref, ret_ref, ssd_ref, mod_ref, n2_ref, wo_ref, wr_ref,
                  x1_ref, h_ref, slot_ref, gate_ref, cnt_ref):
    x1, h = _mix_residual(x_ref, att_ref, ret_ref, ssd_ref, mod_ref, n2_ref, wo_ref)
    x1_ref[0] = x1
    tm = h.shape[0]
    h_hi = h.astype(BF16)
    h_ref[0] = h_hi
    h_lo = (h - h_hi.astype(F32)).astype(BF16)
    logits = _dot(h_hi, wr_ref[0]) + _dot(h_lo, wr_ref[0]) + _dot(h_hi, wr_ref[1])
    lane = _iota(logits.shape, 1).astype(F32)
    logits = jnp.where(lane < N_EXPERTS, logits, NEG)
    m1 = jnp.max(logits, axis=-1, keepdims=True)
    i1 = jnp.min(jnp.where(logits == m1, lane, float(LANES)), axis=-1, keepdims=True)
    rest = jnp.where(lane == i1, NEG, logits)
    m2 = jnp.max(rest, axis=-1, keepdims=True)
    i2 = jnp.min(jnp.where(rest == m2, lane, float(LANES)), axis=-1, keepdims=True)
    e2 = jnp.exp(m2 - m1)
    den = 1.0 + e2
    sel1, sel2 = lane == i1, lane == i2
    chosen = jnp.where(sel1 | sel2, 1.0, 0.0)
    before = (_iota((tm, tm), 0) > _iota((tm, tm), 1)).astype(BF16)
    pos = _dot(before, chosen.astype(BF16))
    cnt = jnp.sum(chosen, axis=0, keepdims=True)
    nblk = sum(jnp.where(cnt > float(k * MOE_ROWS), 1.0, 0.0) for k in range(pl.cdiv(tm, MOE_ROWS)))
    lower_e = (_iota((LANES, LANES), 0) < _iota((LANES, LANES), 1)).astype(BF16)
    off = _dot_exact_rhs01(jnp.broadcast_to(nblk * float(MOE_ROWS), (SUBLANES, LANES)), lower_e)[0:1]
    slot = off + pos
    s1 = jnp.sum(jnp.where(sel1, slot, 0.0), axis=-1, keepdims=True)
    s2 = jnp.sum(jnp.where(sel2, slot, 0.0), axis=-1, keepdims=True)
    lane_i = _iota(logits.shape, 1)
    slots = jnp.where(lane_i == 0, s1, jnp.where(lane_i == 1, s2, -1.0))
    slot_ref[0, 0] = slots.T[0:SUBLANES].astype(jnp.int32)
    gate_ref[0] = jnp.where(sel1, 1.0 / den, 0.0) + jnp.where(sel2, e2 / den, 0.0)
    row = _iota((SUBLANES, LANES), 0)
    cnt_ref[0, 0] = jnp.where(row == 0, cnt, jnp.where(row == 1, off, jnp.where(row == 2, nblk, 0.0))
                              ).astype(jnp.int32)


def _route_block(xs, att, ret, ssd, mods, lw, tm):
    b, t, d = xs.shape
    nt = t // tm
    tok = lambda w: pl.BlockSpec((1, tm, w), lambda i, j: (i, j, 0))
    const = lambda shape: pl.BlockSpec(shape, lambda i, j: (0,) * len(shape))
    per_tile = lambda r, w: pl.BlockSpec((1, 1, r, w), lambda i, j: (i, j, 0, 0))
    return pl.pallas_call(
        _route_kernel,
        grid=(b, nt),
        in_specs=[tok(d), tok(att.shape[-1]), tok(ret.shape[-1]), tok(ssd.shape[-1]),
                  pl.BlockSpec((1, 8, d), lambda i, j: (i, 0, 0)), const((1, d)),
                  const((d, d)), const((2, d, LANES))],
        out_specs=[tok(d), tok(d), per_tile(SUBLANES, tm), tok(LANES), per_tile(SUBLANES, LANES)],
        out_shape=[jax.ShapeDtypeStruct((b, t, d), F32), jax.ShapeDtypeStruct((b, t, d), BF16),
                   jax.ShapeDtypeStruct((b, nt, SUBLANES, tm), jnp.int32),
                   jax.ShapeDtypeStruct((b, t, LANES), F32),
                   jax.ShapeDtypeStruct((b, nt, SUBLANES, LANES), jnp.int32)],
        compiler_params=_cparams(("parallel", "parallel")),
        name="outproj_route",
    )(xs, att, ret, ssd, mods, lw["n2"], lw["wo"], lw["router"])


MOE_ROWS = 256


def _moe_kernel(cnt_ref, slot_ref, h_ref, gate_ref, x1_ref, mod_ref, wgu_ref, wd_ref, fn_ref, o_ref,
                *, final_norm, ff_split):
    e = pl.program_id(2)
    rb = MOE_ROWS
    tm = h_ref.shape[1]

    @pl.when(e == 0)
    def _():
        o_ref[...] = jnp.zeros_like(o_ref)

    off = cnt_ref[0, 0, 1, e]
    nblocks = cnt_ref[0, 0, 2, e]
    slot1 = slot_ref[0, 0, 0:1, :]
    slot2 = slot_ref[0, 0, 1:2, :]
    gates = gate_ref[0]
    ge = jnp.sum(jnp.where(_iota(gates.shape, 1) == e, gates, 0.0), axis=-1, keepdims=True)

    def block(bi, carry):
        rows = off + bi * rb + _iota((rb, tm), 0)
        sel = jnp.where((rows == slot1) | (rows == slot2), 1.0, 0.0).astype(BF16)
        lhs = _dot(sel, h_ref[0]).astype(BF16)
        y = None
        for s0, s1 in ff_split:
            w = s1 - s0
            gu = _dot(lhs, wgu_ref[0, :, 2 * s0:2 * s1])
            part = _dot((_silu(gu[:, :w]) * gu[:, w:]).astype(BF16), wd_ref[0, s0:s1, :])
            y = part if y is None else y + part
        o_ref[0] += ge * _dot_tn(sel, y.astype(BF16))
        return carry

    lax.fori_loop(0, nblocks, block, 0)

    @pl.when(e == N_EXPERTS - 1)
    def _():
        out = x1_ref[0] + mod_ref[0, 5:6, :] * o_ref[0]
        o_ref[0] = _rms(out, fn_ref[...]) if final_norm else out


MOE_FF_SLICE = 512


def _moe_ff_split(ff):
    return tuple((s, min(s + MOE_FF_SLICE, ff)) for s in range(0, ff, MOE_FF_SLICE))


def _moe_weights(w_gate, w_up, w_down):
    ff = w_gate.shape[-1]
    wgu = jnp.concatenate([w[..., s0:s1] for s0, s1 in _moe_ff_split(ff) for w in (w_gate, w_up)], axis=-1)
    return wgu.astype(BF16), w_down.astype(BF16)


def _moe_block(x1, h, slots, gates, counts, mods, lw, fnw, final_norm, tm):
    b, t, d = x1.shape
    ff = lw["moe_wd"].shape[1]
    assert MOE_ROWS % (2 * SUBLANES) == 0
    tok = lambda w: pl.BlockSpec((1, tm, w), lambda i, j, e: (i, j, 0))
    per_tile = lambda w, **kw: pl.BlockSpec((1, 1, SUBLANES, w), lambda i, j, e: (i, j, 0, 0), **kw)
    return pl.pallas_call(
        functools.partial(_moe_kernel, final_norm=final_norm, ff_split=_moe_ff_split(ff)),
        grid=(b, t // tm, N_EXPERTS),
        in_specs=[per_tile(LANES, memory_space=pltpu.SMEM), per_tile(tm), tok(d), tok(LANES), tok(d),
                  pl.BlockSpec((1, 8, d), lambda i, j, e: (i, 0, 0)),
                  pl.BlockSpec((1, d, 2 * ff), lambda i, j, e: (e, 0, 0)),
                  pl.BlockSpec((1, ff, d), lambda i, j, e: (e, 0, 0)),
                  pl.BlockSpec((1, d), lambda i, j, e: (0, 0))],
        out_specs=tok(d),
        out_shape=jax.ShapeDtypeStruct((b, t, d), F32),
        compiler_params=_cparams(("parallel", "parallel", "arbitrary")),
        name="moe",
    )(counts, slots, h, gates, x1, mods, lw["moe_wgu"], lw["moe_wd"], fnw)


def _final_norm_kernel(x_ref, w_ref, o_ref):
    o_ref[0] = _rms(x_ref[0], w_ref[...])


def _final_norm(xs, fnw, tm):
    b, t, d = xs.shape
    return pl.pallas_call(
        _final_norm_kernel,
        grid=(b, t // tm),
        in_specs=[pl.BlockSpec((1, tm, d), lambda i, j: (i, j, 0)), pl.BlockSpec((1, d), lambda i, j: (0, 0))],
        out_specs=pl.BlockSpec((1, tm, d), lambda i, j: (i, j, 0)),
        out_shape=jax.ShapeDtypeStruct((b, t, d), F32),
        compiler_params=_cparams(("parallel", "parallel")),
        name="final_norm",
    )(xs, fnw)


def _rot_cols(w):
    a, b2, c, d = jnp.split(w, 4, axis=-1)
    return jnp.concatenate([-b2, a, -d, c], axis=-1)


def _layer_weights(i, norm1_w, norm2_w, w_in, w_out, mla_q_norm_w, mla_w_qb, mla_kv_norm_w, mla_w_kb, mla_w_vb,
                   ret_decay_logit, ssd_conv_w, ssd_conv_b, ssd_dt_bias, ssd_a_log, ssd_d, ssd_norm_w):
    d = w_in.shape[1]
    offs = np.cumsum((0,) + IN_SIZES)
    seg = [w_in[i][:, offs[j]:offs[j + 1]] for j in range(len(IN_SIZES))]
    zeros = lambda n: jnp.zeros((d, n), F32)
    w_all = jnp.concatenate(
        [seg[0], seg[1], seg[2], _rot_cols(seg[2]), zeros(W_MLA - MLA_Q_LORA - MLA_KV_LORA - 2 * MLA_ROPE),
         seg[3] * (RET_QK ** -0.5), seg[4], seg[5], seg[6], seg[7], seg[8], seg[9], seg[10],
         zeros(W_DT - 2 * SSD_HEADS)], axis=1).astype(BF16)

    dh = MLA_NOPE + MLA_ROPE
    wqb = mla_w_qb[i].reshape(MLA_Q_LORA, MLA_HEADS, dh)
    pad = lambda a, lo, hi: jnp.pad(a, ((0, 0), (0, 0), (lo, hi)))
    wq = pad(wqb, 0, MLA_HP - dh).reshape(MLA_Q_LORA, MLA_HEADS * MLA_HP).astype(BF16)
    wqr = pad(_rot_cols(wqb[:, :, MLA_NOPE:]), MLA_NOPE, MLA_HP - dh).reshape(MLA_Q_LORA, MLA_HEADS * MLA_HP).astype(BF16)
    wkb = pad(mla_w_kb[i].reshape(MLA_KV_LORA, MLA_HEADS, MLA_NOPE), 0, MLA_HP - MLA_NOPE)
    place = np.zeros((MLA_ROPE, MLA_HEADS, MLA_HP), np.float32)
    for r in range(MLA_ROPE):
        place[r, :, MLA_NOPE + r] = 1.0
    place = jnp.asarray(place)
    wk = jnp.concatenate([wkb, place, place, jnp.zeros((MLA_KV_LORA - 2 * MLA_ROPE, MLA_HEADS, MLA_HP), F32)],
                         axis=0).reshape(2 * MLA_KV_LORA, MLA_HEADS * MLA_HP).astype(BF16)

    wvb = mla_w_vb[i].reshape(MLA_KV_LORA, MLA_HEADS, MLA_V)
    wv = jnp.concatenate(
        [pad(wvb[:, hh:hh + 1], (hh % 2) * MLA_V, MLA_HP - MLA_V - (hh % 2) * MLA_V) for hh in range(MLA_HEADS)],
        axis=1).reshape(MLA_KV_LORA, MLA_HEADS * MLA_HP).astype(BF16)

    lg = ret_decay_logit[i]
    padl = lambda v: jnp.pad(v.reshape(1, -1), ((0, 0), (0, LANES - 2 * SSD_HEADS)))
    return {
        "n1": norm1_w[i].reshape(1, d), "n2": norm2_w[i].reshape(1, d),
        "w_all": w_all, "qn": mla_q_norm_w[i].reshape(1, -1), "wq": wq, "wqr": wqr,
        "kvn": mla_kv_norm_w[i].reshape(1, -1), "wk": wk, "wv": wv,
        "wo": w_out[i].astype(BF16),
        "ret_lgl": jnp.repeat(lg, RET_V, axis=1),
        "ret_lgh": jnp.broadcast_to(lg.reshape(2 * RET_HEADS, 1), (2 * RET_HEADS, CHUNK)),
        "conv_w": jnp.pad(ssd_conv_w[i], ((0, 8 - ssd_conv_w.shape[1]), (0, 0))),
        "conv_b": ssd_conv_b[i].reshape(1, -1),
        "dt_bias": padl(ssd_dt_bias[i]), "a_log": padl(ssd_a_log[i]),
        "d_vec": jnp.repeat(ssd_d[i], SSD_HEADDIM).reshape(1, -1),
        "ssd_nw": ssd_norm_w[i].reshape(1, -1),
    }


def _rope_tables(n_lat, n_ctx):
    rows = n_lat // GRID_W
    row = jnp.repeat(jnp.arange(rows, dtype=F32), GRID_W)
    col = jnp.tile(jnp.arange(GRID_W, dtype=F32), rows)
    inv_freq = ROPE_BASE ** (-jnp.arange(0, ROPE_AXIS_DIM, 2, dtype=F32) / ROPE_AXIS_DIM)
    ang_r = row[:, None] * inv_freq
    ang_c = col[:, None] * inv_freq
    ang = jnp.concatenate([ang_r, ang_r, ang_c, ang_c], axis=-1)
    c2 = MLA_SCALE * math.log2(math.e)

    def build(cos, sin):
        n = cos.shape[0]
        cosq = jnp.concatenate([jnp.full((n, MLA_NOPE), c2, F32), c2 * cos, jnp.zeros((n, MLA_HP - MLA_NOPE - MLA_ROPE), F32)], 1)
        sinq = jnp.concatenate([jnp.zeros((n, MLA_NOPE), F32), c2 * sin, jnp.zeros((n, MLA_HP - MLA_NOPE - MLA_ROPE), F32)], 1)
        ck = jnp.concatenate([cos, sin, jnp.zeros((n, LANES - 2 * MLA_ROPE), F32)], 1)
        return cosq, sinq, ck

    lat = build(jnp.cos(ang), jnp.sin(ang))
    ctx = build(jnp.ones((n_ctx, MLA_ROPE), F32), jnp.zeros((n_ctx, MLA_ROPE), F32))
    return lat, ctx


def _token_tile(t, pref):
    tm = min(pref, t)
    while t % tm:
        tm //= 2
    return tm


def kernel(x, c, ctx, c_ctx, mod_w, mod_b, norm1_w, norm2_w, w_in, w_out, mla_q_norm_w, mla_w_qb, mla_kv_norm_w, mla_w_kb, mla_w_vb, ret_decay_logit, ssd_conv_w, ssd_conv_b, ssd_dt_bias, ssd_a_log, ssd_d, ssd_norm_w, ffn_w_gate, ffn_w_up, ffn_w_down, moe_router, moe_w_gate, moe_w_up, moe_w_down, final_norm_w):
    b, n_lat, d = x.shape
    n_ctx = ctx.shape[1]
    depth = mod_w.shape[0]
    assert b + 1 <= 8 and n_lat % CHUNK == 0 and n_ctx % CHUNK == 0 and n_lat % GRID_W == 0

    cond = jnp.concatenate([c, c_ctx[None, :], jnp.zeros((8 - b - 1, d), F32)], axis=0)
    mod_all = _modulation(cond, mod_w, mod_b)
    tabs_x, tabs_c = _rope_tables(n_lat, n_ctx)
    fnw = final_norm_w.reshape(1, d)

    tm_x = _token_tile(n_lat, 512)
    tm_c = _token_tile(n_ctx, 256)
    tm_moe = _token_tile(n_lat, 1024)
    tq_x = _token_tile(n_lat, 1024)
    tk_x = 2048
    zero_ret = jnp.zeros((b, 2, RET_QK, RET_HEADS * RET_V), F32)
    zero_ssd = jnp.zeros((b, 2, SSD_STATE, SSD_INNER), F32)

    h_ctx = ctx
    for i in range(depth):
        last = i == depth - 1
        lw = _layer_weights(i, norm1_w, norm2_w, w_in, w_out, mla_q_norm_w, mla_w_qb, mla_kv_norm_w, mla_w_kb,
                            mla_w_vb, ret_decay_logit, ssd_conv_w, ssd_conv_b, ssd_dt_bias, ssd_a_log, ssd_d,
                            ssd_norm_w)
        m6 = mod_all[i].reshape(8, 6, d)
        pad2 = jnp.zeros((b, 2, d), F32)
        mods_x = jnp.concatenate([m6[:b], pad2], axis=1)
        mods_c = jnp.concatenate([jnp.broadcast_to(m6[b:b + 1], (b, 6, d)), pad2], axis=1)

        qc, kc, vc, retc, zc, xbcc, dtc = _inproj(h_ctx, mods_c, lw, tabs_c, tm_c)
        ret_yc, ret_fin = _ret_scan(retc, lw, zero_ret)
        ssd_yc, ssd_fin = _ssd_scan(xbcc, zc, dtc, lw, zero_ssd)

        qx, kx, vx, retx, zx, xbcx, dtx = _inproj(x, mods_x, lw, tabs_x, tm_x)
        att_x = _attention(qx, [(kc, vc), (kx, vx)], tq_x, tk_x)
        ret_yx, _ = _ret_scan(retx, lw, ret_fin)
        ssd_yx, _ = _ssd_scan(xbcx, zx, dtx, lw, ssd_fin)

        if i % 2 == 0:
            j = i // 2
            lw.update(wg=ffn_w_gate[j].astype(BF16), wu=ffn_w_up[j].astype(BF16), wd=ffn_w_down[j].astype(BF16))
            x = _dense_block(x, att_x, ret_yx, ssd_yx, mods_x, lw, tm_x)
            if last:
                x = _final_norm(x, fnw, tm_x)
        else:
            j = i // 2
            wgu, wd = _moe_weights(moe_w_gate[j], moe_w_up[j], moe_w_down[j])
            r_f32 = jnp.pad(moe_router[j], ((0, 0), (0, LANES - N_EXPERTS)))
            r_hi = r_f32.astype(BF16)
            r_lo = (r_f32 - r_hi.astype(F32)).astype(BF16)
            lw.update(router=jnp.stack([r_hi, r_lo]), moe_wgu=wgu, moe_wd=wd)
            x1, *routed = _route_block(x, att_x, ret_yx, ssd_yx, mods_x, lw, tm_moe)
            x = _moe_block(x1, *routed, mods_x, lw, fnw, last, tm_moe)

        if not last:
            att_c = _attention(qc, [(kc, vc)], tm_c, tk_x)
            if i % 2 == 0:
                h_ctx = _dense_block(h_ctx, att_c, ret_yc, ssd_yc, mods_c, lw, tm_c)
            else:
                c1, *routed = _route_block(h_ctx, att_c, ret_yc, ssd_yc, mods_c, lw, tm_c)
                h_ctx = _moe_block(c1, *routed, mods_c, lw, fnw, False, tm_c)
    return x
```

```python
import functools
import math

import jax
import jax.numpy as jnp
import numpy as np
from jax import lax
from jax.experimental import pallas as pl
from jax.experimental.pallas import tpu as pltpu

F32 = jnp.float32
BF16 = jnp.bfloat16
HIGHEST = lax.Precision.HIGHEST

EPS = 1e-6
CHUNK = 128
GRID_W = 64
MLA_HEADS = 4
MLA_Q_LORA = 256
MLA_KV_LORA = 128
MLA_NOPE = 64
MLA_ROPE = 32
MLA_V = 64
MLA_SCALE = (MLA_NOPE + MLA_ROPE) ** -0.5
ROPE_AXIS_DIM = MLA_ROPE // 2
ROPE_BASE = 10000.0
RET_HEADS = 4
RET_QK = 64
RET_V = 64
SSD_HEADS = 8
SSD_HEADDIM = 64
SSD_GROUPS = 2
SSD_STATE = 64
SSD_INNER = SSD_HEADS * SSD_HEADDIM
SSD_CONV_CH = SSD_INNER + 2 * SSD_GROUPS * SSD_STATE
N_EXPERTS = 8
IN_SIZES = (MLA_Q_LORA, MLA_KV_LORA, MLA_ROPE,
            RET_HEADS * RET_QK, RET_HEADS * RET_QK, RET_HEADS * RET_V, RET_HEADS * RET_V,
            SSD_INNER, SSD_CONV_CH, SSD_HEADS, SSD_HEADS)

LANES = 128
V7X_VMEM_LIMIT = 56 * 1024 * 1024

MLA_HP = 128
NEG = -1e30


def _cparams(sem):
    return pltpu.CompilerParams(dimension_semantics=sem, vmem_limit_bytes=V7X_VMEM_LIMIT)


def _dot(a, b):
    return jnp.dot(a, b, preferred_element_type=F32)


def _dot_nt(a, b):
    return lax.dot_general(a, b, (((1,), (1,)), ((), ())), preferred_element_type=F32)


def _dot_tn(a, b):
    return lax.dot_general(a, b, (((0,), (0,)), ((), ())), preferred_element_type=F32)


def _split3(x):
    hi = x.astype(BF16)
    r1 = x - hi.astype(F32)
    mid = r1.astype(BF16)
    lo = (r1 - mid.astype(F32)).astype(BF16)
    return hi, mid, lo


def _dot_exact_rhs01(x, e):
    hi, mid, lo = _split3(x)
    return _dot(hi, e) + _dot(mid, e) + _dot(lo, e)


def _dot_exact_lhs01(e, x):
    hi, mid, lo = _split3(x)
    return _dot(e, hi) + _dot(e, mid) + _dot(e, lo)


def _silu(x):
    return x / (1.0 + jnp.exp(-x))


def _rms(x, w):
    return x * lax.rsqrt(jnp.mean(x * x, axis=-1, keepdims=True) + EPS) * w


def _iota(shape, dim):
    return lax.broadcasted_iota(jnp.int32, shape, dim)


def _mod_kernel(a_ref, w_ref, b_ref, o_ref):
    a = _silu(a_ref[...])
    o_ref[0] = jnp.dot(a, w_ref[0], precision=HIGHEST, preferred_element_type=F32) + b_ref[0]


def _modulation(cond, mod_w, mod_b):
    depth, d, n = mod_w.shape
    tn = 1536
    return pl.pallas_call(
        _mod_kernel,
        grid=(depth, n // tn),
        in_specs=[pl.BlockSpec((8, d), lambda i, j: (0, 0)),
                  pl.BlockSpec((1, d, tn), lambda i, j: (i, 0, j)),
                  pl.BlockSpec((1, 1, tn), lambda i, j: (i, 0, j))],
        out_specs=pl.BlockSpec((1, 8, tn), lambda i, j: (i, 0, j)),
        out_shape=jax.ShapeDtypeStruct((depth, 8, n), F32),
        compiler_params=_cparams(("parallel", "parallel")),
        name="modulation",
    )(cond, mod_w, mod_b.reshape(depth, 1, n))


W_MLA, W_RET, W_Z, W_XBC, W_DT = 512, 1024, 512, 768, 128
W_ALL = W_MLA + W_RET + W_Z + W_XBC + W_DT


def _ones_lane_mask(shape):
    lane = _iota(shape, len(shape) - 1) % (2 * MLA_HP)
    return (lane == MLA_HP - 1) | (lane == MLA_HP)


def _inproj_kernel(x_ref, mod_ref, n1_ref, win_ref, qn_ref, wq_ref, wqr_ref, kvn_ref, wk_ref, wv_ref,
                   cq_ref, sq_ref, ck_ref,
                   q_out, k_out, v_out, ret_out, z_out, xbc_out, dt_out):
    x = x_ref[0]
    h = _rms(x, n1_ref[...]) * (1.0 + mod_ref[0, 1:2, :]) + mod_ref[0, 0:1, :]
    hb = h.astype(BF16)

    o = 0
    pm = _dot(hb, win_ref[:, o:o + W_MLA]); o += W_MLA
    ret_out[0] = _dot(hb, win_ref[:, o:o + W_RET]).astype(BF16); o += W_RET
    z_out[0] = _dot(hb, win_ref[:, o:o + W_Z]).astype(BF16); o += W_Z
    xbc_out[0] = _dot(hb, win_ref[:, o:o + W_XBC]).astype(BF16); o += W_XBC
    dt_out[0] = _dot(hb, win_ref[:, o:o + W_DT])

    cqb = _rms(pm[:, 0:MLA_Q_LORA], qn_ref[...]).astype(BF16)
    qm = _dot(cqb, wq_ref[...])
    qr = _dot(cqb, wqr_ref[...])
    cosq, sinq = cq_ref[...], sq_ref[...]
    for hh in range(MLA_HEADS):
        sl = slice(hh * MLA_HP, (hh + 1) * MLA_HP)
        q_out[0, :, sl] = (qm[:, sl] * cosq + qr[:, sl] * sinq).astype(BF16)

    ckvn = _rms(pm[:, MLA_Q_LORA:MLA_Q_LORA + MLA_KV_LORA], kvn_ref[...])
    u = pm[:, MLA_Q_LORA + MLA_KV_LORA:W_MLA] * ck_ref[...]
    kin = jnp.concatenate([ckvn, u], axis=-1).astype(BF16)
    k_out[0] = _dot(kin, wk_ref[...]).astype(BF16)
    v = _dot(ckvn.astype(BF16), wv_ref[...])
    v_out[0] = jnp.where(_ones_lane_mask(v.shape), 1.0, v).astype(BF16)


def _inproj(xs, mods, lw, tabs, tm):
    b, t, d = xs.shape
    const = lambda shape: pl.BlockSpec(shape, lambda i, j: (0,) * len(shape))
    tok = lambda w: pl.BlockSpec((1, tm, w), lambda i, j: (i, j, 0))
    tab = pl.BlockSpec((tm, LANES), lambda i, j: (j, 0))
    widths = (MLA_HEADS * MLA_HP, MLA_HEADS * MLA_HP, MLA_HEADS * MLA_HP, W_RET, W_Z, W_XBC, W_DT)
    dtypes = (BF16, BF16, BF16, BF16, BF16, BF16, F32)
    return pl.pallas_call(
        _inproj_kernel,
        grid=(b, t // tm),
        in_specs=[tok(d), pl.BlockSpec((1, 8, d), lambda i, j: (i, 0, 0)), const((1, d)),
                  const((d, W_ALL)), const((1, MLA_Q_LORA)), const((MLA_Q_LORA, MLA_HEADS * MLA_HP)),
                  const((MLA_Q_LORA, MLA_HEADS * MLA_HP)), const((1, MLA_KV_LORA)),
                  const((2 * MLA_KV_LORA, MLA_HEADS * MLA_HP)), const((MLA_KV_LORA, MLA_HEADS * MLA_HP)),
                  tab, tab, tab],
        out_specs=[tok(w) for w in widths],
        out_shape=[jax.ShapeDtypeStruct((b, t, w), dt) for w, dt in zip(widths, dtypes)],
        compiler_params=_cparams(("parallel", "parallel")),
        name="inproj",
    )(xs, mods, lw["n1"], lw["w_all"], lw["qn"], lw["wq"], lw["wqr"], lw["kvn"], lw["wk"], lw["wv"], *tabs)


def _attn_kernel(*refs, seg_blocks, tq):
    nseg = len(seg_blocks)
    q_ref = refs[0]
    kv_refs = refs[1:1 + 2 * nseg]
    o_ref = refs[1 + 2 * nseg]
    carry = tuple((jnp.full((tq, 1), NEG, F32), jnp.zeros((tq, MLA_HP), F32)) for _ in range(MLA_HEADS))
    for si, (tk, nblk) in enumerate(seg_blocks):
        k_ref, v_ref = kv_refs[2 * si], kv_refs[2 * si + 1]

        def body(j, carry, k_ref=k_ref, v_ref=v_ref, tk=tk):
            start = 0 if isinstance(j, int) else pl.multiple_of(j * tk, tk)
            new = []
            for hh in range(MLA_HEADS):
                sl = slice(hh * MLA_HP, (hh + 1) * MLA_HP)
                m, acc = carry[hh]
                s = _dot_nt(q_ref[0, :, sl], k_ref[0, pl.ds(start, tk), sl])
                m_new = jnp.maximum(m, jnp.max(s, axis=-1, keepdims=True))
                p = jnp.exp2(s - m_new).astype(BF16)
                acc = jnp.exp2(m - m_new) * acc + _dot(p, v_ref[0, pl.ds(start, tk), sl])
                new.append((m_new, acc))
            return tuple(new)

        carry = body(0, carry) if nblk == 1 else lax.fori_loop(0, nblk, body, carry)
    lane = _iota((tq, MLA_HP), 1)
    pairs = []
    for hh in range(0, MLA_HEADS, 2):
        acc_e, acc_o = carry[hh][1], carry[hh + 1][1]
        l_e = jnp.sum(jnp.where(lane == MLA_HP - 1, acc_e, 0.0), axis=-1, keepdims=True)
        l_o = jnp.sum(jnp.where(lane == 0, acc_o, 0.0), axis=-1, keepdims=True)
        pairs.append(jnp.where(lane < MLA_V, acc_e / l_e, acc_o / l_o))
    o_ref[0] = jnp.concatenate(pairs, axis=-1).astype(BF16)


def _attention(q, segs, tq, tk):
    b, l, _ = q.shape
    seg_blocks = []
    in_specs = [pl.BlockSpec((1, tq, MLA_HEADS * MLA_HP), lambda i, j: (i, j, 0))]
    args = [q]
    for k, v in segs:
        s = k.shape[1]
        t = min(tk, s)
        seg_blocks.append((t, s // t))
        for _ in range(2):
            in_specs.append(pl.BlockSpec((1, s, MLA_HEADS * MLA_HP), lambda i, j: (i, 0, 0),
                                         pipeline_mode=pl.Buffered(1)))
        args += [k, v]
    return pl.pallas_call(
        functools.partial(_attn_kernel, seg_blocks=tuple(seg_blocks), tq=tq),
        grid=(b, l // tq),
        in_specs=in_specs,
        out_specs=pl.BlockSpec((1, tq, MLA_HEADS * MLA_V), lambda i, j: (i, j, 0)),
        out_shape=jax.ShapeDtypeStruct((b, l, MLA_HEADS * MLA_V), BF16),
        compiler_params=_cparams(("parallel", "arbitrary")),
        name="attention",
    )(*args)


def _ret_kernel(x_ref, lgl_ref, lgh_ref, init_ref, y_ref, fin_ref, s_ref, g_ref, gbuf_ref, *, nblk, sub):
    C, H, N, P = CHUNK, RET_HEADS, RET_QK, RET_V
    HP = H * P
    ph, c = pl.program_id(1), pl.program_id(2)
    blk = jnp.where(ph == 0, nblk - 1 - c, c)

    def log_sigmoid(v):
        return jnp.minimum(v, 0.0) - jnp.log(1.0 + jnp.exp(-jnp.abs(v)))

    lgl = log_sigmoid(lgl_ref[...])
    lgf, lgb = lgl[0:1], lgl[1:2]
    rows = _iota((C, HP), 0).astype(F32)
    head_n = _iota((N, HP), 1) // P

    def diag_blocks(r):
        out = r[0:N]
        for hh in range(1, H):
            out = jnp.where(head_n == hh, r[hh * N:(hh + 1) * N], out)
        return out

    def as_rows(s_c):
        return jnp.concatenate([jnp.where(head_n == hh, s_c, 0.0) for hh in range(H)], axis=0)

    @pl.when((ph == 0) & (c == 0))
    def _():
        g_ref[...] = init_ref[0, 1]

    @pl.when(ph == 0)
    def _():
        w_b = jnp.exp(rows * lgb)
        tot_b = jnp.exp(C * lgb)
        for sb in reversed(range(sub)):
            r0 = sb * C
            k = x_ref[0, r0:r0 + C, HP:2 * HP].astype(F32)
            v = x_ref[0, r0:r0 + C, 2 * HP:3 * HP]
            g = g_ref[...]
            gbuf_ref[blk * sub + sb] = g
            g_ref[...] = g * tot_b + diag_blocks(_dot_tn((k * w_b).astype(BF16), v))

        @pl.when(c == nblk - 1)
        def _():
            fin_ref[0, 1] = g_ref[...]

    @pl.when((ph == 1) & (c == 0))
    def _():
        s_ref[...] = init_ref[0, 0]

    @pl.when(ph == 1)
    def _():
        lgh = log_sigmoid(lgh_ref[...])
        grp = _iota((C, HP), 1) // P
        di = (_iota((C, C), 0) - _iota((C, C), 1)).astype(F32)
        decay = [jnp.exp(jnp.where(di >= 0, di * lgh[hh:hh + 1, :], -di * lgh[H + hh:H + hh + 1, :]))
                 for hh in range(H)]
        e_f = jnp.exp((rows + 1.0) * lgf)
        e_b = jnp.exp((C - rows) * lgb)
        w_f = jnp.exp((C - 1.0 - rows) * lgf)
        tot_f = jnp.exp(C * lgf)
        for sb in range(sub):
            r0 = sb * C
            q = x_ref[0, r0:r0 + C, 0:HP]
            kb = x_ref[0, r0:r0 + C, HP:2 * HP]
            v = x_ref[0, r0:r0 + C, 2 * HP:3 * HP]
            gate = x_ref[0, r0:r0 + C, 3 * HP:4 * HP].astype(F32)
            ws, vs = [], []
            for hh in range(H):
                sc = _dot_nt(jnp.where(grp == hh, q, jnp.zeros_like(q)), kb)
                ws.append((sc * decay[hh]).astype(BF16))
                vs.append(jnp.where(grp == hh, v, jnp.zeros_like(v)))
            y = _dot(jnp.concatenate(ws, axis=1), jnp.concatenate(vs, axis=0))
            qf32 = q.astype(F32)
            s = s_ref[...]
            g_in = gbuf_ref[blk * sub + sb]
            y = y + _dot(jnp.concatenate([(qf32 * e_f).astype(BF16), (qf32 * e_b).astype(BF16)], axis=1),
                         jnp.concatenate([as_rows(s), as_rows(g_in)], axis=0).astype(BF16))
            s_ref[...] = s * tot_f + diag_blocks(_dot_tn((kb.astype(F32) * w_f).astype(BF16), v))

            mu = jnp.zeros_like(y)
            for hh in range(H):
                mh = jnp.sum(jnp.where(grp == hh, y, 0.0), axis=-1, keepdims=True) * (1.0 / P)
                mu = jnp.where(grp == hh, mh, mu)
            yc = y - mu
            var = jnp.zeros_like(y)
            for hh in range(H):
                vh = jnp.sum(jnp.where(grp == hh, yc * yc, 0.0), axis=-1, keepdims=True) * (1.0 / P)
                var = jnp.where(grp == hh, vh, var)
            y_ref[0, r0:r0 + C, :] = (yc * lax.rsqrt(var + EPS) * _silu(gate)).astype(BF16)

        @pl.when(c == nblk - 1)
        def _():
            fin_ref[0, 0] = s_ref[...]


def _block_index(nblk):
    return lambda i, ph, c: (i, (1 - ph) * (nblk - 1 - c) + ph * c, 0)


def _scan_sub(t):
    nc = t // CHUNK
    return 4 if nc % 4 == 0 else (2 if nc % 2 == 0 else 1)


def _ret_scan(ret, lw, init):
    b, t, _ = ret.shape
    sub = _scan_sub(t)
    rows = sub * CHUNK
    nblk = t // rows
    hp = RET_HEADS * RET_V
    return pl.pallas_call(
        functools.partial(_ret_kernel, nblk=nblk, sub=sub),
        grid=(b, 2, nblk),
        in_specs=[pl.BlockSpec((1, rows, W_RET), _block_index(nblk)),
                  pl.BlockSpec((2, hp), lambda i, ph, c: (0, 0)),
                  pl.BlockSpec((2 * RET_HEADS, CHUNK), lambda i, ph, c: (0, 0)),
                  pl.BlockSpec((1, 2, RET_QK, hp), lambda i, ph, c: (i, 0, 0, 0))],
        out_specs=[pl.BlockSpec((1, rows, hp), lambda i, ph, c: (i, ph * c, 0)),
                   pl.BlockSpec((1, 2, RET_QK, hp), lambda i, ph, c: (i, 0, 0, 0))],
        out_shape=[jax.ShapeDtypeStruct((b, t, hp), BF16),
                   jax.ShapeDtypeStruct((b, 2, RET_QK, hp), F32)],
        scratch_shapes=[pltpu.VMEM((RET_QK, hp), F32),
                        pltpu.VMEM((RET_QK, hp), F32),
                        pltpu.VMEM((t // CHUNK, RET_QK, hp), F32)],
        compiler_params=_cparams(("parallel", "arbitrary", "arbitrary")),
        name="retention_scan",
    )(ret, lw["ret_lgl"], lw["ret_lgh"], init)


def _ssd_kernel(x_ref, xp_ref, xn_ref, z_ref, dt_ref, cw_ref, cb_ref, dtb_ref, alog_ref, dvec_ref, nw_ref,
                init_ref, y_ref, fin_ref, s_ref, g_ref, gbuf_ref, act_ref, *, nblk, sub):
    C, H, N, P, G = CHUNK, SSD_HEADS, SSD_STATE, SSD_HEADDIM, SSD_GROUPS
    HP = H * P
    HPG = H // G
    HALO = SSD_HALO
    ph, c = pl.program_id(1), pl.program_id(2)
    blk = jnp.where(ph == 0, nblk - 1 - c, c)

    ri = _iota((C, C + 2 * HALO), 0)
    ci = _iota((C, C + 2 * HALO), 1)
    sel_prev = (ci == ri + HALO - 1).astype(BF16)
    sel_next = (ci == ri + HALO + 1).astype(BF16)
    lane = _iota((LANES, HP), 0)
    head_of = _iota((LANES, HP), 1) // P
    e_f = (lane == head_of).astype(BF16)
    e_b = (lane == head_of + H).astype(BF16)
    r_i = _iota((C, C), 0)
    c_i = _iota((C, C), 1)
    low = (r_i >= c_i).astype(BF16)
    upp = (r_i <= c_i).astype(BF16)
    grp_n = _iota((N, HP), 1) // (HPG * P)
    neg_a = -jnp.exp(alog_ref[...])
    cw = cw_ref[...]

    def prep(sb, first_pass):
        r0 = sb * C
        if first_pass:
            xc = x_ref[0, r0:r0 + C, :]
            if sb == 0:
                prev = jnp.where(blk > 0, xp_ref[0], jnp.zeros_like(xp_ref[0]))
            else:
                prev = x_ref[0, r0 - HALO:r0, :]
            if sb == sub - 1:
                nxt = jnp.where(blk < nblk - 1, xn_ref[0], jnp.zeros_like(xn_ref[0]))
            else:
                nxt = x_ref[0, r0 + C:r0 + C + HALO, :]
            stacked = jnp.concatenate([prev, xc, nxt], axis=0)
            conv = (cw[0:1] * _dot(sel_prev, stacked) + cw[1:2] * xc.astype(F32)
                    + cw[2:3] * _dot(sel_next, stacked) + cb_ref[...])
            act = _silu(conv)
            act_ref[blk * sub + sb] = act
        else:
            act = act_ref[blk * sub + sb]
        xs = act[:, 0:HP]
        bm = act[:, HP:HP + G * N].astype(BF16)
        cm = act[:, HP + G * N:HP + 2 * G * N].astype(BF16)
        raw = dt_ref[0, r0:r0 + C, :] + dtb_ref[...]
        dt = jnp.maximum(raw, 0.0) + jnp.log(1.0 + jnp.exp(-jnp.abs(raw)))
        la = dt * neg_a
        a_f = _dot_exact_lhs01(low, la)
        a_b = _dot_exact_lhs01(upp, la)
        return xs, bm, cm, dt, la, a_f, a_b

    def state_update(state, bm, xs, wdt, tot8, e_dir, row):
        xw = (xs * _dot(wdt.astype(BF16), e_dir)).astype(BF16)
        tot = _dot_exact_rhs01(tot8, e_dir)[row:row + 1]
        r = _dot_tn(bm, xw)
        upd = r[0:N]
        for gg in range(1, G):
            upd = jnp.where(grp_n == gg, r[gg * N:(gg + 1) * N], upd)
        return state * tot + upd

    def as_rows(s_c):
        return jnp.concatenate([jnp.where(grp_n == gg, s_c, 0.0) for gg in range(G)], axis=0).astype(BF16)

    @pl.when((ph == 0) & (c == 0))
    def _():
        g_ref[...] = init_ref[0, 1]

    @pl.when(ph == 0)
    def _():
        for sb in reversed(range(sub)):
            xs, bm, _, dt, _, _, a_b = prep(sb, True)
            g = g_ref[...]
            gbuf_ref[blk * sub + sb] = g
            wdt = jnp.exp(a_b[0:1, :] - a_b) * dt
            g_ref[...] = state_update(g, bm, xs, wdt, jnp.exp(a_b[0:8, :]), e_b, 0)

        @pl.when(c == nblk - 1)
        def _():
            fin_ref[0, 1] = g_ref[...]

    @pl.when((ph == 1) & (c == 0))
    def _():
        s_ref[...] = init_ref[0, 0]

    @pl.when(ph == 1)
    def _():
        grp_l = _iota((C, G * N), 1) // N
        head_l = _iota((C, HP), 1) // P
        tri_f = r_i >= c_i
        diag = r_i == c_i
        for sb in range(sub):
            r0 = sb * C
            xs, bm, cm, dt, la, a_f, a_b = prep(sb, False)
            la_t = la.T[0:2 * H]
            dt_t = dt.T[0:2 * H]
            af_row = _dot_exact_rhs01(la_t, upp)
            ab_row = _dot_exact_rhs01(la_t, low)
            sc = [_dot_nt(jnp.where(grp_l == gg, cm, jnp.zeros_like(cm)), bm) for gg in range(G)]
            xsb = xs.astype(BF16)
            ws, vs = [], []
            for hh in range(H):
                seg_f = a_f[:, hh:hh + 1] - af_row[hh:hh + 1, :]
                seg_b = a_b[:, H + hh:H + hh + 1] - ab_row[H + hh:H + hh + 1, :]
                dt_f, dt_b = dt_t[hh:hh + 1, :], dt_t[H + hh:H + hh + 1, :]
                dm = (jnp.exp(jnp.where(tri_f, seg_f, seg_b)) * jnp.where(tri_f, dt_f, dt_b)
                      + jnp.where(diag, dt_b, 0.0))
                ws.append((sc[hh // HPG] * dm).astype(BF16))
                vs.append(jnp.where(head_l == hh, xsb, jnp.zeros_like(xsb)))
            y = _dot(jnp.concatenate(ws, axis=1), jnp.concatenate(vs, axis=0))

            s = s_ref[...]
            g_in = gbuf_ref[blk * sub + sb]
            y = y + _dot(jnp.exp(a_f).astype(BF16), e_f) * _dot(cm, as_rows(s))
            y = y + _dot(jnp.exp(a_b).astype(BF16), e_b) * _dot(cm, as_rows(g_in))

            wdt = jnp.exp(a_f[C - 1:C, :] - a_f) * dt
            s_ref[...] = state_update(s, bm, xs, wdt, jnp.exp(a_f[C - 8:C, :]), e_f, 7)

            y = y + xs * dvec_ref[...]
            y = y * _silu(z_ref[0, r0:r0 + C, :].astype(F32))
            y_ref[0, r0:r0 + C, :] = _rms(y, nw_ref[...]).astype(BF16)

        @pl.when(c == nblk - 1)
        def _():
            fin_ref[0, 0] = s_ref[...]


SSD_HALO = 16


def _ssd_scan(xbc, z, dt, lw, init):
    b, t, _ = xbc.shape
    sub = _scan_sub(t)
    rows = sub * CHUNK
    nblk = t // rows
    hp = SSD_INNER
    per = rows // SSD_HALO
    nhalo = t // SSD_HALO
    bidx = _block_index(nblk)

    def x_idx(i, ph, c):
        return (i, (1 - ph) * (nblk - 1 - c), 0)

    def prev_idx(i, ph, c):
        return (i, jnp.maximum((1 - ph) * (nblk - 1 - c) * per - 1, 0), 0)

    def next_idx(i, ph, c):
        return (i, jnp.minimum(((1 - ph) * (nblk - 1 - c) + 1) * per, nhalo - 1), 0)

    const = lambda shape: pl.BlockSpec(shape, lambda i, ph, c: (0,) * len(shape))
    return pl.pallas_call(
        functools.partial(_ssd_kernel, nblk=nblk, sub=sub),
        grid=(b, 2, nblk),
        in_specs=[pl.BlockSpec((1, rows, W_XBC), x_idx),
                  pl.BlockSpec((1, SSD_HALO, W_XBC), prev_idx),
                  pl.BlockSpec((1, SSD_HALO, W_XBC), next_idx),
                  pl.BlockSpec((1, rows, W_Z), bidx),
                  pl.BlockSpec((1, rows, W_DT), bidx),
                  const((8, W_XBC)), const((1, W_XBC)), const((1, LANES)), const((1, LANES)),
                  const((1, hp)), const((1, hp)),
                  pl.BlockSpec((1, 2, SSD_STATE, hp), lambda i, ph, c: (i, 0, 0, 0))],
        out_specs=[pl.BlockSpec((1, rows, hp), lambda i, ph, c: (i, ph * c, 0)),
                   pl.BlockSpec((1, 2, SSD_STATE, hp), lambda i, ph, c: (i, 0, 0, 0))],
        out_shape=[jax.ShapeDtypeStruct((b, t, hp), BF16),
                   jax.ShapeDtypeStruct((b, 2, SSD_STATE, hp), F32)],
        scratch_shapes=[pltpu.VMEM((SSD_STATE, hp), F32),
                        pltpu.VMEM((SSD_STATE, hp), F32),
                        pltpu.VMEM((t // CHUNK, SSD_STATE, hp), F32),
                        pltpu.VMEM((t // CHUNK, CHUNK, W_XBC), F32)],
        compiler_params=_cparams(("parallel", "arbitrary", "arbitrary")),
        name="ssd_scan",
    )(xbc, xbc, xbc, z, dt, lw["conv_w"], lw["conv_b"], lw["dt_bias"], lw["a_log"], lw["d_vec"], lw["ssd_nw"], init)


def _mix_residual(x_ref, att_ref, ret_ref, ssd_ref, mod_ref, n2_ref, wo_ref):
    mix = jnp.concatenate([att_ref[0], ret_ref[0], ssd_ref[0]], axis=-1)
    x1 = x_ref[0] + mod_ref[0, 2:3, :] * _dot(mix, wo_ref[...])
    h = _rms(x1, n2_ref[...]) * (1.0 + mod_ref[0, 4:5, :]) + mod_ref[0, 3:4, :]
    return x1, h


def _dense_kernel(x_ref, att_ref, ret_ref, ssd_ref, mod_ref, n2_ref, wo_ref, wg_ref, wu_ref, wd_ref, o_ref,
                  *, ff_chunk):
    x1, h = _mix_residual(x_ref, att_ref, ret_ref, ssd_ref, mod_ref, n2_ref, wo_ref)
    hb = h.astype(BF16)
    ff = wg_ref.shape[1]
    f = jnp.zeros_like(x1)
    for s in range(0, ff, ff_chunk):
        g = _dot(hb, wg_ref[:, s:s + ff_chunk])
        u = _dot(hb, wu_ref[:, s:s + ff_chunk])
        f = f + _dot((_silu(g) * u).astype(BF16), wd_ref[s:s + ff_chunk, :])
    o_ref[0] = x1 + mod_ref[0, 5:6, :] * f


def _dense_block(xs, att, ret, ssd, mods, lw, tm):
    b, t, d = xs.shape
    ff = lw["wg"].shape[1]
    tok = lambda w: pl.BlockSpec((1, tm, w), lambda i, j: (i, j, 0))
    const = lambda shape: pl.BlockSpec(shape, lambda i, j: (0,) * len(shape), pipeline_mode=pl.Buffered(1))
    return pl.pallas_call(
        functools.partial(_dense_kernel, ff_chunk=ff // 2),
        grid=(b, t // tm),
        in_specs=[tok(d), tok(att.shape[-1]), tok(ret.shape[-1]), tok(ssd.shape[-1]),
                  pl.BlockSpec((1, 8, d), lambda i, j: (i, 0, 0)), const((1, d)),
                  const((d, d)), const((d, ff)), const((d, ff)), const((ff, d))],
        out_specs=tok(d),
        out_shape=jax.ShapeDtypeStruct((b, t, d), F32),
        compiler_params=_cparams(("parallel", "parallel")),
        name="outproj_ffn",
    )(xs, att, ret, ssd, mods, lw["n2"], lw["wo"], lw["wg"], lw["wu"], lw["wd"])


SUBLANES = 8


def _route_kernel(x_ref, att_ref, ret_ref, ssd_ref, mod_ref, n2_ref, wo_ref, wr_ref,
                  x1_ref, h_ref, slot_ref, gate_ref, cnt_ref):
    x1, h = _mix_residual(x_ref, att_ref, ret_ref, ssd_ref, mod_ref, n2_ref, wo_ref)
    x1_ref[0] = x1
    tm = h.shape[0]
    h_hi = h.astype(BF16)
    h_ref[0] = h_hi
    h_lo = (h - h_hi.astype(F32)).astype(BF16)
    logits = _dot(h_hi, wr_ref[0]) + _dot(h_lo, wr_ref[0]) + _dot(h_hi, wr_ref[1])
    lane = _iota(logits.shape, 1).astype(F32)
    logits = jnp.where(lane < N_EXPERTS, logits, NEG)
    m1 = jnp.max(logits, axis=-1, keepdims=True)
    i1 = jnp.min(jnp.where(logits == m1, lane, float(LANES)), axis=-1, keepdims=True)
    rest = jnp.where(lane == i1, NEG, logits)
    m2 = jnp.max(rest, axis=-1, keepdims=True)
    i2 = jnp.min(jnp.where(rest == m2, lane, float(LANES)), axis=-1, keepdims=True)
    e2 = jnp.exp(m2 - m1)
    den = 1.0 + e2
    sel1, sel2 = lane == i1, lane == i2
    chosen = jnp.where(sel1 | sel2, 1.0, 0.0)
    before = (_iota((tm, tm), 0) > _iota((tm, tm), 1)).astype(BF16)
    pos = _dot(before, chosen.astype(BF16))
    cnt = jnp.sum(chosen, axis=0, keepdims=True)
    nblk = sum(jnp.where(cnt > float(k * MOE_ROWS), 1.0, 0.0) for k in range(pl.cdiv(tm, MOE_ROWS)))
    lower_e = (_iota((LANES, LANES), 0) < _iota((LANES, LANES), 1)).astype(BF16)
    off = _dot_exact_rhs01(jnp.broadcast_to(nblk * float(MOE_ROWS), (SUBLANES, LANES)), lower_e)[0:1]
    slot = off + pos
    s1 = jnp.sum(jnp.where(sel1, slot, 0.0), axis=-1, keepdims=True)
    s2 = jnp.sum(jnp.where(sel2, slot, 0.0), axis=-1, keepdims=True)
    lane_i = _iota(logits.shape, 1)
    slots = jnp.where(lane_i == 0, s1, jnp.where(lane_i == 1, s2, -1.0))
    slot_ref[0, 0] = slots.T[0:SUBLANES].astype(jnp.int32)
    gate_ref[0] = jnp.where(sel1, 1.0 / den, 0.0) + jnp.where(sel2, e2 / den, 0.0)
    row = _iota((SUBLANES, LANES), 0)
    cnt_ref[0, 0] = jnp.where(row == 0, cnt, jnp.where(row == 1, off, jnp.where(row == 2, nblk, 0.0))
                              ).astype(jnp.int32)


def _route_block(xs, att, ret, ssd, mods, lw, tm):
    b, t, d = xs.shape
    nt = t // tm
    tok = lambda w: pl.BlockSpec((1, tm, w), lambda i, j: (i, j, 0))
    const = lambda shape: pl.BlockSpec(shape, lambda i, j: (0,) * len(shape))
    per_tile = lambda r, w: pl.BlockSpec((1, 1, r, w), lambda i, j: (i, j, 0, 0))
    return pl.pallas_call(
        _route_kernel,
        grid=(b, nt),
        in_specs=[tok(d), tok(att.shape[-1]), tok(ret.shape[-1]), tok(ssd.shape[-1]),
                  pl.BlockSpec((1, 8, d), lambda i, j: (i, 0, 0)), const((1, d)),
                  const((d, d)), const((2, d, LANES))],
        out_specs=[tok(d), tok(d), per_tile(SUBLANES, tm), tok(LANES), per_tile(SUBLANES, LANES)],
        out_shape=[jax.ShapeDtypeStruct((b, t, d), F32), jax.ShapeDtypeStruct((b, t, d), BF16),
                   jax.ShapeDtypeStruct((b, nt, SUBLANES, tm), jnp.int32),
                   jax.ShapeDtypeStruct((b, t, LANES), F32),
                   jax.ShapeDtypeStruct((b, nt, SUBLANES, LANES), jnp.int32)],
        compiler_params=_cparams(("parallel", "parallel")),
        name="outproj_route",
    )(xs, att, ret, ssd, mods, lw["n2"], lw["wo"], lw["router"])


MOE_ROWS = 256


def _moe_kernel(cnt_ref, slot_ref, h_ref, gate_ref, x1_ref, mod_ref, wgu_ref, wd_ref, fn_ref, o_ref,
                *, final_norm, ff_split):
    e = pl.program_id(2)
    rb = MOE_ROWS
    tm = h_ref.shape[1]

    @pl.when(e == 0)
    def _():
        o_ref[...] = jnp.zeros_like(o_ref)

    off = cnt_ref[0, 0, 1, e]
    nblocks = cnt_ref[0, 0, 2, e]
    slot1 = slot_ref[0, 0, 0:1, :]
    slot2 = slot_ref[0, 0, 1:2, :]
    gates = gate_ref[0]
    ge = jnp.sum(jnp.where(_iota(gates.shape, 1) == e, gates, 0.0), axis=-1, keepdims=True)

    def block(bi, carry):
        rows = off + bi * rb + _iota((rb, tm), 0)
        sel = jnp.where((rows == slot1) | (rows == slot2), 1.0, 0.0).astype(BF16)
        lhs = _dot(sel, h_ref[0]).astype(BF16)
        y = None
        for s0, s1 in ff_split:
            w = s1 - s0
            gu = _dot(lhs, wgu_ref[0, :, 2 * s0:2 * s1])
            part = _dot((_silu(gu[:, :w]) * gu[:, w:]).astype(BF16), wd_ref[0, s0:s1, :])
            y = part if y is None else y + part
        o_ref[0] += ge * _dot_tn(sel, y.astype(BF16))
        return carry

    lax.fori_loop(0, nblocks, block, 0)

    @pl.when(e == N_EXPERTS - 1)
    def _():
        out = x1_ref[0] + mod_ref[0, 5:6, :] * o_ref[0]
        o_ref[0] = _rms(out, fn_ref[...]) if final_norm else out


MOE_FF_SLICE = 512


def _moe_ff_split(ff):
    return tuple((s, min(s + MOE_FF_SLICE, ff)) for s in range(0, ff, MOE_FF_SLICE))


def _moe_weights(w_gate, w_up, w_down):
    ff = w_gate.shape[-1]
    wgu = jnp.concatenate([w[..., s0:s1] for s0, s1 in _moe_ff_split(ff) for w in (w_gate, w_up)], axis=-1)
    return wgu.astype(BF16), w_down.astype(BF16)


def _moe_block(x1, h, slots, gates, counts, mods, lw, fnw, final_norm, tm):
    b, t, d = x1.shape
    ff = lw["moe_wd"].shape[1]
    assert MOE_ROWS % (2 * SUBLANES) == 0
    tok = lambda w: pl.BlockSpec((1, tm, w), lambda i, j, e: (i, j, 0))
    per_tile = lambda w, **kw: pl.BlockSpec((1, 1, SUBLANES, w), lambda i, j, e: (i, j, 0, 0), **kw)
    return pl.pallas_call(
        functools.partial(_moe_kernel, final_norm=final_norm, ff_split=_moe_ff_split(ff)),
        grid=(b, t // tm, N_EXPERTS),
        in_specs=[per_tile(LANES, memory_space=pltpu.SMEM), per_tile(tm), tok(d), tok(LANES), tok(d),
                  pl.BlockSpec((1, 8, d), lambda i, j, e: (i, 0, 0)),
                  pl.BlockSpec((1, d, 2 * ff), lambda i, j, e: (e, 0, 0)),
                  pl.BlockSpec((1, ff, d), lambda i, j, e: (e, 0, 0)),
                  pl.BlockSpec((1, d), lambda i, j, e: (0, 0))],
        out_specs=tok(d),
        out_shape=jax.ShapeDtypeStruct((b, t, d), F32),
        compiler_params=_cparams(("parallel", "parallel", "arbitrary")),
        name="moe",
    )(counts, slots, h, gates, x1, mods, lw["moe_wgu"], lw["moe_wd"], fnw)


def _final_norm_kernel(x_ref, w_ref, o_ref):
    o_ref[0] = _rms(x_ref[0], w_ref[...])


def _final_norm(xs, fnw, tm):
    b, t, d = xs.shape
    return pl.pallas_call(
        _final_norm_kernel,
        grid=(b, t // tm),
        in_specs=[pl.BlockSpec((1, tm, d), lambda i, j: (i, j, 0)), pl.BlockSpec((1, d), lambda i, j: (0, 0))],
        out_specs=pl.BlockSpec((1, tm, d), lambda i, j: (i, j, 0)),
        out_shape=jax.ShapeDtypeStruct((b, t, d), F32),
        compiler_params=_cparams(("parallel", "parallel")),
        name="final_norm",
    )(xs, fnw)


def _rot_cols(w):
    a, b2, c, d = jnp.split(w, 4, axis=-1)
    return jnp.concatenate([-b2, a, -d, c], axis=-1)


def _layer_weights(i, norm1_w, norm2_w, w_in, w_out, mla_q_norm_w, mla_w_qb, mla_kv_norm_w, mla_w_kb, mla_w_vb,
                   ret_decay_logit, ssd_conv_w, ssd_conv_b, ssd_dt_bias, ssd_a_log, ssd_d, ssd_norm_w):
    d = w_in.shape[1]
    offs = np.cumsum((0,) + IN_SIZES)
    seg = [w_in[i][:, offs[j]:offs[j + 1]] for j in range(len(IN_SIZES))]
    zeros = lambda n: jnp.zeros((d, n), F32)
    w_all = jnp.concatenate(
        [seg[0], seg[1], seg[2], _rot_cols(seg[2]), zeros(W_MLA - MLA_Q_LORA - MLA_KV_LORA - 2 * MLA_ROPE),
         seg[3] * (RET_QK ** -0.5), seg[4], seg[5], seg[6], seg[7], seg[8], seg[9], seg[10],
         zeros(W_DT - 2 * SSD_HEADS)], axis=1).astype(BF16)

    dh = MLA_NOPE + MLA_ROPE
    wqb = mla_w_qb[i].reshape(MLA_Q_LORA, MLA_HEADS, dh)
    pad = lambda a, lo, hi: jnp.pad(a, ((0, 0), (0, 0), (lo, hi)))
    wq = pad(wqb, 0, MLA_HP - dh).reshape(MLA_Q_LORA, MLA_HEADS * MLA_HP).astype(BF16)
    wqr = pad(_rot_cols(wqb[:, :, MLA_NOPE:]), MLA_NOPE, MLA_HP - dh).reshape(MLA_Q_LORA, MLA_HEADS * MLA_HP).astype(BF16)
    wkb = pad(mla_w_kb[i].reshape(MLA_KV_LORA, MLA_HEADS, MLA_NOPE), 0, MLA_HP - MLA_NOPE)
    place = np.zeros((MLA_ROPE, MLA_HEADS, MLA_HP), np.float32)
    for r in range(MLA_ROPE):
        place[r, :, MLA_NOPE + r] = 1.0
    place = jnp.asarray(place)
    wk = jnp.concatenate([wkb, place, place, jnp.zeros((MLA_KV_LORA - 2 * MLA_ROPE, MLA_HEADS, MLA_HP), F32)],
                         axis=0).reshape(2 * MLA_KV_LORA, MLA_HEADS * MLA_HP).astype(BF16)

    wvb = mla_w_vb[i].reshape(MLA_KV_LORA, MLA_HEADS, MLA_V)
    wv = jnp.concatenate(
        [pad(wvb[:, hh:hh + 1], (hh % 2) * MLA_V, MLA_HP - MLA_V - (hh % 2) * MLA_V) for hh in range(MLA_HEADS)],
        axis=1).reshape(MLA_KV_LORA, MLA_HEADS * MLA_HP).astype(BF16)

    lg = ret_decay_logit[i]
    padl = lambda v: jnp.pad(v.reshape(1, -1), ((0, 0), (0, LANES - 2 * SSD_HEADS)))
    return {
        "n1": norm1_w[i].reshape(1, d), "n2": norm2_w[i].reshape(1, d),
        "w_all": w_all, "qn": mla_q_norm_w[i].reshape(1, -1), "wq": wq, "wqr": wqr,
        "kvn": mla_kv_norm_w[i].reshape(1, -1), "wk": wk, "wv": wv,
        "wo": w_out[i].astype(BF16),
        "ret_lgl": jnp.repeat(lg, RET_V, axis=1),
        "ret_lgh": jnp.broadcast_to(lg.reshape(2 * RET_HEADS, 1), (2 * RET_HEADS, CHUNK)),
        "conv_w": jnp.pad(ssd_conv_w[i], ((0, 8 - ssd_conv_w.shape[1]), (0, 0))),
        "conv_b": ssd_conv_b[i].reshape(1, -1),
        "dt_bias": padl(ssd_dt_bias[i]), "a_log": padl(ssd_a_log[i]),
        "d_vec": jnp.repeat(ssd_d[i], SSD_HEADDIM).reshape(1, -1),
        "ssd_nw": ssd_norm_w[i].reshape(1, -1),
    }


def _rope_tables(n_lat, n_ctx):
    rows = n_lat // GRID_W
    row = jnp.repeat(jnp.arange(rows, dtype=F32), GRID_W)
    col = jnp.tile(jnp.arange(GRID_W, dtype=F32), rows)
    inv_freq = ROPE_BASE ** (-jnp.arange(0, ROPE_AXIS_DIM, 2, dtype=F32) / ROPE_AXIS_DIM)
    ang_r = row[:, None] * inv_freq
    ang_c = col[:, None] * inv_freq
    ang = jnp.concatenate([ang_r, ang_r, ang_c, ang_c], axis=-1)
    c2 = MLA_SCALE * math.log2(math.e)

    def build(cos, sin):
        n = cos.shape[0]
        cosq = jnp.concatenate([jnp.full((n, MLA_NOPE), c2, F32), c2 * cos, jnp.zeros((n, MLA_HP - MLA_NOPE - MLA_ROPE), F32)], 1)
        sinq = jnp.concatenate([jnp.zeros((n, MLA_NOPE), F32), c2 * sin, jnp.zeros((n, MLA_HP - MLA_NOPE - MLA_ROPE), F32)], 1)
        ck = jnp.concatenate([cos, sin, jnp.zeros((n, LANES - 2 * MLA_ROPE), F32)], 1)
        return cosq, sinq, ck

    lat = build(jnp.cos(ang), jnp.sin(ang))
    ctx = build(jnp.ones((n_ctx, MLA_ROPE), F32), jnp.zeros((n_ctx, MLA_ROPE), F32))
    return lat, ctx


def _token_tile(t, pref):
    tm = min(pref, t)
    while t % tm:
        tm //= 2
    return tm


def kernel(x, c, ctx, c_ctx, mod_w, mod_b, norm1_w, norm2_w, w_in, w_out, mla_q_norm_w, mla_w_qb, mla_kv_norm_w, mla_w_kb, mla_w_vb, ret_decay_logit, ssd_conv_w, ssd_conv_b, ssd_dt_bias, ssd_a_log, ssd_d, ssd_norm_w, ffn_w_gate, ffn_w_up, ffn_w_down, moe_router, moe_w_gate, moe_w_up, moe_w_down, final_norm_w):
    b, n_lat, d = x.shape
    n_ctx = ctx.shape[1]
    depth = mod_w.shape[0]
    assert b + 1 <= 8 and n_lat % CHUNK == 0 and n_ctx % CHUNK == 0 and n_lat % GRID_W == 0

    cond = jnp.concatenate([c, c_ctx[None, :], jnp.zeros((8 - b - 1, d), F32)], axis=0)
    mod_all = _modulation(cond, mod_w, mod_b)
    tabs_x, tabs_c = _rope_tables(n_lat, n_ctx)
    fnw = final_norm_w.reshape(1, d)

    tm_x = _token_tile(n_lat, 1024)
    tm_c = _token_tile(n_ctx, 256)
    tm_moe = _token_tile(n_lat, 1024)
    tq_x = _token_tile(n_lat, 1024)
    tk_x = 2048
    zero_ret = jnp.zeros((b, 2, RET_QK, RET_HEADS * RET_V), F32)
    zero_ssd = jnp.zeros((b, 2, SSD_STATE, SSD_INNER), F32)

    h_ctx = ctx
    for i in range(depth):
        last = i == depth - 1
        lw = _layer_weights(i, norm1_w, norm2_w, w_in, w_out, mla_q_norm_w, mla_w_qb, mla_kv_norm_w, mla_w_kb,
                            mla_w_vb, ret_decay_logit, ssd_conv_w, ssd_conv_b, ssd_dt_bias, ssd_a_log, ssd_d,
                            ssd_norm_w)
        m6 = mod_all[i].reshape(8, 6, d)
        pad2 = jnp.zeros((b, 2, d), F32)
        mods_x = jnp.concatenate([m6[:b], pad2], axis=1)
        mods_c = jnp.concatenate([jnp.broadcast_to(m6[b:b + 1], (b, 6, d)), pad2], axis=1)

        qc, kc, vc, retc, zc, xbcc, dtc = _inproj(h_ctx, mods_c, lw, tabs_c, tm_c)
        ret_yc, ret_fin = _ret_scan(retc, lw, zero_ret)
        ssd_yc, ssd_fin = _ssd_scan(xbcc, zc, dtc, lw, zero_ssd)

        qx, kx, vx, retx, zx, xbcx, dtx = _inproj(x, mods_x, lw, tabs_x, tm_x)
        att_x = _attention(qx, [(kc, vc), (kx, vx)], tq_x, tk_x)
        ret_yx, _ = _ret_scan(retx, lw, ret_fin)
        ssd_yx, _ = _ssd_scan(xbcx, zx, dtx, lw, ssd_fin)

        if i % 2 == 0:
            j = i // 2
            lw.update(wg=ffn_w_gate[j].astype(BF16), wu=ffn_w_up[j].astype(BF16), wd=ffn_w_down[j].astype(BF16))
            x = _dense_block(x, att_x, ret_yx, ssd_yx, mods_x, lw, tm_x)
            if last:
                x = _final_norm(x, fnw, tm_x)
        else:
            j = i // 2
            wgu, wd = _moe_weights(moe_w_gate[j], moe_w_up[j], moe_w_down[j])
            r_f32 = jnp.pad(moe_router[j], ((0, 0), (0, LANES - N_EXPERTS)))
            r_hi = r_f32.astype(BF16)
            r_lo = (r_f32 - r_hi.astype(F32)).astype(BF16)
            lw.update(router=jnp.stack([r_hi, r_lo]), moe_wgu=wgu, moe_wd=wd)
            x1, *routed = _route_block(x, att_x, ret_yx, ssd_yx, mods_x, lw, tm_moe)
            x = _moe_block(x1, *routed, mods_x, lw, fnw, last, tm_moe)

        if not last:
            att_c = _attention(qc, [(kc, vc)], tm_c, tk_x)
            if i % 2 == 0:
                h_ctx = _dense_block(h_ctx, att_c, ret_yc, ssd_yc, mods_c, lw, tm_c)
            else:
                c1, *routed = _route_block(h_ctx, att_c, ret_yc, ssd_yc, mods_c, lw, tm_c)
                h_ctx = _moe_block(c1, *routed, mods_c, lw, fnw, False, tm_c)
    return x
```

```python
import functools
import math

import jax
import jax.numpy as jnp
import numpy as np
from jax import lax
from jax.experimental import pallas as pl
from jax.experimental.pallas import tpu as pltpu

F32 = jnp.float32
BF16 = jnp.bfloat16
HIGHEST = lax.Precision.HIGHEST

EPS = 1e-6
CHUNK = 128
GRID_W = 64
MLA_HEADS = 4
MLA_Q_LORA = 256
MLA_KV_LORA = 128
MLA_NOPE = 64
MLA_ROPE = 32
MLA_V = 64
MLA_SCALE = (MLA_NOPE + MLA_ROPE) ** -0.5
ROPE_AXIS_DIM = MLA_ROPE // 2
ROPE_BASE = 10000.0
RET_HEADS = 4
RET_QK = 64
RET_V = 64
SSD_HEADS = 8
SSD_HEADDIM = 64
SSD_GROUPS = 2
SSD_STATE = 64
SSD_INNER = SSD_HEADS * SSD_HEADDIM
SSD_CONV_CH = SSD_INNER + 2 * SSD_GROUPS * SSD_STATE
N_EXPERTS = 8
IN_SIZES = (MLA_Q_LORA, MLA_KV_LORA, MLA_ROPE,
            RET_HEADS * RET_QK, RET_HEADS * RET_QK, RET_HEADS * RET_V, RET_HEADS * RET_V,
            SSD_INNER, SSD_CONV_CH, SSD_HEADS, SSD_HEADS)

LANES = 128
V7X_VMEM_LIMIT = 56 * 1024 * 1024

MLA_HP = 128
NEG = -1e30


def _cparams(sem):
    return pltpu.CompilerParams(dimension_semantics=sem, vmem_limit_bytes=V7X_VMEM_LIMIT)


def _dot(a, b):
    return jnp.dot(a, b, preferred_element_type=F32)


def _dot_nt(a, b):
    return lax.dot_general(a, b, (((1,), (1,)), ((), ())), preferred_element_type=F32)


def _dot_tn(a, b):
    return lax.dot_general(a, b, (((0,), (0,)), ((), ())), preferred_element_type=F32)


def _split3(x):
    hi = x.astype(BF16)
    r1 = x - hi.astype(F32)
    mid = r1.astype(BF16)
    lo = (r1 - mid.astype(F32)).astype(BF16)
    return hi, mid, lo


def _dot_exact_rhs01(x, e):
    hi, mid, lo = _split3(x)
    return _dot(hi, e) + _dot(mid, e) + _dot(lo, e)


def _dot_exact_lhs01(e, x):
    hi, mid, lo = _split3(x)
    return _dot(e, hi) + _dot(e, mid) + _dot(e, lo)


def _silu(x):
    return x / (1.0 + jnp.exp(-x))


def _rms(x, w):
    return x * lax.rsqrt(jnp.mean(x * x, axis=-1, keepdims=True) + EPS) * w


def _iota(shape, dim):
    return lax.broadcasted_iota(jnp.int32, shape, dim)


def _mod_kernel(a_ref, w_ref, b_ref, o_ref):
    k = pl.program_id(1)

    @pl.when(k == 0)
    def _():
        o_ref[0] = jnp.broadcast_to(b_ref[0], o_ref.shape[1:])

    o_ref[0] += jnp.dot(_silu(a_ref[...]), w_ref[0], precision=HIGHEST, preferred_element_type=F32)


def _modulation(cond, mod_w, mod_b):
    depth, d, n = mod_w.shape
    tk = 256
    return pl.pallas_call(
        _mod_kernel,
        grid=(depth, d // tk),
        in_specs=[pl.BlockSpec((8, tk), lambda i, k: (0, k)),
                  pl.BlockSpec((1, tk, n), lambda i, k: (i, k, 0)),
                  pl.BlockSpec((1, 1, n), lambda i, k: (i, 0, 0))],
        out_specs=pl.BlockSpec((1, 8, n), lambda i, k: (i, 0, 0)),
        out_shape=jax.ShapeDtypeStruct((depth, 8, n), F32),
        compiler_params=_cparams(("parallel", "arbitrary")),
        name="modulation",
    )(cond, mod_w, mod_b.reshape(depth, 1, n))


W_MLA, W_RET, W_Z, W_XBC, W_DT = 512, 1024, 512, 768, 128
W_ALL = W_MLA + W_RET + W_Z + W_XBC + W_DT


def _ones_lane_mask(shape):
    lane = _iota(shape, len(shape) - 1) % (2 * MLA_HP)
    return (lane == MLA_HP - 1) | (lane == MLA_HP)


def _inproj_kernel(x_ref, mod_ref, n1_ref, win_ref, qn_ref, wq_ref, wqr_ref, kvn_ref, wk_ref, wv_ref,
                   cq_ref, sq_ref, ck_ref,
                   q_out, k_out, v_out, ret_out, z_out, xbc_out, dt_out):
    x = x_ref[0]
    h = _rms(x, n1_ref[...]) * (1.0 + mod_ref[0, 1:2, :]) + mod_ref[0, 0:1, :]
    hb = h.astype(BF16)

    o = 0
    pm = _dot(hb, win_ref[:, o:o + W_MLA]); o += W_MLA
    ret_out[0] = _dot(hb, win_ref[:, o:o + W_RET]).astype(BF16); o += W_RET
    z_out[0] = _dot(hb, win_ref[:, o:o + W_Z]).astype(BF16); o += W_Z
    xbc_out[0] = _dot(hb, win_ref[:, o:o + W_XBC]).astype(BF16); o += W_XBC
    dt_out[0] = _dot(hb, win_ref[:, o:o + W_DT])

    cqb = _rms(pm[:, 0:MLA_Q_LORA], qn_ref[...]).astype(BF16)
    qm = _dot(cqb, wq_ref[...])
    qr = _dot(cqb, wqr_ref[...])
    cosq, sinq = cq_ref[...], sq_ref[...]
    for hh in range(MLA_HEADS):
        sl = slice(hh * MLA_HP, (hh + 1) * MLA_HP)
        q_out[0, :, sl] = (qm[:, sl] * cosq + qr[:, sl] * sinq).astype(BF16)

    ckvn = _rms(pm[:, MLA_Q_LORA:MLA_Q_LORA + MLA_KV_LORA], kvn_ref[...])
    u = pm[:, MLA_Q_LORA + MLA_KV_LORA:W_MLA] * ck_ref[...]
    kin = jnp.concatenate([ckvn, u], axis=-1).astype(BF16)
    k_out[0] = _dot(kin, wk_ref[...]).astype(BF16)
    v = _dot(ckvn.astype(BF16), wv_ref[...])
    v_out[0] = jnp.where(_ones_lane_mask(v.shape), 1.0, v).astype(BF16)


def _inproj(xs, mods, lw, tabs, tm):
    b, t, d = xs.shape
    const = lambda shape: pl.BlockSpec(shape, lambda i, j: (0,) * len(shape))
    tok = lambda w: pl.BlockSpec((1, tm, w), lambda i, j: (i, j, 0))
    tab = pl.BlockSpec((tm, LANES), lambda i, j: (j, 0))
    widths = (MLA_HEADS * MLA_HP, MLA_HEADS * MLA_HP, MLA_HEADS * MLA_HP, W_RET, W_Z, W_XBC, W_DT)
    dtypes = (BF16, BF16, BF16, BF16, BF16, BF16, F32)
    return pl.pallas_call(
        _inproj_kernel,
        grid=(b, t // tm),
        in_specs=[tok(d), pl.BlockSpec((1, 8, d), lambda i, j: (i, 0, 0)), const((1, d)),
                  const((d, W_ALL)), const((1, MLA_Q_LORA)), const((MLA_Q_LORA, MLA_HEADS * MLA_HP)),
                  const((MLA_Q_LORA, MLA_HEADS * MLA_HP)), const((1, MLA_KV_LORA)),
                  const((2 * MLA_KV_LORA, MLA_HEADS * MLA_HP)), const((MLA_KV_LORA, MLA_HEADS * MLA_HP)),
                  tab, tab, tab],
        out_specs=[tok(w) for w in widths],
        out_shape=[jax.ShapeDtypeStruct((b, t, w), dt) for w, dt in zip(widths, dtypes)],
        compiler_params=_cparams(("parallel", "parallel")),
        name="inproj",
    )(xs, mods, lw["n1"], lw["w_all"], lw["qn"], lw["wq"], lw["wqr"], lw["kvn"], lw["wk"], lw["wv"], *tabs)


def _attn_kernel(*refs, seg_blocks, tq):
    nseg = len(seg_blocks)
    q_ref = refs[0]
    kv_refs = refs[1:1 + 2 * nseg]
    o_ref = refs[1 + 2 * nseg]
    carry = tuple((jnp.full((tq, 1), NEG, F32), jnp.zeros((tq, MLA_HP), F32)) for _ in range(MLA_HEADS))
    for si, (tk, nblk) in enumerate(seg_blocks):
        k_ref, v_ref = kv_refs[2 * si], kv_refs[2 * si + 1]

        def body(j, carry, k_ref=k_ref, v_ref=v_ref, tk=tk):
            start = 0 if isinstance(j, int) else pl.multiple_of(j * tk, tk)
            new = []
            for hh in range(MLA_HEADS):
                sl = slice(hh * MLA_HP, (hh + 1) * MLA_HP)
                m, acc = carry[hh]
                s = _dot_nt(q_ref[0, :, sl], k_ref[0, pl.ds(start, tk), sl])
                m_new = jnp.maximum(m, jnp.max(s, axis=-1, keepdims=True))
                p = jnp.exp2(s - m_new).astype(BF16)
                acc = jnp.exp2(m - m_new) * acc + _dot(p, v_ref[0, pl.ds(start, tk), sl])
                new.append((m_new, acc))
            return tuple(new)

        carry = body(0, carry) if nblk == 1 else lax.fori_loop(0, nblk, body, carry)
    lane = _iota((tq, MLA_HP), 1)
    pairs = []
    for hh in range(0, MLA_HEADS, 2):
        acc_e, acc_o = carry[hh][1], carry[hh + 1][1]
        l_e = jnp.sum(jnp.where(lane == MLA_HP - 1, acc_e, 0.0), axis=-1, keepdims=True)
        l_o = jnp.sum(jnp.where(lane == 0, acc_o, 0.0), axis=-1, keepdims=True)
        pairs.append(jnp.where(lane < MLA_V, acc_e / l_e, acc_o / l_o))
    o_ref[0] = jnp.concatenate(pairs, axis=-1).astype(BF16)


def _attention(q, segs, tq, tk):
    b, l, _ = q.shape
    seg_blocks = []
    in_specs = [pl.BlockSpec((1, tq, MLA_HEADS * MLA_HP), lambda i, j: (i, j, 0))]
    args = [q]
    for k, v in segs:
        s = k.shape[1]
        t = min(tk, s)
        seg_blocks.append((t, s // t))
        for _ in range(2):
            in_specs.append(pl.BlockSpec((1, s, MLA_HEADS * MLA_HP), lambda i, j: (i, 0, 0),
                                         pipeline_mode=pl.Buffered(1)))
        args += [k, v]
    return pl.pallas_call(
        functools.partial(_attn_kernel, seg_blocks=tuple(seg_blocks), tq=tq),
        grid=(b, l // tq),
        in_specs=in_specs,
        out_specs=pl.BlockSpec((1, tq, MLA_HEADS * MLA_V), lambda i, j: (i, j, 0)),
        out_shape=jax.ShapeDtypeStruct((b, l, MLA_HEADS * MLA_V), BF16),
        compiler_params=_cparams(("parallel", "arbitrary")),
        name="attention",
    )(*args)


def _ret_kernel(x_ref, lgl_ref, lgh_ref, init_ref, y_ref, fin_ref, s_ref, g_ref, gbuf_ref, *, nblk, sub):
    C, H, N, P = CHUNK, RET_HEADS, RET_QK, RET_V
    HP = H * P
    ph, c = pl.program_id(1), pl.program_id(2)
    blk = jnp.where(ph == 0, nblk - 1 - c, c)

    def log_sigmoid(v):
        return jnp.minimum(v, 0.0) - jnp.log(1.0 + jnp.exp(-jnp.abs(v)))

    lgl = log_sigmoid(lgl_ref[...])
    lgf, lgb = lgl[0:1], lgl[1:2]
    rows = _iota((C, HP), 0).astype(F32)
    head_n = _iota((N, HP), 1) // P

    def diag_blocks(r):
        out = r[0:N]
        for hh in range(1, H):
            out = jnp.where(head_n == hh, r[hh * N:(hh + 1) * N], out)
        return out

    def as_rows(s_c):
        return jnp.concatenate([jnp.where(head_n == hh, s_c, 0.0) for hh in range(H)], axis=0)

    @pl.when((ph == 0) & (c == 0))
    def _():
        g_ref[...] = init_ref[0, 1]

    @pl.when(ph == 0)
    def _():
        w_b = jnp.exp(rows * lgb)
        tot_b = jnp.exp(C * lgb)
        for sb in reversed(range(sub)):
            r0 = sb * C
            k = x_ref[0, r0:r0 + C, HP:2 * HP].astype(F32)
            v = x_ref[0, r0:r0 + C, 2 * HP:3 * HP]
            g = g_ref[...]
            gbuf_ref[blk * sub + sb] = g
            g_ref[...] = g * tot_b + diag_blocks(_dot_tn((k * w_b).astype(BF16), v))

        @pl.when(c == nblk - 1)
        def _():
            fin_ref[0, 1] = g_ref[...]

    @pl.when((ph == 1) & (c == 0))
    def _():
        s_ref[...] = init_ref[0, 0]

    @pl.when(ph == 1)
    def _():
        lgh = log_sigmoid(lgh_ref[...])
        grp = _iota((C, HP), 1) // P
        di = (_iota((C, C), 0) - _iota((C, C), 1)).astype(F32)
        decay = [jnp.exp(jnp.where(di >= 0, di * lgh[hh:hh + 1, :], -di * lgh[H + hh:H + hh + 1, :]))
                 for hh in range(H)]
        e_f = jnp.exp((rows + 1.0) * lgf)
        e_b = jnp.exp((C - rows) * lgb)
        w_f = jnp.exp((C - 1.0 - rows) * lgf)
        tot_f = jnp.exp(C * lgf)
        for sb in range(sub):
            r0 = sb * C
            q = x_ref[0, r0:r0 + C, 0:HP]
            kb = x_ref[0, r0:r0 + C, HP:2 * HP]
            v = x_ref[0, r0:r0 + C, 2 * HP:3 * HP]
            gate = x_ref[0, r0:r0 + C, 3 * HP:4 * HP].astype(F32)
            ws, vs = [], []
            for hh in range(H):
                sc = _dot_nt(jnp.where(grp == hh, q, jnp.zeros_like(q)), kb)
                ws.append((sc * decay[hh]).astype(BF16))
                vs.append(jnp.where(grp == hh, v, jnp.zeros_like(v)))
            y = _dot(jnp.concatenate(ws, axis=1), jnp.concatenate(vs, axis=0))
            qf32 = q.astype(F32)
            s = s_ref[...]
            g_in = gbuf_ref[blk * sub + sb]
            y = y + _dot(jnp.concatenate([(qf32 * e_f).astype(BF16), (qf32 * e_b).astype(BF16)], axis=1),
                         jnp.concatenate([as_rows(s), as_rows(g_in)], axis=0).astype(BF16))
            s_ref[...] = s * tot_f + diag_blocks(_dot_tn((kb.astype(F32) * w_f).astype(BF16), v))

            mu = jnp.zeros_like(y)
            for hh in range(H):
                mh = jnp.sum(jnp.where(grp == hh, y, 0.0), axis=-1, keepdims=True) * (1.0 / P)
                mu = jnp.where(grp == hh, mh, mu)
            yc = y - mu
            var = jnp.zeros_like(y)
            for hh in range(H):
                vh = jnp.sum(jnp.where(grp == hh, yc * yc, 0.0), axis=-1, keepdims=True) * (1.0 / P)
                var = jnp.where(grp == hh, vh, var)
            y_ref[0, r0:r0 + C, :] = (yc * lax.rsqrt(var + EPS) * _silu(gate)).astype(BF16)

        @pl.when(c == nblk - 1)
        def _():
            fin_ref[0, 0] = s_ref[...]


def _block_index(nblk):
    return lambda i, ph, c: (i, (1 - ph) * (nblk - 1 - c) + ph * c, 0)


def _scan_sub(t):
    nc = t // CHUNK
    return 4 if nc % 4 == 0 else (2 if nc % 2 == 0 else 1)


def _ret_scan(ret, lw, init):
    b, t, _ = ret.shape
    sub = _scan_sub(t)
    rows = sub * CHUNK
    nblk = t // rows
    hp = RET_HEADS * RET_V
    return pl.pallas_call(
        functools.partial(_ret_kernel, nblk=nblk, sub=sub),
        grid=(b, 2, nblk),
        in_specs=[pl.BlockSpec((1, rows, W_RET), _block_index(nblk)),
                  pl.BlockSpec((2, hp), lambda i, ph, c: (0, 0)),
                  pl.BlockSpec((2 * RET_HEADS, CHUNK), lambda i, ph, c: (0, 0)),
                  pl.BlockSpec((1, 2, RET_QK, hp), lambda i, ph, c: (i, 0, 0, 0))],
        out_specs=[pl.BlockSpec((1, rows, hp), lambda i, ph, c: (i, ph * c, 0)),
                   pl.BlockSpec((1, 2, RET_QK, hp), lambda i, ph, c: (i, 0, 0, 0))],
        out_shape=[jax.ShapeDtypeStruct((b, t, hp), BF16),
                   jax.ShapeDtypeStruct((b, 2, RET_QK, hp), F32)],
        scratch_shapes=[pltpu.VMEM((RET_QK, hp), F32),
                        pltpu.VMEM((RET_QK, hp), F32),
                        pltpu.VMEM((t // CHUNK, RET_QK, hp), F32)],
        compiler_params=_cparams(("parallel", "arbitrary", "arbitrary")),
        name="retention_scan",
    )(ret, lw["ret_lgl"], lw["ret_lgh"], init)


def _ssd_kernel(x_ref, xp_ref, xn_ref, z_ref, dt_ref, cw_ref, cb_ref, dtb_ref, alog_ref, dvec_ref, nw_ref,
                init_ref, y_ref, fin_ref, s_ref, g_ref, gbuf_ref, act_ref, dtc_ref, abc_ref, *, nblk, sub):
    C, H, N, P, G = CHUNK, SSD_HEADS, SSD_STATE, SSD_HEADDIM, SSD_GROUPS
    HP = H * P
    HPG = H // G
    HALO = SSD_HALO
    ph, c = pl.program_id(1), pl.program_id(2)
    blk = jnp.where(ph == 0, nblk - 1 - c, c)

    ri = _iota((C, C + 2 * HALO), 0)
    ci = _iota((C, C + 2 * HALO), 1)
    sel_prev = (ci == ri + HALO - 1).astype(BF16)
    sel_next = (ci == ri + HALO + 1).astype(BF16)
    lane = _iota((LANES, HP), 0)
    head_of = _iota((LANES, HP), 1) // P
    e_f = (lane == head_of).astype(BF16)
    e_b = (lane == head_of + H).astype(BF16)
    r_i = _iota((C, C), 0)
    c_i = _iota((C, C), 1)
    low = (r_i >= c_i).astype(BF16)
    upp = (r_i <= c_i).astype(BF16)
    grp_n = _iota((N, HP), 1) // (HPG * P)
    neg_a = -jnp.exp(alog_ref[...])
    cw = cw_ref[...]

    def prep(sb, first_pass):
        r0 = sb * C
        if first_pass:
            xc = x_ref[0, r0:r0 + C, :]
            if sb == 0:
                prev = jnp.where(blk > 0, xp_ref[0], jnp.zeros_like(xp_ref[0]))
            else:
                prev = x_ref[0, r0 - HALO:r0, :]
            if sb == sub - 1:
                nxt = jnp.where(blk < nblk - 1, xn_ref[0], jnp.zeros_like(xn_ref[0]))
            else:
                nxt = x_ref[0, r0 + C:r0 + C + HALO, :]
            stacked = jnp.concatenate([prev, xc, nxt], axis=0)
            conv = (cw[0:1] * _dot(sel_prev, stacked) + cw[1:2] * xc.astype(F32)
                    + cw[2:3] * _dot(sel_next, stacked) + cb_ref[...])
            act = _silu(conv)
            raw = dt_ref[0, r0:r0 + C, :] + dtb_ref[...]
            dt = jnp.maximum(raw, 0.0) + jnp.log(1.0 + jnp.exp(-jnp.abs(raw)))
            la = dt * neg_a
            a_f = None
            a_b = _dot_exact_lhs01(upp, la)
            act_ref[blk * sub + sb] = act
            dtc_ref[blk * sub + sb] = dt
            abc_ref[blk * sub + sb] = a_b
        else:
            act = act_ref[blk * sub + sb]
            dt = dtc_ref[blk * sub + sb]
            a_b = abc_ref[blk * sub + sb]
            la = dt * neg_a
            a_f = _dot_exact_lhs01(low, la)
        xs = act[:, 0:HP]
        bm = act[:, HP:HP + G * N].astype(BF16)
        cm = act[:, HP + G * N:HP + 2 * G * N].astype(BF16)
        return xs, bm, cm, dt, la, a_f, a_b

    def state_update(state, bm, xs, wdt, tot8, e_dir, row):
        xw = (xs * _dot(wdt.astype(BF16), e_dir)).astype(BF16)
        tot = _dot_exact_rhs01(tot8, e_dir)[row:row + 1]
        r = _dot_tn(bm, xw)
        upd = r[0:N]
        for gg in range(1, G):
            upd = jnp.where(grp_n == gg, r[gg * N:(gg + 1) * N], upd)
        return state * tot + upd

    def as_rows(s_c):
        return jnp.concatenate([jnp.where(grp_n == gg, s_c, 0.0) for gg in range(G)], axis=0).astype(BF16)

    @pl.when((ph == 0) & (c == 0))
    def _():
        g_ref[...] = init_ref[0, 1]

    @pl.when(ph == 0)
    def _():
        for sb in reversed(range(sub)):
            xs, bm, _, dt, _, _, a_b = prep(sb, True)
            g = g_ref[...]
            gbuf_ref[blk * sub + sb] = g
            wdt = jnp.exp(a_b[0:1, :] - a_b) * dt
            g_ref[...] = state_update(g, bm, xs, wdt, jnp.exp(a_b[0:8, :]), e_b, 0)

        @pl.when(c == nblk - 1)
        def _():
            fin_ref[0, 1] = g_ref[...]

    @pl.when((ph == 1) & (c == 0))
    def _():
        s_ref[...] = init_ref[0, 0]

    @pl.when(ph == 1)
    def _():
        grp_l = _iota((C, G * N), 1) // N
        head_l = _iota((C, HP), 1) // P
        tri_f = r_i >= c_i
        diag = r_i == c_i
        for sb in range(sub):
            r0 = sb * C
            xs, bm, cm, dt, la, a_f, a_b = prep(sb, False)
            la_t = la.T[0:2 * H]
            dt_t = dt.T[0:2 * H]
            af_row = _dot_exact_rhs01(la_t, upp)
            ab_row = _dot_exact_rhs01(la_t, low)
            sc = [_dot_nt(jnp.where(grp_l == gg, cm, jnp.zeros_like(cm)), bm) for gg in range(G)]
            xsb = xs.astype(BF16)
            ws, vs = [], []
            for hh in range(H):
                seg_f = a_f[:, hh:hh + 1] - af_row[hh:hh + 1, :]
                seg_b = a_b[:, H + hh:H + hh + 1] - ab_row[H + hh:H + hh + 1, :]
                dt_f, dt_b = dt_t[hh:hh + 1, :], dt_t[H + hh:H + hh + 1, :]
                dm = (jnp.exp(jnp.where(tri_f, seg_f, seg_b)) * jnp.where(tri_f, dt_f, dt_b)
                      + jnp.where(diag, dt_b, 0.0))
                ws.append((sc[hh // HPG] * dm).astype(BF16))
                vs.append(jnp.where(head_l == hh, xsb, jnp.zeros_like(xsb)))
            y = _dot(jnp.concatenate(ws, axis=1), jnp.concatenate(vs, axis=0))

            s = s_ref[...]
            g_in = gbuf_ref[blk * sub + sb]
            y = y + _dot(jnp.exp(a_f).astype(BF16), e_f) * _dot(cm, as_rows(s))
            y = y + _dot(jnp.exp(a_b).astype(BF16), e_b) * _dot(cm, as_rows(g_in))

            wdt = jnp.exp(a_f[C - 1:C, :] - a_f) * dt
            s_ref[...] = state_update(s, bm, xs, wdt, jnp.exp(a_f[C - 8:C, :]), e_f, 7)

            y = y + xs * dvec_ref[...]
            y = y * _silu(z_ref[0, r0:r0 + C, :].astype(F32))
            y_ref[0, r0:r0 + C, :] = _rms(y, nw_ref[...]).astype(BF16)

        @pl.when(c == nblk - 1)
        def _():
            fin_ref[0, 0] = s_ref[...]


SSD_HALO = 16


def _ssd_scan(xbc, z, dt, lw, init):
    b, t, _ = xbc.shape
    sub = _scan_sub(t)
    rows = sub * CHUNK
    nblk = t // rows
    hp = SSD_INNER
    per = rows // SSD_HALO
    nhalo = t // SSD_HALO
    bidx = _block_index(nblk)

    def x_idx(i, ph, c):
        return (i, (1 - ph) * (nblk - 1 - c), 0)

    def prev_idx(i, ph, c):
        return (i, jnp.maximum((1 - ph) * (nblk - 1 - c) * per - 1, 0), 0)

    def next_idx(i, ph, c):
        return (i, jnp.minimum(((1 - ph) * (nblk - 1 - c) + 1) * per, nhalo - 1), 0)

    const = lambda shape: pl.BlockSpec(shape, lambda i, ph, c: (0,) * len(shape))
    return pl.pallas_call(
        functools.partial(_ssd_kernel, nblk=nblk, sub=sub),
        grid=(b, 2, nblk),
        in_specs=[pl.BlockSpec((1, rows, W_XBC), x_idx),
                  pl.BlockSpec((1, SSD_HALO, W_XBC), prev_idx),
                  pl.BlockSpec((1, SSD_HALO, W_XBC), next_idx),
                  pl.BlockSpec((1, rows, W_Z), bidx),
                  pl.BlockSpec((1, rows, W_DT), x_idx),
                  const((8, W_XBC)), const((1, W_XBC)), const((1, LANES)), const((1, LANES)),
                  const((1, hp)), const((1, hp)),
                  pl.BlockSpec((1, 2, SSD_STATE, hp), lambda i, ph, c: (i, 0, 0, 0))],
        out_specs=[pl.BlockSpec((1, rows, hp), lambda i, ph, c: (i, ph * c, 0)),
                   pl.BlockSpec((1, 2, SSD_STATE, hp), lambda i, ph, c: (i, 0, 0, 0))],
        out_shape=[jax.ShapeDtypeStruct((b, t, hp), BF16),
                   jax.ShapeDtypeStruct((b, 2, SSD_STATE, hp), F32)],
        scratch_shapes=[pltpu.VMEM((SSD_STATE, hp), F32),
                        pltpu.VMEM((SSD_STATE, hp), F32),
                        pltpu.VMEM((t // CHUNK, SSD_STATE, hp), F32),
                        pltpu.VMEM((t // CHUNK, CHUNK, W_XBC), F32),
                        pltpu.VMEM((t // CHUNK, CHUNK, W_DT), F32),
                        pltpu.VMEM((t // CHUNK, CHUNK, W_DT), F32)],
        compiler_params=_cparams(("parallel", "arbitrary", "arbitrary")),
        name="ssd_scan",
    )(xbc, xbc, xbc, z, dt, lw["conv_w"], lw["conv_b"], lw["dt_bias"], lw["a_log"], lw["d_vec"], lw["ssd_nw"], init)


def _mix_residual(x_ref, att_ref, ret_ref, ssd_ref, mod_ref, n2_ref, wo_ref):
    mix = jnp.concatenate([att_ref[0], ret_ref[0], ssd_ref[0]], axis=-1)
    x1 = x_ref[0] + mod_ref[0, 2:3, :] * _dot(mix, wo_ref[...])
    h = _rms(x1, n2_ref[...]) * (1.0 + mod_ref[0, 4:5, :]) + mod_ref[0, 3:4, :]
    return x1, h


def _dense_kernel(x_ref, att_ref, ret_ref, ssd_ref, mod_ref, n2_ref, wo_ref, wg_ref, wu_ref, wd_ref, o_ref,
                  *, ff_chunk):
    x1, h = _mix_residual(x_ref, att_ref, ret_ref, ssd_ref, mod_ref, n2_ref, wo_ref)
    hb = h.astype(BF16)
    ff = wg_ref.shape[1]
    f = jnp.zeros_like(x1)
    for s in range(0, ff, ff_chunk):
        g = _dot(hb, wg_ref[:, s:s + ff_chunk])
        u = _dot(hb, wu_ref[:, s:s + ff_chunk])
        f = f + _dot((_silu(g) * u).astype(BF16), wd_ref[s:s + ff_chunk, :])
    o_ref[0] = x1 + mod_ref[0, 5:6, :] * f


def _dense_block(xs, att, ret, ssd, mods, lw, tm):
    b, t, d = xs.shape
    ff = lw["wg"].shape[1]
    tok = lambda w: pl.BlockSpec((1, tm, w), lambda i, j: (i, j, 0))
    const = lambda shape: pl.BlockSpec(shape, lambda i, j: (0,) * len(shape), pipeline_mode=pl.Buffered(1))
    return pl.pallas_call(
        functools.partial(_dense_kernel, ff_chunk=ff // 2),
        grid=(b, t // tm),
        in_specs=[tok(d), tok(att.shape[-1]), tok(ret.shape[-1]), tok(ssd.shape[-1]),
                  pl.BlockSpec((1, 8, d), lambda i, j: (i, 0, 0)), const((1, d)),
                  const((d, d)), const((d, ff)), const((d, ff)), const((ff, d))],
        out_specs=tok(d),
        out_shape=jax.ShapeDtypeStruct((b, t, d), F32),
        compiler_params=_cparams(("parallel", "parallel")),
        name="outproj_ffn",
    )(xs, att, ret, ssd, mods, lw["n2"], lw["wo"], lw["wg"], lw["wu"], lw["wd"])


SUBLANES = 8


def _route_kernel(x_ref, att_ref, ret_ref, ssd_ref, mod_ref, n2_ref, wo_ref, wr_ref,
                  x1_ref, h_ref, slot_ref, gate_ref, cnt_ref):
    x1, h = _mix_residual(x_ref, att_ref, ret_ref, ssd_ref, mod_ref, n2_ref, wo_ref)
    x1_ref[0] = x1
    tm = h.shape[0]
    h_hi = h.astype(BF16)
    h_ref[0] = h_hi
    h_lo = (h - h_hi.astype(F32)).astype(BF16)
    logits = _dot(h_hi, wr_ref[0]) + _dot(h_lo, wr_ref[0]) + _dot(h_hi, wr_ref[1])
    lane = _iota(logits.shape, 1).astype(F32)
    logits = jnp.where(lane < N_EXPERTS, logits, NEG)
    m1 = jnp.max(logits, axis=-1, keepdims=True)
    i1 = jnp.min(jnp.where(logits == m1, lane, float(LANES)), axis=-1, keepdims=True)
    rest = jnp.where(lane == i1, NEG, logits)
    m2 = jnp.max(rest, axis=-1, keepdims=True)
    i2 = jnp.min(jnp.where(rest == m2, lane, float(LANES)), axis=-1, keepdims=True)
    e2 = jnp.exp(m2 - m1)
    den = 1.0 + e2
    sel1, sel2 = lane == i1, lane == i2
    chosen = jnp.where(sel1 | sel2, 1.0, 0.0)
    before = (_iota((tm, tm), 0) > _iota((tm, tm), 1)).astype(BF16)
    pos = _dot(before, chosen.astype(BF16))
    cnt = jnp.sum(chosen, axis=0, keepdims=True)
    nblk = sum(jnp.where(cnt > float(k * MOE_ROWS), 1.0, 0.0) for k in range(pl.cdiv(tm, MOE_ROWS)))
    lower_e = (_iota((LANES, LANES), 0) < _iota((LANES, LANES), 1)).astype(BF16)
    off = _dot_exact_rhs01(jnp.broadcast_to(nblk * float(MOE_ROWS), (SUBLANES, LANES)), lower_e)[0:1]
    slot = off + pos
    s1 = jnp.sum(jnp.where(sel1, slot, 0.0), axis=-1, keepdims=True)
    s2 = jnp.sum(jnp.where(sel2, slot, 0.0), axis=-1, keepdims=True)
    lane_i = _iota(logits.shape, 1)
    slots = jnp.where(lane_i == 0, s1, jnp.where(lane_i == 1, s2, -1.0))
    slot_ref[0, 0] = slots.T[0:SUBLANES].astype(jnp.int32)
    gate_ref[0] = jnp.where(sel1, 1.0 / den, 0.0) + jnp.where(sel2, e2 / den, 0.0)
    row = _iota((SUBLANES, LANES), 0)
    cnt_ref[0, 0] = jnp.where(row == 0, cnt, jnp.where(row == 1, off, jnp.where(row == 2, nblk, 0.0))
                              ).astype(jnp.int32)


def _route_block(xs, att, ret, ssd, mods, lw, tm):
    b, t, d = xs.shape
    nt = t // tm
    tok = lambda w: pl.BlockSpec((1, tm, w), lambda i, j: (i, j, 0))
    const = lambda shape: pl.BlockSpec(shape, lambda i, j: (0,) * len(shape))
    per_tile = lambda r, w: pl.BlockSpec((1, 1, r, w), lambda i, j: (i, j, 0, 0))
    return pl.pallas_call(
        _route_kernel,
        grid=(b, nt),
        in_specs=[tok(d), tok(att.shape[-1]), tok(ret.shape[-1]), tok(ssd.shape[-1]),
                  pl.BlockSpec((1, 8, d), lambda i, j: (i, 0, 0)), const((1, d)),
                  const((d, d)), const((2, d, LANES))],
        out_specs=[tok(d), tok(d), per_tile(SUBLANES, tm), tok(LANES), per_tile(SUBLANES, LANES)],
        out_shape=[jax.ShapeDtypeStruct((b, t, d), F32), jax.ShapeDtypeStruct((b, t, d), BF16),
                   jax.ShapeDtypeStruct((b, nt, SUBLANES, tm), jnp.int32),
                   jax.ShapeDtypeStruct((b, t, LANES), F32),
                   jax.ShapeDtypeStruct((b, nt, SUBLANES, LANES), jnp.int32)],
        compiler_params=_cparams(("parallel", "parallel")),
        name="outproj_route",
    )(xs, att, ret, ssd, mods, lw["n2"], lw["wo"], lw["router"])


MOE_ROWS = 256


def _moe_kernel(cnt_ref, slot_ref, h_ref, gate_ref, x1_ref, mod_ref, wgu_ref, wd_ref, fn_ref, o_ref,
                *, final_norm, ff_split):
    e = pl.program_id(2)
    rb = MOE_ROWS
    tm = h_ref.shape[1]

    @pl.when(e == 0)
    def _():
        o_ref[...] = jnp.zeros_like(o_ref)

    off = cnt_ref[0, 0, 1, e]
    nblocks = cnt_ref[0, 0, 2, e]
    slot1 = slot_ref[0, 0, 0:1, :]
    slot2 = slot_ref[0, 0, 1:2, :]
    gates = gate_ref[0]
    ge = jnp.sum(jnp.where(_iota(gates.shape, 1) == e, gates, 0.0), axis=-1, keepdims=True)

    def block(bi, carry):
        rows = off + bi * rb + _iota((rb, tm), 0)
        sel = jnp.where((rows == slot1) | (rows == slot2), 1.0, 0.0).astype(BF16)
        lhs = _dot(sel, h_ref[0]).astype(BF16)
        y = None
        for s0, s1 in ff_split:
            w = s1 - s0
            gu = _dot(lhs, wgu_ref[0, :, 2 * s0:2 * s1])
            part = _dot((_silu(gu[:, :w]) * gu[:, w:]).astype(BF16), wd_ref[0, s0:s1, :])
            y = part if y is None else y + part
        o_ref[0] += ge * _dot_tn(sel, y.astype(BF16))
        return carry

    lax.fori_loop(0, nblocks, block, 0)

    @pl.when(e == N_EXPERTS - 1)
    def _():
        out = x1_ref[0] + mod_ref[0, 5:6, :] * o_ref[0]
        o_ref[0] = _rms(out, fn_ref[...]) if final_norm else out


MOE_FF_SLICE = 512


def _moe_ff_split(ff):
    return tuple((s, min(s + MOE_FF_SLICE, ff)) for s in range(0, ff, MOE_FF_SLICE))


def _moe_weights(w_gate, w_up, w_down):
    ff = w_gate.shape[-1]
    wgu = jnp.concatenate([w[..., s0:s1] for s0, s1 in _moe_ff_split(ff) for w in (w_gate, w_up)], axis=-1)
    return wgu.astype(BF16), w_down.astype(BF16)


def _moe_block(x1, h, slots, gates, counts, mods, lw, fnw, final_norm, tm):
    b, t, d = x1.shape
    ff = lw["moe_wd"].shape[1]
    assert MOE_ROWS % (2 * SUBLANES) == 0
    tok = lambda w: pl.BlockSpec((1, tm, w), lambda i, j, e: (i, j, 0))
    per_tile = lambda w, **kw: pl.BlockSpec((1, 1, SUBLANES, w), lambda i, j, e: (i, j, 0, 0), **kw)
    return pl.pallas_call(
        functools.partial(_moe_kernel, final_norm=final_norm, ff_split=_moe_ff_split(ff)),
        grid=(b, t // tm, N_EXPERTS),
        in_specs=[per_tile(LANES, memory_space=pltpu.SMEM), per_tile(tm), tok(d), tok(LANES), tok(d),
                  pl.BlockSpec((1, 8, d), lambda i, j, e: (i, 0, 0)),
                  pl.BlockSpec((1, d, 2 * ff), lambda i, j, e: (e, 0, 0)),
                  pl.BlockSpec((1, ff, d), lambda i, j, e: (e, 0, 0)),
                  pl.BlockSpec((1, d), lambda i, j, e: (0, 0))],
        out_specs=tok(d),
        out_shape=jax.ShapeDtypeStruct((b, t, d), F32),
        compiler_params=_cparams(("parallel", "parallel", "arbitrary")),
        name="moe",
    )(counts, slots, h, gates, x1, mods, lw["moe_wgu"], lw["moe_wd"], fnw)


def _final_norm_kernel(x_ref, w_ref, o_ref):
    o_ref[0] = _rms(x_ref[0], w_ref[...])


def _final_norm(xs, fnw, tm):
    b, t, d = xs.shape
    return pl.pallas_call(
        _final_norm_kernel,
        grid=(b, t // tm),
        in_specs=[pl.BlockSpec((1, tm, d), lambda i, j: (i, j, 0)), pl.BlockSpec((1, d), lambda i, j: (0, 0))],
        out_specs=pl.BlockSpec((1, tm, d), lambda i, j: (i, j, 0)),
        out_shape=jax.ShapeDtypeStruct((b, t, d), F32),
        compiler_params=_cparams(("parallel", "parallel")),
        name="final_norm",
    )(xs, fnw)


def _rot_cols(w):
    a, b2, c, d = jnp.split(w, 4, axis=-1)
    return jnp.concatenate([-b2, a, -d, c], axis=-1)


def _layer_weights(i, norm1_w, norm2_w, w_in, w_out, mla_q_norm_w, mla_w_qb, mla_kv_norm_w, mla_w_kb, mla_w_vb,
                   ret_decay_logit, ssd_conv_w, ssd_conv_b, ssd_dt_bias, ssd_a_log, ssd_d, ssd_norm_w):
    d = w_in.shape[1]
    offs = np.cumsum((0,) + IN_SIZES)
    seg = [w_in[i][:, offs[j]:offs[j + 1]] for j in range(len(IN_SIZES))]
    zeros = lambda n: jnp.zeros((d, n), F32)
    w_all = jnp.concatenate(
        [seg[0], seg[1], seg[2], _rot_cols(seg[2]), zeros(W_MLA - MLA_Q_LORA - MLA_KV_LORA - 2 * MLA_ROPE),
         seg[3] * (RET_QK ** -0.5), seg[4], seg[5], seg[6], seg[7], seg[8], seg[9], seg[10],
         zeros(W_DT - 2 * SSD_HEADS)], axis=1).astype(BF16)

    dh = MLA_NOPE + MLA_ROPE
    wqb = mla_w_qb[i].reshape(MLA_Q_LORA, MLA_HEADS, dh)
    pad = lambda a, lo, hi: jnp.pad(a, ((0, 0), (0, 0), (lo, hi)))
    wq = pad(wqb, 0, MLA_HP - dh).reshape(MLA_Q_LORA, MLA_HEADS * MLA_HP).astype(BF16)
    wqr = pad(_rot_cols(wqb[:, :, MLA_NOPE:]), MLA_NOPE, MLA_HP - dh).reshape(MLA_Q_LORA, MLA_HEADS * MLA_HP).astype(BF16)
    wkb = pad(mla_w_kb[i].reshape(MLA_KV_LORA, MLA_HEADS, MLA_NOPE), 0, MLA_HP - MLA_NOPE)
    place = np.zeros((MLA_ROPE, MLA_HEADS, MLA_HP), np.float32)
    for r in range(MLA_ROPE):
        place[r, :, MLA_NOPE + r] = 1.0
    place = jnp.asarray(place)
    wk = jnp.concatenate([wkb, place, place, jnp.zeros((MLA_KV_LORA - 2 * MLA_ROPE, MLA_HEADS, MLA_HP), F32)],
                         axis=0).reshape(2 * MLA_KV_LORA, MLA_HEADS * MLA_HP).astype(BF16)

    wvb = mla_w_vb[i].reshape(MLA_KV_LORA, MLA_HEADS, MLA_V)
    wv = jnp.concatenate(
        [pad(wvb[:, hh:hh + 1], (hh % 2) * MLA_V, MLA_HP - MLA_V - (hh % 2) * MLA_V) for hh in range(MLA_HEADS)],
        axis=1).reshape(MLA_KV_LORA, MLA_HEADS * MLA_HP).astype(BF16)

    lg = ret_decay_logit[i]
    padl = lambda v: jnp.pad(v.reshape(1, -1), ((0, 0), (0, LANES - 2 * SSD_HEADS)))
    return {
        "n1": norm1_w[i].reshape(1, d), "n2": norm2_w[i].reshape(1, d),
        "w_all": w_all, "qn": mla_q_norm_w[i].reshape(1, -1), "wq": wq, "wqr": wqr,
        "kvn": mla_kv_norm_w[i].reshape(1, -1), "wk": wk, "wv": wv,
        "wo": w_out[i].astype(BF16),
        "ret_lgl": jnp.repeat(lg, RET_V, axis=1),
        "ret_lgh": jnp.broadcast_to(lg.reshape(2 * RET_HEADS, 1), (2 * RET_HEADS, CHUNK)),
        "conv_w": jnp.pad(ssd_conv_w[i], ((0, 8 - ssd_conv_w.shape[1]), (0, 0))),
        "conv_b": ssd_conv_b[i].reshape(1, -1),
        "dt_bias": padl(ssd_dt_bias[i]), "a_log": padl(ssd_a_log[i]),
        "d_vec": jnp.repeat(ssd_d[i], SSD_HEADDIM).reshape(1, -1),
        "ssd_nw": ssd_norm_w[i].reshape(1, -1),
    }


def _rope_tables(n_lat, n_ctx):
    f32 = np.float32
    rows = n_lat // GRID_W
    row = np.repeat(np.arange(rows, dtype=f32), GRID_W)
    col = np.tile(np.arange(GRID_W, dtype=f32), rows)
    inv_freq = (f32(ROPE_BASE) ** (-np.arange(0, ROPE_AXIS_DIM, 2, dtype=f32) / f32(ROPE_AXIS_DIM))).astype(f32)
    ang_r = row[:, None] * inv_freq
    ang_c = col[:, None] * inv_freq
    ang = np.concatenate([ang_r, ang_r, ang_c, ang_c], axis=-1).astype(f32)
    c2 = f32(MLA_SCALE * math.log2(math.e))

    def build(cos, sin):
        n = cos.shape[0]
        tail = np.zeros((n, MLA_HP - MLA_NOPE - MLA_ROPE), f32)
        cosq = np.concatenate([np.full((n, MLA_NOPE), c2, f32), c2 * cos, tail], 1)
        sinq = np.concatenate([np.zeros((n, MLA_NOPE), f32), c2 * sin, tail], 1)
        ck = np.concatenate([cos, sin, np.zeros((n, LANES - 2 * MLA_ROPE), f32)], 1)
        return tuple(jnp.asarray(a, F32) for a in (cosq, sinq, ck))

    lat = build(np.cos(ang).astype(f32), np.sin(ang).astype(f32))
    ctx = build(np.ones((n_ctx, MLA_ROPE), f32), np.zeros((n_ctx, MLA_ROPE), f32))
    return lat, ctx


def _token_tile(t, pref):
    tm = min(pref, t)
    while t % tm:
        tm //= 2
    return tm


def kernel(x, c, ctx, c_ctx, mod_w, mod_b, norm1_w, norm2_w, w_in, w_out, mla_q_norm_w, mla_w_qb, mla_kv_norm_w, mla_w_kb, mla_w_vb, ret_decay_logit, ssd_conv_w, ssd_conv_b, ssd_dt_bias, ssd_a_log, ssd_d, ssd_norm_w, ffn_w_gate, ffn_w_up, ffn_w_down, moe_router, moe_w_gate, moe_w_up, moe_w_down, final_norm_w):
    b, n_lat, d = x.shape
    n_ctx = ctx.shape[1]
    depth = mod_w.shape[0]
    assert b + 1 <= 8 and n_lat % CHUNK == 0 and n_ctx % CHUNK == 0 and n_lat % GRID_W == 0

    cond = jnp.concatenate([c, c_ctx[None, :], jnp.zeros((8 - b - 1, d), F32)], axis=0)
    mod_all = _modulation(cond, mod_w, mod_b)
    tabs_x, tabs_c = _rope_tables(n_lat, n_ctx)
    fnw = final_norm_w.reshape(1, d)

    tm_x = _token_tile(n_lat, 1024)
    tm_c = _token_tile(n_ctx, 256)
    tm_moe = _token_tile(n_lat, 1024)
    tq_x = _token_tile(n_lat, 1024)
    tk_x = 2048
    zero_ret = jnp.zeros((b, 2, RET_QK, RET_HEADS * RET_V), F32)
    zero_ssd = jnp.zeros((b, 2, SSD_STATE, SSD_INNER), F32)

    h_ctx = ctx
    for i in range(depth):
        last = i == depth - 1
        lw = _layer_weights(i, norm1_w, norm2_w, w_in, w_out, mla_q_norm_w, mla_w_qb, mla_kv_norm_w, mla_w_kb,
                            mla_w_vb, ret_decay_logit, ssd_conv_w, ssd_conv_b, ssd_dt_bias, ssd_a_log, ssd_d,
                            ssd_norm_w)
        m6 = mod_all[i].reshape(8, 6, d)
        pad2 = jnp.zeros((b, 2, d), F32)
        mods_x = jnp.concatenate([m6[:b], pad2], axis=1)
        mods_c = jnp.concatenate([jnp.broadcast_to(m6[b:b + 1], (b, 6, d)), pad2], axis=1)

        qc, kc, vc, retc, zc, xbcc, dtc = _inproj(h_ctx, mods_c, lw, tabs_c, tm_c)
        ret_yc, ret_fin = _ret_scan(retc, lw, zero_ret)
        ssd_yc, ssd_fin = _ssd_scan(xbcc, zc, dtc, lw, zero_ssd)

        qx, kx, vx, retx, zx, xbcx, dtx = _inproj(x, mods_x, lw, tabs_x, tm_x)
        att_x = _attention(qx, [(kc, vc), (kx, vx)], tq_x, tk_x)
        ret_yx, _ = _ret_scan(retx, lw, ret_fin)
        ssd_yx, _ = _ssd_scan(xbcx, zx, dtx, lw, ssd_fin)

        if i % 2 == 0:
            j = i // 2
            lw.update(wg=ffn_w_gate[j].astype(BF16), wu=ffn_w_up[j].astype(BF16), wd=ffn_w_down[j].astype(BF16))
            x = _dense_block(x, att_x, ret_yx, ssd_yx, mods_x, lw, tm_x)
            if last:
                x = _final_norm(x, fnw, tm_x)
        else:
            j = i // 2
            wgu, wd = _moe_weights(moe_w_gate[j], moe_w_up[j], moe_w_down[j])
            r_f32 = jnp.pad(moe_router[j], ((0, 0), (0, LANES - N_EXPERTS)))
            r_hi = r_f32.astype(BF16)
            r_lo = (r_f32 - r_hi.astype(F32)).astype(BF16)
            lw.update(router=jnp.stack([r_hi, r_lo]), moe_wgu=wgu, moe_wd=wd)
            x1, *routed = _route_block(x, att_x, ret_yx, ssd_yx, mods_x, lw, tm_moe)
            x = _moe_block(x1, *routed, mods_x, lw, fnw, last, tm_moe)

        if not last:
            att_c = _attention(qc, [(kc, vc)], tm_c, tk_x)
            if i % 2 == 0:
                h_ctx = _dense_block(h_ctx, att_c, ret_yc, ssd_yc, mods_c, lw, tm_c)
            else:
                c1, *routed = _route_block(h_ctx, att_c, ret_yc, ssd_yc, mods_c, lw, tm_c)
                h_ctx = _moe_block(c1, *routed, mods_c, lw, fnw, False, tm_c)
    return x
```

```python
import functools
import math

import jax
import jax.numpy as jnp
import numpy as np
from jax import lax
from jax.experimental import pallas as pl
from jax.experimental.pallas import tpu as pltpu

F32 = jnp.float32
BF16 = jnp.bfloat16
HIGHEST = lax.Precision.HIGHEST

EPS = 1e-6
CHUNK = 128
GRID_W = 64
MLA_HEADS = 4
MLA_Q_LORA = 256
MLA_KV_LORA = 128
MLA_NOPE = 64
MLA_ROPE = 32
MLA_V = 64
MLA_SCALE = (MLA_NOPE + MLA_ROPE) ** -0.5
ROPE_AXIS_DIM = MLA_ROPE // 2
ROPE_BASE = 10000.0
RET_HEADS = 4
RET_QK = 64
RET_V = 64
SSD_HEADS = 8
SSD_HEADDIM = 64
SSD_GROUPS = 2
SSD_STATE = 64
SSD_INNER = SSD_HEADS * SSD_HEADDIM
SSD_CONV_CH = SSD_INNER + 2 * SSD_GROUPS * SSD_STATE
N_EXPERTS = 8
IN_SIZES = (MLA_Q_LORA, MLA_KV_LORA, MLA_ROPE,
            RET_HEADS * RET_QK, RET_HEADS * RET_QK, RET_HEADS * RET_V, RET_HEADS * RET_V,
            SSD_INNER, SSD_CONV_CH, SSD_HEADS, SSD_HEADS)

LANES = 128
V7X_VMEM_LIMIT = 56 * 1024 * 1024

MLA_HP = 128
NEG = -1e30


def _cparams(sem):
    return pltpu.CompilerParams(dimension_semantics=sem, vmem_limit_bytes=V7X_VMEM_LIMIT)


def _dot(a, b):
    return jnp.dot(a, b, preferred_element_type=F32)


def _dot_nt(a, b):
    return lax.dot_general(a, b, (((1,), (1,)), ((), ())), preferred_element_type=F32)


def _dot_tn(a, b):
    return lax.dot_general(a, b, (((0,), (0,)), ((), ())), preferred_element_type=F32)


def _split3(x):
    hi = x.astype(BF16)
    r1 = x - hi.astype(F32)
    mid = r1.astype(BF16)
    lo = (r1 - mid.astype(F32)).astype(BF16)
    return hi, mid, lo


def _dot_exact_rhs01(x, e):
    hi, mid, lo = _split3(x)
    return _dot(hi, e) + _dot(mid, e) + _dot(lo, e)


def _dot_exact_lhs01(e, x):
    hi, mid, lo = _split3(x)
    return _dot(e, hi) + _dot(e, mid) + _dot(e, lo)


def _silu(x):
    return x / (1.0 + jnp.exp(-x))


def _rms(x, w):
    return x * lax.rsqrt(jnp.mean(x * x, axis=-1, keepdims=True) + EPS) * w


def _iota(shape, dim):
    return lax.broadcasted_iota(jnp.int32, shape, dim)


def _mod_kernel(a_ref, w_ref, b_ref, o_ref):
    k = pl.program_id(1)

    @pl.when(k == 0)
    def _():
        o_ref[0] = jnp.broadcast_to(b_ref[0], o_ref.shape[1:])

    o_ref[0] += jnp.dot(_silu(a_ref[...]), w_ref[0], precision=HIGHEST, preferred_element_type=F32)


def _modulation(cond, mod_w, mod_b):
    depth, d, n = mod_w.shape
    tk = 256
    return pl.pallas_call(
        _mod_kernel,
        grid=(depth, d // tk),
        in_specs=[pl.BlockSpec((8, tk), lambda i, k: (0, k)),
                  pl.BlockSpec((1, tk, n), lambda i, k: (i, k, 0)),
                  pl.BlockSpec((1, 1, n), lambda i, k: (i, 0, 0))],
        out_specs=pl.BlockSpec((1, 8, n), lambda i, k: (i, 0, 0)),
        out_shape=jax.ShapeDtypeStruct((depth, 8, n), F32),
        compiler_params=_cparams(("parallel", "arbitrary")),
        name="modulation",
    )(cond, mod_w, mod_b.reshape(depth, 1, n))


W_MLA, W_RET, W_Z, W_XBC, W_DT = 512, 1024, 512, 768, 128


def _ones_lane_mask(shape):
    lane = _iota(shape, len(shape) - 1) % (2 * MLA_HP)
    return (lane == MLA_HP - 1) | (lane == MLA_HP)


def _inproj_kernel(x_ref, mod_ref, n1_ref, wmla_ref, wret_ref, wz_ref, wxbc_ref, wdt_ref,
                   qn_ref, wq_ref, wqr_ref, kvn_ref, wk_ref, wv_ref, cq_ref, sq_ref, ck_ref,
                   q_out, k_out, v_out, ret_out, z_out, xbc_out, dt_out):
    x = x_ref[0]
    h = _rms(x, n1_ref[...]) * (1.0 + mod_ref[0, 1:2, :]) + mod_ref[0, 0:1, :]
    hb = h.astype(BF16)

    pm = _dot(hb, wmla_ref[...])
    ret = _dot(hb, wret_ref[...])
    qcols = _iota(ret.shape, 1) < RET_HEADS * RET_QK
    ret_out[0] = jnp.where(qcols, ret * (RET_QK ** -0.5), ret).astype(BF16)
    z_out[0] = _dot(hb, wz_ref[...]).astype(BF16)
    xbc_out[0] = _dot(hb, wxbc_ref[...]).astype(BF16)
    dt_out[0] = _dot(hb, wdt_ref[...])

    cqb = _rms(pm[:, 0:MLA_Q_LORA], qn_ref[...]).astype(BF16)
    qm = _dot(cqb, wq_ref[...])
    qr = _dot(cqb, wqr_ref[...])
    cosq, sinq = cq_ref[...], sq_ref[...]
    for hh in range(MLA_HEADS):
        sl = slice(hh * MLA_HP, (hh + 1) * MLA_HP)
        q_out[0, :, sl] = (qm[:, sl] * cosq + qr[:, sl] * sinq).astype(BF16)

    ckvn = _rms(pm[:, MLA_Q_LORA:MLA_Q_LORA + MLA_KV_LORA], kvn_ref[...])
    u = pm[:, MLA_Q_LORA + MLA_KV_LORA:W_MLA] * ck_ref[...]
    kin = jnp.concatenate([ckvn, u], axis=-1).astype(BF16)
    k_out[0] = _dot(kin, wk_ref[...]).astype(BF16)
    v = _dot(ckvn.astype(BF16), wv_ref[...])
    v_out[0] = jnp.where(_ones_lane_mask(v.shape), 1.0, v).astype(BF16)


def _inproj(xs, mods, lw, tabs, tm):
    b, t, d = xs.shape
    const = lambda shape: pl.BlockSpec(shape, lambda i, j: (0,) * len(shape))
    tok = lambda w: pl.BlockSpec((1, tm, w), lambda i, j: (i, j, 0))
    tab = pl.BlockSpec((tm, LANES), lambda i, j: (j, 0))
    widths = (MLA_HEADS * MLA_HP, MLA_HEADS * MLA_HP, MLA_HEADS * MLA_HP, W_RET, W_Z, W_XBC, W_DT)
    dtypes = (BF16, BF16, BF16, BF16, BF16, BF16, F32)
    return pl.pallas_call(
        _inproj_kernel,
        grid=(b, t // tm),
        in_specs=[tok(d), pl.BlockSpec((1, 8, d), lambda i, j: (i, 0, 0)), const((1, d)),
                  const((d, W_MLA)), const((d, W_RET)), const((d, W_Z)), const((d, W_XBC)), const((d, W_DT)),
                  const((1, MLA_Q_LORA)), const((MLA_Q_LORA, MLA_HEADS * MLA_HP)),
                  const((MLA_Q_LORA, MLA_HEADS * MLA_HP)), const((1, MLA_KV_LORA)),
                  const((2 * MLA_KV_LORA, MLA_HEADS * MLA_HP)), const((MLA_KV_LORA, MLA_HEADS * MLA_HP)),
                  tab, tab, tab],
        out_specs=[tok(w) for w in widths],
        out_shape=[jax.ShapeDtypeStruct((b, t, w), dt) for w, dt in zip(widths, dtypes)],
        compiler_params=_cparams(("parallel", "parallel")),
        name="inproj",
    )(xs, mods, lw["n1"], lw["w_mla"], lw["w_ret"], lw["w_z"], lw["w_xbc"], lw["w_dt"],
      lw["qn"], lw["wq"], lw["wqr"], lw["kvn"], lw["wk"], lw["wv"], *tabs)


def _attn_kernel(*refs, seg_blocks, tq):
    nseg = len(seg_blocks)
    q_ref = refs[0]
    kv_refs = refs[1:1 + 2 * nseg]
    o_ref = refs[1 + 2 * nseg]
    carry = tuple((jnp.full((tq, 1), NEG, F32), jnp.zeros((tq, MLA_HP), F32)) for _ in range(MLA_HEADS))
    for si, (tk, nblk) in enumerate(seg_blocks):
        k_ref, v_ref = kv_refs[2 * si], kv_refs[2 * si + 1]

        def body(j, carry, k_ref=k_ref, v_ref=v_ref, tk=tk):
            start = 0 if isinstance(j, int) else pl.multiple_of(j * tk, tk)
            new = []
            for hh in range(MLA_HEADS):
                sl = slice(hh * MLA_HP, (hh + 1) * MLA_HP)
                m, acc = carry[hh]
                s = _dot_nt(q_ref[0, :, sl], k_ref[0, pl.ds(start, tk), sl])
                m_new = jnp.maximum(m, jnp.max(s, axis=-1, keepdims=True))
                p = jnp.exp2(s - m_new).astype(BF16)
                acc = jnp.exp2(m - m_new) * acc + _dot(p, v_ref[0, pl.ds(start, tk), sl])
                new.append((m_new, acc))
            return tuple(new)

        carry = body(0, carry) if nblk == 1 else lax.fori_loop(0, nblk, body, carry)
    lane = _iota((tq, MLA_HP), 1)
    pairs = []
    for hh in range(0, MLA_HEADS, 2):
        acc_e, acc_o = carry[hh][1], carry[hh + 1][1]
        l_e = jnp.sum(jnp.where(lane == MLA_HP - 1, acc_e, 0.0), axis=-1, keepdims=True)
        l_o = jnp.sum(jnp.where(lane == 0, acc_o, 0.0), axis=-1, keepdims=True)
        pairs.append(jnp.where(lane < MLA_V, acc_e / l_e, acc_o / l_o))
    o_ref[0] = jnp.concatenate(pairs, axis=-1).astype(BF16)


def _attention(q, segs, tq, tk):
    b, l, _ = q.shape
    seg_blocks = []
    in_specs = [pl.BlockSpec((1, tq, MLA_HEADS * MLA_HP), lambda i, j: (i, j, 0))]
    args = [q]
    for k, v in segs:
        s = k.shape[1]
        t = min(tk, s)
        seg_blocks.append((t, s // t))
        for _ in range(2):
            in_specs.append(pl.BlockSpec((1, s, MLA_HEADS * MLA_HP), lambda i, j: (i, 0, 0),
                                         pipeline_mode=pl.Buffered(1)))
        args += [k, v]
    return pl.pallas_call(
        functools.partial(_attn_kernel, seg_blocks=tuple(seg_blocks), tq=tq),
        grid=(b, l // tq),
        in_specs=in_specs,
        out_specs=pl.BlockSpec((1, tq, MLA_HEADS * MLA_V), lambda i, j: (i, j, 0)),
        out_shape=jax.ShapeDtypeStruct((b, l, MLA_HEADS * MLA_V), BF16),
        compiler_params=_cparams(("parallel", "arbitrary")),
        name="attention",
    )(*args)


def _ret_kernel(x_ref, lgl_ref, lgh_ref, init_ref, y_ref, fin_ref, s_ref, g_ref, gbuf_ref, *, nblk, sub):
    C, H, N, P = CHUNK, RET_HEADS, RET_QK, RET_V
    HP = H * P
    ph, c = pl.program_id(1), pl.program_id(2)
    blk = jnp.where(ph == 0, nblk - 1 - c, c)

    def log_sigmoid(v):
        return jnp.minimum(v, 0.0) - jnp.log(1.0 + jnp.exp(-jnp.abs(v)))

    lgl = log_sigmoid(lgl_ref[...])
    lgf, lgb = lgl[0:1], lgl[1:2]
    rows = _iota((C, HP), 0).astype(F32)
    head_n = _iota((N, HP), 1) // P

    def diag_blocks(r):
        out = r[0:N]
        for hh in range(1, H):
            out = jnp.where(head_n == hh, r[hh * N:(hh + 1) * N], out)
        return out

    def as_rows(s_c):
        return jnp.concatenate([jnp.where(head_n == hh, s_c, 0.0) for hh in range(H)], axis=0)

    @pl.when((ph == 0) & (c == 0))
    def _():
        g_ref[...] = init_ref[0, 1]

    @pl.when(ph == 0)
    def _():
        w_b = jnp.exp(rows * lgb)
        tot_b = jnp.exp(C * lgb)
        for sb in reversed(range(sub)):
            r0 = sb * C
            k = x_ref[0, r0:r0 + C, HP:2 * HP].astype(F32)
            v = x_ref[0, r0:r0 + C, 2 * HP:3 * HP]
            g = g_ref[...]
            gbuf_ref[blk * sub + sb] = g
            g_ref[...] = g * tot_b + diag_blocks(_dot_tn((k * w_b).astype(BF16), v))

        @pl.when(c == nblk - 1)
        def _():
            fin_ref[0, 1] = g_ref[...]

    @pl.when((ph == 1) & (c == 0))
    def _():
        s_ref[...] = init_ref[0, 0]

    @pl.when(ph == 1)
    def _():
        lgh = log_sigmoid(lgh_ref[...])
        grp = _iota((C, HP), 1) // P
        di = (_iota((C, C), 0) - _iota((C, C), 1)).astype(F32)
        decay = [jnp.exp(jnp.where(di >= 0, di * lgh[hh:hh + 1, :], -di * lgh[H + hh:H + hh + 1, :]))
                 for hh in range(H)]
        e_f = jnp.exp((rows + 1.0) * lgf)
        e_b = jnp.exp((C - rows) * lgb)
        w_f = jnp.exp((C - 1.0 - rows) * lgf)
        tot_f = jnp.exp(C * lgf)
        for sb in range(sub):
            r0 = sb * C
            q = x_ref[0, r0:r0 + C, 0:HP]
            kb = x_ref[0, r0:r0 + C, HP:2 * HP]
            v = x_ref[0, r0:r0 + C, 2 * HP:3 * HP]
            gate = x_ref[0, r0:r0 + C, 3 * HP:4 * HP].astype(F32)
            ws, vs = [], []
            for hh in range(H):
                sc = _dot_nt(jnp.where(grp == hh, q, jnp.zeros_like(q)), kb)
                ws.append((sc * decay[hh]).astype(BF16))
                vs.append(jnp.where(grp == hh, v, jnp.zeros_like(v)))
            y = _dot(jnp.concatenate(ws, axis=1), jnp.concatenate(vs, axis=0))
            qf32 = q.astype(F32)
            s = s_ref[...]
            g_in = gbuf_ref[blk * sub + sb]
            y = y + _dot(jnp.concatenate([(qf32 * e_f).astype(BF16), (qf32 * e_b).astype(BF16)], axis=1),
                         jnp.concatenate([as_rows(s), as_rows(g_in)], axis=0).astype(BF16))
            s_ref[...] = s * tot_f + diag_blocks(_dot_tn((kb.astype(F32) * w_f).astype(BF16), v))

            mu = jnp.zeros_like(y)
            for hh in range(H):
                mh = jnp.sum(jnp.where(grp == hh, y, 0.0), axis=-1, keepdims=True) * (1.0 / P)
                mu = jnp.where(grp == hh, mh, mu)
            yc = y - mu
            var = jnp.zeros_like(y)
            for hh in range(H):
                vh = jnp.sum(jnp.where(grp == hh, yc * yc, 0.0), axis=-1, keepdims=True) * (1.0 / P)
                var = jnp.where(grp == hh, vh, var)
            y_ref[0, r0:r0 + C, :] = (yc * lax.rsqrt(var + EPS) * _silu(gate)).astype(BF16)

        @pl.when(c == nblk - 1)
        def _():
            fin_ref[0, 0] = s_ref[...]


def _block_index(nblk):
    return lambda i, ph, c: (i, (1 - ph) * (nblk - 1 - c) + ph * c, 0)


def _scan_sub(t):
    nc = t // CHUNK
    return 4 if nc % 4 == 0 else (2 if nc % 2 == 0 else 1)


def _ret_scan(ret, lw, init):
    b, t, _ = ret.shape
    sub = _scan_sub(t)
    rows = sub * CHUNK
    nblk = t // rows
    hp = RET_HEADS * RET_V
    return pl.pallas_call(
        functools.partial(_ret_kernel, nblk=nblk, sub=sub),
        grid=(b, 2, nblk),
        in_specs=[pl.BlockSpec((1, rows, W_RET), _block_index(nblk)),
                  pl.BlockSpec((2, hp), lambda i, ph, c: (0, 0)),
                  pl.BlockSpec((2 * RET_HEADS, CHUNK), lambda i, ph, c: (0, 0)),
                  pl.BlockSpec((1, 2, RET_QK, hp), lambda i, ph, c: (i, 0, 0, 0))],
        out_specs=[pl.BlockSpec((1, rows, hp), lambda i, ph, c: (i, ph * c, 0)),
                   pl.BlockSpec((1, 2, RET_QK, hp), lambda i, ph, c: (i, 0, 0, 0))],
        out_shape=[jax.ShapeDtypeStruct((b, t, hp), BF16),
                   jax.ShapeDtypeStruct((b, 2, RET_QK, hp), F32)],
        scratch_shapes=[pltpu.VMEM((RET_QK, hp), F32),
                        pltpu.VMEM((RET_QK, hp), F32),
                        pltpu.VMEM((t // CHUNK, RET_QK, hp), F32)],
        compiler_params=_cparams(("parallel", "arbitrary", "arbitrary")),
        name="retention_scan",
    )(ret, lw["ret_lgl"], lw["ret_lgh"], init)


def _ssd_kernel(x_ref, xp_ref, xn_ref, z_ref, dt_ref, cw_ref, cb_ref, dtb_ref, alog_ref, dvec_ref, nw_ref,
                init_ref, y_ref, fin_ref, s_ref, g_ref, gbuf_ref, act_ref, dtc_ref, abc_ref, *, nblk, sub):
    C, H, N, P, G = CHUNK, SSD_HEADS, SSD_STATE, SSD_HEADDIM, SSD_GROUPS
    HP = H * P
    HPG = H // G
    HALO = SSD_HALO
    ph, c = pl.program_id(1), pl.program_id(2)
    blk = jnp.where(ph == 0, nblk - 1 - c, c)

    ri = _iota((C, C + 2 * HALO), 0)
    ci = _iota((C, C + 2 * HALO), 1)
    sel_prev = (ci == ri + HALO - 1).astype(BF16)
    sel_next = (ci == ri + HALO + 1).astype(BF16)
    lane = _iota((LANES, HP), 0)
    head_of = _iota((LANES, HP), 1) // P
    e_f = (lane == head_of).astype(BF16)
    e_b = (lane == head_of + H).astype(BF16)
    r_i = _iota((C, C), 0)
    c_i = _iota((C, C), 1)
    low = (r_i >= c_i).astype(BF16)
    upp = (r_i <= c_i).astype(BF16)
    grp_n = _iota((N, HP), 1) // (HPG * P)
    neg_a = -jnp.exp(alog_ref[...])
    cw = cw_ref[...]

    def prep(sb, first_pass):
        r0 = sb * C
        if first_pass:
            xc = x_ref[0, r0:r0 + C, :]
            if sb == 0:
                prev = jnp.where(blk > 0, xp_ref[0], jnp.zeros_like(xp_ref[0]))
            else:
                prev = x_ref[0, r0 - HALO:r0, :]
            if sb == sub - 1:
                nxt = jnp.where(blk < nblk - 1, xn_ref[0], jnp.zeros_like(xn_ref[0]))
            else:
                nxt = x_ref[0, r0 + C:r0 + C + HALO, :]
            stacked = jnp.concatenate([prev, xc, nxt], axis=0)
            conv = (cw[0:1] * _dot(sel_prev, stacked) + cw[1:2] * xc.astype(F32)
                    + cw[2:3] * _dot(sel_next, stacked) + cb_ref[...])
            act = _silu(conv)
            raw = dt_ref[0, r0:r0 + C, :] + dtb_ref[...]
            dt = jnp.maximum(raw, 0.0) + jnp.log(1.0 + jnp.exp(-jnp.abs(raw)))
            la = dt * neg_a
            a_f = None
            a_b = _dot_exact_lhs01(upp, la)
            act_ref[blk * sub + sb] = act
            dtc_ref[blk * sub + sb] = dt
            abc_ref[blk * sub + sb] = a_b
        else:
            act = act_ref[blk * sub + sb]
            dt = dtc_ref[blk * sub + sb]
            a_b = abc_ref[blk * sub + sb]
            la = dt * neg_a
            a_f = _dot_exact_lhs01(low, la)
        xs = act[:, 0:HP]
        bm = act[:, HP:HP + G * N].astype(BF16)
        cm = act[:, HP + G * N:HP + 2 * G * N].astype(BF16)
        return xs, bm, cm, dt, la, a_f, a_b

    def state_update(state, bm, xs, wdt, tot8, e_dir, row):
        xw = (xs * _dot(wdt.astype(BF16), e_dir)).astype(BF16)
        tot = _dot_exact_rhs01(tot8, e_dir)[row:row + 1]
        r = _dot_tn(bm, xw)
        upd = r[0:N]
        for gg in range(1, G):
            upd = jnp.where(grp_n == gg, r[gg * N:(gg + 1) * N], upd)
        return state * tot + upd

    def as_rows(s_c):
        return jnp.concatenate([jnp.where(grp_n == gg, s_c, 0.0) for gg in range(G)], axis=0).astype(BF16)

    @pl.when((ph == 0) & (c == 0))
    def _():
        g_ref[...] = init_ref[0, 1]

    @pl.when(ph == 0)
    def _():
        for sb in reversed(range(sub)):
            xs, bm, _, dt, _, _, a_b = prep(sb, True)
            g = g_ref[...]
            gbuf_ref[blk * sub + sb] = g
            wdt = jnp.exp(a_b[0:1, :] - a_b) * dt
            g_ref[...] = state_update(g, bm, xs, wdt, jnp.exp(a_b[0:8, :]), e_b, 0)

        @pl.when(c == nblk - 1)
        def _():
            fin_ref[0, 1] = g_ref[...]

    @pl.when((ph == 1) & (c == 0))
    def _():
        s_ref[...] = init_ref[0, 0]

    @pl.when(ph == 1)
    def _():
        grp_l = _iota((C, G * N), 1) // N
        head_l = _iota((C, HP), 1) // P
        tri_f = r_i >= c_i
        diag = r_i == c_i
        for sb in range(sub):
            r0 = sb * C
            xs, bm, cm, dt, la, a_f, a_b = prep(sb, False)
            la_t = la.T[0:2 * H]
            dt_t = dt.T[0:2 * H]
            af_row = _dot_exact_rhs01(la_t, upp)
            ab_row = _dot_exact_rhs01(la_t, low)
            sc = [_dot_nt(jnp.where(grp_l == gg, cm, jnp.zeros_like(cm)), bm) for gg in range(G)]
            xsb = xs.astype(BF16)
            ws, vs = [], []
            for hh in range(H):
                seg_f = a_f[:, hh:hh + 1] - af_row[hh:hh + 1, :]
                seg_b = a_b[:, H + hh:H + hh + 1] - ab_row[H + hh:H + hh + 1, :]
                dt_f, dt_b = dt_t[hh:hh + 1, :], dt_t[H + hh:H + hh + 1, :]
                dm = (jnp.exp(jnp.where(tri_f, seg_f, seg_b)) * jnp.where(tri_f, dt_f, dt_b)
                      + jnp.where(diag, dt_b, 0.0))
                ws.append((sc[hh // HPG] * dm).astype(BF16))
                vs.append(jnp.where(head_l == hh, xsb, jnp.zeros_like(xsb)))
            y = _dot(jnp.concatenate(ws, axis=1), jnp.concatenate(vs, axis=0))

            s = s_ref[...]
            g_in = gbuf_ref[blk * sub + sb]
            y = y + _dot(jnp.exp(a_f).astype(BF16), e_f) * _dot(cm, as_rows(s))
            y = y + _dot(jnp.exp(a_b).astype(BF16), e_b) * _dot(cm, as_rows(g_in))

            wdt = jnp.exp(a_f[C - 1:C, :] - a_f) * dt
            s_ref[...] = state_update(s, bm, xs, wdt, jnp.exp(a_f[C - 8:C, :]), e_f, 7)

            y = y + xs * dvec_ref[...]
            y = y * _silu(z_ref[0, r0:r0 + C, :].astype(F32))
            y_ref[0, r0:r0 + C, :] = _rms(y, nw_ref[...]).astype(BF16)

        @pl.when(c == nblk - 1)
        def _():
            fin_ref[0, 0] = s_ref[...]


SSD_HALO = 16


def _ssd_scan(xbc, z, dt, lw, init):
    b, t, _ = xbc.shape
    sub = _scan_sub(t)
    rows = sub * CHUNK
    nblk = t // rows
    hp = SSD_INNER
    per = rows // SSD_HALO
    nhalo = t // SSD_HALO
    bidx = _block_index(nblk)

    def x_idx(i, ph, c):
        return (i, (1 - ph) * (nblk - 1 - c), 0)

    def prev_idx(i, ph, c):
        return (i, jnp.maximum((1 - ph) * (nblk - 1 - c) * per - 1, 0), 0)

    def next_idx(i, ph, c):
        return (i, jnp.minimum(((1 - ph) * (nblk - 1 - c) + 1) * per, nhalo - 1), 0)

    const = lambda shape: pl.BlockSpec(shape, lambda i, ph, c: (0,) * len(shape))
    return pl.pallas_call(
        functools.partial(_ssd_kernel, nblk=nblk, sub=sub),
        grid=(b, 2, nblk),
        in_specs=[pl.BlockSpec((1, rows, W_XBC), x_idx),
                  pl.BlockSpec((1, SSD_HALO, W_XBC), prev_idx),
                  pl.BlockSpec((1, SSD_HALO, W_XBC), next_idx),
                  pl.BlockSpec((1, rows, W_Z), bidx),
                  pl.BlockSpec((1, rows, W_DT), x_idx),
                  const((8, W_XBC)), const((1, W_XBC)), const((1, LANES)), const((1, LANES)),
                  const((1, hp)), const((1, hp)),
                  pl.BlockSpec((1, 2, SSD_STATE, hp), lambda i, ph, c: (i, 0, 0, 0))],
        out_specs=[pl.BlockSpec((1, rows, hp), lambda i, ph, c: (i, ph * c, 0)),
                   pl.BlockSpec((1, 2, SSD_STATE, hp), lambda i, ph, c: (i, 0, 0, 0))],
        out_shape=[jax.ShapeDtypeStruct((b, t, hp), BF16),
                   jax.ShapeDtypeStruct((b, 2, SSD_STATE, hp), F32)],
        scratch_shapes=[pltpu.VMEM((SSD_STATE, hp), F32),
                        pltpu.VMEM((SSD_STATE, hp), F32),
                        pltpu.VMEM((t // CHUNK, SSD_STATE, hp), F32),
                        pltpu.VMEM((t // CHUNK, CHUNK, W_XBC), F32),
                        pltpu.VMEM((t // CHUNK, CHUNK, W_DT), F32),
                        pltpu.VMEM((t // CHUNK, CHUNK, W_DT), F32)],
        compiler_params=_cparams(("parallel", "arbitrary", "arbitrary")),
        name="ssd_scan",
    )(xbc, xbc, xbc, z, dt, lw["conv_w"], lw["conv_b"], lw["dt_bias"], lw["a_log"], lw["d_vec"], lw["ssd_nw"], init)


def _mix_residual(x_ref, att_ref, ret_ref, ssd_ref, mod_ref, n2_ref, wo_ref):
    mix = jnp.concatenate([att_ref[0], ret_ref[0], ssd_ref[0]], axis=-1)
    x1 = x_ref[0] + mod_ref[0, 2:3, :] * _dot(mix, wo_ref[...])
    h = _rms(x1, n2_ref[...]) * (1.0 + mod_ref[0, 4:5, :]) + mod_ref[0, 3:4, :]
    return x1, h


def _dense_kernel(x_ref, att_ref, ret_ref, ssd_ref, mod_ref, n2_ref, wo_ref, wg_ref, wu_ref, wd_ref, o_ref,
                  *, ff_chunk):
    x1, h = _mix_residual(x_ref, att_ref, ret_ref, ssd_ref, mod_ref, n2_ref, wo_ref)
    hb = h.astype(BF16)
    ff = wg_ref.shape[1]
    f = jnp.zeros_like(x1)
    for s in range(0, ff, ff_chunk):
        g = _dot(hb, wg_ref[:, s:s + ff_chunk])
        u = _dot(hb, wu_ref[:, s:s + ff_chunk])
        f = f + _dot((_silu(g) * u).astype(BF16), wd_ref[s:s + ff_chunk, :])
    o_ref[0] = x1 + mod_ref[0, 5:6, :] * f


def _dense_block(xs, att, ret, ssd, mods, lw, tm):
    b, t, d = xs.shape
    ff = lw["wg"].shape[1]
    tok = lambda w: pl.BlockSpec((1, tm, w), lambda i, j: (i, j, 0))
    const = lambda shape: pl.BlockSpec(shape, lambda i, j: (0,) * len(shape), pipeline_mode=pl.Buffered(1))
    return pl.pallas_call(
        functools.partial(_dense_kernel, ff_chunk=ff // 2),
        grid=(b, t // tm),
        in_specs=[tok(d), tok(att.shape[-1]), tok(ret.shape[-1]), tok(ssd.shape[-1]),
                  pl.BlockSpec((1, 8, d), lambda i, j: (i, 0, 0)), const((1, d)),
                  const((d, d)), const((d, ff)), const((d, ff)), const((ff, d))],
        out_specs=tok(d),
        out_shape=jax.ShapeDtypeStruct((b, t, d), F32),
        compiler_params=_cparams(("parallel", "parallel")),
        name="outproj_ffn",
    )(xs, att, ret, ssd, mods, lw["n2"], lw["wo"], lw["wg"], lw["wu"], lw["wd"])


SUBLANES = 8


def _route_kernel(x_ref, att_ref, ret_ref, ssd_ref, mod_ref, n2_ref, wo_ref, wr_ref,
                  x1_ref, h_ref, slot_ref, gate_ref, cnt_ref):
    x1, h = _mix_residual(x_ref, att_ref, ret_ref, ssd_ref, mod_ref, n2_ref, wo_ref)
    x1_ref[0] = x1
    tm = h.shape[0]
    h_hi = h.astype(BF16)
    h_ref[0] = h_hi
    h_lo = (h - h_hi.astype(F32)).astype(BF16)
    logits = _dot(h_hi, wr_ref[0]) + _dot(h_lo, wr_ref[0]) + _dot(h_hi, wr_ref[1])
    lane = _iota(logits.shape, 1).astype(F32)
    logits = jnp.where(lane < N_EXPERTS, logits, NEG)
    m1 = jnp.max(logits, axis=-1, keepdims=True)
    i1 = jnp.min(jnp.where(logits == m1, lane, float(LANES)), axis=-1, keepdims=True)
    rest = jnp.where(lane == i1, NEG, logits)
    m2 = jnp.max(rest, axis=-1, keepdims=True)
    i2 = jnp.min(jnp.where(rest == m2, lane, float(LANES)), axis=-1, keepdims=True)
    e2 = jnp.exp(m2 - m1)
    den = 1.0 + e2
    sel1, sel2 = lane == i1, lane == i2
    chosen = jnp.where(sel1 | sel2, 1.0, 0.0)
    before = (_iota((tm, tm), 0) > _iota((tm, tm), 1)).astype(BF16)
    pos = _dot(before, chosen.astype(BF16))
    cnt = jnp.sum(chosen, axis=0, keepdims=True)
    nblk = sum(jnp.where(cnt > float(k * MOE_ROWS), 1.0, 0.0) for k in range(pl.cdiv(tm, MOE_ROWS)))
    lower_e = (_iota((LANES, LANES), 0) < _iota((LANES, LANES), 1)).astype(BF16)
    off = _dot_exact_rhs01(jnp.broadcast_to(nblk * float(MOE_ROWS), (SUBLANES, LANES)), lower_e)[0:1]
    slot = off + pos
    s1 = jnp.sum(jnp.where(sel1, slot, 0.0), axis=-1, keepdims=True)
    s2 = jnp.sum(jnp.where(sel2, slot, 0.0), axis=-1, keepdims=True)
    lane_i = _iota(logits.shape, 1)
    slots = jnp.where(lane_i == 0, s1, jnp.where(lane_i == 1, s2, -1.0))
    slot_ref[0, 0] = slots.T[0:SUBLANES].astype(jnp.int32)
    gate_ref[0] = jnp.where(sel1, 1.0 / den, 0.0) + jnp.where(sel2, e2 / den, 0.0)
    row = _iota((SUBLANES, LANES), 0)
    cnt_ref[0, 0] = jnp.where(row == 0, cnt, jnp.where(row == 1, off, jnp.where(row == 2, nblk, 0.0))
                              ).astype(jnp.int32)


def _route_block(xs, att, ret, ssd, mods, lw, tm):
    b, t, d = xs.shape
    nt = t // tm
    tok = lambda w: pl.BlockSpec((1, tm, w), lambda i, j: (i, j, 0))
    const = lambda shape: pl.BlockSpec(shape, lambda i, j: (0,) * len(shape))
    per_tile = lambda r, w: pl.BlockSpec((1, 1, r, w), lambda i, j: (i, j, 0, 0))
    return pl.pallas_call(
        _route_kernel,
        grid=(b, nt),
        in_specs=[tok(d), tok(att.shape[-1]), tok(ret.shape[-1]), tok(ssd.shape[-1]),
                  pl.BlockSpec((1, 8, d), lambda i, j: (i, 0, 0)), const((1, d)),
                  const((d, d)), const((2, d, LANES))],
        out_specs=[tok(d), tok(d), per_tile(SUBLANES, tm), tok(LANES), per_tile(SUBLANES, LANES)],
        out_shape=[jax.ShapeDtypeStruct((b, t, d), F32), jax.ShapeDtypeStruct((b, t, d), BF16),
                   jax.ShapeDtypeStruct((b, nt, SUBLANES, tm), jnp.int32),
                   jax.ShapeDtypeStruct((b, t, LANES), F32),
                   jax.ShapeDtypeStruct((b, nt, SUBLANES, LANES), jnp.int32)],
        compiler_params=_cparams(("parallel", "parallel")),
        name="outproj_route",
    )(xs, att, ret, ssd, mods, lw["n2"], lw["wo"], lw["router"])


MOE_ROWS = 256


def _moe_kernel(cnt_ref, slot_ref, h_ref, gate_ref, x1_ref, mod_ref, wgu_ref, wd_ref, fn_ref, o_ref,
                *, final_norm, ff_split):
    e = pl.program_id(2)
    rb = MOE_ROWS
    tm = h_ref.shape[1]

    @pl.when(e == 0)
    def _():
        o_ref[...] = jnp.zeros_like(o_ref)

    off = cnt_ref[0, 0, 1, e]
    nblocks = cnt_ref[0, 0, 2, e]
    slot1 = slot_ref[0, 0, 0:1, :]
    slot2 = slot_ref[0, 0, 1:2, :]
    gates = gate_ref[0]
    ge = jnp.sum(jnp.where(_iota(gates.shape, 1) == e, gates, 0.0), axis=-1, keepdims=True)

    def block(bi, carry):
        rows = off + bi * rb + _iota((rb, tm), 0)
        sel = jnp.where((rows == slot1) | (rows == slot2), 1.0, 0.0).astype(BF16)
        lhs = _dot(sel, h_ref[0]).astype(BF16)
        y = None
        for s0, s1 in ff_split:
            w = s1 - s0
            gu = _dot(lhs, wgu_ref[0, :, 2 * s0:2 * s1])
            part = _dot((_silu(gu[:, :w]) * gu[:, w:]).astype(BF16), wd_ref[0, s0:s1, :])
            y = part if y is None else y + part
        o_ref[0] += ge * _dot_tn(sel, y.astype(BF16))
        return carry

    lax.fori_loop(0, nblocks, block, 0)

    @pl.when(e == N_EXPERTS - 1)
    def _():
        out = x1_ref[0] + mod_ref[0, 5:6, :] * o_ref[0]
        o_ref[0] = _rms(out, fn_ref[...]) if final_norm else out


MOE_FF_SLICE = 512


def _moe_ff_split(ff):
    return tuple((s, min(s + MOE_FF_SLICE, ff)) for s in range(0, ff, MOE_FF_SLICE))


def _moe_weights(w_gate, w_up, w_down):
    ff = w_gate.shape[-1]
    wgu = jnp.concatenate([w[..., s0:s1] for s0, s1 in _moe_ff_split(ff) for w in (w_gate, w_up)], axis=-1)
    return wgu.astype(BF16), w_down.astype(BF16)


def _moe_block(x1, h, slots, gates, counts, mods, lw, fnw, final_norm, tm):
    b, t, d = x1.shape
    ff = lw["moe_wd"].shape[1]
    assert MOE_ROWS % (2 * SUBLANES) == 0
    tok = lambda w: pl.BlockSpec((1, tm, w), lambda i, j, e: (i, j, 0))
    per_tile = lambda w, **kw: pl.BlockSpec((1, 1, SUBLANES, w), lambda i, j, e: (i, j, 0, 0), **kw)
    return pl.pallas_call(
        functools.partial(_moe_kernel, final_norm=final_norm, ff_split=_moe_ff_split(ff)),
        grid=(b, t // tm, N_EXPERTS),
        in_specs=[per_tile(LANES, memory_space=pltpu.SMEM), per_tile(tm), tok(d), tok(LANES), tok(d),
                  pl.BlockSpec((1, 8, d), lambda i, j, e: (i, 0, 0)),
                  pl.BlockSpec((1, d, 2 * ff), lambda i, j, e: (e, 0, 0)),
                  pl.BlockSpec((1, ff, d), lambda i, j, e: (e, 0, 0)),
                  pl.BlockSpec((1, d), lambda i, j, e: (0, 0))],
        out_specs=tok(d),
        out_shape=jax.ShapeDtypeStruct((b, t, d), F32),
        compiler_params=_cparams(("parallel", "parallel", "arbitrary")),
        name="moe",
    )(counts, slots, h, gates, x1, mods, lw["moe_wgu"], lw["moe_wd"], fnw)


def _final_norm_kernel(x_ref, w_ref, o_ref):
    o_ref[0] = _rms(x_ref[0], w_ref[...])


def _final_norm(xs, fnw, tm):
    b, t, d = xs.shape
    return pl.pallas_call(
        _final_norm_kernel,
        grid=(b, t // tm),
        in_specs=[pl.BlockSpec((1, tm, d), lambda i, j: (i, j, 0)), pl.BlockSpec((1, d), lambda i, j: (0, 0))],
        out_specs=pl.BlockSpec((1, tm, d), lambda i, j: (i, j, 0)),
        out_shape=jax.ShapeDtypeStruct((b, t, d), F32),
        compiler_params=_cparams(("parallel", "parallel")),
        name="final_norm",
    )(xs, fnw)


def _rot_cols(w):
    a, b2, c, d = jnp.split(w, 4, axis=-1)
    return jnp.concatenate([-b2, a, -d, c], axis=-1)


def _layer_weights(i, norm1_w, norm2_w, w_in, w_out, mla_q_norm_w, mla_w_qb, mla_kv_norm_w, mla_w_kb, mla_w_vb,
                   ret_decay_logit, ssd_conv_w, ssd_conv_b, ssd_dt_bias, ssd_a_log, ssd_d, ssd_norm_w):
    d = w_in.shape[1]
    offs = np.cumsum((0,) + IN_SIZES)
    wi = w_in[i]
    zeros = lambda n: jnp.zeros((d, n), F32)
    kpe = wi[:, offs[2]:offs[3]]
    w_mla = jnp.concatenate([wi[:, offs[0]:offs[3]], _rot_cols(kpe),
                             zeros(W_MLA - MLA_Q_LORA - MLA_KV_LORA - 2 * MLA_ROPE)], axis=1).astype(BF16)
    w_ret = wi[:, offs[3]:offs[7]].astype(BF16)
    w_z = wi[:, offs[7]:offs[8]].astype(BF16)
    w_xbc = wi[:, offs[8]:offs[9]].astype(BF16)
    w_dt = jnp.concatenate([wi[:, offs[9]:offs[11]], zeros(W_DT - 2 * SSD_HEADS)], axis=1).astype(BF16)

    dh = MLA_NOPE + MLA_ROPE
    wqb = mla_w_qb[i].reshape(MLA_Q_LORA, MLA_HEADS, dh)
    pad = lambda a, lo, hi: jnp.pad(a, ((0, 0), (0, 0), (lo, hi)))
    wq = pad(wqb, 0, MLA_HP - dh).reshape(MLA_Q_LORA, MLA_HEADS * MLA_HP).astype(BF16)
    wqr = pad(_rot_cols(wqb[:, :, MLA_NOPE:]), MLA_NOPE, MLA_HP - dh).reshape(MLA_Q_LORA, MLA_HEADS * MLA_HP).astype(BF16)
    wkb = pad(mla_w_kb[i].reshape(MLA_KV_LORA, MLA_HEADS, MLA_NOPE), 0, MLA_HP - MLA_NOPE)
    place = np.zeros((MLA_ROPE, MLA_HEADS, MLA_HP), np.float32)
    for r in range(MLA_ROPE):
        place[r, :, MLA_NOPE + r] = 1.0
    place = jnp.asarray(place)
    wk = jnp.concatenate([wkb, place, place, jnp.zeros((MLA_KV_LORA - 2 * MLA_ROPE, MLA_HEADS, MLA_HP), F32)],
                         axis=0).reshape(2 * MLA_KV_LORA, MLA_HEADS * MLA_HP).astype(BF16)

    wvb = mla_w_vb[i].reshape(MLA_KV_LORA, MLA_HEADS, MLA_V)
    wv = jnp.concatenate(
        [pad(wvb[:, hh:hh + 1], (hh % 2) * MLA_V, MLA_HP - MLA_V - (hh % 2) * MLA_V) for hh in range(MLA_HEADS)],
        axis=1).reshape(MLA_KV_LORA, MLA_HEADS * MLA_HP).astype(BF16)

    lg = ret_decay_logit[i]
    padl = lambda v: jnp.pad(v.reshape(1, -1), ((0, 0), (0, LANES - 2 * SSD_HEADS)))
    return {
        "n1": norm1_w[i].reshape(1, d), "n2": norm2_w[i].reshape(1, d),
        "w_mla": w_mla, "w_ret": w_ret, "w_z": w_z, "w_xbc": w_xbc, "w_dt": w_dt,
        "qn": mla_q_norm_w[i].reshape(1, -1), "wq": wq, "wqr": wqr,
        "kvn": mla_kv_norm_w[i].reshape(1, -1), "wk": wk, "wv": wv,
        "wo": w_out[i].astype(BF16),
        "ret_lgl": jnp.repeat(lg, RET_V, axis=1),
        "ret_lgh": jnp.broadcast_to(lg.reshape(2 * RET_HEADS, 1), (2 * RET_HEADS, CHUNK)),
        "conv_w": jnp.pad(ssd_conv_w[i], ((0, 8 - ssd_conv_w.shape[1]), (0, 0))),
        "conv_b": ssd_conv_b[i].reshape(1, -1),
        "dt_bias": padl(ssd_dt_bias[i]), "a_log": padl(ssd_a_log[i]),
        "d_vec": jnp.repeat(ssd_d[i], SSD_HEADDIM).reshape(1, -1),
        "ssd_nw": ssd_norm_w[i].reshape(1, -1),
    }


def _rope_tables(n_lat, n_ctx):
    f32 = np.float32
    rows = n_lat // GRID_W
    row = np.repeat(np.arange(rows, dtype=f32), GRID_W)
    col = np.tile(np.arange(GRID_W, dtype=f32), rows)
    inv_freq = (f32(ROPE_BASE) ** (-np.arange(0, ROPE_AXIS_DIM, 2, dtype=f32) / f32(ROPE_AXIS_DIM))).astype(f32)
    ang_r = row[:, None] * inv_freq
    ang_c = col[:, None] * inv_freq
    ang = np.concatenate([ang_r, ang_r, ang_c, ang_c], axis=-1).astype(f32)
    c2 = f32(MLA_SCALE * math.log2(math.e))

    def build(cos, sin):
        n = cos.shape[0]
        tail = np.zeros((n, MLA_HP - MLA_NOPE - MLA_ROPE), f32)
        cosq = np.concatenate([np.full((n, MLA_NOPE), c2, f32), c2 * cos, tail], 1)
        sinq = np.concatenate([np.zeros((n, MLA_NOPE), f32), c2 * sin, tail], 1)
        ck = np.concatenate([cos, sin, np.zeros((n, LANES - 2 * MLA_ROPE), f32)], 1)
        return tuple(jnp.asarray(a, F32) for a in (cosq, sinq, ck))

    lat = build(np.cos(ang).astype(f32), np.sin(ang).astype(f32))
    ctx = build(np.ones((n_ctx, MLA_ROPE), f32), np.zeros((n_ctx, MLA_ROPE), f32))
    return lat, ctx


def _token_tile(t, pref):
    tm = min(pref, t)
    while t % tm:
        tm //= 2
    return tm


def kernel(x, c, ctx, c_ctx, mod_w, mod_b, norm1_w, norm2_w, w_in, w_out, mla_q_norm_w, mla_w_qb, mla_kv_norm_w, mla_w_kb, mla_w_vb, ret_decay_logit, ssd_conv_w, ssd_conv_b, ssd_dt_bias, ssd_a_log, ssd_d, ssd_norm_w, ffn_w_gate, ffn_w_up, ffn_w_down, moe_router, moe_w_gate, moe_w_up, moe_w_down, final_norm_w):
    b, n_lat, d = x.shape
    n_ctx = ctx.shape[1]
    depth = mod_w.shape[0]
    assert b + 1 <= 8 and n_lat % CHUNK == 0 and n_ctx % CHUNK == 0 and n_lat % GRID_W == 0

    cond = jnp.concatenate([c, c_ctx[None, :], jnp.zeros((8 - b - 1, d), F32)], axis=0)
    mod_all = _modulation(cond, mod_w, mod_b)
    tabs_x, tabs_c = _rope_tables(n_lat, n_ctx)
    fnw = final_norm_w.reshape(1, d)

    tm_x = _token_tile(n_lat, 1024)
    tm_c = _token_tile(n_ctx, 256)
    tm_moe = _token_tile(n_lat, 1024)
    tq_x = _token_tile(n_lat, 1024)
    tk_x = 2048
    zero_ret = jnp.zeros((b, 2, RET_QK, RET_HEADS * RET_V), F32)
    zero_ssd = jnp.zeros((b, 2, SSD_STATE, SSD_INNER), F32)

    h_ctx = ctx
    for i in range(depth):
        last = i == depth - 1
        lw = _layer_weights(i, norm1_w, norm2_w, w_in, w_out, mla_q_norm_w, mla_w_qb, mla_kv_norm_w, mla_w_kb,
                            mla_w_vb, ret_decay_logit, ssd_conv_w, ssd_conv_b, ssd_dt_bias, ssd_a_log, ssd_d,
                            ssd_norm_w)
        m6 = mod_all[i].reshape(8, 6, d)
        pad2 = jnp.zeros((b, 2, d), F32)
        mods_x = jnp.concatenate([m6[:b], pad2], axis=1)
        mods_c = jnp.concatenate([jnp.broadcast_to(m6[b:b + 1], (b, 6, d)), pad2], axis=1)

        qc, kc, vc, retc, zc, xbcc, dtc = _inproj(h_ctx, mods_c, lw, tabs_c, tm_c)
        ret_yc, ret_fin = _ret_scan(retc, lw, zero_ret)
        ssd_yc, ssd_fin = _ssd_scan(xbcc, zc, dtc, lw, zero_ssd)

        qx, kx, vx, retx, zx, xbcx, dtx = _inproj(x, mods_x, lw, tabs_x, tm_x)
        att_x = _attention(qx, [(kc, vc), (kx, vx)], tq_x, tk_x)
        ret_yx, _ = _ret_scan(retx, lw, ret_fin)
        ssd_yx, _ = _ssd_scan(xbcx, zx, dtx, lw, ssd_fin)

        if i % 2 == 0:
            j = i // 2
            lw.update(wg=ffn_w_gate[j].astype(BF16), wu=ffn_w_up[j].astype(BF16), wd=ffn_w_down[j].astype(BF16))
            x = _dense_block(x, att_x, ret_yx, ssd_yx, mods_x, lw, tm_x)
            if last:
                x = _final_norm(x, fnw, tm_x)
        else:
            j = i // 2
            wgu, wd = _moe_weights(moe_w_gate[j], moe_w_up[j], moe_w_down[j])
            r_f32 = jnp.pad(moe_router[j], ((0, 0), (0, LANES - N_EXPERTS)))
            r_hi = r_f32.astype(BF16)
            r_lo = (r_f32 - r_hi.astype(F32)).astype(BF16)
            lw.update(router=jnp.stack([r_hi, r_lo]), moe_wgu=wgu, moe_wd=wd)
            x1, *routed = _route_block(x, att_x, ret_yx, ssd_yx, mods_x, lw, tm_moe)
            x = _moe_block(x1, *routed, mods_x, lw, fnw, last, tm_moe)

        if not last:
            att_c = _attention(qc, [(kc, vc)], tm_c, tk_x)
            if i % 2 == 0:
                h_ctx = _dense_block(h_ctx, att_c, ret_yc, ssd_yc, mods_c, lw, tm_c)
            else:
                c1, *routed = _route_block(h_ctx, att_c, ret_yc, ssd_yc, mods_c, lw, tm_c)
                h_ctx = _moe_block(c1, *routed, mods_c, lw, fnw, False, tm_c)
    return x
```

```python
import functools
import math

import jax
import jax.numpy as jnp
import numpy as np
from jax import lax
from jax.experimental import pallas as pl
from jax.experimental.pallas import tpu as pltpu

F32 = jnp.float32
BF16 = jnp.bfloat16
HIGHEST = lax.Precision.HIGHEST

EPS = 1e-6
CHUNK = 128
GRID_W = 64
MLA_HEADS = 4
MLA_Q_LORA = 256
MLA_KV_LORA = 128
MLA_NOPE = 64
MLA_ROPE = 32
MLA_V = 64
MLA_SCALE = (MLA_NOPE + MLA_ROPE) ** -0.5
ROPE_AXIS_DIM = MLA_ROPE // 2
ROPE_BASE = 10000.0
RET_HEADS = 4
RET_QK = 64
RET_V = 64
SSD_HEADS = 8
SSD_HEADDIM = 64
SSD_GROUPS = 2
SSD_STATE = 64
SSD_INNER = SSD_HEADS * SSD_HEADDIM
SSD_CONV_CH = SSD_INNER + 2 * SSD_GROUPS * SSD_STATE
N_EXPERTS = 8
IN_SIZES = (MLA_Q_LORA, MLA_KV_LORA, MLA_ROPE,
            RET_HEADS * RET_QK, RET_HEADS * RET_QK, RET_HEADS * RET_V, RET_HEADS * RET_V,
            SSD_INNER, SSD_CONV_CH, SSD_HEADS, SSD_HEADS)

LANES = 128
V7X_VMEM_LIMIT = 56 * 1024 * 1024

MLA_HP = 128
NEG = -1e30


def _cparams(sem):
    return pltpu.CompilerParams(dimension_semantics=sem, vmem_limit_bytes=V7X_VMEM_LIMIT)


def _dot(a, b):
    return jnp.dot(a, b, preferred_element_type=F32)


def _dot_nt(a, b):
    return lax.dot_general(a, b, (((1,), (1,)), ((), ())), preferred_element_type=F32)


def _dot_tn(a, b):
    return lax.dot_general(a, b, (((0,), (0,)), ((), ())), preferred_element_type=F32)


def _split3(x):
    hi = x.astype(BF16)
    r1 = x - hi.astype(F32)
    mid = r1.astype(BF16)
    lo = (r1 - mid.astype(F32)).astype(BF16)
    return hi, mid, lo


def _dot_exact_rhs01(x, e):
    hi, mid, lo = _split3(x)
    return _dot(hi, e) + _dot(mid, e) + _dot(lo, e)


def _dot_exact_lhs01(e, x):
    hi, mid, lo = _split3(x)
    return _dot(e, hi) + _dot(e, mid) + _dot(e, lo)


def _silu(x):
    return x / (1.0 + jnp.exp(-x))


def _rms(x, w):
    return x * lax.rsqrt(jnp.mean(x * x, axis=-1, keepdims=True) + EPS) * w


def _iota(shape, dim):
    return lax.broadcasted_iota(jnp.int32, shape, dim)


def _mod_kernel(a_ref, w_ref, b_ref, o_ref):
    k = pl.program_id(1)

    @pl.when(k == 0)
    def _():
        o_ref[0] = jnp.broadcast_to(b_ref[0], o_ref.shape[1:])

    o_ref[0] += jnp.dot(_silu(a_ref[...]), w_ref[0], precision=HIGHEST, preferred_element_type=F32)


def _modulation(cond, mod_w, mod_b):
    depth, d, n = mod_w.shape
    tk = 256
    return pl.pallas_call(
        _mod_kernel,
        grid=(depth, d // tk),
        in_specs=[pl.BlockSpec((8, tk), lambda i, k: (0, k)),
                  pl.BlockSpec((1, tk, n), lambda i, k: (i, k, 0)),
                  pl.BlockSpec((1, 1, n), lambda i, k: (i, 0, 0))],
        out_specs=pl.BlockSpec((1, 8, n), lambda i, k: (i, 0, 0)),
        out_shape=jax.ShapeDtypeStruct((depth, 8, n), F32),
        compiler_params=_cparams(("parallel", "arbitrary")),
        name="modulation",
    )(cond, mod_w, mod_b.reshape(depth, 1, n))


W_MLA, W_RET, W_Z, W_XBC, W_DT = 512, 1024, 512, 768, 128


def _ones_lane_mask(shape):
    lane = _iota(shape, len(shape) - 1) % (2 * MLA_HP)
    return (lane == MLA_HP - 1) | (lane == MLA_HP)


def _inproj_kernel(x_ref, mod_ref, n1_ref, wmla_ref, wret_ref, wz_ref, wxbc_ref, wdt_ref,
                   qn_ref, wq_ref, wqr_ref, kvn_ref, wk_ref, wv_ref, cq_ref, sq_ref, ck_ref,
                   q_out, k_out, v_out, ret_out, z_out, xbc_out, dt_out):
    x = x_ref[0]
    h = _rms(x, n1_ref[...]) * (1.0 + mod_ref[0, 1:2, :]) + mod_ref[0, 0:1, :]
    hb = h.astype(BF16)

    pm = _dot(hb, wmla_ref[...])
    ret = _dot(hb, wret_ref[...])
    qcols = _iota(ret.shape, 1) < RET_HEADS * RET_QK
    ret_out[0] = jnp.where(qcols, ret * (RET_QK ** -0.5), ret).astype(BF16)
    z_out[0] = _dot(hb, wz_ref[...]).astype(BF16)
    xbc_out[0] = _dot(hb, wxbc_ref[...]).astype(BF16)
    dt_out[0] = _dot(hb, wdt_ref[...])

    cqb = _rms(pm[:, 0:MLA_Q_LORA], qn_ref[...]).astype(BF16)
    qm = _dot(cqb, wq_ref[...])
    qr = _dot(cqb, wqr_ref[...])
    cosq, sinq = cq_ref[...], sq_ref[...]
    for hh in range(MLA_HEADS):
        sl = slice(hh * MLA_HP, (hh + 1) * MLA_HP)
        q_out[0, :, sl] = (qm[:, sl] * cosq + qr[:, sl] * sinq).astype(BF16)

    ckvn = _rms(pm[:, MLA_Q_LORA:MLA_Q_LORA + MLA_KV_LORA], kvn_ref[...])
    u = pm[:, MLA_Q_LORA + MLA_KV_LORA:W_MLA] * ck_ref[...]
    kin = jnp.concatenate([ckvn, u], axis=-1).astype(BF16)
    k_out[0] = _dot(kin, wk_ref[...]).astype(BF16)
    v = _dot(ckvn.astype(BF16), wv_ref[...])
    v_out[0] = jnp.where(_ones_lane_mask(v.shape), 1.0, v).astype(BF16)


def _inproj(xs, mods, lw, tabs, tm):
    b, t, d = xs.shape
    const = lambda shape: pl.BlockSpec(shape, lambda i, j: (0,) * len(shape))
    tok = lambda w: pl.BlockSpec((1, tm, w), lambda i, j: (i, j, 0))
    tab = pl.BlockSpec((tm, LANES), lambda i, j: (j, 0))
    widths = (MLA_HEADS * MLA_HP, MLA_HEADS * MLA_HP, MLA_HEADS * MLA_HP, W_RET, W_Z, W_XBC, W_DT)
    dtypes = (BF16, BF16, BF16, BF16, BF16, BF16, F32)
    return pl.pallas_call(
        _inproj_kernel,
        grid=(b, t // tm),
        in_specs=[tok(d), pl.BlockSpec((1, 8, d), lambda i, j: (i, 0, 0)), const((1, d)),
                  const((d, W_MLA)), const((d, W_RET)), const((d, W_Z)), const((d, W_XBC)), const((d, W_DT)),
                  const((1, MLA_Q_LORA)), const((MLA_Q_LORA, MLA_HEADS * MLA_HP)),
                  const((MLA_Q_LORA, MLA_HEADS * MLA_HP)), const((1, MLA_KV_LORA)),
                  const((2 * MLA_KV_LORA, MLA_HEADS * MLA_HP)), const((MLA_KV_LORA, MLA_HEADS * MLA_HP)),
                  tab, tab, tab],
        out_specs=[tok(w) for w in widths],
        out_shape=[jax.ShapeDtypeStruct((b, t, w), dt) for w, dt in zip(widths, dtypes)],
        compiler_params=_cparams(("parallel", "parallel")),
        name="inproj",
    )(xs, mods, lw["n1"], lw["w_mla"], lw["w_ret"], lw["w_z"], lw["w_xbc"], lw["w_dt"],
      lw["qn"], lw["wq"], lw["wqr"], lw["kvn"], lw["wk"], lw["wv"], *tabs)


def _attn_kernel(*refs, seg_blocks, tq):
    nseg = len(seg_blocks)
    q_ref = refs[0]
    kv_refs = refs[1:1 + 2 * nseg]
    o_ref = refs[1 + 2 * nseg]
    carry = tuple((jnp.full((tq, 1), NEG, F32), jnp.zeros((tq, MLA_HP), F32)) for _ in range(MLA_HEADS))
    for si, (tk, nblk) in enumerate(seg_blocks):
        k_ref, v_ref = kv_refs[2 * si], kv_refs[2 * si + 1]

        def body(j, carry, k_ref=k_ref, v_ref=v_ref, tk=tk):
            start = 0 if isinstance(j, int) else pl.multiple_of(j * tk, tk)
            new = []
            for hh in range(MLA_HEADS):
                sl = slice(hh * MLA_HP, (hh + 1) * MLA_HP)
                m, acc = carry[hh]
                s = _dot_nt(q_ref[0, :, sl], k_ref[0, pl.ds(start, tk), sl])
                m_new = jnp.maximum(m, jnp.max(s, axis=-1, keepdims=True))
                p = jnp.exp2(s - m_new).astype(BF16)
                acc = jnp.exp2(m - m_new) * acc + _dot(p, v_ref[0, pl.ds(start, tk), sl])
                new.append((m_new, acc))
            return tuple(new)

        carry = body(0, carry) if nblk == 1 else lax.fori_loop(0, nblk, body, carry)
    lane = _iota((tq, MLA_HP), 1)
    pairs = []
    for hh in range(0, MLA_HEADS, 2):
        acc_e, acc_o = carry[hh][1], carry[hh + 1][1]
        l_e = jnp.sum(jnp.where(lane == MLA_HP - 1, acc_e, 0.0), axis=-1, keepdims=True)
        l_o = jnp.sum(jnp.where(lane == 0, acc_o, 0.0), axis=-1, keepdims=True)
        pairs.append(jnp.where(lane < MLA_V, acc_e / l_e, acc_o / l_o))
    o_ref[0] = jnp.concatenate(pairs, axis=-1).astype(BF16)


def _attention(q, segs, tq, tk):
    b, l, _ = q.shape
    seg_blocks = []
    in_specs = [pl.BlockSpec((1, tq, MLA_HEADS * MLA_HP), lambda i, j: (i, j, 0))]
    args = [q]
    for k, v in segs:
        s = k.shape[1]
        t = min(tk, s)
        seg_blocks.append((t, s // t))
        for _ in range(2):
            in_specs.append(pl.BlockSpec((1, s, MLA_HEADS * MLA_HP), lambda i, j: (i, 0, 0),
                                         pipeline_mode=pl.Buffered(1)))
        args += [k, v]
    return pl.pallas_call(
        functools.partial(_attn_kernel, seg_blocks=tuple(seg_blocks), tq=tq),
        grid=(b, l // tq),
        in_specs=in_specs,
        out_specs=pl.BlockSpec((1, tq, MLA_HEADS * MLA_V), lambda i, j: (i, j, 0)),
        out_shape=jax.ShapeDtypeStruct((b, l, MLA_HEADS * MLA_V), BF16),
        compiler_params=_cparams(("parallel", "arbitrary")),
        name="attention",
    )(*args)


def _ret_kernel(x_ref, lgl_ref, lgh_ref, init_ref, y_ref, fin_ref, s_ref, g_ref, gbuf_ref, *, nblk, sub):
    C, H, N, P = CHUNK, RET_HEADS, RET_QK, RET_V
    HP = H * P
    ph, c = pl.program_id(1), pl.program_id(2)
    blk = jnp.where(ph == 0, nblk - 1 - c, c)

    def log_sigmoid(v):
        return jnp.minimum(v, 0.0) - jnp.log(1.0 + jnp.exp(-jnp.abs(v)))

    lgl = log_sigmoid(lgl_ref[...])
    lgf, lgb = lgl[0:1], lgl[1:2]
    rows = _iota((C, HP), 0).astype(F32)
    head_n = _iota((N, HP), 1) // P

    def diag_blocks(r):
        out = r[0:N]
        for hh in range(1, H):
            out = jnp.where(head_n == hh, r[hh * N:(hh + 1) * N], out)
        return out

    def as_rows(s_c):
        return jnp.concatenate([jnp.where(head_n == hh, s_c, 0.0) for hh in range(H)], axis=0)

    @pl.when((ph == 0) & (c == 0))
    def _():
        g_ref[...] = init_ref[0, 1]

    @pl.when(ph == 0)
    def _():
        w_b = jnp.exp(rows * lgb)
        tot_b = jnp.exp(C * lgb)
        for sb in reversed(range(sub)):
            r0 = sb * C
            k = x_ref[0, r0:r0 + C, HP:2 * HP].astype(F32)
            v = x_ref[0, r0:r0 + C, 2 * HP:3 * HP]
            g = g_ref[...]
            gbuf_ref[blk * sub + sb] = g
            g_ref[...] = g * tot_b + diag_blocks(_dot_tn((k * w_b).astype(BF16), v))

        @pl.when(c == nblk - 1)
        def _():
            fin_ref[0, 1] = g_ref[...]

    @pl.when((ph == 1) & (c == 0))
    def _():
        s_ref[...] = init_ref[0, 0]

    @pl.when(ph == 1)
    def _():
        lgh = log_sigmoid(lgh_ref[...])
        grp = _iota((C, HP), 1) // P
        di = (_iota((C, C), 0) - _iota((C, C), 1)).astype(F32)
        decay = [jnp.exp(jnp.where(di >= 0, di * lgh[hh:hh + 1, :], -di * lgh[H + hh:H + hh + 1, :]))
                 for hh in range(H)]
        e_f = jnp.exp((rows + 1.0) * lgf)
        e_b = jnp.exp((C - rows) * lgb)
        w_f = jnp.exp((C - 1.0 - rows) * lgf)
        tot_f = jnp.exp(C * lgf)
        for sb in range(sub):
            r0 = sb * C
            q = x_ref[0, r0:r0 + C, 0:HP]
            kb = x_ref[0, r0:r0 + C, HP:2 * HP]
            v = x_ref[0, r0:r0 + C, 2 * HP:3 * HP]
            gate = x_ref[0, r0:r0 + C, 3 * HP:4 * HP].astype(F32)
            ws, vs = [], []
            for hh in range(H):
                sc = _dot_nt(jnp.where(grp == hh, q, jnp.zeros_like(q)), kb)
                ws.append((sc * decay[hh]).astype(BF16))
                vs.append(jnp.where(grp == hh, v, jnp.zeros_like(v)))
            y = _dot(jnp.concatenate(ws, axis=1), jnp.concatenate(vs, axis=0))
            qf32 = q.astype(F32)
            s = s_ref[...]
            g_in = gbuf_ref[blk * sub + sb]
            y = y + _dot(jnp.concatenate([(qf32 * e_f).astype(BF16), (qf32 * e_b).astype(BF16)], axis=1),
                         jnp.concatenate([as_rows(s), as_rows(g_in)], axis=0).astype(BF16))
            s_ref[...] = s * tot_f + diag_blocks(_dot_tn((kb.astype(F32) * w_f).astype(BF16), v))

            mu = jnp.zeros_like(y)
            for hh in range(H):
                mh = jnp.sum(jnp.where(grp == hh, y, 0.0), axis=-1, keepdims=True) * (1.0 / P)
                mu = jnp.where(grp == hh, mh, mu)
            yc = y - mu
            var = jnp.zeros_like(y)
            for hh in range(H):
                vh = jnp.sum(jnp.where(grp == hh, yc * yc, 0.0), axis=-1, keepdims=True) * (1.0 / P)
                var = jnp.where(grp == hh, vh, var)
            y_ref[0, r0:r0 + C, :] = (yc * lax.rsqrt(var + EPS) * _silu(gate)).astype(BF16)

        @pl.when(c == nblk - 1)
        def _():
            fin_ref[0, 0] = s_ref[...]


def _block_index(nblk):
    return lambda i, ph, c: (i, (1 - ph) * (nblk - 1 - c) + ph * c, 0)


def _scan_sub(t):
    nc = t // CHUNK
    return 4 if nc % 4 == 0 else (2 if nc % 2 == 0 else 1)


def _ret_scan(ret, lw, init):
    b, t, _ = ret.shape
    sub = _scan_sub(t)
    rows = sub * CHUNK
    nblk = t // rows
    hp = RET_HEADS * RET_V
    return pl.pallas_call(
        functools.partial(_ret_kernel, nblk=nblk, sub=sub),
        grid=(b, 2, nblk),
        in_specs=[pl.BlockSpec((1, rows, W_RET), _block_index(nblk)),
                  pl.BlockSpec((2, hp), lambda i, ph, c: (0, 0)),
                  pl.BlockSpec((2 * RET_HEADS, CHUNK), lambda i, ph, c: (0, 0)),
                  pl.BlockSpec((1, 2, RET_QK, hp), lambda i, ph, c: (i, 0, 0, 0))],
        out_specs=[pl.BlockSpec((1, rows, hp), lambda i, ph, c: (i, ph * c, 0)),
                   pl.BlockSpec((1, 2, RET_QK, hp), lambda i, ph, c: (i, 0, 0, 0))],
        out_shape=[jax.ShapeDtypeStruct((b, t, hp), BF16),
                   jax.ShapeDtypeStruct((b, 2, RET_QK, hp), F32)],
        scratch_shapes=[pltpu.VMEM((RET_QK, hp), F32),
                        pltpu.VMEM((RET_QK, hp), F32),
                        pltpu.VMEM((t // CHUNK, RET_QK, hp), F32)],
        compiler_params=_cparams(("parallel", "arbitrary", "arbitrary")),
        name="retention_scan",
    )(ret, lw["ret_lgl"], lw["ret_lgh"], init)


def _ssd_kernel(x_ref, xp_ref, xn_ref, z_ref, dt_ref, cw_ref, cb_ref, dtb_ref, alog_ref, dvec_ref, nw_ref,
                init_ref, y_ref, fin_ref, s_ref, g_ref, gbuf_ref, act_ref, dtc_ref, abc_ref, *, nblk, sub):
    C, H, N, P, G = CHUNK, SSD_HEADS, SSD_STATE, SSD_HEADDIM, SSD_GROUPS
    HP = H * P
    HPG = H // G
    HALO = SSD_HALO
    ph, c = pl.program_id(1), pl.program_id(2)
    blk = jnp.where(ph == 0, nblk - 1 - c, c)

    ri = _iota((C, C + 2 * HALO), 0)
    ci = _iota((C, C + 2 * HALO), 1)
    sel_prev = (ci == ri + HALO - 1).astype(BF16)
    sel_next = (ci == ri + HALO + 1).astype(BF16)
    lane = _iota((LANES, HP), 0)
    head_of = _iota((LANES, HP), 1) // P
    e_f = (lane == head_of).astype(BF16)
    e_b = (lane == head_of + H).astype(BF16)
    r_i = _iota((C, C), 0)
    c_i = _iota((C, C), 1)
    low = (r_i >= c_i).astype(BF16)
    upp = (r_i <= c_i).astype(BF16)
    grp_n = _iota((N, HP), 1) // (HPG * P)
    neg_a = -jnp.exp(alog_ref[...])
    cw = cw_ref[...]

    def prep(sb, first_pass):
        r0 = sb * C
        if first_pass:
            xc = x_ref[0, r0:r0 + C, :]
            if sb == 0:
                prev = jnp.where(blk > 0, xp_ref[0], jnp.zeros_like(xp_ref[0]))
            else:
                prev = x_ref[0, r0 - HALO:r0, :]
            if sb == sub - 1:
                nxt = jnp.where(blk < nblk - 1, xn_ref[0], jnp.zeros_like(xn_ref[0]))
            else:
                nxt = x_ref[0, r0 + C:r0 + C + HALO, :]
            stacked = jnp.concatenate([prev, xc, nxt], axis=0)
            conv = (cw[0:1] * _dot(sel_prev, stacked) + cw[1:2] * xc.astype(F32)
                    + cw[2:3] * _dot(sel_next, stacked) + cb_ref[...])
            act = _silu(conv)
            raw = dt_ref[0, r0:r0 + C, :] + dtb_ref[...]
            dt = jnp.maximum(raw, 0.0) + jnp.log(1.0 + jnp.exp(-jnp.abs(raw)))
            la = dt * neg_a
            a_f = None
            a_b = _dot_exact_lhs01(upp, la)
            act_ref[blk * sub + sb] = act
            dtc_ref[blk * sub + sb] = dt
            abc_ref[blk * sub + sb] = a_b
        else:
            act = act_ref[blk * sub + sb]
            dt = dtc_ref[blk * sub + sb]
            a_b = abc_ref[blk * sub + sb]
            la = dt * neg_a
            a_f = _dot_exact_lhs01(low, la)
        xs = act[:, 0:HP]
        bm = act[:, HP:HP + G * N].astype(BF16)
        cm = act[:, HP + G * N:HP + 2 * G * N].astype(BF16)
        return xs, bm, cm, dt, la, a_f, a_b

    def state_update(state, bm, xs, wdt, tot8, e_dir, row):
        xw = (xs * _dot(wdt.astype(BF16), e_dir)).astype(BF16)
        tot = _dot_exact_rhs01(tot8, e_dir)[row:row + 1]
        r = _dot_tn(bm, xw)
        upd = r[0:N]
        for gg in range(1, G):
            upd = jnp.where(grp_n == gg, r[gg * N:(gg + 1) * N], upd)
        return state * tot + upd

    def as_rows(s_c):
        return jnp.concatenate([jnp.where(grp_n == gg, s_c, 0.0) for gg in range(G)], axis=0).astype(BF16)

    @pl.when((ph == 0) & (c == 0))
    def _():
        g_ref[...] = init_ref[0, 1]

    @pl.when(ph == 0)
    def _():
        for sb in reversed(range(sub)):
            xs, bm, _, dt, _, _, a_b = prep(sb, True)
            g = g_ref[...]
            gbuf_ref[blk * sub + sb] = g
            wdt = jnp.exp(a_b[0:1, :] - a_b) * dt
            g_ref[...] = state_update(g, bm, xs, wdt, jnp.exp(a_b[0:8, :]), e_b, 0)

        @pl.when(c == nblk - 1)
        def _():
            fin_ref[0, 1] = g_ref[...]

    @pl.when((ph == 1) & (c == 0))
    def _():
        s_ref[...] = init_ref[0, 0]

    @pl.when(ph == 1)
    def _():
        grp_l = _iota((C, G * N), 1) // N
        head_l = _iota((C, HP), 1) // P
        tri_f = r_i >= c_i
        diag = r_i == c_i
        for sb in range(sub):
            r0 = sb * C
            xs, bm, cm, dt, la, a_f, a_b = prep(sb, False)
            la_t = la.T[0:2 * H]
            dt_t = dt.T[0:2 * H]
            af_row = _dot_exact_rhs01(la_t, upp)
            ab_row = _dot_exact_rhs01(la_t, low)
            sc = [_dot_nt(jnp.where(grp_l == gg, cm, jnp.zeros_like(cm)), bm) for gg in range(G)]
            xsb = xs.astype(BF16)
            ws, vs = [], []
            for hh in range(H):
                seg_f = a_f[:, hh:hh + 1] - af_row[hh:hh + 1, :]
                seg_b = a_b[:, H + hh:H + hh + 1] - ab_row[H + hh:H + hh + 1, :]
                dt_f, dt_b = dt_t[hh:hh + 1, :], dt_t[H + hh:H + hh + 1, :]
                dm = (jnp.exp(jnp.where(tri_f, seg_f, seg_b)) * jnp.where(tri_f, dt_f, dt_b)
                      + jnp.where(diag, dt_b, 0.0))
                ws.append((sc[hh // HPG] * dm).astype(BF16))
                vs.append(jnp.where(head_l == hh, xsb, jnp.zeros_like(xsb)))
            y = _dot(jnp.concatenate(ws, axis=1), jnp.concatenate(vs, axis=0))

            s = s_ref[...]
            g_in = gbuf_ref[blk * sub + sb]
            y = y + _dot(jnp.exp(a_f).astype(BF16), e_f) * _dot(cm, as_rows(s))
            y = y + _dot(jnp.exp(a_b).astype(BF16), e_b) * _dot(cm, as_rows(g_in))

            wdt = jnp.exp(a_f[C - 1:C, :] - a_f) * dt
            s_ref[...] = state_update(s, bm, xs, wdt, jnp.exp(a_f[C - 8:C, :]), e_f, 7)

            y = y + xs * dvec_ref[...]
            y = y * _silu(z_ref[0, r0:r0 + C, :].astype(F32))
            y_ref[0, r0:r0 + C, :] = _rms(y, nw_ref[...]).astype(BF16)

        @pl.when(c == nblk - 1)
        def _():
            fin_ref[0, 0] = s_ref[...]


SSD_HALO = 16


def _ssd_scan(xbc, z, dt, lw, init):
    b, t, _ = xbc.shape
    sub = _scan_sub(t)
    rows = sub * CHUNK
    nblk = t // rows
    hp = SSD_INNER
    per = rows // SSD_HALO
    nhalo = t // SSD_HALO
    bidx = _block_index(nblk)

    def x_idx(i, ph, c):
        return (i, (1 - ph) * (nblk - 1 - c), 0)

    def prev_idx(i, ph, c):
        return (i, jnp.maximum((1 - ph) * (nblk - 1 - c) * per - 1, 0), 0)

    def next_idx(i, ph, c):
        return (i, jnp.minimum(((1 - ph) * (nblk - 1 - c) + 1) * per, nhalo - 1), 0)

    const = lambda shape: pl.BlockSpec(shape, lambda i, ph, c: (0,) * len(shape))
    return pl.pallas_call(
        functools.partial(_ssd_kernel, nblk=nblk, sub=sub),
        grid=(b, 2, nblk),
        in_specs=[pl.BlockSpec((1, rows, W_XBC), x_idx),
                  pl.BlockSpec((1, SSD_HALO, W_XBC), prev_idx),
                  pl.BlockSpec((1, SSD_HALO, W_XBC), next_idx),
                  pl.BlockSpec((1, rows, W_Z), bidx),
                  pl.BlockSpec((1, rows, W_DT), x_idx),
                  const((8, W_XBC)), const((1, W_XBC)), const((1, LANES)), const((1, LANES)),
                  const((1, hp)), const((1, hp)),
                  pl.BlockSpec((1, 2, SSD_STATE, hp), lambda i, ph, c: (i, 0, 0, 0))],
        out_specs=[pl.BlockSpec((1, rows, hp), lambda i, ph, c: (i, ph * c, 0)),
                   pl.BlockSpec((1, 2, SSD_STATE, hp), lambda i, ph, c: (i, 0, 0, 0))],
        out_shape=[jax.ShapeDtypeStruct((b, t, hp), BF16),
                   jax.ShapeDtypeStruct((b, 2, SSD_STATE, hp), F32)],
        scratch_shapes=[pltpu.VMEM((SSD_STATE, hp), F32),
                        pltpu.VMEM((SSD_STATE, hp), F32),
                        pltpu.VMEM((t // CHUNK, SSD_STATE, hp), F32),
                        pltpu.VMEM((t // CHUNK, CHUNK, W_XBC), F32),
                        pltpu.VMEM((t // CHUNK, CHUNK, W_DT), F32),
                        pltpu.VMEM((t // CHUNK, CHUNK, W_DT), F32)],
        compiler_params=_cparams(("parallel", "arbitrary", "arbitrary")),
        name="ssd_scan",
    )(xbc, xbc, xbc, z, dt, lw["conv_w"], lw["conv_b"], lw["dt_bias"], lw["a_log"], lw["d_vec"], lw["ssd_nw"], init)


def _mix_residual(x_ref, att_ref, ret_ref, ssd_ref, mod_ref, n2_ref, wo_ref):
    mix = jnp.concatenate([att_ref[0], ret_ref[0], ssd_ref[0]], axis=-1)
    x1 = x_ref[0] + mod_ref[0, 2:3, :] * _dot(mix, wo_ref[...])
    h = _rms(x1, n2_ref[...]) * (1.0 + mod_ref[0, 4:5, :]) + mod_ref[0, 3:4, :]
    return x1, h


def _dense_kernel(x_ref, att_ref, ret_ref, ssd_ref, mod_ref, n2_ref, wo_ref, wg_ref, wu_ref, wd_ref, o_ref,
                  *, ff_chunk):
    x1, h = _mix_residual(x_ref, att_ref, ret_ref, ssd_ref, mod_ref, n2_ref, wo_ref)
    hb = h.astype(BF16)
    ff = wg_ref.shape[1]
    f = jnp.zeros_like(x1)
    for s in range(0, ff, ff_chunk):
        g = _dot(hb, wg_ref[:, s:s + ff_chunk])
        u = _dot(hb, wu_ref[:, s:s + ff_chunk])
        f = f + _dot((_silu(g) * u).astype(BF16), wd_ref[s:s + ff_chunk, :])
    o_ref[0] = x1 + mod_ref[0, 5:6, :] * f


def _dense_block(xs, att, ret, ssd, mods, lw, tm):
    b, t, d = xs.shape
    ff = lw["wg"].shape[1]
    tok = lambda w: pl.BlockSpec((1, tm, w), lambda i, j: (i, j, 0))
    const = lambda shape: pl.BlockSpec(shape, lambda i, j: (0,) * len(shape), pipeline_mode=pl.Buffered(1))
    return pl.pallas_call(
        functools.partial(_dense_kernel, ff_chunk=ff // 2),
        grid=(b, t // tm),
        in_specs=[tok(d), tok(att.shape[-1]), tok(ret.shape[-1]), tok(ssd.shape[-1]),
                  pl.BlockSpec((1, 8, d), lambda i, j: (i, 0, 0)), const((1, d)),
                  const((d, d)), const((d, ff)), const((d, ff)), const((ff, d))],
        out_specs=tok(d),
        out_shape=jax.ShapeDtypeStruct((b, t, d), F32),
        compiler_params=_cparams(("parallel", "parallel")),
        name="outproj_ffn",
    )(xs, att, ret, ssd, mods, lw["n2"], lw["wo"], lw["wg"], lw["wu"], lw["wd"])


SUBLANES = 8


def _route_kernel(x_ref, att_ref, ret_ref, ssd_ref, mod_ref, n2_ref, wo_ref, wr_ref,
                  x1_ref, h_ref, slot_ref, gate_ref, cnt_ref):
    x1, h = _mix_residual(x_ref, att_ref, ret_ref, ssd_ref, mod_ref, n2_ref, wo_ref)
    x1_ref[0] = x1
    tm = h.shape[0]
    h_hi = h.astype(BF16)
    h_ref[0] = h_hi
    h_lo = (h - h_hi.astype(F32)).astype(BF16)
    logits = _dot(h_hi, wr_ref[0]) + _dot(h_lo, wr_ref[0]) + _dot(h_hi, wr_ref[1])
    lane = _iota(logits.shape, 1).astype(F32)
    logits = jnp.where(lane < N_EXPERTS, logits, NEG)
    m1 = jnp.max(logits, axis=-1, keepdims=True)
    i1 = jnp.min(jnp.where(logits == m1, lane, float(LANES)), axis=-1, keepdims=True)
    rest = jnp.where(lane == i1, NEG, logits)
    m2 = jnp.max(rest, axis=-1, keepdims=True)
    i2 = jnp.min(jnp.where(rest == m2, lane, float(LANES)), axis=-1, keepdims=True)
    e2 = jnp.exp(m2 - m1)
    den = 1.0 + e2
    sel1, sel2 = lane == i1, lane == i2
    chosen = jnp.where(sel1 | sel2, 1.0, 0.0)
    before = (_iota((tm, tm), 0) > _iota((tm, tm), 1)).astype(BF16)
    pos = _dot(before, chosen.astype(BF16))
    cnt = jnp.sum(chosen, axis=0, keepdims=True)
    nblk = sum(jnp.where(cnt > float(k * MOE_ROWS), 1.0, 0.0) for k in range(pl.cdiv(tm, MOE_ROWS)))
    lower_e = (_iota((LANES, LANES), 0) < _iota((LANES, LANES), 1)).astype(BF16)
    off = _dot_exact_rhs01(jnp.broadcast_to(nblk * float(MOE_ROWS), (SUBLANES, LANES)), lower_e)[0:1]
    slot = off + pos
    s1 = jnp.sum(jnp.where(sel1, slot, 0.0), axis=-1, keepdims=True)
    s2 = jnp.sum(jnp.where(sel2, slot, 0.0), axis=-1, keepdims=True)
    lane_i = _iota(logits.shape, 1)
    slots = jnp.where(lane_i == 0, s1, jnp.where(lane_i == 1, s2, -1.0))
    slot_ref[0, 0] = slots.T[0:SUBLANES].astype(jnp.int32)
    gate_ref[0] = jnp.where(sel1, 1.0 / den, 0.0) + jnp.where(sel2, e2 / den, 0.0)
    row = _iota((SUBLANES, LANES), 0)
    cnt_ref[0, 0] = jnp.where(row == 0, cnt, jnp.where(row == 1, off, jnp.where(row == 2, nblk, 0.0))
                              ).astype(jnp.int32)


def _route_block(xs, att, ret, ssd, mods, lw, tm):
    b, t, d = xs.shape
    nt = t // tm
    tok = lambda w: pl.BlockSpec((1, tm, w), lambda i, j: (i, j, 0))
    const = lambda shape: pl.BlockSpec(shape, lambda i, j: (0,) * len(shape))
    per_tile = lambda r, w: pl.BlockSpec((1, 1, r, w), lambda i, j: (i, j, 0, 0))
    return pl.pallas_call(
        _route_kernel,
        grid=(b, nt),
        in_specs=[tok(d), tok(att.shape[-1]), tok(ret.shape[-1]), tok(ssd.shape[-1]),
                  pl.BlockSpec((1, 8, d), lambda i, j: (i, 0, 0)), const((1, d)),
                  const((d, d)), const((2, d, LANES))],
        out_specs=[tok(d), tok(d), per_tile(SUBLANES, tm), tok(LANES), per_tile(SUBLANES, LANES)],
        out_shape=[jax.ShapeDtypeStruct((b, t, d), F32), jax.ShapeDtypeStruct((b, t, d), BF16),
                   jax.ShapeDtypeStruct((b, nt, SUBLANES, tm), jnp.int32),
                   jax.ShapeDtypeStruct((b, t, LANES), F32),
                   jax.ShapeDtypeStruct((b, nt, SUBLANES, LANES), jnp.int32)],
        compiler_params=_cparams(("parallel", "parallel")),
        name="outproj_route",
    )(xs, att, ret, ssd, mods, lw["n2"], lw["wo"], lw["router"])


MOE_ROWS = 256


def _moe_kernel(cnt_ref, slot_ref, h_ref, gate_ref, x1_ref, mod_ref, wg_ref, wu_ref, wt_ref, wd_ref, fn_ref, o_ref,
                *, final_norm, ff_split):
    e = pl.program_id(2)
    rb = MOE_ROWS
    tm = h_ref.shape[1]

    @pl.when(e == 0)
    def _():
        o_ref[...] = jnp.zeros_like(o_ref)

    off = cnt_ref[0, 0, 1, e]
    nblocks = cnt_ref[0, 0, 2, e]
    slot1 = slot_ref[0, 0, 0:1, :]
    slot2 = slot_ref[0, 0, 1:2, :]
    gates = gate_ref[0]
    ge = jnp.sum(jnp.where(_iota(gates.shape, 1) == e, gates, 0.0), axis=-1, keepdims=True)

    def block(bi, carry):
        rows = off + bi * rb + _iota((rb, tm), 0)
        sel = jnp.where((rows == slot1) | (rows == slot2), 1.0, 0.0).astype(BF16)
        lhs = _dot(sel, h_ref[0]).astype(BF16)
        y = None
        for s0, s1 in ff_split:
            if (s1 - s0) % MXU_WIDTH == 0:
                g, u = _dot(lhs, wg_ref[0, :, s0:s1]), _dot(lhs, wu_ref[0, :, s0:s1])
            else:
                gu = _dot(lhs, wt_ref[0])
                g, u = gu[:, :s1 - s0], gu[:, s1 - s0:]
            part = _dot((_silu(g) * u).astype(BF16), wd_ref[0, s0:s1, :])
            y = part if y is None else y + part
        o_ref[0] += ge * _dot_tn(sel, y.astype(BF16))
        return carry

    lax.fori_loop(0, nblocks, block, 0)

    @pl.when(e == N_EXPERTS - 1)
    def _():
        out = x1_ref[0] + mod_ref[0, 5:6, :] * o_ref[0]
        o_ref[0] = _rms(out, fn_ref[...]) if final_norm else out


MXU_WIDTH = 2 * LANES
MOE_FF_SLICE = 512


def _moe_ff_split(ff):
    aligned = ff // MXU_WIDTH * MXU_WIDTH
    cuts = list(range(0, aligned, MOE_FF_SLICE)) + [aligned]
    return tuple(zip(cuts[:-1], cuts[1:])) + (((aligned, ff),) if aligned < ff else ())


def _moe_weights(w_gate, w_up, w_down):
    ff = w_gate.shape[-1]
    aligned = ff // MXU_WIDTH * MXU_WIDTH
    tail = jnp.concatenate([w_gate[..., aligned:], w_up[..., aligned:]], axis=-1)
    if aligned == ff:
        tail = jnp.zeros(w_gate.shape[:-1] + (MXU_WIDTH,), w_gate.dtype)
    return w_gate.astype(BF16), w_up.astype(BF16), tail.astype(BF16), w_down.astype(BF16)


def _moe_block(x1, h, slots, gates, counts, mods, lw, fnw, final_norm, tm):
    b, t, d = x1.shape
    ff = lw["moe_wd"].shape[1]
    assert MOE_ROWS % (2 * SUBLANES) == 0
    tok = lambda w: pl.BlockSpec((1, tm, w), lambda i, j, e: (i, j, 0))
    per_tile = lambda w, **kw: pl.BlockSpec((1, 1, SUBLANES, w), lambda i, j, e: (i, j, 0, 0), **kw)
    expert = lambda r, w: pl.BlockSpec((1, r, w), lambda i, j, e: (e, 0, 0))
    return pl.pallas_call(
        functools.partial(_moe_kernel, final_norm=final_norm, ff_split=_moe_ff_split(ff)),
        grid=(b, t // tm, N_EXPERTS),
        in_specs=[per_tile(LANES, memory_space=pltpu.SMEM), per_tile(tm), tok(d), tok(LANES), tok(d),
                  pl.BlockSpec((1, 8, d), lambda i, j, e: (i, 0, 0)),
                  expert(d, ff), expert(d, ff), expert(d, lw["moe_wt"].shape[2]), expert(ff, d),
                  pl.BlockSpec((1, d), lambda i, j, e: (0, 0))],
        out_specs=tok(d),
        out_shape=jax.ShapeDtypeStruct((b, t, d), F32),
        compiler_params=_cparams(("parallel", "parallel", "arbitrary")),
        name="moe",
    )(counts, slots, h, gates, x1, mods, lw["moe_wg"], lw["moe_wu"], lw["moe_wt"], lw["moe_wd"], fnw)


def _final_norm_kernel(x_ref, w_ref, o_ref):
    o_ref[0] = _rms(x_ref[0], w_ref[...])


def _final_norm(xs, fnw, tm):
    b, t, d = xs.shape
    return pl.pallas_call(
        _final_norm_kernel,
        grid=(b, t // tm),
        in_specs=[pl.BlockSpec((1, tm, d), lambda i, j: (i, j, 0)), pl.BlockSpec((1, d), lambda i, j: (0, 0))],
        out_specs=pl.BlockSpec((1, tm, d), lambda i, j: (i, j, 0)),
        out_shape=jax.ShapeDtypeStruct((b, t, d), F32),
        compiler_params=_cparams(("parallel", "parallel")),
        name="final_norm",
    )(xs, fnw)


def _rot_cols(w):
    a, b2, c, d = jnp.split(w, 4, axis=-1)
    return jnp.concatenate([-b2, a, -d, c], axis=-1)


def _layer_weights(i, norm1_w, norm2_w, w_in, w_out, mla_q_norm_w, mla_w_qb, mla_kv_norm_w, mla_w_kb, mla_w_vb,
                   ret_decay_logit, ssd_conv_w, ssd_conv_b, ssd_dt_bias, ssd_a_log, ssd_d, ssd_norm_w):
    d = w_in.shape[1]
    offs = np.cumsum((0,) + IN_SIZES)
    wi = w_in[i]
    zeros = lambda n: jnp.zeros((d, n), F32)
    kpe = wi[:, offs[2]:offs[3]]
    w_mla = jnp.concatenate([wi[:, offs[0]:offs[3]], _rot_cols(kpe),
                             zeros(W_MLA - MLA_Q_LORA - MLA_KV_LORA - 2 * MLA_ROPE)], axis=1).astype(BF16)
    w_ret = wi[:, offs[3]:offs[7]].astype(BF16)
    w_z = wi[:, offs[7]:offs[8]].astype(BF16)
    w_xbc = wi[:, offs[8]:offs[9]].astype(BF16)
    w_dt = jnp.concatenate([wi[:, offs[9]:offs[11]], zeros(W_DT - 2 * SSD_HEADS)], axis=1).astype(BF16)

    dh = MLA_NOPE + MLA_ROPE
    wqb = mla_w_qb[i].reshape(MLA_Q_LORA, MLA_HEADS, dh)
    pad = lambda a, lo, hi: jnp.pad(a, ((0, 0), (0, 0), (lo, hi)))
    wq = pad(wqb, 0, MLA_HP - dh).reshape(MLA_Q_LORA, MLA_HEADS * MLA_HP).astype(BF16)
    wqr = pad(_rot_cols(wqb[:, :, MLA_NOPE:]), MLA_NOPE, MLA_HP - dh).reshape(MLA_Q_LORA, MLA_HEADS * MLA_HP).astype(BF16)
    wkb = pad(mla_w_kb[i].reshape(MLA_KV_LORA, MLA_HEADS, MLA_NOPE), 0, MLA_HP - MLA_NOPE)
    place = np.zeros((MLA_ROPE, MLA_HEADS, MLA_HP), np.float32)
    for r in range(MLA_ROPE):
        place[r, :, MLA_NOPE + r] = 1.0
    place = jnp.asarray(place)
    wk = jnp.concatenate([wkb, place, place, jnp.zeros((MLA_KV_LORA - 2 * MLA_ROPE, MLA_HEADS, MLA_HP), F32)],
                         axis=0).reshape(2 * MLA_KV_LORA, MLA_HEADS * MLA_HP).astype(BF16)

    wvb = mla_w_vb[i].reshape(MLA_KV_LORA, MLA_HEADS, MLA_V)
    wv = jnp.concatenate(
        [pad(wvb[:, hh:hh + 1], (hh % 2) * MLA_V, MLA_HP - MLA_V - (hh % 2) * MLA_V) for hh in range(MLA_HEADS)],
        axis=1).reshape(MLA_KV_LORA, MLA_HEADS * MLA_HP).astype(BF16)

    lg = ret_decay_logit[i]
    padl = lambda v: jnp.pad(v.reshape(1, -1), ((0, 0), (0, LANES - 2 * SSD_HEADS)))
    return {
        "n1": norm1_w[i].reshape(1, d), "n2": norm2_w[i].reshape(1, d),
        "w_mla": w_mla, "w_ret": w_ret, "w_z": w_z, "w_xbc": w_xbc, "w_dt": w_dt,
        "qn": mla_q_norm_w[i].reshape(1, -1), "wq": wq, "wqr": wqr,
        "kvn": mla_kv_norm_w[i].reshape(1, -1), "wk": wk, "wv": wv,
        "wo": w_out[i].astype(BF16),
        "ret_lgl": jnp.repeat(lg, RET_V, axis=1),
        "ret_lgh": jnp.broadcast_to(lg.reshape(2 * RET_HEADS, 1), (2 * RET_HEADS, CHUNK)),
        "conv_w": jnp.pad(ssd_conv_w[i], ((0, 8 - ssd_conv_w.shape[1]), (0, 0))),
        "conv_b": ssd_conv_b[i].reshape(1, -1),
        "dt_bias": padl(ssd_dt_bias[i]), "a_log": padl(ssd_a_log[i]),
        "d_vec": jnp.repeat(ssd_d[i], SSD_HEADDIM).reshape(1, -1),
        "ssd_nw": ssd_norm_w[i].reshape(1, -1),
    }


def _rope_tables(n_lat, n_ctx):
    f32 = np.float32
    rows = n_lat // GRID_W
    row = np.repeat(np.arange(rows, dtype=f32), GRID_W)
    col = np.tile(np.arange(GRID_W, dtype=f32), rows)
    inv_freq = (f32(ROPE_BASE) ** (-np.arange(0, ROPE_AXIS_DIM, 2, dtype=f32) / f32(ROPE_AXIS_DIM))).astype(f32)
    ang_r = row[:, None] * inv_freq
    ang_c = col[:, None] * inv_freq
    ang = np.concatenate([ang_r, ang_r, ang_c, ang_c], axis=-1).astype(f32)
    c2 = f32(MLA_SCALE * math.log2(math.e))

    def build(cos, sin):
        n = cos.shape[0]
        tail = np.zeros((n, MLA_HP - MLA_NOPE - MLA_ROPE), f32)
        cosq = np.concatenate([np.full((n, MLA_NOPE), c2, f32), c2 * cos, tail], 1)
        sinq = np.concatenate([np.zeros((n, MLA_NOPE), f32), c2 * sin, tail], 1)
        ck = np.concatenate([cos, sin, np.zeros((n, LANES - 2 * MLA_ROPE), f32)], 1)
        return tuple(jnp.asarray(a, F32) for a in (cosq, sinq, ck))

    lat = build(np.cos(ang).astype(f32), np.sin(ang).astype(f32))
    ctx = build(np.ones((n_ctx, MLA_ROPE), f32), np.zeros((n_ctx, MLA_ROPE), f32))
    return lat, ctx


def _token_tile(t, pref):
    tm = min(pref, t)
    while t % tm:
        tm //= 2
    return tm


def kernel(x, c, ctx, c_ctx, mod_w, mod_b, norm1_w, norm2_w, w_in, w_out, mla_q_norm_w, mla_w_qb, mla_kv_norm_w, mla_w_kb, mla_w_vb, ret_decay_logit, ssd_conv_w, ssd_conv_b, ssd_dt_bias, ssd_a_log, ssd_d, ssd_norm_w, ffn_w_gate, ffn_w_up, ffn_w_down, moe_router, moe_w_gate, moe_w_up, moe_w_down, final_norm_w):
    b, n_lat, d = x.shape
    n_ctx = ctx.shape[1]
    depth = mod_w.shape[0]
    assert b + 1 <= 8 and n_lat % CHUNK == 0 and n_ctx % CHUNK == 0 and n_lat % GRID_W == 0

    cond = jnp.concatenate([c, c_ctx[None, :], jnp.zeros((8 - b - 1, d), F32)], axis=0)
    mod_all = _modulation(cond, mod_w, mod_b)
    tabs_x, tabs_c = _rope_tables(n_lat, n_ctx)
    fnw = final_norm_w.reshape(1, d)

    tm_x = _token_tile(n_lat, 1024)
    tm_c = _token_tile(n_ctx, 256)
    tm_moe = _token_tile(n_lat, 1024)
    tq_x = _token_tile(n_lat, 1024)
    tk_x = 2048
    zero_ret = jnp.zeros((b, 2, RET_QK, RET_HEADS * RET_V), F32)
    zero_ssd = jnp.zeros((b, 2, SSD_STATE, SSD_INNER), F32)

    h_ctx = ctx
    for i in range(depth):
        last = i == depth - 1
        lw = _layer_weights(i, norm1_w, norm2_w, w_in, w_out, mla_q_norm_w, mla_w_qb, mla_kv_norm_w, mla_w_kb,
                            mla_w_vb, ret_decay_logit, ssd_conv_w, ssd_conv_b, ssd_dt_bias, ssd_a_log, ssd_d,
                            ssd_norm_w)
        m6 = mod_all[i].reshape(8, 6, d)
        pad2 = jnp.zeros((b, 2, d), F32)
        mods_x = jnp.concatenate([m6[:b], pad2], axis=1)
        mods_c = jnp.concatenate([jnp.broadcast_to(m6[b:b + 1], (b, 6, d)), pad2], axis=1)

        qc, kc, vc, retc, zc, xbcc, dtc = _inproj(h_ctx, mods_c, lw, tabs_c, tm_c)
        ret_yc, ret_fin = _ret_scan(retc, lw, zero_ret)
        ssd_yc, ssd_fin = _ssd_scan(xbcc, zc, dtc, lw, zero_ssd)

        qx, kx, vx, retx, zx, xbcx, dtx = _inproj(x, mods_x, lw, tabs_x, tm_x)
        att_x = _attention(qx, [(kc, vc), (kx, vx)], tq_x, tk_x)
        ret_yx, _ = _ret_scan(retx, lw, ret_fin)
        ssd_yx, _ = _ssd_scan(xbcx, zx, dtx, lw, ssd_fin)

        if i % 2 == 0:
            j = i // 2
            lw.update(wg=ffn_w_gate[j].astype(BF16), wu=ffn_w_up[j].astype(BF16), wd=ffn_w_down[j].astype(BF16))
            x = _dense_block(x, att_x, ret_yx, ssd_yx, mods_x, lw, tm_x)
            if last:
                x = _final_norm(x, fnw, tm_x)
        else:
            j = i // 2
            wg, wu, wt, wd = _moe_weights(moe_w_gate[j], moe_w_up[j], moe_w_down[j])
            r_f32 = jnp.pad(moe_router[j], ((0, 0), (0, LANES - N_EXPERTS)))
            r_hi = r_f32.astype(BF16)
            r_lo = (r_f32 - r_hi.astype(F32)).astype(BF16)
            lw.update(router=jnp.stack([r_hi, r_lo]), moe_wg=wg, moe_wu=wu, moe_wt=wt, moe_wd=wd)
            x1, *routed = _route_block(x, att_x, ret_yx, ssd_yx, mods_x, lw, tm_moe)
            x = _moe_block(x1, *routed, mods_x, lw, fnw, last, tm_moe)

        if not last:
            att_c = _attention(qc, [(kc, vc)], tm_c, tk_x)
            if i % 2 == 0:
                h_ctx = _dense_block(h_ctx, att_c, ret_yc, ssd_yc, mods_c, lw, tm_c)
            else:
                c1, *routed = _route_block(h_ctx, att_c, ret_yc, ssd_yc, mods_c, lw, tm_c)
                h_ctx = _moe_block(c1, *routed, mods_c, lw, fnw, False, tm_c)
    return x
```

```python
import functools
import math

import jax
import jax.numpy as jnp
import numpy as np
from jax import lax
from jax.experimental import pallas as pl
from jax.experimental.pallas import tpu as pltpu

F32 = jnp.float32
BF16 = jnp.bfloat16
HIGHEST = lax.Precision.HIGHEST

EPS = 1e-6
CHUNK = 128
GRID_W = 64
MLA_HEADS = 4
MLA_Q_LORA = 256
MLA_KV_LORA = 128
MLA_NOPE = 64
MLA_ROPE = 32
MLA_V = 64
MLA_SCALE = (MLA_NOPE + MLA_ROPE) ** -0.5
ROPE_AXIS_DIM = MLA_ROPE // 2
ROPE_BASE = 10000.0
RET_HEADS = 4
RET_QK = 64
RET_V = 64
SSD_HEADS = 8
SSD_HEADDIM = 64
SSD_GROUPS = 2
SSD_STATE = 64
SSD_INNER = SSD_HEADS * SSD_HEADDIM
SSD_CONV_CH = SSD_INNER + 2 * SSD_GROUPS * SSD_STATE
N_EXPERTS = 8
IN_SIZES = (MLA_Q_LORA, MLA_KV_LORA, MLA_ROPE,
            RET_HEADS * RET_QK, RET_HEADS * RET_QK, RET_HEADS * RET_V, RET_HEADS * RET_V,
            SSD_INNER, SSD_CONV_CH, SSD_HEADS, SSD_HEADS)

LANES = 128
V7X_VMEM_LIMIT = 56 * 1024 * 1024

MLA_HP = 128
NEG = -1e30


def _cparams(sem):
    return pltpu.CompilerParams(dimension_semantics=sem, vmem_limit_bytes=V7X_VMEM_LIMIT)


def _dot(a, b):
    return jnp.dot(a, b, preferred_element_type=F32)


def _dot_nt(a, b):
    return lax.dot_general(a, b, (((1,), (1,)), ((), ())), preferred_element_type=F32)


def _dot_tn(a, b):
    return lax.dot_general(a, b, (((0,), (0,)), ((), ())), preferred_element_type=F32)


def _split3(x):
    hi = x.astype(BF16)
    r1 = x - hi.astype(F32)
    mid = r1.astype(BF16)
    lo = (r1 - mid.astype(F32)).astype(BF16)
    return hi, mid, lo


def _dot_exact_rhs01(x, e):
    hi, mid, lo = _split3(x)
    return _dot(hi, e) + _dot(mid, e) + _dot(lo, e)


def _dot_exact_lhs01(e, x):
    hi, mid, lo = _split3(x)
    return _dot(e, hi) + _dot(e, mid) + _dot(e, lo)


def _silu(x):
    return x / (1.0 + jnp.exp(-x))


def _rms(x, w):
    return x * lax.rsqrt(jnp.mean(x * x, axis=-1, keepdims=True) + EPS) * w


def _iota(shape, dim):
    return lax.broadcasted_iota(jnp.int32, shape, dim)


def _mod_kernel(a_ref, w_ref, b_ref, o_ref):
    k = pl.program_id(1)

    @pl.when(k == 0)
    def _():
        o_ref[0] = jnp.broadcast_to(b_ref[0], o_ref.shape[1:])

    o_ref[0] += jnp.dot(_silu(a_ref[...]), w_ref[0], precision=HIGHEST, preferred_element_type=F32)


def _modulation(cond, mod_w, mod_b):
    depth, d, n = mod_w.shape
    tk = 256
    return pl.pallas_call(
        _mod_kernel,
        grid=(depth, d // tk),
        in_specs=[pl.BlockSpec((8, tk), lambda i, k: (0, k)),
                  pl.BlockSpec((1, tk, n), lambda i, k: (i, k, 0)),
                  pl.BlockSpec((1, 1, n), lambda i, k: (i, 0, 0))],
        out_specs=pl.BlockSpec((1, 8, n), lambda i, k: (i, 0, 0)),
        out_shape=jax.ShapeDtypeStruct((depth, 8, n), F32),
        compiler_params=_cparams(("parallel", "arbitrary")),
        name="modulation",
    )(cond, mod_w, mod_b.reshape(depth, 1, n))


W_MLA, W_RET, W_Z, W_XBC, W_DT = 512, 1024, 512, 768, 128


def _ones_lane_mask(shape):
    lane = _iota(shape, len(shape) - 1) % (2 * MLA_HP)
    return (lane == MLA_HP - 1) | (lane == MLA_HP)


def _inproj_kernel(x_ref, mod_ref, n1_ref, wmla_ref, wret_ref, wz_ref, wxbc_ref, wdt_ref,
                   qn_ref, wq_ref, wqr_ref, kvn_ref, wk_ref, wv_ref, cq_ref, sq_ref, ck_ref,
                   q_out, k_out, v_out, ret_out, z_out, xbc_out, dt_out):
    x = x_ref[0]
    h = _rms(x, n1_ref[...]) * (1.0 + mod_ref[0, 1:2, :]) + mod_ref[0, 0:1, :]
    hb = h.astype(BF16)

    pm = _dot(hb, wmla_ref[...])
    ret = _dot(hb, wret_ref[...])
    qcols = _iota(ret.shape, 1) < RET_HEADS * RET_QK
    ret_out[0] = jnp.where(qcols, ret * (RET_QK ** -0.5), ret).astype(BF16)
    z_out[0] = _dot(hb, wz_ref[...]).astype(BF16)
    xbc_out[0] = _dot(hb, wxbc_ref[...]).astype(BF16)
    dt_out[0] = _dot(hb, wdt_ref[...])

    cqb = _rms(pm[:, 0:MLA_Q_LORA], qn_ref[...]).astype(BF16)
    qm = _dot(cqb, wq_ref[...])
    qr = _dot(cqb, wqr_ref[...])
    cosq, sinq = cq_ref[...], sq_ref[...]
    for hh in range(MLA_HEADS):
        sl = slice(hh * MLA_HP, (hh + 1) * MLA_HP)
        q_out[0, :, sl] = (qm[:, sl] * cosq + qr[:, sl] * sinq).astype(BF16)

    ckvn = _rms(pm[:, MLA_Q_LORA:MLA_Q_LORA + MLA_KV_LORA], kvn_ref[...])
    u = pm[:, MLA_Q_LORA + MLA_KV_LORA:W_MLA] * ck_ref[...]
    kin = jnp.concatenate([ckvn, u], axis=-1).astype(BF16)
    k_out[0] = _dot(kin, wk_ref[...]).astype(BF16)
    v = _dot(ckvn.astype(BF16), wv_ref[...])
    v_out[0] = jnp.where(_ones_lane_mask(v.shape), 1.0, v).astype(BF16)


def _inproj(xs, mods, lw, tabs, tm):
    b, t, d = xs.shape
    const = lambda shape: pl.BlockSpec(shape, lambda i, j: (0,) * len(shape))
    tok = lambda w: pl.BlockSpec((1, tm, w), lambda i, j: (i, j, 0))
    tab = pl.BlockSpec((tm, LANES), lambda i, j: (j, 0))
    widths = (MLA_HEADS * MLA_HP, MLA_HEADS * MLA_HP, MLA_HEADS * MLA_HP, W_RET, W_Z, W_XBC, W_DT)
    dtypes = (BF16, BF16, BF16, BF16, BF16, BF16, F32)
    return pl.pallas_call(
        _inproj_kernel,
        grid=(b, t // tm),
        in_specs=[tok(d), pl.BlockSpec((1, 8, d), lambda i, j: (i, 0, 0)), const((1, d)),
                  const((d, W_MLA)), const((d, W_RET)), const((d, W_Z)), const((d, W_XBC)), const((d, W_DT)),
                  const((1, MLA_Q_LORA)), const((MLA_Q_LORA, MLA_HEADS * MLA_HP)),
                  const((MLA_Q_LORA, MLA_HEADS * MLA_HP)), const((1, MLA_KV_LORA)),
                  const((2 * MLA_KV_LORA, MLA_HEADS * MLA_HP)), const((MLA_KV_LORA, MLA_HEADS * MLA_HP)),
                  tab, tab, tab],
        out_specs=[tok(w) for w in widths],
        out_shape=[jax.ShapeDtypeStruct((b, t, w), dt) for w, dt in zip(widths, dtypes)],
        compiler_params=_cparams(("parallel", "parallel")),
        name="inproj",
    )(xs, mods, lw["n1"], lw["w_mla"], lw["w_ret"], lw["w_z"], lw["w_xbc"], lw["w_dt"],
      lw["qn"], lw["wq"], lw["wqr"], lw["kvn"], lw["wk"], lw["wv"], *tabs)


def _attn_kernel(*refs, seg_blocks, tq):
    nseg = len(seg_blocks)
    q_ref = refs[0]
    kv_refs = refs[1:1 + 2 * nseg]
    o_ref = refs[1 + 2 * nseg]
    carry = tuple((jnp.full((tq, 1), NEG, F32), jnp.zeros((tq, MLA_HP), F32)) for _ in range(MLA_HEADS))
    for si, (tk, nblk) in enumerate(seg_blocks):
        k_ref, v_ref = kv_refs[2 * si], kv_refs[2 * si + 1]

        def body(j, carry, k_ref=k_ref, v_ref=v_ref, tk=tk):
            start = 0 if isinstance(j, int) else pl.multiple_of(j * tk, tk)
            new = []
            for hh in range(MLA_HEADS):
                sl = slice(hh * MLA_HP, (hh + 1) * MLA_HP)
                m, acc = carry[hh]
                s = _dot_nt(q_ref[0, :, sl], k_ref[0, pl.ds(start, tk), sl])
                m_new = jnp.maximum(m, jnp.max(s, axis=-1, keepdims=True))
                p = jnp.exp2(s - m_new).astype(BF16)
                acc = jnp.exp2(m - m_new) * acc + _dot(p, v_ref[0, pl.ds(start, tk), sl])
                new.append((m_new, acc))
            return tuple(new)

        carry = body(0, carry) if nblk == 1 else lax.fori_loop(0, nblk, body, carry)
    lane = _iota((tq, MLA_HP), 1)
    pairs = []
    for hh in range(0, MLA_HEADS, 2):
        acc_e, acc_o = carry[hh][1], carry[hh + 1][1]
        l_e = jnp.sum(jnp.where(lane == MLA_HP - 1, acc_e, 0.0), axis=-1, keepdims=True)
        l_o = jnp.sum(jnp.where(lane == 0, acc_o, 0.0), axis=-1, keepdims=True)
        pairs.append(jnp.where(lane < MLA_V, acc_e / l_e, acc_o / l_o))
    o_ref[0] = jnp.concatenate(pairs, axis=-1).astype(BF16)


def _attention(q, segs, tq, tk):
    b, l, _ = q.shape
    seg_blocks = []
    in_specs = [pl.BlockSpec((1, tq, MLA_HEADS * MLA_HP), lambda i, j: (i, j, 0))]
    args = [q]
    for k, v in segs:
        s = k.shape[1]
        t = min(tk, s)
        seg_blocks.append((t, s // t))
        for _ in range(2):
            in_specs.append(pl.BlockSpec((1, s, MLA_HEADS * MLA_HP), lambda i, j: (i, 0, 0),
                                         pipeline_mode=pl.Buffered(1)))
        args += [k, v]
    return pl.pallas_call(
        functools.partial(_attn_kernel, seg_blocks=tuple(seg_blocks), tq=tq),
        grid=(b, l // tq),
        in_specs=in_specs,
        out_specs=pl.BlockSpec((1, tq, MLA_HEADS * MLA_V), lambda i, j: (i, j, 0)),
        out_shape=jax.ShapeDtypeStruct((b, l, MLA_HEADS * MLA_V), BF16),
        compiler_params=_cparams(("parallel", "arbitrary")),
        name="attention",
    )(*args)


def _ret_kernel(x_ref, lgl_ref, lgh_ref, init_ref, y_ref, fin_ref, s_ref, g_ref, gbuf_ref, *, nblk, sub):
    C, H, N, P = CHUNK, RET_HEADS, RET_QK, RET_V
    HP = H * P
    ph, c = pl.program_id(1), pl.program_id(2)
    blk = jnp.where(ph == 0, nblk - 1 - c, c)

    def log_sigmoid(v):
        return jnp.minimum(v, 0.0) - jnp.log(1.0 + jnp.exp(-jnp.abs(v)))

    lgl = log_sigmoid(lgl_ref[...])
    lgf, lgb = lgl[0:1], lgl[1:2]
    rows = _iota((C, HP), 0).astype(F32)
    head_n = _iota((N, HP), 1) // P

    def diag_blocks(r):
        out = r[0:N]
        for hh in range(1, H):
            out = jnp.where(head_n == hh, r[hh * N:(hh + 1) * N], out)
        return out

    def as_rows(s_c):
        return jnp.concatenate([jnp.where(head_n == hh, s_c, 0.0) for hh in range(H)], axis=0)

    @pl.when((ph == 0) & (c == 0))
    def _():
        g_ref[...] = init_ref[0, 1]

    @pl.when(ph == 0)
    def _():
        w_b = jnp.exp(rows * lgb)
        tot_b = jnp.exp(C * lgb)
        for sb in reversed(range(sub)):
            r0 = sb * C
            k = x_ref[0, r0:r0 + C, HP:2 * HP].astype(F32)
            v = x_ref[0, r0:r0 + C, 2 * HP:3 * HP]
            g = g_ref[...]
            gbuf_ref[blk * sub + sb] = g
            g_ref[...] = g * tot_b + diag_blocks(_dot_tn((k * w_b).astype(BF16), v))

        @pl.when(c == nblk - 1)
        def _():
            fin_ref[0, 1] = g_ref[...]

    @pl.when((ph == 1) & (c == 0))
    def _():
        s_ref[...] = init_ref[0, 0]

    @pl.when(ph == 1)
    def _():
        lgh = log_sigmoid(lgh_ref[...])
        grp = _iota((C, HP), 1) // P
        di = (_iota((C, C), 0) - _iota((C, C), 1)).astype(F32)
        decay = [jnp.exp(jnp.where(di >= 0, di * lgh[hh:hh + 1, :], -di * lgh[H + hh:H + hh + 1, :]))
                 for hh in range(H)]
        e_f = jnp.exp((rows + 1.0) * lgf)
        e_b = jnp.exp((C - rows) * lgb)
        w_f = jnp.exp((C - 1.0 - rows) * lgf)
        tot_f = jnp.exp(C * lgf)
        for sb in range(sub):
            r0 = sb * C
            q = x_ref[0, r0:r0 + C, 0:HP]
            kb = x_ref[0, r0:r0 + C, HP:2 * HP]
            v = x_ref[0, r0:r0 + C, 2 * HP:3 * HP]
            gate = x_ref[0, r0:r0 + C, 3 * HP:4 * HP].astype(F32)
            ws, vs = [], []
            for hh in range(H):
                sc = _dot_nt(jnp.where(grp == hh, q, jnp.zeros_like(q)), kb)
                ws.append((sc * decay[hh]).astype(BF16))
                vs.append(jnp.where(grp == hh, v, jnp.zeros_like(v)))
            y = _dot(jnp.concatenate(ws, axis=1), jnp.concatenate(vs, axis=0))
            qf32 = q.astype(F32)
            s = s_ref[...]
            g_in = gbuf_ref[blk * sub + sb]
            y = y + _dot(jnp.concatenate([(qf32 * e_f).astype(BF16), (qf32 * e_b).astype(BF16)], axis=1),
                         jnp.concatenate([as_rows(s), as_rows(g_in)], axis=0).astype(BF16))
            s_ref[...] = s * tot_f + diag_blocks(_dot_tn((kb.astype(F32) * w_f).astype(BF16), v))

            mu = jnp.zeros_like(y)
            for hh in range(H):
                mh = jnp.sum(jnp.where(grp == hh, y, 0.0), axis=-1, keepdims=True) * (1.0 / P)
                mu = jnp.where(grp == hh, mh, mu)
            yc = y - mu
            var = jnp.zeros_like(y)
            for hh in range(H):
                vh = jnp.sum(jnp.where(grp == hh, yc * yc, 0.0), axis=-1, keepdims=True) * (1.0 / P)
                var = jnp.where(grp == hh, vh, var)
            y_ref[0, r0:r0 + C, :] = (yc * lax.rsqrt(var + EPS) * _silu(gate)).astype(BF16)

        @pl.when(c == nblk - 1)
        def _():
            fin_ref[0, 0] = s_ref[...]


def _block_index(nblk):
    return lambda i, ph, c: (i, (1 - ph) * (nblk - 1 - c) + ph * c, 0)


def _scan_sub(t):
    nc = t // CHUNK
    return 4 if nc % 4 == 0 else (2 if nc % 2 == 0 else 1)


def _ret_scan(ret, lw, init):
    b, t, _ = ret.shape
    sub = _scan_sub(t)
    rows = sub * CHUNK
    nblk = t // rows
    hp = RET_HEADS * RET_V
    return pl.pallas_call(
        functools.partial(_ret_kernel, nblk=nblk, sub=sub),
        grid=(b, 2, nblk),
        in_specs=[pl.BlockSpec((1, rows, W_RET), _block_index(nblk)),
                  pl.BlockSpec((2, hp), lambda i, ph, c: (0, 0)),
                  pl.BlockSpec((2 * RET_HEADS, CHUNK), lambda i, ph, c: (0, 0)),
                  pl.BlockSpec((1, 2, RET_QK, hp), lambda i, ph, c: (i, 0, 0, 0))],
        out_specs=[pl.BlockSpec((1, rows, hp), lambda i, ph, c: (i, ph * c, 0)),
                   pl.BlockSpec((1, 2, RET_QK, hp), lambda i, ph, c: (i, 0, 0, 0))],
        out_shape=[jax.ShapeDtypeStruct((b, t, hp), BF16),
                   jax.ShapeDtypeStruct((b, 2, RET_QK, hp), F32)],
        scratch_shapes=[pltpu.VMEM((RET_QK, hp), F32),
                        pltpu.VMEM((RET_QK, hp), F32),
                        pltpu.VMEM((t // CHUNK, RET_QK, hp), F32)],
        compiler_params=_cparams(("parallel", "arbitrary", "arbitrary")),
        name="retention_scan",
    )(ret, lw["ret_lgl"], lw["ret_lgh"], init)


def _ssd_kernel(x_ref, xp_ref, xn_ref, z_ref, dt_ref, cw_ref, cb_ref, dtb_ref, alog_ref, dvec_ref, nw_ref,
                init_ref, y_ref, fin_ref, s_ref, g_ref, gbuf_ref, act_ref, dtc_ref, abc_ref, *, nblk, sub):
    C, H, N, P, G = CHUNK, SSD_HEADS, SSD_STATE, SSD_HEADDIM, SSD_GROUPS
    HP = H * P
    HPG = H // G
    HALO = SSD_HALO
    ph, c = pl.program_id(1), pl.program_id(2)
    blk = jnp.where(ph == 0, nblk - 1 - c, c)

    ri = _iota((C, C + 2 * HALO), 0)
    ci = _iota((C, C + 2 * HALO), 1)
    sel_prev = (ci == ri + HALO - 1).astype(BF16)
    sel_next = (ci == ri + HALO + 1).astype(BF16)
    lane = _iota((LANES, HP), 0)
    head_of = _iota((LANES, HP), 1) // P
    e_f = (lane == head_of).astype(BF16)
    e_b = (lane == head_of + H).astype(BF16)
    r_i = _iota((C, C), 0)
    c_i = _iota((C, C), 1)
    low = (r_i >= c_i).astype(BF16)
    upp = (r_i <= c_i).astype(BF16)
    grp_n = _iota((N, HP), 1) // (HPG * P)
    neg_a = -jnp.exp(alog_ref[...])
    cw = cw_ref[...]

    def prep(sb, first_pass):
        r0 = sb * C
        if first_pass:
            xc = x_ref[0, r0:r0 + C, :]
            if sb == 0:
                prev = jnp.where(blk > 0, xp_ref[0], jnp.zeros_like(xp_ref[0]))
            else:
                prev = x_ref[0, r0 - HALO:r0, :]
            if sb == sub - 1:
                nxt = jnp.where(blk < nblk - 1, xn_ref[0], jnp.zeros_like(xn_ref[0]))
            else:
                nxt = x_ref[0, r0 + C:r0 + C + HALO, :]
            stacked = jnp.concatenate([prev, xc, nxt], axis=0)
            conv = (cw[0:1] * _dot(sel_prev, stacked) + cw[1:2] * xc.astype(F32)
                    + cw[2:3] * _dot(sel_next, stacked) + cb_ref[...])
            act = _silu(conv)
            raw = dt_ref[0, r0:r0 + C, :] + dtb_ref[...]
            dt = jnp.maximum(raw, 0.0) + jnp.log(1.0 + jnp.exp(-jnp.abs(raw)))
            la = dt * neg_a
            a_f = None
            a_b = _dot_exact_lhs01(upp, la)
            act_ref[blk * sub + sb] = act
            dtc_ref[blk * sub + sb] = dt
            abc_ref[blk * sub + sb] = a_b
        else:
            act = act_ref[blk * sub + sb]
            dt = dtc_ref[blk * sub + sb]
            a_b = abc_ref[blk * sub + sb]
            la = dt * neg_a
            a_f = _dot_exact_lhs01(low, la)
        xs = act[:, 0:HP]
        bm = act[:, HP:HP + G * N].astype(BF16)
        cm = act[:, HP + G * N:HP + 2 * G * N].astype(BF16)
        return xs, bm, cm, dt, la, a_f, a_b

    def state_update(state, bm, xs, wdt, tot8, e_dir, row):
        xw = (xs * _dot(wdt.astype(BF16), e_dir)).astype(BF16)
        tot = _dot_exact_rhs01(tot8, e_dir)[row:row + 1]
        r = _dot_tn(bm, xw)
        upd = r[0:N]
        for gg in range(1, G):
            upd = jnp.where(grp_n == gg, r[gg * N:(gg + 1) * N], upd)
        return state * tot + upd

    def as_rows(s_c):
        return jnp.concatenate([jnp.where(grp_n == gg, s_c, 0.0) for gg in range(G)], axis=0).astype(BF16)

    @pl.when((ph == 0) & (c == 0))
    def _():
        g_ref[...] = init_ref[0, 1]

    @pl.when(ph == 0)
    def _():
        for sb in reversed(range(sub)):
            xs, bm, _, dt, _, _, a_b = prep(sb, True)
            g = g_ref[...]
            gbuf_ref[blk * sub + sb] = g
            wdt = jnp.exp(a_b[0:1, :] - a_b) * dt
            g_ref[...] = state_update(g, bm, xs, wdt, jnp.exp(a_b[0:8, :]), e_b, 0)

        @pl.when(c == nblk - 1)
        def _():
            fin_ref[0, 1] = g_ref[...]

    @pl.when((ph == 1) & (c == 0))
    def _():
        s_ref[...] = init_ref[0, 0]

    @pl.when(ph == 1)
    def _():
        grp_l = _iota((C, G * N), 1) // N
        head_l = _iota((C, HP), 1) // P
        tri_f = r_i >= c_i
        diag = r_i == c_i
        for sb in range(sub):
            r0 = sb * C
            xs, bm, cm, dt, la, a_f, a_b = prep(sb, False)
            la_t = la.T[0:2 * H]
            dt_t = dt.T[0:2 * H]
            af_row = _dot_exact_rhs01(la_t, upp)
            ab_row = _dot_exact_rhs01(la_t, low)
            sc = [_dot_nt(jnp.where(grp_l == gg, cm, jnp.zeros_like(cm)), bm) for gg in range(G)]
            xsb = xs.astype(BF16)
            ws, vs = [], []
            for hh in range(H):
                seg_f = a_f[:, hh:hh + 1] - af_row[hh:hh + 1, :]
                seg_b = a_b[:, H + hh:H + hh + 1] - ab_row[H + hh:H + hh + 1, :]
                dt_f, dt_b = dt_t[hh:hh + 1, :], dt_t[H + hh:H + hh + 1, :]
                dm = (jnp.exp(jnp.where(tri_f, seg_f, seg_b)) * jnp.where(tri_f, dt_f, dt_b)
                      + jnp.where(diag, dt_b, 0.0))
                ws.append((sc[hh // HPG] * dm).astype(BF16))
                vs.append(jnp.where(head_l == hh, xsb, jnp.zeros_like(xsb)))
            y = _dot(jnp.concatenate(ws, axis=1), jnp.concatenate(vs, axis=0))

            s = s_ref[...]
            g_in = gbuf_ref[blk * sub + sb]
            y = y + _dot(jnp.exp(a_f).astype(BF16), e_f) * _dot(cm, as_rows(s))
            y = y + _dot(jnp.exp(a_b).astype(BF16), e_b) * _dot(cm, as_rows(g_in))

            wdt = jnp.exp(a_f[C - 1:C, :] - a_f) * dt
            s_ref[...] = state_update(s, bm, xs, wdt, jnp.exp(a_f[C - 8:C, :]), e_f, 7)

            y = y + xs * dvec_ref[...]
            y = y * _silu(z_ref[0, r0:r0 + C, :].astype(F32))
            y_ref[0, r0:r0 + C, :] = _rms(y, nw_ref[...]).astype(BF16)

        @pl.when(c == nblk - 1)
        def _():
            fin_ref[0, 0] = s_ref[...]


SSD_HALO = 16


def _ssd_scan(xbc, z, dt, lw, init):
    b, t, _ = xbc.shape
    sub = _scan_sub(t)
    rows = sub * CHUNK
    nblk = t // rows
    hp = SSD_INNER
    per = rows // SSD_HALO
    nhalo = t // SSD_HALO
    bidx = _block_index(nblk)

    def x_idx(i, ph, c):
        return (i, (1 - ph) * (nblk - 1 - c), 0)

    def prev_idx(i, ph, c):
        return (i, jnp.maximum((1 - ph) * (nblk - 1 - c) * per - 1, 0), 0)

    def next_idx(i, ph, c):
        return (i, jnp.minimum(((1 - ph) * (nblk - 1 - c) + 1) * per, nhalo - 1), 0)

    const = lambda shape: pl.BlockSpec(shape, lambda i, ph, c: (0,) * len(shape))
    return pl.pallas_call(
        functools.partial(_ssd_kernel, nblk=nblk, sub=sub),
        grid=(b, 2, nblk),
        in_specs=[pl.BlockSpec((1, rows, W_XBC), x_idx),
                  pl.BlockSpec((1, SSD_HALO, W_XBC), prev_idx),
                  pl.BlockSpec((1, SSD_HALO, W_XBC), next_idx),
                  pl.BlockSpec((1, rows, W_Z), bidx),
                  pl.BlockSpec((1, rows, W_DT), x_idx),
                  const((8, W_XBC)), const((1, W_XBC)), const((1, LANES)), const((1, LANES)),
                  const((1, hp)), const((1, hp)),
                  pl.BlockSpec((1, 2, SSD_STATE, hp), lambda i, ph, c: (i, 0, 0, 0))],
        out_specs=[pl.BlockSpec((1, rows, hp), lambda i, ph, c: (i, ph * c, 0)),
                   pl.BlockSpec((1, 2, SSD_STATE, hp), lambda i, ph, c: (i, 0, 0, 0))],
        out_shape=[jax.ShapeDtypeStruct((b, t, hp), BF16),
                   jax.ShapeDtypeStruct((b, 2, SSD_STATE, hp), F32)],
        scratch_shapes=[pltpu.VMEM((SSD_STATE, hp), F32),
                        pltpu.VMEM((SSD_STATE, hp), F32),
                        pltpu.VMEM((t // CHUNK, SSD_STATE, hp), F32),
                        pltpu.VMEM((t // CHUNK, CHUNK, W_XBC), F32),
                        pltpu.VMEM((t // CHUNK, CHUNK, W_DT), F32),
                        pltpu.VMEM((t // CHUNK, CHUNK, W_DT), F32)],
        compiler_params=_cparams(("parallel", "arbitrary", "arbitrary")),
        name="ssd_scan",
    )(xbc, xbc, xbc, z, dt, lw["conv_w"], lw["conv_b"], lw["dt_bias"], lw["a_log"], lw["d_vec"], lw["ssd_nw"], init)


def _mix_residual(x_ref, att_ref, ret_ref, ssd_ref, mod_ref, n2_ref, wo_ref):
    mix = jnp.concatenate([att_ref[0], ret_ref[0], ssd_ref[0]], axis=-1)
    x1 = x_ref[0] + mod_ref[0, 2:3, :] * _dot(mix, wo_ref[...])
    h = _rms(x1, n2_ref[...]) * (1.0 + mod_ref[0, 4:5, :]) + mod_ref[0, 3:4, :]
    return x1, h


def _dense_kernel(x_ref, att_ref, ret_ref, ssd_ref, mod_ref, n2_ref, wo_ref, wg_ref, wu_ref, wd_ref, o_ref,
                  *, ff_chunk):
    x1, h = _mix_residual(x_ref, att_ref, ret_ref, ssd_ref, mod_ref, n2_ref, wo_ref)
    hb = h.astype(BF16)
    ff = wg_ref.shape[1]
    f = jnp.zeros_like(x1)
    for s in range(0, ff, ff_chunk):
        g = _dot(hb, wg_ref[:, s:s + ff_chunk])
        u = _dot(hb, wu_ref[:, s:s + ff_chunk])
        f = f + _dot((_silu(g) * u).astype(BF16), wd_ref[s:s + ff_chunk, :])
    o_ref[0] = x1 + mod_ref[0, 5:6, :] * f


def _dense_block(xs, att, ret, ssd, mods, lw, tm):
    b, t, d = xs.shape
    ff = lw["wg"].shape[1]
    tok = lambda w: pl.BlockSpec((1, tm, w), lambda i, j: (i, j, 0))
    const = lambda shape: pl.BlockSpec(shape, lambda i, j: (0,) * len(shape), pipeline_mode=pl.Buffered(1))
    return pl.pallas_call(
        functools.partial(_dense_kernel, ff_chunk=ff // 2),
        grid=(b, t // tm),
        in_specs=[tok(d), tok(att.shape[-1]), tok(ret.shape[-1]), tok(ssd.shape[-1]),
                  pl.BlockSpec((1, 8, d), lambda i, j: (i, 0, 0)), const((1, d)),
                  const((d, d)), const((d, ff)), const((d, ff)), const((ff, d))],
        out_specs=tok(d),
        out_shape=jax.ShapeDtypeStruct((b, t, d), F32),
        compiler_params=_cparams(("parallel", "parallel")),
        name="outproj_ffn",
    )(xs, att, ret, ssd, mods, lw["n2"], lw["wo"], lw["wg"], lw["wu"], lw["wd"])


SUBLANES = 8


def _route_kernel(x_ref, att_ref, ret_ref, ssd_ref, mod_ref, n2_ref, wo_ref, wr_ref,
                  x1_ref, h_ref, slot_ref, gate_ref, cnt_ref):
    x1, h = _mix_residual(x_ref, att_ref, ret_ref, ssd_ref, mod_ref, n2_ref, wo_ref)
    x1_ref[0] = x1
    tm = h.shape[0]
    h_hi = h.astype(BF16)
    h_ref[0] = h_hi
    h_lo = (h - h_hi.astype(F32)).astype(BF16)
    logits = _dot(h_hi, wr_ref[0]) + _dot(h_lo, wr_ref[0]) + _dot(h_hi, wr_ref[1])
    lane = _iota(logits.shape, 1).astype(F32)
    logits = jnp.where(lane < N_EXPERTS, logits, NEG)
    m1 = jnp.max(logits, axis=-1, keepdims=True)
    i1 = jnp.min(jnp.where(logits == m1, lane, float(LANES)), axis=-1, keepdims=True)
    rest = jnp.where(lane == i1, NEG, logits)
    m2 = jnp.max(rest, axis=-1, keepdims=True)
    i2 = jnp.min(jnp.where(rest == m2, lane, float(LANES)), axis=-1, keepdims=True)
    e2 = jnp.exp(m2 - m1)
    den = 1.0 + e2
    sel1, sel2 = lane == i1, lane == i2
    chosen = jnp.where(sel1 | sel2, 1.0, 0.0)
    before = (_iota((tm, tm), 0) > _iota((tm, tm), 1)).astype(BF16)
    pos = _dot(before, chosen.astype(BF16))
    cnt = jnp.sum(chosen, axis=0, keepdims=True)
    nblk = sum(jnp.where(cnt > float(k * MOE_ROWS), 1.0, 0.0) for k in range(pl.cdiv(tm, MOE_ROWS)))
    lower_e = (_iota((LANES, LANES), 0) < _iota((LANES, LANES), 1)).astype(BF16)
    off = _dot_exact_rhs01(jnp.broadcast_to(nblk * float(MOE_ROWS), (SUBLANES, LANES)), lower_e)[0:1]
    slot = off + pos
    s1 = jnp.sum(jnp.where(sel1, slot, 0.0), axis=-1, keepdims=True)
    s2 = jnp.sum(jnp.where(sel2, slot, 0.0), axis=-1, keepdims=True)
    lane_i = _iota(logits.shape, 1)
    slots = jnp.where(lane_i == 0, s1, jnp.where(lane_i == 1, s2, -1.0))
    slot_ref[0, 0] = slots.T[0:SUBLANES].astype(jnp.int32)
    gate_ref[0] = jnp.where(sel1, 1.0 / den, 0.0) + jnp.where(sel2, e2 / den, 0.0)
    row = _iota((SUBLANES, LANES), 0)
    cnt_ref[0, 0] = jnp.where(row == 0, cnt, jnp.where(row == 1, off, jnp.where(row == 2, nblk, 0.0))
                              ).astype(jnp.int32)


def _route_block(xs, att, ret, ssd, mods, lw, tm):
    b, t, d = xs.shape
    nt = t // tm
    tok = lambda w: pl.BlockSpec((1, tm, w), lambda i, j: (i, j, 0))
    const = lambda shape: pl.BlockSpec(shape, lambda i, j: (0,) * len(shape))
    per_tile = lambda r, w: pl.BlockSpec((1, 1, r, w), lambda i, j: (i, j, 0, 0))
    return pl.pallas_call(
        _route_kernel,
        grid=(b, nt),
        in_specs=[tok(d), tok(att.shape[-1]), tok(ret.shape[-1]), tok(ssd.shape[-1]),
                  pl.BlockSpec((1, 8, d), lambda i, j: (i, 0, 0)), const((1, d)),
                  const((d, d)), const((2, d, LANES))],
        out_specs=[tok(d), tok(d), per_tile(SUBLANES, tm), tok(LANES), per_tile(SUBLANES, LANES)],
        out_shape=[jax.ShapeDtypeStruct((b, t, d), F32), jax.ShapeDtypeStruct((b, t, d), BF16),
                   jax.ShapeDtypeStruct((b, nt, SUBLANES, tm), jnp.int32),
                   jax.ShapeDtypeStruct((b, t, LANES), F32),
                   jax.ShapeDtypeStruct((b, nt, SUBLANES, LANES), jnp.int32)],
        compiler_params=_cparams(("parallel", "parallel")),
        name="outproj_route",
    )(xs, att, ret, ssd, mods, lw["n2"], lw["wo"], lw["router"])


MOE_ROWS = 256


def _moe_kernel(cnt_ref, slot_ref, h_ref, gate_ref, x1_ref, mod_ref, wgu_ref, wd_ref, fn_ref, o_ref,
                *, final_norm, ff_split):
    e = pl.program_id(2)
    rb = MOE_ROWS
    tm = h_ref.shape[1]

    @pl.when(e == 0)
    def _():
        o_ref[...] = jnp.zeros_like(o_ref)

    off = cnt_ref[0, 0, 1, e]
    nblocks = cnt_ref[0, 0, 2, e]
    slot1 = slot_ref[0, 0, 0:1, :]
    slot2 = slot_ref[0, 0, 1:2, :]
    gates = gate_ref[0]
    ge = jnp.sum(jnp.where(_iota(gates.shape, 1) == e, gates, 0.0), axis=-1, keepdims=True)

    def block(bi, carry):
        rows = off + bi * rb + _iota((rb, tm), 0)
        sel = jnp.where((rows == slot1) | (rows == slot2), 1.0, 0.0).astype(BF16)
        lhs = _dot(sel, h_ref[0]).astype(BF16)
        y = None
        for s0, s1 in ff_split:
            gu = _dot(lhs, wgu_ref[0, :, 2 * s0:2 * s1])
            tiles = [gu[:, t * LANES:(t + 1) * LANES] for t in range(2 * (s1 - s0) // LANES)]
            g, u = jnp.concatenate(tiles[0::2], axis=1), jnp.concatenate(tiles[1::2], axis=1)
            part = _dot((_silu(g) * u).astype(BF16), wd_ref[0, s0:s1, :])
            y = part if y is None else y + part
        o_ref[0] += ge * _dot_tn(sel, y.astype(BF16))
        return carry

    lax.fori_loop(0, nblocks, block, 0)

    @pl.when(e == N_EXPERTS - 1)
    def _():
        out = x1_ref[0] + mod_ref[0, 5:6, :] * o_ref[0]
        o_ref[0] = _rms(out, fn_ref[...]) if final_norm else out


MOE_FF_SLICE = 512


def _moe_ff_split(ff):
    return tuple((s, min(s + MOE_FF_SLICE, ff)) for s in range(0, ff, MOE_FF_SLICE))


def _moe_weights(w_gate, w_up, w_down):
    e, d, ff = w_gate.shape
    both = jnp.stack([w_gate.reshape(e, d, ff // LANES, LANES), w_up.reshape(e, d, ff // LANES, LANES)], axis=3)
    return both.reshape(e, d, 2 * ff).astype(BF16), w_down.astype(BF16)


def _moe_block(x1, h, slots, gates, counts, mods, lw, fnw, final_norm, tm):
    b, t, d = x1.shape
    ff = lw["moe_wd"].shape[1]
    assert MOE_ROWS % (2 * SUBLANES) == 0
    tok = lambda w: pl.BlockSpec((1, tm, w), lambda i, j, e: (i, j, 0))
    per_tile = lambda w, **kw: pl.BlockSpec((1, 1, SUBLANES, w), lambda i, j, e: (i, j, 0, 0), **kw)
    expert = lambda r, w: pl.BlockSpec((1, r, w), lambda i, j, e: (e, 0, 0))
    return pl.pallas_call(
        functools.partial(_moe_kernel, final_norm=final_norm, ff_split=_moe_ff_split(ff)),
        grid=(b, t // tm, N_EXPERTS),
        in_specs=[per_tile(LANES, memory_space=pltpu.SMEM), per_tile(tm), tok(d), tok(LANES), tok(d),
                  pl.BlockSpec((1, 8, d), lambda i, j, e: (i, 0, 0)),
                  expert(d, 2 * ff), expert(ff, d),
                  pl.BlockSpec((1, d), lambda i, j, e: (0, 0))],
        out_specs=tok(d),
        out_shape=jax.ShapeDtypeStruct((b, t, d), F32),
        compiler_params=_cparams(("parallel", "parallel", "arbitrary")),
        name="moe",
    )(counts, slots, h, gates, x1, mods, lw["moe_wgu"], lw["moe_wd"], fnw)


def _final_norm_kernel(x_ref, w_ref, o_ref):
    o_ref[0] = _rms(x_ref[0], w_ref[...])


def _final_norm(xs, fnw, tm):
    b, t, d = xs.shape
    return pl.pallas_call(
        _final_norm_kernel,
        grid=(b, t // tm),
        in_specs=[pl.BlockSpec((1, tm, d), lambda i, j: (i, j, 0)), pl.BlockSpec((1, d), lambda i, j: (0, 0))],
        out_specs=pl.BlockSpec((1, tm, d), lambda i, j: (i, j, 0)),
        out_shape=jax.ShapeDtypeStruct((b, t, d), F32),
        compiler_params=_cparams(("parallel", "parallel")),
        name="final_norm",
    )(xs, fnw)


def _rot_cols(w):
    a, b2, c, d = jnp.split(w, 4, axis=-1)
    return jnp.concatenate([-b2, a, -d, c], axis=-1)


def _layer_weights(i, norm1_w, norm2_w, w_in, w_out, mla_q_norm_w, mla_w_qb, mla_kv_norm_w, mla_w_kb, mla_w_vb,
                   ret_decay_logit, ssd_conv_w, ssd_conv_b, ssd_dt_bias, ssd_a_log, ssd_d, ssd_norm_w):
    d = w_in.shape[1]
    offs = np.cumsum((0,) + IN_SIZES)
    wi = w_in[i]
    zeros = lambda n: jnp.zeros((d, n), F32)
    kpe = wi[:, offs[2]:offs[3]]
    w_mla = jnp.concatenate([wi[:, offs[0]:offs[3]], _rot_cols(kpe),
                             zeros(W_MLA - MLA_Q_LORA - MLA_KV_LORA - 2 * MLA_ROPE)], axis=1).astype(BF16)
    w_ret = wi[:, offs[3]:offs[7]].astype(BF16)
    w_z = wi[:, offs[7]:offs[8]].astype(BF16)
    w_xbc = wi[:, offs[8]:offs[9]].astype(BF16)
    w_dt = jnp.concatenate([wi[:, offs[9]:offs[11]], zeros(W_DT - 2 * SSD_HEADS)], axis=1).astype(BF16)

    dh = MLA_NOPE + MLA_ROPE
    wqb = mla_w_qb[i].reshape(MLA_Q_LORA, MLA_HEADS, dh)
    pad = lambda a, lo, hi: jnp.pad(a, ((0, 0), (0, 0), (lo, hi)))
    wq = pad(wqb, 0, MLA_HP - dh).reshape(MLA_Q_LORA, MLA_HEADS * MLA_HP).astype(BF16)
    wqr = pad(_rot_cols(wqb[:, :, MLA_NOPE:]), MLA_NOPE, MLA_HP - dh).reshape(MLA_Q_LORA, MLA_HEADS * MLA_HP).astype(BF16)
    wkb = pad(mla_w_kb[i].reshape(MLA_KV_LORA, MLA_HEADS, MLA_NOPE), 0, MLA_HP - MLA_NOPE)
    place = np.zeros((MLA_ROPE, MLA_HEADS, MLA_HP), np.float32)
    for r in range(MLA_ROPE):
        place[r, :, MLA_NOPE + r] = 1.0
    place = jnp.asarray(place)
    wk = jnp.concatenate([wkb, place, place, jnp.zeros((MLA_KV_LORA - 2 * MLA_ROPE, MLA_HEADS, MLA_HP), F32)],
                         axis=0).reshape(2 * MLA_KV_LORA, MLA_HEADS * MLA_HP).astype(BF16)

    wvb = mla_w_vb[i].reshape(MLA_KV_LORA, MLA_HEADS, MLA_V)
    wv = jnp.concatenate(
        [pad(wvb[:, hh:hh + 1], (hh % 2) * MLA_V, MLA_HP - MLA_V - (hh % 2) * MLA_V) for hh in range(MLA_HEADS)],
        axis=1).reshape(MLA_KV_LORA, MLA_HEADS * MLA_HP).astype(BF16)

    lg = ret_decay_logit[i]
    padl = lambda v: jnp.pad(v.reshape(1, -1), ((0, 0), (0, LANES - 2 * SSD_HEADS)))
    return {
        "n1": norm1_w[i].reshape(1, d), "n2": norm2_w[i].reshape(1, d),
        "w_mla": w_mla, "w_ret": w_ret, "w_z": w_z, "w_xbc": w_xbc, "w_dt": w_dt,
        "qn": mla_q_norm_w[i].reshape(1, -1), "wq": wq, "wqr": wqr,
        "kvn": mla_kv_norm_w[i].reshape(1, -1), "wk": wk, "wv": wv,
        "wo": w_out[i].astype(BF16),
        "ret_lgl": jnp.repeat(lg, RET_V, axis=1),
        "ret_lgh": jnp.broadcast_to(lg.reshape(2 * RET_HEADS, 1), (2 * RET_HEADS, CHUNK)),
        "conv_w": jnp.pad(ssd_conv_w[i], ((0, 8 - ssd_conv_w.shape[1]), (0, 0))),
        "conv_b": ssd_conv_b[i].reshape(1, -1),
        "dt_bias": padl(ssd_dt_bias[i]), "a_log": padl(ssd_a_log[i]),
        "d_vec": jnp.repeat(ssd_d[i], SSD_HEADDIM).reshape(1, -1),
        "ssd_nw": ssd_norm_w[i].reshape(1, -1),
    }


def _rope_tables(n_lat, n_ctx):
    f32 = np.float32
    rows = n_lat // GRID_W
    row = np.repeat(np.arange(rows, dtype=f32), GRID_W)
    col = np.tile(np.arange(GRID_W, dtype=f32), rows)
    inv_freq = (f32(ROPE_BASE) ** (-np.arange(0, ROPE_AXIS_DIM, 2, dtype=f32) / f32(ROPE_AXIS_DIM))).astype(f32)
    ang_r = row[:, None] * inv_freq
    ang_c = col[:, None] * inv_freq
    ang = np.concatenate([ang_r, ang_r, ang_c, ang_c], axis=-1).astype(f32)
    c2 = f32(MLA_SCALE * math.log2(math.e))

    def build(cos, sin):
        n = cos.shape[0]
        tail = np.zeros((n, MLA_HP - MLA_NOPE - MLA_ROPE), f32)
        cosq = np.concatenate([np.full((n, MLA_NOPE), c2, f32), c2 * cos, tail], 1)
        sinq = np.concatenate([np.zeros((n, MLA_NOPE), f32), c2 * sin, tail], 1)
        ck = np.concatenate([cos, sin, np.zeros((n, LANES - 2 * MLA_ROPE), f32)], 1)
        return tuple(jnp.asarray(a, F32) for a in (cosq, sinq, ck))

    lat = build(np.cos(ang).astype(f32), np.sin(ang).astype(f32))
    ctx = build(np.ones((n_ctx, MLA_ROPE), f32), np.zeros((n_ctx, MLA_ROPE), f32))
    return lat, ctx


def _token_tile(t, pref):
    tm = min(pref, t)
    while t % tm:
        tm //= 2
    return tm


def kernel(x, c, ctx, c_ctx, mod_w, mod_b, norm1_w, norm2_w, w_in, w_out, mla_q_norm_w, mla_w_qb, mla_kv_norm_w, mla_w_kb, mla_w_vb, ret_decay_logit, ssd_conv_w, ssd_conv_b, ssd_dt_bias, ssd_a_log, ssd_d, ssd_norm_w, ffn_w_gate, ffn_w_up, ffn_w_down, moe_router, moe_w_gate, moe_w_up, moe_w_down, final_norm_w):
    b, n_lat, d = x.shape
    n_ctx = ctx.shape[1]
    depth = mod_w.shape[0]
    assert b + 1 <= 8 and n_lat % CHUNK == 0 and n_ctx % CHUNK == 0 and n_lat % GRID_W == 0

    cond = jnp.concatenate([c, c_ctx[None, :], jnp.zeros((8 - b - 1, d), F32)], axis=0)
    mod_all = _modulation(cond, mod_w, mod_b)
    tabs_x, tabs_c = _rope_tables(n_lat, n_ctx)
    fnw = final_norm_w.reshape(1, d)

    tm_x = _token_tile(n_lat, 1024)
    tm_c = _token_tile(n_ctx, 256)
    tm_moe = _token_tile(n_lat, 1024)
    tq_x = _token_tile(n_lat, 1024)
    tk_x = 2048
    zero_ret = jnp.zeros((b, 2, RET_QK, RET_HEADS * RET_V), F32)
    zero_ssd = jnp.zeros((b, 2, SSD_STATE, SSD_INNER), F32)

    h_ctx = ctx
    for i in range(depth):
        last = i == depth - 1
        lw = _layer_weights(i, norm1_w, norm2_w, w_in, w_out, mla_q_norm_w, mla_w_qb, mla_kv_norm_w, mla_w_kb,
                            mla_w_vb, ret_decay_logit, ssd_conv_w, ssd_conv_b, ssd_dt_bias, ssd_a_log, ssd_d,
                            ssd_norm_w)
        m6 = mod_all[i].reshape(8, 6, d)
        pad2 = jnp.zeros((b, 2, d), F32)
        mods_x = jnp.concatenate([m6[:b], pad2], axis=1)
        mods_c = jnp.concatenate([jnp.broadcast_to(m6[b:b + 1], (b, 6, d)), pad2], axis=1)

        qc, kc, vc, retc, zc, xbcc, dtc = _inproj(h_ctx, mods_c, lw, tabs_c, tm_c)
        ret_yc, ret_fin = _ret_scan(retc, lw, zero_ret)
        ssd_yc, ssd_fin = _ssd_scan(xbcc, zc, dtc, lw, zero_ssd)

        qx, kx, vx, retx, zx, xbcx, dtx = _inproj(x, mods_x, lw, tabs_x, tm_x)
        att_x = _attention(qx, [(kc, vc), (kx, vx)], tq_x, tk_x)
        ret_yx, _ = _ret_scan(retx, lw, ret_fin)
        ssd_yx, _ = _ssd_scan(xbcx, zx, dtx, lw, ssd_fin)

        if i % 2 == 0:
            j = i // 2
            lw.update(wg=ffn_w_gate[j].astype(BF16), wu=ffn_w_up[j].astype(BF16), wd=ffn_w_down[j].astype(BF16))
            x = _dense_block(x, att_x, ret_yx, ssd_yx, mods_x, lw, tm_x)
            if last:
                x = _final_norm(x, fnw, tm_x)
        else:
            j = i // 2
            wgu, wd = _moe_weights(moe_w_gate[j], moe_w_up[j], moe_w_down[j])
            r_f32 = jnp.pad(moe_router[j], ((0, 0), (0, LANES - N_EXPERTS)))
            r_hi = r_f32.astype(BF16)
            r_lo = (r_f32 - r_hi.astype(F32)).astype(BF16)
            lw.update(router=jnp.stack([r_hi, r_lo]), moe_wgu=wgu, moe_wd=wd)
            x1, *routed = _route_block(x, att_x, ret_yx, ssd_yx, mods_x, lw, tm_moe)
            x = _moe_block(x1, *routed, mods_x, lw, fnw, last, tm_moe)

        if not last:
            att_c = _attention(qc, [(kc, vc)], tm_c, tk_x)
            if i % 2 == 0:
                h_ctx = _dense_block(h_ctx, att_c, ret_yc, ssd_yc, mods_c, lw, tm_c)
            else:
                c1, *routed = _route_block(h_ctx, att_c, ret_yc, ssd_yc, mods_c, lw, tm_c)
                h_ctx = _moe_block(c1, *routed, mods_c, lw, fnw, False, tm_c)
    return x
```

```python
import functools
import math

import jax
import jax.numpy as jnp
import numpy as np
from jax import lax
from jax.experimental import pallas as pl
from jax.experimental.pallas import tpu as pltpu

F32 = jnp.float32
BF16 = jnp.bfloat16
HIGHEST = lax.Precision.HIGHEST

EPS = 1e-6
CHUNK = 128
GRID_W = 64
MLA_HEADS = 4
MLA_Q_LORA = 256
MLA_KV_LORA = 128
MLA_NOPE = 64
MLA_ROPE = 32
MLA_V = 64
MLA_SCALE = (MLA_NOPE + MLA_ROPE) ** -0.5
ROPE_AXIS_DIM = MLA_ROPE // 2
ROPE_BASE = 10000.0
RET_HEADS = 4
RET_QK = 64
RET_V = 64
SSD_HEADS = 8
SSD_HEADDIM = 64
SSD_GROUPS = 2
SSD_STATE = 64
SSD_INNER = SSD_HEADS * SSD_HEADDIM
SSD_CONV_CH = SSD_INNER + 2 * SSD_GROUPS * SSD_STATE
N_EXPERTS = 8
IN_SIZES = (MLA_Q_LORA, MLA_KV_LORA, MLA_ROPE,
            RET_HEADS * RET_QK, RET_HEADS * RET_QK, RET_HEADS * RET_V, RET_HEADS * RET_V,
            SSD_INNER, SSD_CONV_CH, SSD_HEADS, SSD_HEADS)

LANES = 128
V7X_VMEM_LIMIT = 56 * 1024 * 1024

MLA_HP = 128
NEG = -1e30


def _cparams(sem):
    return pltpu.CompilerParams(dimension_semantics=sem, vmem_limit_bytes=V7X_VMEM_LIMIT)


def _dot(a, b):
    return jnp.dot(a, b, preferred_element_type=F32)


def _dot_nt(a, b):
    return lax.dot_general(a, b, (((1,), (1,)), ((), ())), preferred_element_type=F32)


def _dot_tn(a, b):
    return lax.dot_general(a, b, (((0,), (0,)), ((), ())), preferred_element_type=F32)


def _split3(x):
    hi = x.astype(BF16)
    r1 = x - hi.astype(F32)
    mid = r1.astype(BF16)
    lo = (r1 - mid.astype(F32)).astype(BF16)
    return hi, mid, lo


def _dot_exact_rhs01(x, e):
    hi, mid, lo = _split3(x)
    return _dot(hi, e) + _dot(mid, e) + _dot(lo, e)


def _dot_exact_lhs01(e, x):
    hi, mid, lo = _split3(x)
    return _dot(e, hi) + _dot(e, mid) + _dot(e, lo)


def _silu(x):
    return x / (1.0 + jnp.exp(-x))


def _rms(x, w):
    return x * lax.rsqrt(jnp.mean(x * x, axis=-1, keepdims=True) + EPS) * w


def _iota(shape, dim):
    return lax.broadcasted_iota(jnp.int32, shape, dim)


def _mod_kernel(a_ref, w_ref, b_ref, o_ref):
    k = pl.program_id(1)

    @pl.when(k == 0)
    def _():
        o_ref[0] = jnp.broadcast_to(b_ref[0], o_ref.shape[1:])

    o_ref[0] += jnp.dot(_silu(a_ref[...]), w_ref[0], precision=HIGHEST, preferred_element_type=F32)


def _modulation(cond, mod_w, mod_b):
    depth, d, n = mod_w.shape
    tk = 256
    return pl.pallas_call(
        _mod_kernel,
        grid=(depth, d // tk),
        in_specs=[pl.BlockSpec((8, tk), lambda i, k: (0, k)),
                  pl.BlockSpec((1, tk, n), lambda i, k: (i, k, 0)),
                  pl.BlockSpec((1, 1, n), lambda i, k: (i, 0, 0))],
        out_specs=pl.BlockSpec((1, 8, n), lambda i, k: (i, 0, 0)),
        out_shape=jax.ShapeDtypeStruct((depth, 8, n), F32),
        compiler_params=_cparams(("parallel", "arbitrary")),
        name="modulation",
    )(cond, mod_w, mod_b.reshape(depth, 1, n))


W_MLA, W_RET, W_Z, W_XBC, W_DT = 512, 1024, 512, 768, 128


def _ones_lane_mask(shape):
    lane = _iota(shape, len(shape) - 1) % (2 * MLA_HP)
    return (lane == MLA_HP - 1) | (lane == MLA_HP)


def _inproj_kernel(x_ref, mod_ref, n1_ref, wmla_ref, wret_ref, wz_ref, wxbc_ref, wdt_ref,
                   qn_ref, wq_ref, wqr_ref, kvn_ref, wk_ref, wv_ref, cq_ref, sq_ref, ck_ref,
                   q_out, k_out, v_out, ret_out, z_out, xbc_out, dt_out):
    x = x_ref[0]
    h = _rms(x, n1_ref[...]) * (1.0 + mod_ref[0, 1:2, :]) + mod_ref[0, 0:1, :]
    hb = h.astype(BF16)

    pm = _dot(hb, wmla_ref[...])
    ret = _dot(hb, wret_ref[...])
    qcols = _iota(ret.shape, 1) < RET_HEADS * RET_QK
    ret_out[0] = jnp.where(qcols, ret * (RET_QK ** -0.5), ret).astype(BF16)
    z_out[0] = _dot(hb, wz_ref[...]).astype(BF16)
    xbc_out[0] = _dot(hb, wxbc_ref[...]).astype(BF16)
    dt_out[0] = _dot(hb, wdt_ref[...])

    cqb = _rms(pm[:, 0:MLA_Q_LORA], qn_ref[...]).astype(BF16)
    qm = _dot(cqb, wq_ref[...])
    qr = _dot(cqb, wqr_ref[...])
    cosq, sinq = cq_ref[...], sq_ref[...]
    for hh in range(MLA_HEADS):
        sl = slice(hh * MLA_HP, (hh + 1) * MLA_HP)
        q_out[0, :, sl] = (qm[:, sl] * cosq + qr[:, sl] * sinq).astype(BF16)

    ckvn = _rms(pm[:, MLA_Q_LORA:MLA_Q_LORA + MLA_KV_LORA], kvn_ref[...])
    u = pm[:, MLA_Q_LORA + MLA_KV_LORA:W_MLA] * ck_ref[...]
    kin = jnp.concatenate([ckvn, u], axis=-1).astype(BF16)
    k_out[0] = _dot(kin, wk_ref[...]).astype(BF16)
    v = _dot(ckvn.astype(BF16), wv_ref[...])
    v_out[0] = jnp.where(_ones_lane_mask(v.shape), 1.0, v).astype(BF16)


def _inproj(xs, mods, lw, tabs, tm):
    b, t, d = xs.shape
    const = lambda shape: pl.BlockSpec(shape, lambda i, j: (0,) * len(shape))
    tok = lambda w: pl.BlockSpec((1, tm, w), lambda i, j: (i, j, 0))
    tab = pl.BlockSpec((tm, LANES), lambda i, j: (j, 0))
    widths = (MLA_HEADS * MLA_HP, MLA_HEADS * MLA_HP, MLA_HEADS * MLA_HP, W_RET, W_Z, W_XBC, W_DT)
    dtypes = (BF16, BF16, BF16, BF16, BF16, BF16, F32)
    return pl.pallas_call(
        _inproj_kernel,
        grid=(b, t // tm),
        in_specs=[tok(d), pl.BlockSpec((1, 8, d), lambda i, j: (i, 0, 0)), const((1, d)),
                  const((d, W_MLA)), const((d, W_RET)), const((d, W_Z)), const((d, W_XBC)), const((d, W_DT)),
                  const((1, MLA_Q_LORA)), const((MLA_Q_LORA, MLA_HEADS * MLA_HP)),
                  const((MLA_Q_LORA, MLA_HEADS * MLA_HP)), const((1, MLA_KV_LORA)),
                  const((2 * MLA_KV_LORA, MLA_HEADS * MLA_HP)), const((MLA_KV_LORA, MLA_HEADS * MLA_HP)),
                  tab, tab, tab],
        out_specs=[tok(w) for w in widths],
        out_shape=[jax.ShapeDtypeStruct((b, t, w), dt) for w, dt in zip(widths, dtypes)],
        compiler_params=_cparams(("parallel", "parallel")),
        name="inproj",
    )(xs, mods, lw["n1"], lw["w_mla"], lw["w_ret"], lw["w_z"], lw["w_xbc"], lw["w_dt"],
      lw["qn"], lw["wq"], lw["wqr"], lw["kvn"], lw["wk"], lw["wv"], *tabs)


def _attn_kernel(*refs, seg_blocks, tq):
    nseg = len(seg_blocks)
    q_ref = refs[0]
    kv_refs = refs[1:1 + 2 * nseg]
    o_ref = refs[1 + 2 * nseg]
    carry = tuple((jnp.full((tq, 1), NEG, F32), jnp.zeros((tq, MLA_HP), F32)) for _ in range(MLA_HEADS))
    for si, (tk, nblk) in enumerate(seg_blocks):
        k_ref, v_ref = kv_refs[2 * si], kv_refs[2 * si + 1]

        def body(j, carry, k_ref=k_ref, v_ref=v_ref, tk=tk):
            start = 0 if isinstance(j, int) else pl.multiple_of(j * tk, tk)
            new = []
            for hh in range(MLA_HEADS):
                sl = slice(hh * MLA_HP, (hh + 1) * MLA_HP)
                m, acc = carry[hh]
                s = _dot_nt(q_ref[0, :, sl], k_ref[0, pl.ds(start, tk), sl])
                m_new = jnp.maximum(m, jnp.max(s, axis=-1, keepdims=True))
                p = jnp.exp2(s - m_new).astype(BF16)
                acc = jnp.exp2(m - m_new) * acc + _dot(p, v_ref[0, pl.ds(start, tk), sl])
                new.append((m_new, acc))
            return tuple(new)

        carry = body(0, carry) if nblk == 1 else lax.fori_loop(0, nblk, body, carry)
    lane = _iota((tq, MLA_HP), 1)
    pairs = []
    for hh in range(0, MLA_HEADS, 2):
        acc_e, acc_o = carry[hh][1], carry[hh + 1][1]
        l_e = jnp.sum(jnp.where(lane == MLA_HP - 1, acc_e, 0.0), axis=-1, keepdims=True)
        l_o = jnp.sum(jnp.where(lane == 0, acc_o, 0.0), axis=-1, keepdims=True)
        pairs.append(jnp.where(lane < MLA_V, acc_e / l_e, acc_o / l_o))
    o_ref[0] = jnp.concatenate(pairs, axis=-1).astype(BF16)


def _attention(q, segs, tq, tk):
    b, l, _ = q.shape
    seg_blocks = []
    in_specs = [pl.BlockSpec((1, tq, MLA_HEADS * MLA_HP), lambda i, j: (i, j, 0))]
    args = [q]
    for k, v in segs:
        s = k.shape[1]
        t = min(tk, s)
        seg_blocks.append((t, s // t))
        for _ in range(2):
            in_specs.append(pl.BlockSpec((1, s, MLA_HEADS * MLA_HP), lambda i, j: (i, 0, 0),
                                         pipeline_mode=pl.Buffered(1)))
        args += [k, v]
    return pl.pallas_call(
        functools.partial(_attn_kernel, seg_blocks=tuple(seg_blocks), tq=tq),
        grid=(b, l // tq),
        in_specs=in_specs,
        out_specs=pl.BlockSpec((1, tq, MLA_HEADS * MLA_V), lambda i, j: (i, j, 0)),
        out_shape=jax.ShapeDtypeStruct((b, l, MLA_HEADS * MLA_V), BF16),
        compiler_params=_cparams(("parallel", "arbitrary")),
        name="attention",
    )(*args)


def _ret_kernel(x_ref, lgl_ref, lgh_ref, init_ref, y_ref, fin_ref, s_ref, g_ref, gbuf_ref, *, nblk, sub):
    C, H, N, P = CHUNK, RET_HEADS, RET_QK, RET_V
    HP = H * P
    ph, c = pl.program_id(1), pl.program_id(2)
    blk = jnp.where(ph == 0, nblk - 1 - c, c)

    def log_sigmoid(v):
        return jnp.minimum(v, 0.0) - jnp.log(1.0 + jnp.exp(-jnp.abs(v)))

    lgl = log_sigmoid(lgl_ref[...])
    lgf, lgb = lgl[0:1], lgl[1:2]
    rows = _iota((C, HP), 0).astype(F32)
    head_n = _iota((N, HP), 1) // P

    def diag_blocks(r):
        out = r[0:N]
        for hh in range(1, H):
            out = jnp.where(head_n == hh, r[hh * N:(hh + 1) * N], out)
        return out

    def as_rows(s_c):
        return jnp.concatenate([jnp.where(head_n == hh, s_c, 0.0) for hh in range(H)], axis=0)

    @pl.when((ph == 0) & (c == 0))
    def _():
        g_ref[...] = init_ref[0, 1]

    @pl.when(ph == 0)
    def _():
        w_b = jnp.exp(rows * lgb)
        tot_b = jnp.exp(C * lgb)
        for sb in reversed(range(sub)):
            r0 = sb * C
            k = x_ref[0, r0:r0 + C, HP:2 * HP].astype(F32)
            v = x_ref[0, r0:r0 + C, 2 * HP:3 * HP]
            g = g_ref[...]
            gbuf_ref[blk * sub + sb] = g
            g_ref[...] = g * tot_b + diag_blocks(_dot_tn((k * w_b).astype(BF16), v))

        @pl.when(c == nblk - 1)
        def _():
            fin_ref[0, 1] = g_ref[...]

    @pl.when((ph == 1) & (c == 0))
    def _():
        s_ref[...] = init_ref[0, 0]

    @pl.when(ph == 1)
    def _():
        lgh = log_sigmoid(lgh_ref[...])
        grp = _iota((C, HP), 1) // P
        di = (_iota((C, C), 0) - _iota((C, C), 1)).astype(F32)
        decay = [jnp.exp(jnp.where(di >= 0, di * lgh[hh:hh + 1, :], -di * lgh[H + hh:H + hh + 1, :]))
                 for hh in range(H)]
        e_f = jnp.exp((rows + 1.0) * lgf)
        e_b = jnp.exp((C - rows) * lgb)
        w_f = jnp.exp((C - 1.0 - rows) * lgf)
        tot_f = jnp.exp(C * lgf)
        for sb in range(sub):
            r0 = sb * C
            q = x_ref[0, r0:r0 + C, 0:HP]
            kb = x_ref[0, r0:r0 + C, HP:2 * HP]
            v = x_ref[0, r0:r0 + C, 2 * HP:3 * HP]
            gate = x_ref[0, r0:r0 + C, 3 * HP:4 * HP].astype(F32)
            ws, vs = [], []
            for hh in range(H):
                sc = _dot_nt(jnp.where(grp == hh, q, jnp.zeros_like(q)), kb)
                ws.append((sc * decay[hh]).astype(BF16))
                vs.append(jnp.where(grp == hh, v, jnp.zeros_like(v)))
            y = _dot(jnp.concatenate(ws, axis=1), jnp.concatenate(vs, axis=0))
            qf32 = q.astype(F32)
            s = s_ref[...]
            g_in = gbuf_ref[blk * sub + sb]
            y = y + _dot(jnp.concatenate([(qf32 * e_f).astype(BF16), (qf32 * e_b).astype(BF16)], axis=1),
                         jnp.concatenate([as_rows(s), as_rows(g_in)], axis=0).astype(BF16))
            s_ref[...] = s * tot_f + diag_blocks(_dot_tn((kb.astype(F32) * w_f).astype(BF16), v))

            mu = jnp.zeros_like(y)
            for hh in range(H):
                mh = jnp.sum(jnp.where(grp == hh, y, 0.0), axis=-1, keepdims=True) * (1.0 / P)
                mu = jnp.where(grp == hh, mh, mu)
            yc = y - mu
            var = jnp.zeros_like(y)
            for hh in range(H):
                vh = jnp.sum(jnp.where(grp == hh, yc * yc, 0.0), axis=-1, keepdims=True) * (1.0 / P)
                var = jnp.where(grp == hh, vh, var)
            y_ref[0, r0:r0 + C, :] = (yc * lax.rsqrt(var + EPS) * _silu(gate)).astype(BF16)

        @pl.when(c == nblk - 1)
        def _():
            fin_ref[0, 0] = s_ref[...]


def _block_index(nblk):
    return lambda i, ph, c: (i, (1 - ph) * (nblk - 1 - c) + ph * c, 0)


def _scan_sub(t):
    nc = t // CHUNK
    return 4 if nc % 4 == 0 else (2 if nc % 2 == 0 else 1)


def _ret_scan(ret, lw, init):
    b, t, _ = ret.shape
    sub = _scan_sub(t)
    rows = sub * CHUNK
    nblk = t // rows
    hp = RET_HEADS * RET_V
    return pl.pallas_call(
        functools.partial(_ret_kernel, nblk=nblk, sub=sub),
        grid=(b, 2, nblk),
        in_specs=[pl.BlockSpec((1, rows, W_RET), _block_index(nblk)),
                  pl.BlockSpec((2, hp), lambda i, ph, c: (0, 0)),
                  pl.BlockSpec((2 * RET_HEADS, CHUNK), lambda i, ph, c: (0, 0)),
                  pl.BlockSpec((1, 2, RET_QK, hp), lambda i, ph, c: (i, 0, 0, 0))],
        out_specs=[pl.BlockSpec((1, rows, hp), lambda i, ph, c: (i, ph * c, 0)),
                   pl.BlockSpec((1, 2, RET_QK, hp), lambda i, ph, c: (i, 0, 0, 0))],
        out_shape=[jax.ShapeDtypeStruct((b, t, hp), BF16),
                   jax.ShapeDtypeStruct((b, 2, RET_QK, hp), F32)],
        scratch_shapes=[pltpu.VMEM((RET_QK, hp), F32),
                        pltpu.VMEM((RET_QK, hp), F32),
                        pltpu.VMEM((t // CHUNK, RET_QK, hp), F32)],
        compiler_params=_cparams(("parallel", "arbitrary", "arbitrary")),
        name="retention_scan",
    )(ret, lw["ret_lgl"], lw["ret_lgh"], init)


def _ssd_kernel(x_ref, xp_ref, xn_ref, z_ref, dt_ref, cw_ref, cb_ref, dtb_ref, alog_ref, dvec_ref, nw_ref,
                init_ref, y_ref, fin_ref, s_ref, g_ref, gbuf_ref, act_ref, dtc_ref, abc_ref, *, nblk, sub):
    C, H, N, P, G = CHUNK, SSD_HEADS, SSD_STATE, SSD_HEADDIM, SSD_GROUPS
    HP = H * P
    HPG = H // G
    HALO = SSD_HALO
    ph, c = pl.program_id(1), pl.program_id(2)
    blk = jnp.where(ph == 0, nblk - 1 - c, c)

    ri = _iota((C, C + 2 * HALO), 0)
    ci = _iota((C, C + 2 * HALO), 1)
    sel_prev = (ci == ri + HALO - 1).astype(BF16)
    sel_next = (ci == ri + HALO + 1).astype(BF16)
    lane = _iota((LANES, HP), 0)
    head_of = _iota((LANES, HP), 1) // P
    e_f = (lane == head_of).astype(BF16)
    e_b = (lane == head_of + H).astype(BF16)
    r_i = _iota((C, C), 0)
    c_i = _iota((C, C), 1)
    low = (r_i >= c_i).astype(BF16)
    upp = (r_i <= c_i).astype(BF16)
    grp_n = _iota((N, HP), 1) // (HPG * P)
    neg_a = -jnp.exp(alog_ref[...])
    cw = cw_ref[...]

    def prep(sb, first_pass):
        r0 = sb * C
        if first_pass:
            xc = x_ref[0, r0:r0 + C, :]
            if sb == 0:
                prev = jnp.where(blk > 0, xp_ref[0], jnp.zeros_like(xp_ref[0]))
            else:
                prev = x_ref[0, r0 - HALO:r0, :]
            if sb == sub - 1:
                nxt = jnp.where(blk < nblk - 1, xn_ref[0], jnp.zeros_like(xn_ref[0]))
            else:
                nxt = x_ref[0, r0 + C:r0 + C + HALO, :]
            stacked = jnp.concatenate([prev, xc, nxt], axis=0)
            conv = (cw[0:1] * _dot(sel_prev, stacked) + cw[1:2] * xc.astype(F32)
                    + cw[2:3] * _dot(sel_next, stacked) + cb_ref[...])
            act = _silu(conv)
            raw = dt_ref[0, r0:r0 + C, :] + dtb_ref[...]
            dt = jnp.maximum(raw, 0.0) + jnp.log(1.0 + jnp.exp(-jnp.abs(raw)))
            la = dt * neg_a
            a_f = None
            a_b = _dot_exact_lhs01(upp, la)
            act_ref[blk * sub + sb] = act
            dtc_ref[blk * sub + sb] = dt
            abc_ref[blk * sub + sb] = a_b
        else:
            act = act_ref[blk * sub + sb]
            dt = dtc_ref[blk * sub + sb]
            a_b = abc_ref[blk * sub + sb]
            la = dt * neg_a
            a_f = _dot_exact_lhs01(low, la)
        xs = act[:, 0:HP]
        bm = act[:, HP:HP + G * N].astype(BF16)
        cm = act[:, HP + G * N:HP + 2 * G * N].astype(BF16)
        return xs, bm, cm, dt, la, a_f, a_b

    def state_update(state, bm, xs, wdt, tot8, e_dir, row):
        xw = (xs * _dot(wdt.astype(BF16), e_dir)).astype(BF16)
        tot = _dot_exact_rhs01(tot8, e_dir)[row:row + 1]
        r = _dot_tn(bm, xw)
        upd = r[0:N]
        for gg in range(1, G):
            upd = jnp.where(grp_n == gg, r[gg * N:(gg + 1) * N], upd)
        return state * tot + upd

    def as_rows(s_c):
        return jnp.concatenate([jnp.where(grp_n == gg, s_c, 0.0) for gg in range(G)], axis=0).astype(BF16)

    @pl.when((ph == 0) & (c == 0))
    def _():
        g_ref[...] = init_ref[0, 1]

    @pl.when(ph == 0)
    def _():
        for sb in reversed(range(sub)):
            xs, bm, _, dt, _, _, a_b = prep(sb, True)
            g = g_ref[...]
            gbuf_ref[blk * sub + sb] = g
            wdt = jnp.exp(a_b[0:1, :] - a_b) * dt
            g_ref[...] = state_update(g, bm, xs, wdt, jnp.exp(a_b[0:8, :]), e_b, 0)

        @pl.when(c == nblk - 1)
        def _():
            fin_ref[0, 1] = g_ref[...]

    @pl.when((ph == 1) & (c == 0))
    def _():
        s_ref[...] = init_ref[0, 0]

    @pl.when(ph == 1)
    def _():
        grp_l = _iota((C, G * N), 1) // N
        head_l = _iota((C, HP), 1) // P
        tri_f = r_i >= c_i
        diag = r_i == c_i
        for sb in range(sub):
            r0 = sb * C
            xs, bm, cm, dt, la, a_f, a_b = prep(sb, False)
            la_t = la.T[0:2 * H]
            dt_t = dt.T[0:2 * H]
            af_row = _dot_exact_rhs01(la_t, upp)
            ab_row = _dot_exact_rhs01(la_t, low)
            sc = [_dot_nt(jnp.where(grp_l == gg, cm, jnp.zeros_like(cm)), bm) for gg in range(G)]
            xsb = xs.astype(BF16)
            ws, vs = [], []
            for hh in range(H):
                seg_f = a_f[:, hh:hh + 1] - af_row[hh:hh + 1, :]
                seg_b = a_b[:, H + hh:H + hh + 1] - ab_row[H + hh:H + hh + 1, :]
                dt_f, dt_b = dt_t[hh:hh + 1, :], dt_t[H + hh:H + hh + 1, :]
                dm = (jnp.exp(jnp.where(tri_f, seg_f, seg_b)) * jnp.where(tri_f, dt_f, dt_b)
                      + jnp.where(diag, dt_b, 0.0))
                ws.append((sc[hh // HPG] * dm).astype(BF16))
                vs.append(jnp.where(head_l == hh, xsb, jnp.zeros_like(xsb)))
            y = _dot(jnp.concatenate(ws, axis=1), jnp.concatenate(vs, axis=0))

            s = s_ref[...]
            g_in = gbuf_ref[blk * sub + sb]
            y = y + _dot(jnp.exp(a_f).astype(BF16), e_f) * _dot(cm, as_rows(s))
            y = y + _dot(jnp.exp(a_b).astype(BF16), e_b) * _dot(cm, as_rows(g_in))

            wdt = jnp.exp(a_f[C - 1:C, :] - a_f) * dt
            s_ref[...] = state_update(s, bm, xs, wdt, jnp.exp(a_f[C - 8:C, :]), e_f, 7)

            y = y + xs * dvec_ref[...]
            y = y * _silu(z_ref[0, r0:r0 + C, :].astype(F32))
            y_ref[0, r0:r0 + C, :] = _rms(y, nw_ref[...]).astype(BF16)

        @pl.when(c == nblk - 1)
        def _():
            fin_ref[0, 0] = s_ref[...]


SSD_HALO = 16


def _ssd_scan(xbc, z, dt, lw, init):
    b, t, _ = xbc.shape
    sub = _scan_sub(t)
    rows = sub * CHUNK
    nblk = t // rows
    hp = SSD_INNER
    per = rows // SSD_HALO
    nhalo = t // SSD_HALO
    bidx = _block_index(nblk)

    def x_idx(i, ph, c):
        return (i, (1 - ph) * (nblk - 1 - c), 0)

    def prev_idx(i, ph, c):
        return (i, jnp.maximum((1 - ph) * (nblk - 1 - c) * per - 1, 0), 0)

    def next_idx(i, ph, c):
        return (i, jnp.minimum(((1 - ph) * (nblk - 1 - c) + 1) * per, nhalo - 1), 0)

    const = lambda shape: pl.BlockSpec(shape, lambda i, ph, c: (0,) * len(shape))
    return pl.pallas_call(
        functools.partial(_ssd_kernel, nblk=nblk, sub=sub),
        grid=(b, 2, nblk),
        in_specs=[pl.BlockSpec((1, rows, W_XBC), x_idx),
                  pl.BlockSpec((1, SSD_HALO, W_XBC), prev_idx),
                  pl.BlockSpec((1, SSD_HALO, W_XBC), next_idx),
                  pl.BlockSpec((1, rows, W_Z), bidx),
                  pl.BlockSpec((1, rows, W_DT), x_idx),
                  const((8, W_XBC)), const((1, W_XBC)), const((1, LANES)), const((1, LANES)),
                  const((1, hp)), const((1, hp)),
                  pl.BlockSpec((1, 2, SSD_STATE, hp), lambda i, ph, c: (i, 0, 0, 0))],
        out_specs=[pl.BlockSpec((1, rows, hp), lambda i, ph, c: (i, ph * c, 0)),
                   pl.BlockSpec((1, 2, SSD_STATE, hp), lambda i, ph, c: (i, 0, 0, 0))],
        out_shape=[jax.ShapeDtypeStruct((b, t, hp), BF16),
                   jax.ShapeDtypeStruct((b, 2, SSD_STATE, hp), F32)],
        scratch_shapes=[pltpu.VMEM((SSD_STATE, hp), F32),
                        pltpu.VMEM((SSD_STATE, hp), F32),
                        pltpu.VMEM((t // CHUNK, SSD_STATE, hp), F32),
                        pltpu.VMEM((t // CHUNK, CHUNK, W_XBC), F32),
                        pltpu.VMEM((t // CHUNK, CHUNK, W_DT), F32),
                        pltpu.VMEM((t // CHUNK, CHUNK, W_DT), F32)],
        compiler_params=_cparams(("parallel", "arbitrary", "arbitrary")),
        name="ssd_scan",
    )(xbc, xbc, xbc, z, dt, lw["conv_w"], lw["conv_b"], lw["dt_bias"], lw["a_log"], lw["d_vec"], lw["ssd_nw"], init)


def _mix_residual(x_ref, att_ref, ret_ref, ssd_ref, mod_ref, n2_ref, wo_ref):
    mix = jnp.concatenate([att_ref[0], ret_ref[0], ssd_ref[0]], axis=-1)
    x1 = x_ref[0] + mod_ref[0, 2:3, :] * _dot(mix, wo_ref[...])
    h = _rms(x1, n2_ref[...]) * (1.0 + mod_ref[0, 4:5, :]) + mod_ref[0, 3:4, :]
    return x1, h


def _dense_kernel(x_ref, att_ref, ret_ref, ssd_ref, mod_ref, n2_ref, wo_ref, wg_ref, wu_ref, wd_ref, o_ref,
                  *, ff_chunk):
    x1, h = _mix_residual(x_ref, att_ref, ret_ref, ssd_ref, mod_ref, n2_ref, wo_ref)
    hb = h.astype(BF16)
    ff = wg_ref.shape[1]
    f = jnp.zeros_like(x1)
    for s in range(0, ff, ff_chunk):
        e = min(s + ff_chunk, ff)
        g = _dot(hb, wg_ref[:, s:e])
        u = _dot(hb, wu_ref[:, s:e])
        f = f + _dot((_silu(g) * u).astype(BF16), wd_ref[s:e, :])
    o_ref[0] = x1 + mod_ref[0, 5:6, :] * f


def _dense_block(xs, att, ret, ssd, mods, lw, tm):
    b, t, d = xs.shape
    ff = lw["wg"].shape[1]
    tok = lambda w: pl.BlockSpec((1, tm, w), lambda i, j: (i, j, 0))
    const = lambda shape: pl.BlockSpec(shape, lambda i, j: (0,) * len(shape), pipeline_mode=pl.Buffered(1))
    return pl.pallas_call(
        functools.partial(_dense_kernel, ff_chunk=pl.cdiv(ff // 2, MXU_WIDTH) * MXU_WIDTH),
        grid=(b, t // tm),
        in_specs=[tok(d), tok(att.shape[-1]), tok(ret.shape[-1]), tok(ssd.shape[-1]),
                  pl.BlockSpec((1, 8, d), lambda i, j: (i, 0, 0)), const((1, d)),
                  const((d, d)), const((d, ff)), const((d, ff)), const((ff, d))],
        out_specs=tok(d),
        out_shape=jax.ShapeDtypeStruct((b, t, d), F32),
        compiler_params=_cparams(("parallel", "parallel")),
        name="outproj_ffn",
    )(xs, att, ret, ssd, mods, lw["n2"], lw["wo"], lw["wg"], lw["wu"], lw["wd"])


SUBLANES = 8


def _route_kernel(x_ref, att_ref, ret_ref, ssd_ref, mod_ref, n2_ref, wo_ref, wr_ref,
                  x1_ref, h_ref, slot_ref, gate_ref, cnt_ref):
    x1, h = _mix_residual(x_ref, att_ref, ret_ref, ssd_ref, mod_ref, n2_ref, wo_ref)
    x1_ref[0] = x1
    tm = h.shape[0]
    h_hi = h.astype(BF16)
    h_ref[0] = h_hi
    h_lo = (h - h_hi.astype(F32)).astype(BF16)
    logits = _dot(h_hi, wr_ref[0]) + _dot(h_lo, wr_ref[0]) + _dot(h_hi, wr_ref[1])
    lane = _iota(logits.shape, 1).astype(F32)
    logits = jnp.where(lane < N_EXPERTS, logits, NEG)
    m1 = jnp.max(logits, axis=-1, keepdims=True)
    i1 = jnp.min(jnp.where(logits == m1, lane, float(LANES)), axis=-1, keepdims=True)
    rest = jnp.where(lane == i1, NEG, logits)
    m2 = jnp.max(rest, axis=-1, keepdims=True)
    i2 = jnp.min(jnp.where(rest == m2, lane, float(LANES)), axis=-1, keepdims=True)
    e2 = jnp.exp(m2 - m1)
    den = 1.0 + e2
    sel1, sel2 = lane == i1, lane == i2
    chosen = jnp.where(sel1 | sel2, 1.0, 0.0)
    before = (_iota((tm, tm), 0) > _iota((tm, tm), 1)).astype(BF16)
    pos = _dot(before, chosen.astype(BF16))
    cnt = jnp.sum(chosen, axis=0, keepdims=True)
    nblk = sum(jnp.where(cnt > float(k * MOE_ROWS), 1.0, 0.0) for k in range(pl.cdiv(tm, MOE_ROWS)))
    lower_e = (_iota((LANES, LANES), 0) < _iota((LANES, LANES), 1)).astype(BF16)
    off = _dot_exact_rhs01(jnp.broadcast_to(nblk * float(MOE_ROWS), (SUBLANES, LANES)), lower_e)[0:1]
    slot = off + pos
    s1 = jnp.sum(jnp.where(sel1, slot, 0.0), axis=-1, keepdims=True)
    s2 = jnp.sum(jnp.where(sel2, slot, 0.0), axis=-1, keepdims=True)
    lane_i = _iota(logits.shape, 1)
    slots = jnp.where(lane_i == 0, s1, jnp.where(lane_i == 1, s2, -1.0))
    slot_ref[0, 0] = slots.T[0:SUBLANES].astype(jnp.int32)
    gate_ref[0] = jnp.where(sel1, 1.0 / den, 0.0) + jnp.where(sel2, e2 / den, 0.0)
    row = _iota((SUBLANES, LANES), 0)
    cnt_ref[0, 0] = jnp.where(row == 0, cnt, jnp.where(row == 1, off, jnp.where(row == 2, nblk, 0.0))
                              ).astype(jnp.int32)


def _route_block(xs, att, ret, ssd, mods, lw, tm):
    b, t, d = xs.shape
    nt = t // tm
    tok = lambda w: pl.BlockSpec((1, tm, w), lambda i, j: (i, j, 0))
    const = lambda shape: pl.BlockSpec(shape, lambda i, j: (0,) * len(shape))
    per_tile = lambda r, w: pl.BlockSpec((1, 1, r, w), lambda i, j: (i, j, 0, 0))
    return pl.pallas_call(
        _route_kernel,
        grid=(b, nt),
        in_specs=[tok(d), tok(att.shape[-1]), tok(ret.shape[-1]), tok(ssd.shape[-1]),
                  pl.BlockSpec((1, 8, d), lambda i, j: (i, 0, 0)), const((1, d)),
                  const((d, d)), const((2, d, LANES))],
        out_specs=[tok(d), tok(d), per_tile(SUBLANES, tm), tok(LANES), per_tile(SUBLANES, LANES)],
        out_shape=[jax.ShapeDtypeStruct((b, t, d), F32), jax.ShapeDtypeStruct((b, t, d), BF16),
                   jax.ShapeDtypeStruct((b, nt, SUBLANES, tm), jnp.int32),
                   jax.ShapeDtypeStruct((b, t, LANES), F32),
                   jax.ShapeDtypeStruct((b, nt, SUBLANES, LANES), jnp.int32)],
        compiler_params=_cparams(("parallel", "parallel")),
        name="outproj_route",
    )(xs, att, ret, ssd, mods, lw["n2"], lw["wo"], lw["router"])


MOE_ROWS = 256


def _moe_kernel(cnt_ref, slot_ref, h_ref, gate_ref, x1_ref, mod_ref, wg_ref, wu_ref, wt_ref, wd_ref, fn_ref, o_ref,
                *, final_norm, ff_split):
    e = pl.program_id(2)
    rb = MOE_ROWS
    tm = h_ref.shape[1]

    @pl.when(e == 0)
    def _():
        o_ref[...] = jnp.zeros_like(o_ref)

    off = cnt_ref[0, 0, 1, e]
    nblocks = cnt_ref[0, 0, 2, e]
    slot1 = slot_ref[0, 0, 0:1, :]
    slot2 = slot_ref[0, 0, 1:2, :]
    gates = gate_ref[0]
    ge = jnp.sum(jnp.where(_iota(gates.shape, 1) == e, gates, 0.0), axis=-1, keepdims=True)

    def block(bi, carry):
        rows = off + bi * rb + _iota((rb, tm), 0)
        sel = jnp.where((rows == slot1) | (rows == slot2), 1.0, 0.0).astype(BF16)
        lhs = _dot(sel, h_ref[0]).astype(BF16)
        y = None
        for s0, s1 in ff_split:
            if (s1 - s0) % MXU_WIDTH == 0:
                g, u = _dot(lhs, wg_ref[0, :, s0:s1]), _dot(lhs, wu_ref[0, :, s0:s1])
            else:
                gu = _dot(lhs, wt_ref[0])
                g, u = gu[:, :s1 - s0], gu[:, s1 - s0:]
            part = _dot((_silu(g) * u).astype(BF16), wd_ref[0, s0:s1, :])
            y = part if y is None else y + part
        o_ref[0] += ge * _dot_tn(sel, y.astype(BF16))
        return carry

    lax.fori_loop(0, nblocks, block, 0)

    @pl.when(e == N_EXPERTS - 1)
    def _():
        out = x1_ref[0] + mod_ref[0, 5:6, :] * o_ref[0]
        o_ref[0] = _rms(out, fn_ref[...]) if final_norm else out


MXU_WIDTH = 2 * LANES
MOE_FF_SLICE = 512


def _moe_ff_split(ff):
    aligned = ff // MXU_WIDTH * MXU_WIDTH
    cuts = list(range(0, aligned, MOE_FF_SLICE)) + [aligned]
    return tuple(zip(cuts[:-1], cuts[1:])) + (((aligned, ff),) if aligned < ff else ())


def _moe_weights(w_gate, w_up, w_down):
    ff = w_gate.shape[-1]
    aligned = ff // MXU_WIDTH * MXU_WIDTH
    tail = jnp.concatenate([w_gate[..., aligned:], w_up[..., aligned:]], axis=-1)
    if aligned == ff:
        tail = jnp.zeros(w_gate.shape[:-1] + (MXU_WIDTH,), w_gate.dtype)
    return w_gate.astype(BF16), w_up.astype(BF16), tail.astype(BF16), w_down.astype(BF16)


def _moe_block(x1, h, slots, gates, counts, mods, lw, fnw, final_norm, tm):
    b, t, d = x1.shape
    ff = lw["moe_wd"].shape[1]
    assert MOE_ROWS % (2 * SUBLANES) == 0
    tok = lambda w: pl.BlockSpec((1, tm, w), lambda i, j, e: (i, j, 0))
    per_tile = lambda w, **kw: pl.BlockSpec((1, 1, SUBLANES, w), lambda i, j, e: (i, j, 0, 0), **kw)
    expert = lambda r, w: pl.BlockSpec((1, r, w), lambda i, j, e: (e, 0, 0))
    return pl.pallas_call(
        functools.partial(_moe_kernel, final_norm=final_norm, ff_split=_moe_ff_split(ff)),
        grid=(b, t // tm, N_EXPERTS),
        in_specs=[per_tile(LANES, memory_space=pltpu.SMEM), per_tile(tm), tok(d), tok(LANES), tok(d),
                  pl.BlockSpec((1, 8, d), lambda i, j, e: (i, 0, 0)),
                  expert(d, ff), expert(d, ff), expert(d, lw["moe_wt"].shape[2]), expert(ff, d),
                  pl.BlockSpec((1, d), lambda i, j, e: (0, 0))],
        out_specs=tok(d),
        out_shape=jax.ShapeDtypeStruct((b, t, d), F32),
        compiler_params=_cparams(("parallel", "parallel", "arbitrary")),
        name="moe",
    )(counts, slots, h, gates, x1, mods, lw["moe_wg"], lw["moe_wu"], lw["moe_wt"], lw["moe_wd"], fnw)


def _final_norm_kernel(x_ref, w_ref, o_ref):
    o_ref[0] = _rms(x_ref[0], w_ref[...])


def _final_norm(xs, fnw, tm):
    b, t, d = xs.shape
    return pl.pallas_call(
        _final_norm_kernel,
        grid=(b, t // tm),
        in_specs=[pl.BlockSpec((1, tm, d), lambda i, j: (i, j, 0)), pl.BlockSpec((1, d), lambda i, j: (0, 0))],
        out_specs=pl.BlockSpec((1, tm, d), lambda i, j: (i, j, 0)),
        out_shape=jax.ShapeDtypeStruct((b, t, d), F32),
        compiler_params=_cparams(("parallel", "parallel")),
        name="final_norm",
    )(xs, fnw)


def _rot_cols(w):
    a, b2, c, d = jnp.split(w, 4, axis=-1)
    return jnp.concatenate([-b2, a, -d, c], axis=-1)


def _layer_weights(i, norm1_w, norm2_w, w_in, w_out, mla_q_norm_w, mla_w_qb, mla_kv_norm_w, mla_w_kb, mla_w_vb,
                   ret_decay_logit, ssd_conv_w, ssd_conv_b, ssd_dt_bias, ssd_a_log, ssd_d, ssd_norm_w):
    d = w_in.shape[1]
    offs = np.cumsum((0,) + IN_SIZES)
    wi = w_in[i]
    zeros = lambda n: jnp.zeros((d, n), F32)
    kpe = wi[:, offs[2]:offs[3]]
    w_mla = jnp.concatenate([wi[:, offs[0]:offs[3]], _rot_cols(kpe),
                             zeros(W_MLA - MLA_Q_LORA - MLA_KV_LORA - 2 * MLA_ROPE)], axis=1).astype(BF16)
    w_ret = wi[:, offs[3]:offs[7]].astype(BF16)
    w_z = wi[:, offs[7]:offs[8]].astype(BF16)
    w_xbc = wi[:, offs[8]:offs[9]].astype(BF16)
    w_dt = jnp.concatenate([wi[:, offs[9]:offs[11]], zeros(W_DT - 2 * SSD_HEADS)], axis=1).astype(BF16)

    dh = MLA_NOPE + MLA_ROPE
    wqb = mla_w_qb[i].reshape(MLA_Q_LORA, MLA_HEADS, dh)
    pad = lambda a, lo, hi: jnp.pad(a, ((0, 0), (0, 0), (lo, hi)))
    wq = pad(wqb, 0, MLA_HP - dh).reshape(MLA_Q_LORA, MLA_HEADS * MLA_HP).astype(BF16)
    wqr = pad(_rot_cols(wqb[:, :, MLA_NOPE:]), MLA_NOPE, MLA_HP - dh).reshape(MLA_Q_LORA, MLA_HEADS * MLA_HP).astype(BF16)
    wkb = pad(mla_w_kb[i].reshape(MLA_KV_LORA, MLA_HEADS, MLA_NOPE), 0, MLA_HP - MLA_NOPE)
    place = np.zeros((MLA_ROPE, MLA_HEADS, MLA_HP), np.float32)
    for r in range(MLA_ROPE):
        place[r, :, MLA_NOPE + r] = 1.0
    place = jnp.asarray(place)
    wk = jnp.concatenate([wkb, place, place, jnp.zeros((MLA_KV_LORA - 2 * MLA_ROPE, MLA_HEADS, MLA_HP), F32)],
                         axis=0).reshape(2 * MLA_KV_LORA, MLA_HEADS * MLA_HP).astype(BF16)

    wvb = mla_w_vb[i].reshape(MLA_KV_LORA, MLA_HEADS, MLA_V)
    wv = jnp.concatenate(
        [pad(wvb[:, hh:hh + 1], (hh % 2) * MLA_V, MLA_HP - MLA_V - (hh % 2) * MLA_V) for hh in range(MLA_HEADS)],
        axis=1).reshape(MLA_KV_LORA, MLA_HEADS * MLA_HP).astype(BF16)

    lg = ret_decay_logit[i]
    padl = lambda v: jnp.pad(v.reshape(1, -1), ((0, 0), (0, LANES - 2 * SSD_HEADS)))
    return {
        "n1": norm1_w[i].reshape(1, d), "n2": norm2_w[i].reshape(1, d),
        "w_mla": w_mla, "w_ret": w_ret, "w_z": w_z, "w_xbc": w_xbc, "w_dt": w_dt,
        "qn": mla_q_norm_w[i].reshape(1, -1), "wq": wq, "wqr": wqr,
        "kvn": mla_kv_norm_w[i].reshape(1, -1), "wk": wk, "wv": wv,
        "wo": w_out[i].astype(BF16),
        "ret_lgl": jnp.repeat(lg, RET_V, axis=1),
        "ret_lgh": jnp.broadcast_to(lg.reshape(2 * RET_HEADS, 1), (2 * RET_HEADS, CHUNK)),
        "conv_w": jnp.pad(ssd_conv_w[i], ((0, 8 - ssd_conv_w.shape[1]), (0, 0))),
        "conv_b": ssd_conv_b[i].reshape(1, -1),
        "dt_bias": padl(ssd_dt_bias[i]), "a_log": padl(ssd_a_log[i]),
        "d_vec": jnp.repeat(ssd_d[i], SSD_HEADDIM).reshape(1, -1),
        "ssd_nw": ssd_norm_w[i].reshape(1, -1),
    }


def _rope_tables(n_lat, n_ctx):
    f32 = np.float32
    rows = n_lat // GRID_W
    row = np.repeat(np.arange(rows, dtype=f32), GRID_W)
    col = np.tile(np.arange(GRID_W, dtype=f32), rows)
    inv_freq = (f32(ROPE_BASE) ** (-np.arange(0, ROPE_AXIS_DIM, 2, dtype=f32) / f32(ROPE_AXIS_DIM))).astype(f32)
    ang_r = row[:, None] * inv_freq
    ang_c = col[:, None] * inv_freq
    ang = np.concatenate([ang_r, ang_r, ang_c, ang_c], axis=-1).astype(f32)
    c2 = f32(MLA_SCALE * math.log2(math.e))

    def build(cos, sin):
        n = cos.shape[0]
        tail = np.zeros((n, MLA_HP - MLA_NOPE - MLA_ROPE), f32)
        cosq = np.concatenate([np.full((n, MLA_NOPE), c2, f32), c2 * cos, tail], 1)
        sinq = np.concatenate([np.zeros((n, MLA_NOPE), f32), c2 * sin, tail], 1)
        ck = np.concatenate([cos, sin, np.zeros((n, LANES - 2 * MLA_ROPE), f32)], 1)
        return tuple(jnp.asarray(a, F32) for a in (cosq, sinq, ck))

    lat = build(np.cos(ang).astype(f32), np.sin(ang).astype(f32))
    ctx = build(np.ones((n_ctx, MLA_ROPE), f32), np.zeros((n_ctx, MLA_ROPE), f32))
    return lat, ctx


def _token_tile(t, pref):
    tm = min(pref, t)
    while t % tm:
        tm //= 2
    return tm


def kernel(x, c, ctx, c_ctx, mod_w, mod_b, norm1_w, norm2_w, w_in, w_out, mla_q_norm_w, mla_w_qb, mla_kv_norm_w, mla_w_kb, mla_w_vb, ret_decay_logit, ssd_conv_w, ssd_conv_b, ssd_dt_bias, ssd_a_log, ssd_d, ssd_norm_w, ffn_w_gate, ffn_w_up, ffn_w_down, moe_router, moe_w_gate, moe_w_up, moe_w_down, final_norm_w):
    b, n_lat, d = x.shape
    n_ctx = ctx.shape[1]
    depth = mod_w.shape[0]
    assert b + 1 <= 8 and n_lat % CHUNK == 0 and n_ctx % CHUNK == 0 and n_lat % GRID_W == 0

    cond = jnp.concatenate([c, c_ctx[None, :], jnp.zeros((8 - b - 1, d), F32)], axis=0)
    mod_all = _modulation(cond, mod_w, mod_b)
    tabs_x, tabs_c = _rope_tables(n_lat, n_ctx)
    fnw = final_norm_w.reshape(1, d)

    tm_x = _token_tile(n_lat, 1024)
    tm_c = _token_tile(n_ctx, 256)
    tm_moe = _token_tile(n_lat, 1024)
    tq_x = _token_tile(n_lat, 1024)
    tk_x = 2048
    zero_ret = jnp.zeros((b, 2, RET_QK, RET_HEADS * RET_V), F32)
    zero_ssd = jnp.zeros((b, 2, SSD_STATE, SSD_INNER), F32)

    h_ctx = ctx
    for i in range(depth):
        last = i == depth - 1
        lw = _layer_weights(i, norm1_w, norm2_w, w_in, w_out, mla_q_norm_w, mla_w_qb, mla_kv_norm_w, mla_w_kb,
                            mla_w_vb, ret_decay_logit, ssd_conv_w, ssd_conv_b, ssd_dt_bias, ssd_a_log, ssd_d,
                            ssd_norm_w)
        m6 = mod_all[i].reshape(8, 6, d)
        pad2 = jnp.zeros((b, 2, d), F32)
        mods_x = jnp.concatenate([m6[:b], pad2], axis=1)
        mods_c = jnp.concatenate([jnp.broadcast_to(m6[b:b + 1], (b, 6, d)), pad2], axis=1)

        qc, kc, vc, retc, zc, xbcc, dtc = _inproj(h_ctx, mods_c, lw, tabs_c, tm_c)
        ret_yc, ret_fin = _ret_scan(retc, lw, zero_ret)
        ssd_yc, ssd_fin = _ssd_scan(xbcc, zc, dtc, lw, zero_ssd)

        qx, kx, vx, retx, zx, xbcx, dtx = _inproj(x, mods_x, lw, tabs_x, tm_x)
        att_x = _attention(qx, [(kc, vc), (kx, vx)], tq_x, tk_x)
        ret_yx, _ = _ret_scan(retx, lw, ret_fin)
        ssd_yx, _ = _ssd_scan(xbcx, zx, dtx, lw, ssd_fin)

        if i % 2 == 0:
            j = i // 2
            lw.update(wg=ffn_w_gate[j].astype(BF16), wu=ffn_w_up[j].astype(BF16), wd=ffn_w_down[j].astype(BF16))
            x = _dense_block(x, att_x, ret_yx, ssd_yx, mods_x, lw, tm_x)
            if last:
                x = _final_norm(x, fnw, tm_x)
        else:
            j = i // 2
            wg, wu, wt, wd = _moe_weights(moe_w_gate[j], moe_w_up[j], moe_w_down[j])
            r_f32 = jnp.pad(moe_router[j], ((0, 0), (0, LANES - N_EXPERTS)))
            r_hi = r_f32.astype(BF16)
            r_lo = (r_f32 - r_hi.astype(F32)).astype(BF16)
            lw.update(router=jnp.stack([r_hi, r_lo]), moe_wg=wg, moe_wu=wu, moe_wt=wt, moe_wd=wd)
            x1, *routed = _route_block(x, att_x, ret_yx, ssd_yx, mods_x, lw, tm_moe)
            x = _moe_block(x1, *routed, mods_x, lw, fnw, last, tm_moe)

        if not last:
            att_c = _attention(qc, [(kc, vc)], tm_c, tk_x)
            if i % 2 == 0:
                h_ctx = _dense_block(h_ctx, att_c, ret_yc, ssd_yc, mods_c, lw, tm_c)
            else:
                c1, *routed = _route_block(h_ctx, att_c, ret_yc, ssd_yc, mods_c, lw, tm_c)
                h_ctx = _moe_block(c1, *routed, mods_c, lw, fnw, False, tm_c)
    return x
```
